```python
import math
import jax, jax.numpy as jnp
from jax import lax
import numpy as np

D_MODEL = 1024
BATCH = 16
SEQ = 2048
DEPTH = 1

CHUNK = 64
Q_BLOCK = 128

CONV_WIDTH = 512
CONV_GROUPS = 8
CONV_K = 3

MLA_HEADS = 4
QK_NOPE = 128
QK_ROPE = 64
QK_HEAD = QK_NOPE + QK_ROPE
V_HEAD = 128
Q_LORA = 384
KV_LORA = 256
MLA_WIDTH = MLA_HEADS * V_HEAD
ROPE_THETA = 10000.0

MIX_WIDTH = CONV_WIDTH + MLA_WIDTH
IN_PROJ = 3 * CONV_WIDTH + Q_LORA + KV_LORA + QK_ROPE

N_EXPERTS = 32
TOP_K = 4
D_FF = 1024
SWIGLU_LIMIT = 7.0
SWIGLU_ALPHA = 1.702

EPS = 1e-6

kernel_name = "hymba_conv_mla_moe_streaming_block"


def rmsnorm(x, g):
    xf = x.astype(jnp.float32)
    y = xf * lax.rsqrt(jnp.mean(xf * xf, axis=-1, keepdims=True) + EPS)
    return (y * g.astype(jnp.float32)).astype(x.dtype)


def rope(x, pos):
    half = QK_ROPE // 2
    inv_freq = ROPE_THETA ** (-jnp.arange(half, dtype=jnp.float32) / half)
    ang = pos.astype(jnp.float32)[:, None] * inv_freq[None, :]
    cos = jnp.cos(ang)[None, :, None, :].astype(x.dtype)
    sin = jnp.sin(ang)[None, :, None, :].astype(x.dtype)
    x1, x2 = x[..., :half], x[..., half:]
    return jnp.concatenate([x1 * cos - x2 * sin, x1 * sin + x2 * cos], axis=-1)


def short_conv_mixer(b_gate, c_gate, x_in, conv_w):
    u = c_gate * x_in
    y = lax.conv_general_dilated(
        u, conv_w[:, None, :],
        window_strides=(1,), padding=[(CONV_K - 1, 0)],
        dimension_numbers=("NWC", "WIO", "NWC"),
        feature_group_count=CONV_WIDTH)
    return b_gate * y


def mla_mixer(c_q, c_kv, k_pe, q_norm, w_uq, kv_norm, w_ukv, q_head_norm, k_head_norm):
    bsz, seq = c_q.shape[0], c_q.shape[1]
    pos = jnp.arange(seq)
    q = (rmsnorm(c_q, q_norm) @ w_uq).reshape(bsz, seq, MLA_HEADS, QK_HEAD)
    kv = (rmsnorm(c_kv, kv_norm) @ w_ukv).reshape(bsz, seq, MLA_HEADS, QK_NOPE + V_HEAD)
    k_nope, v = kv[..., :QK_NOPE], kv[..., QK_NOPE:]
    k_pe_h = jnp.broadcast_to(k_pe[:, :, None, :], (bsz, seq, MLA_HEADS, QK_ROPE))
    k = jnp.concatenate([k_nope, k_pe_h], axis=-1)
    q = rmsnorm(q, q_head_norm)
    k = rmsnorm(k, k_head_norm)
    q = jnp.concatenate([q[..., :QK_NOPE], rope(q[..., QK_NOPE:], pos)], axis=-1)
    k = jnp.concatenate([k[..., :QK_NOPE], rope(k[..., QK_NOPE:], pos)], axis=-1)

    scale = 1.0 / math.sqrt(QK_HEAD)
    n_blk = seq // Q_BLOCK
    q_blocks = jnp.moveaxis(q.reshape(bsz, n_blk, Q_BLOCK, MLA_HEADS, QK_HEAD), 1, 0)
    k_chunk = jnp.arange(seq) // CHUNK
    neg = jnp.finfo(jnp.float32).min

    def attend(args):
        qb, i = args
        q_chunk = (i * Q_BLOCK + jnp.arange(Q_BLOCK)) // CHUNK
        allowed = k_chunk[None, :] <= q_chunk[:, None]
        s = jnp.einsum("bqhd,bkhd->bhqk", qb, k,
                       preferred_element_type=jnp.float32) * scale
        s = jnp.where(allowed[None, None], s, neg)
        p = jax.nn.softmax(s, axis=-1).astype(v.dtype)
        return jnp.einsum("bhqk,bkhd->bqhd", p, v)

    o = lax.map(attend, (q_blocks, jnp.arange(n_blk)))
    return jnp.moveaxis(o, 0, 1).reshape(bsz, seq, MLA_WIDTH)


def moe_ffn(h, w_router, b_router, w_gate_up, b_gate_up, w_down, b_down):
    bsz, seq, d = h.shape
    n_tok = bsz * seq
    hf = h.reshape(n_tok, d)
    logits = (hf @ w_router + b_router).astype(jnp.float32)
    top_vals, top_idx = lax.top_k(logits, TOP_K)
    gates = jax.nn.softmax(top_vals, axis=-1).astype(h.dtype)
    flat_e = top_idx.reshape(-1)
    order = jnp.argsort(flat_e)
    token_of = order // TOP_K
    e_sorted = flat_e[order]
    xs = hf[token_of]
    sizes = jnp.bincount(flat_e, length=N_EXPERTS).astype(jnp.int32)
    gu = lax.ragged_dot(xs, w_gate_up, sizes) + b_gate_up[e_sorted]
    gate = jnp.minimum(gu[:, :D_FF], SWIGLU_LIMIT)
    up = jnp.clip(gu[:, D_FF:], -SWIGLU_LIMIT, SWIGLU_LIMIT)
    act = (up + 1.0) * (gate * jax.nn.sigmoid(SWIGLU_ALPHA * gate))
    out = lax.ragged_dot(act, w_down, sizes) + b_down[e_sorted]
    out = out * gates.reshape(-1)[order][:, None]
    y = jax.ops.segment_sum(out, token_of, num_segments=n_tok)
    return y.reshape(bsz, seq, d)


def setup_inputs(seed: int = 0) -> dict:
    key = jax.random.key(seed)
    ks = jax.random.split(key, 18)

    def nrm(k, shape, scale):
        return jax.random.normal(k, shape, jnp.float32) * scale

    def gain(k, shape):
        return 1.0 + 0.02 * jax.random.normal(k, shape, jnp.float32)

    L = DEPTH
    return {
        "x": nrm(ks[0], (BATCH, SEQ, D_MODEL), 1.0),
        "norm_mix": gain(ks[1], (L, D_MODEL)),
        "w_in": nrm(ks[2], (L, D_MODEL, IN_PROJ), D_MODEL ** -0.5),
        "conv_w": nrm(ks[3], (L, CONV_K, CONV_WIDTH), CONV_K ** -0.5),
        "q_norm": gain(ks[4], (L, Q_LORA)),
        "w_uq": nrm(ks[5], (L, Q_LORA, MLA_HEADS * QK_HEAD), Q_LORA ** -0.5),
        "kv_norm": gain(ks[6], (L, KV_LORA)),
        "w_ukv": nrm(ks[7], (L, KV_LORA, MLA_HEADS * (QK_NOPE + V_HEAD)), KV_LORA ** -0.5),
        "q_head_norm": gain(ks[8], (L, QK_HEAD)),
        "k_head_norm": gain(ks[9], (L, QK_HEAD)),
        "w_o": nrm(ks[10], (L, MIX_WIDTH, D_MODEL), MIX_WIDTH ** -0.5),
        "norm_ffn": gain(ks[11], (L, D_MODEL)),
        "w_router": nrm(ks[12], (L, D_MODEL, N_EXPERTS), D_MODEL ** -0.5),
        "b_router": nrm(ks[13], (L, N_EXPERTS), 0.01),
        "w_gate_up": nrm(ks[14], (L, N_EXPERTS, D_MODEL, 2 * D_FF), D_MODEL ** -0.5),
        "b_gate_up": nrm(ks[15], (L, N_EXPERTS, 2 * D_FF), 0.02),
        "w_down": nrm(ks[16], (L, N_EXPERTS, D_FF, D_MODEL), D_FF ** -0.5),
        "b_down": nrm(ks[17], (L, N_EXPERTS, D_MODEL), 0.02),
    }


def reference(x, norm_mix, w_in, conv_w, q_norm, w_uq, kv_norm, w_ukv, q_head_norm,
              k_head_norm, w_o, norm_ffn, w_router, b_router, w_gate_up, b_gate_up,
              w_down, b_down):
    h = x
    splits = np.cumsum([CONV_WIDTH, CONV_WIDTH, CONV_WIDTH, Q_LORA, KV_LORA]).tolist()
    for l in range(DEPTH):
        hn = rmsnorm(h, norm_mix[l])
        proj = hn @ w_in[l]
        b_gate, c_gate, x_in, c_q, c_kv, k_pe = jnp.split(proj, splits, axis=-1)
        conv_out = short_conv_mixer(b_gate, c_gate, x_in, conv_w[l])
        attn_out = mla_mixer(c_q, c_kv, k_pe, q_norm[l], w_uq[l], kv_norm[l],
                             w_ukv[l], q_head_norm[l], k_head_norm[l])
        h = h + jnp.concatenate([conv_out, attn_out], axis=-1) @ w_o[l]
        h = h + moe_ffn(rmsnorm(h, norm_ffn[l]), w_router[l], b_router[l],
                        w_gate_up[l], b_gate_up[l], w_down[l], b_down[l])
    return h
```

```python
import functools
import math

import jax
import jax.numpy as jnp
import numpy as np
from jax import lax
from jax.experimental import pallas as pl
from jax.experimental.pallas import tpu as pltpu

F32 = jnp.float32
BF16 = jnp.bfloat16

D_MODEL = 1024
CONV_WIDTH = 512
MLA_HEADS = 4
QK_NOPE = 128
QK_ROPE = 64
QK_HEAD = QK_NOPE + QK_ROPE
QK_PAD = 256
V_HEAD = 128
Q_LORA = 384
KV_LORA = 256
ROPE_THETA = 10000.0
CHUNK = 64
N_EXPERTS = 32
TOP_K = 4
D_FF = 1024
SWIGLU_LIMIT = 7.0
SWIGLU_ALPHA = 1.702
EPS = 1e-6

OFF_B, OFF_C, OFF_X, OFF_CQ, OFF_CKV, OFF_KPE = 0, 512, 1024, 1536, 1920, 2176
IN_PROJ_EXT = 2304

VMEM_LIMIT = 56 * 1024 * 1024

TS_PRE = 512
TQ = 256
TM_POST = 512
TB_ROWS = 256
TM_EXP = 512
HALO = 8


def _rms(x, g):
    return x * lax.rsqrt(jnp.mean(x * x, axis=-1, keepdims=True) + EPS) * g


def _pre_kernel(x_ref, nm_ref, win_ref, convw_ref, qn_ref, wuq_ref, kvn_ref, wukv_ref,
                gqn_ref, gqr_ref, gkn_ref, gkr_ref, tbl_ref,
                conv_ref, q_ref, k_ref, v_ref, ext_ref):
    s = pl.program_id(1)
    ts = x_ref.shape[1]
    hn = _rms(x_ref[0], nm_ref[...])
    proj = jnp.dot(hn.astype(BF16), win_ref[...], preferred_element_type=F32)

    @pl.when(s == 0)
    def _():
        ext_ref[0:HALO, :] = jnp.zeros((HALO, CONV_WIDTH), F32)

    @pl.when(s != 0)
    def _():
        ext_ref[0:HALO, :] = ext_ref[ts:ts + HALO, :]

    u = proj[:, OFF_C:OFF_C + CONV_WIDTH] * proj[:, OFF_X:OFF_X + CONV_WIDTH]
    ext_ref[HALO:HALO + ts, :] = u
    y = (convw_ref[2:3, :] * u
         + convw_ref[1:2, :] * ext_ref[HALO - 1:HALO - 1 + ts, :]
         + convw_ref[0:1, :] * ext_ref[HALO - 2:HALO - 2 + ts, :])
    conv_ref[0] = (proj[:, OFF_B:OFF_B + CONV_WIDTH] * y).astype(BF16)

    hq = _rms(proj[:, OFF_CQ:OFF_CQ + Q_LORA], qn_ref[...]).astype(BF16)
    qp = jnp.dot(hq, wuq_ref[...], preferred_element_type=F32)
    hkv = _rms(proj[:, OFF_CKV:OFF_CKV + KV_LORA], kvn_ref[...]).astype(BF16)
    kvp = jnp.dot(hkv, wukv_ref[...], preferred_element_type=F32)

    tbl = tbl_ref[...]
    lane = lax.broadcasted_iota(jnp.int32, (ts, 128), 1)
    first_half = lane < QK_ROPE

    def rope_block(grp, gain):
        r = grp * tbl * gain
        r = r + pltpu.roll(r, QK_ROPE, axis=1)
        return jnp.where(first_half, r, 0.0)

    kpe = proj[:, OFF_KPE:OFF_KPE + 128]
    ss_kpe = jnp.sum(jnp.where(first_half, kpe * kpe, 0.0), axis=-1, keepdims=True)
    k_rope = rope_block(kpe, gkr_ref[...])
    scale = 1.0 / math.sqrt(QK_HEAD)
    for h in range(MLA_HEADS):
        qn = qp[:, 128 * h:128 * (h + 1)]
        qg = qp[:, 512 + 128 * h:512 + 128 * (h + 1)]
        ss = (jnp.sum(qn * qn, axis=-1, keepdims=True)
              + jnp.sum(jnp.where(first_half, qg * qg, 0.0), axis=-1, keepdims=True))
        rs = lax.rsqrt(ss * (1.0 / QK_HEAD) + EPS) * scale
        q_ref[0, h, :, 0:128] = (qn * rs * gqn_ref[...]).astype(BF16)
        q_ref[0, h, :, 128:256] = (rope_block(qg, gqr_ref[...]) * rs).astype(BF16)

        kn = kvp[:, 256 * h:256 * h + 128]
        ssk = jnp.sum(kn * kn, axis=-1, keepdims=True) + ss_kpe
        rsk = lax.rsqrt(ssk * (1.0 / QK_HEAD) + EPS)
        k_ref[0, h, :, 0:128] = (kn * rsk * gkn_ref[...]).astype(BF16)
        k_ref[0, h, :, 128:256] = (k_rope * rsk).astype(BF16)
        v_ref[0, h] = kvp[:, 256 * h + 128:256 * (h + 1)].astype(BF16)


def _pre_call(x, nm, win, convw, qn, wuq, kvn, wukv, gqn, gqr, gkn, gkr, tbl):
    b, s, d = x.shape
    ts = TS_PRE
    const = lambda shape: pl.BlockSpec(shape, lambda bi, si: (0,) * len(shape))
    return pl.pallas_call(
        _pre_kernel,
        grid=(b, s // ts),
        in_specs=[
            pl.BlockSpec((1, ts, d), lambda bi, si: (bi, si, 0)),
            const((1, d)), const((d, IN_PROJ_EXT)), const((3, CONV_WIDTH)),
            const((1, Q_LORA)), const((Q_LORA, 1024)),
            const((1, KV_LORA)), const((KV_LORA, 1024)),
            const((1, 128)), const((1, 128)), const((1, 128)), const((1, 128)),
            pl.BlockSpec((ts, 128), lambda bi, si: (si, 0)),
        ],
        out_specs=[
            pl.BlockSpec((1, ts, CONV_WIDTH), lambda bi, si: (bi, si, 0)),
            pl.BlockSpec((1, MLA_HEADS, ts, QK_PAD), lambda bi, si: (bi, 0, si, 0)),
            pl.BlockSpec((1, MLA_HEADS, ts, QK_PAD), lambda bi, si: (bi, 0, si, 0)),
            pl.BlockSpec((1, MLA_HEADS, ts, V_HEAD), lambda bi, si: (bi, 0, si, 0)),
        ],
        out_shape=[
            jax.ShapeDtypeStruct((b, s, CONV_WIDTH), BF16),
            jax.ShapeDtypeStruct((b, MLA_HEADS, s, QK_PAD), BF16),
            jax.ShapeDtypeStruct((b, MLA_HEADS, s, QK_PAD), BF16),
            jax.ShapeDtypeStruct((b, MLA_HEADS, s, V_HEAD), BF16),
        ],
        scratch_shapes=[pltpu.VMEM((ts + HALO, CONV_WIDTH), F32)],
        compiler_params=pltpu.CompilerParams(
            dimension_semantics=("arbitrary", "arbitrary"), vmem_limit_bytes=VMEM_LIMIT),
        name="pre_mixer",
    )(x, nm, win, convw, qn, wuq, kvn, wukv, gqn, gqr, gkn, gkr, tbl)


def _attn_kernel(q_ref, k_ref, v_ref, o_ref):
    i = pl.program_id(1)
    tq = q_ref.shape[2]
    row_c = lax.broadcasted_iota(jnp.int32, (tq, tq), 0) // CHUNK
    col_c = lax.broadcasted_iota(jnp.int32, (tq, tq), 1) // CHUNK
    allowed = col_c <= row_c
    for h in range(MLA_HEADS):
        q = q_ref[0, h]

        def block(j, carry, masked):
            m, l, acc = carry
            start = pl.multiple_of(j * tq, tq)
            kb = k_ref[0, h, pl.ds(start, tq), :]
            vb = v_ref[0, h, pl.ds(start, tq), :]
            sc = lax.dot_general(q, kb, (((1,), (1,)), ((), ())),
                                 preferred_element_type=F32)
            if masked:
                sc = jnp.where(allowed, sc, -1e30)
            m_new = jnp.maximum(m, jnp.max(sc, axis=-1, keepdims=True))
            alpha = jnp.exp(m - m_new)
            p = jnp.exp(sc - m_new)
            l = alpha * l + jnp.sum(p, axis=-1, keepdims=True)
            acc = alpha * acc + jnp.dot(p.astype(BF16), vb, preferred_element_type=F32)
            return m_new, l, acc

        init = (jnp.full((tq, 1), -1e30, F32), jnp.zeros((tq, 1), F32),
                jnp.zeros((tq, V_HEAD), F32))
        carry = lax.fori_loop(0, i, lambda j, c: block(j, c, False), init)
        _, l, acc = block(i, carry, True)
        o_ref[0, :, V_HEAD * h:V_HEAD * (h + 1)] = (acc / l).astype(BF16)


def _attn_call(q, k, v):
    b, nh, s, _ = q.shape
    return pl.pallas_call(
        _attn_kernel,
        grid=(b, s // TQ),
        in_specs=[
            pl.BlockSpec((1, nh, TQ, QK_PAD), lambda bi, qi: (bi, 0, qi, 0)),
            pl.BlockSpec((1, nh, s, QK_PAD), lambda bi, qi: (bi, 0, 0, 0)),
            pl.BlockSpec((1, nh, s, V_HEAD), lambda bi, qi: (bi, 0, 0, 0)),
        ],
        out_specs=pl.BlockSpec((1, TQ, nh * V_HEAD), lambda bi, qi: (bi, qi, 0)),
        out_shape=jax.ShapeDtypeStruct((b, s, nh * V_HEAD), BF16),
        compiler_params=pltpu.CompilerParams(
            dimension_semantics=("arbitrary", "arbitrary"), vmem_limit_bytes=VMEM_LIMIT),
        name="mla_attention",
    )(q, k, v)


def _post_kernel(x_ref, conv_ref, attn_ref, wo_ref, g_ref, wr_ref, br_ref, tri_ref,
                 h1_ref, hp_ref, idx_ref, rank_ref, gate_ref, cnt_ref, carry_ref):
    step = pl.program_id(0)
    tm = x_ref.shape[0]

    @pl.when(step == 0)
    def _():
        carry_ref[...] = jnp.zeros_like(carry_ref)

    h1 = (x_ref[...]
          + jnp.dot(conv_ref[...], wo_ref[0:CONV_WIDTH, :], preferred_element_type=F32)
          + jnp.dot(attn_ref[...], wo_ref[CONV_WIDTH:, :], preferred_element_type=F32))
    h1_ref[...] = h1
    hb = _rms(h1, g_ref[...]).astype(BF16)
    half = D_MODEL // 2
    lo = lax.bitcast_convert_type(hb[:, :half].astype(F32), jnp.uint32) >> 16
    hi = lax.bitcast_convert_type(hb[:, half:].astype(F32), jnp.uint32) & jnp.uint32(0xFFFF0000)
    hp_ref[...] = lo | hi

    logits = jnp.dot(hb, wr_ref[...], preferred_element_type=F32) + br_ref[...]
    lane = lax.broadcasted_iota(jnp.int32, (tm, N_EXPERTS), 1).astype(F32)
    lane4 = lax.broadcasted_iota(jnp.int32, (tm, TOP_K), 1)
    member = jnp.zeros((tm, N_EXPERTS), F32)
    vals, idxs, hots = [], [], []
    lg = logits
    for _ in range(TOP_K):
        m = jnp.max(lg, axis=-1, keepdims=True)
        ix = jnp.min(jnp.where(lg == m, lane, float(N_EXPERTS)), axis=-1, keepdims=True)
        hot = lane == ix
        lg = jnp.where(hot, -jnp.inf, lg)
        member = member + hot.astype(F32)
        vals.append(m)
        idxs.append(ix)
        hots.append(hot)

    before = jnp.dot(tri_ref[...], member.astype(BF16), preferred_element_type=F32)
    before = before + carry_ref[...]
    ranks = [jnp.sum(jnp.where(hot, before, 0.0), axis=-1, keepdims=True) for hot in hots]
    carry_ref[...] = carry_ref[...] + jnp.sum(member, axis=0, keepdims=True)
    cnt_ref[...] = carry_ref[...].astype(jnp.int32)

    exps = [jnp.exp(v - vals[0]) for v in vals]
    denom = exps[0] + exps[1] + exps[2] + exps[3]

    def cols(c):
        return jnp.where(lane4 == 0, c[0], jnp.where(lane4 == 1, c[1],
                         jnp.where(lane4 == 2, c[2], c[3])))

    idx_ref[...] = cols(idxs).astype(jnp.int32)
    rank_ref[...] = cols(ranks).astype(jnp.int32)
    gate_ref[...] = cols([e / denom for e in exps])


def _post_call(xf, conv, attn, wo, g, wr, br, tri):
    n, d = xf.shape
    tm = TM_POST
    const = lambda shape: pl.BlockSpec(shape, lambda i: (0,) * len(shape))
    row = lambda w: pl.BlockSpec((tm, w), lambda i: (i, 0))
    return pl.pallas_call(
        _post_kernel,
        grid=(n // tm,),
        in_specs=[row(d), row(CONV_WIDTH), row(CONV_WIDTH), const((d, d)), const((1, d)),
                  const((d, N_EXPERTS)), const((1, N_EXPERTS)), const((tm, tm))],
        out_specs=[row(d), row(d // 2), row(TOP_K), row(TOP_K), row(TOP_K),
                   const((1, N_EXPERTS))],
        out_shape=[
            jax.ShapeDtypeStruct((n, d), F32),
            jax.ShapeDtypeStruct((n, d // 2), jnp.uint32),
            jax.ShapeDtypeStruct((n, TOP_K), jnp.int32),
            jax.ShapeDtypeStruct((n, TOP_K), jnp.int32),
            jax.ShapeDtypeStruct((n, TOP_K), F32),
            jax.ShapeDtypeStruct((1, N_EXPERTS), jnp.int32),
        ],
        scratch_shapes=[pltpu.VMEM((1, N_EXPERTS), F32)],
        compiler_params=pltpu.CompilerParams(
            dimension_semantics=("arbitrary",), vmem_limit_bytes=VMEM_LIMIT),
        name="post_mixer_router",
    )(xf, conv, attn, wo, g, wr, br, tri)


def _scatter_kernel(pos_ref, hp_ref, xs_ref, sem):
    tb = hp_ref.shape[0]

    def issue(t, c):
        for k in range(TOP_K):
            p = pos_ref[t * TOP_K + k]
            pltpu.make_async_copy(hp_ref.at[pl.ds(t, 1), :], xs_ref.at[pl.ds(p, 1), :],
                                  sem).start()
        return c

    lax.fori_loop(0, tb, issue, 0)
    for k in range(TOP_K):
        pltpu.make_async_copy(hp_ref, xs_ref.at[pl.ds(0, tb), :], sem).wait()


def _scatter_call(pos_flat, hp):
    n, w = hp.shape
    tb = TB_ROWS
    return pl.pallas_call(
        _scatter_kernel,
        grid=(n // tb,),
        in_specs=[pl.BlockSpec((tb * TOP_K,), lambda i: (i,), memory_space=pltpu.SMEM),
                  pl.BlockSpec((tb, w), lambda i: (i, 0))],
        out_specs=pl.BlockSpec(memory_space=pl.ANY),
        out_shape=jax.ShapeDtypeStruct((n * TOP_K, w), hp.dtype),
        scratch_shapes=[pltpu.SemaphoreType.DMA],
        compiler_params=pltpu.CompilerParams(
            dimension_semantics=("arbitrary",), vmem_limit_bytes=VMEM_LIMIT),
        name="dispatch_scatter",
    )(pos_flat, hp)


def _expert_kernel(tile_ref, exp_ref, lo_ref, hi_ref, flag_ref,
                   xs_ref, wgu_ref, bgu_ref, wd_ref, bd_ref, o_ref, wgu_bf, wd_bf):
    i = pl.program_id(0)
    tm = xs_ref.shape[0]
    lo, hi, flag = lo_ref[i], hi_ref[i], flag_ref[i]
    cast_rows = 128

    @pl.when((flag & 2) != 0)
    def _():
        def cast(r, c):
            rows = pl.ds(pl.multiple_of(r * cast_rows, cast_rows), cast_rows)
            wgu_bf[rows, :] = wgu_ref[0, rows, :].astype(BF16)
            wd_bf[rows, :] = wd_ref[0, rows, :].astype(BF16)
            return c
        lax.fori_loop(0, D_MODEL // cast_rows, cast, 0)

    @pl.when(hi > lo)
    def _():
        xp = xs_ref[...]
        half = D_MODEL // 2
        x_lo = lax.bitcast_convert_type(xp << 16, F32).astype(BF16)
        x_hi = lax.bitcast_convert_type(xp & jnp.uint32(0xFFFF0000), F32).astype(BF16)
        gu = (jnp.dot(x_lo, wgu_bf[0:half, :], preferred_element_type=F32)
              + jnp.dot(x_hi, wgu_bf[half:, :], preferred_element_type=F32)
              + bgu_ref[0])
        gate = jnp.minimum(gu[:, :D_FF], SWIGLU_LIMIT)
        up = jnp.clip(gu[:, D_FF:], -SWIGLU_LIMIT, SWIGLU_LIMIT)
        act = (up + 1.0) * (gate * (1.0 / (1.0 + jnp.exp(-SWIGLU_ALPHA * gate))))
        out = jnp.dot(act.astype(BF16), wd_bf[...], preferred_element_type=F32) + bd_ref[0]
        row = lax.broadcasted_iota(jnp.int32, (tm, 1), 0)
        mine = (row >= lo) & (row < hi)

        @pl.when((flag & 1) != 0)
        def _():
            o_ref[...] = jnp.where(mine, out, 0.0)

        @pl.when((flag & 1) == 0)
        def _():
            o_ref[...] = jnp.where(mine, out, o_ref[...])


def _expert_call(meta, xs, wgu, bgu, wd, bd):
    rows, w = xs.shape
    tm = TM_EXP
    n_items = meta[0].shape[0]
    return pl.pallas_call(
        _expert_kernel,
        grid_spec=pltpu.PrefetchScalarGridSpec(
            num_scalar_prefetch=5,
            grid=(n_items,),
            in_specs=[
                pl.BlockSpec((tm, w), lambda i, t, e, lo, hi, f: (t[i], 0)),
                pl.BlockSpec((1, D_MODEL, 2 * D_FF), lambda i, t, e, lo, hi, f: (e[i], 0, 0)),
                pl.BlockSpec((1, 1, 2 * D_FF), lambda i, t, e, lo, hi, f: (e[i], 0, 0)),
                pl.BlockSpec((1, D_FF, D_MODEL), lambda i, t, e, lo, hi, f: (e[i], 0, 0)),
                pl.BlockSpec((1, 1, D_MODEL), lambda i, t, e, lo, hi, f: (e[i], 0, 0)),
            ],
            out_specs=pl.BlockSpec((tm, D_MODEL), lambda i, t, e, lo, hi, f: (t[i], 0)),
            scratch_shapes=[pltpu.VMEM((D_MODEL, 2 * D_FF), BF16),
                            pltpu.VMEM((D_FF, D_MODEL), BF16)],
        ),
        out_shape=jax.ShapeDtypeStruct((rows, D_MODEL), F32),
        compiler_params=pltpu.CompilerParams(
            dimension_semantics=("arbitrary",), vmem_limit_bytes=VMEM_LIMIT),
        name="moe_experts",
    )(*meta, xs, wgu, bgu, wd, bd)


def _expert_items(counts, rows, tm):
    n_tiles = rows // tm
    n_items = n_tiles + N_EXPERTS - 1
    ends = jnp.cumsum(counts)
    starts = ends - counts
    first_tile = starts // tm
    last_tile = jnp.maximum(ends - 1, 0) // tm
    per_exp = jnp.where(counts > 0, last_tile - first_tile + 1, 0)
    item_end = jnp.cumsum(per_exp)
    item_start = item_end - per_exp
    total = item_end[-1]
    i = jnp.arange(n_items, dtype=jnp.int32)
    valid = i < total
    e = jnp.minimum(jnp.searchsorted(item_end, i, side="right"), N_EXPERTS - 1).astype(jnp.int32)
    last_e = jnp.minimum(jnp.searchsorted(item_end, total - 1, side="right"),
                         N_EXPERTS - 1).astype(jnp.int32)
    e = jnp.where(valid, e, last_e)
    tile = jnp.where(valid, first_tile[e] + (i - item_start[e]), n_tiles - 1)
    lo = jnp.where(valid, jnp.maximum(starts[e], tile * tm) - tile * tm, 0)
    hi = jnp.where(valid, jnp.minimum(ends[e], (tile + 1) * tm) - tile * tm, 0)
    flag = jnp.where(valid, (lo == 0).astype(jnp.int32) + 2 * (i == item_start[e]), 0)
    cast = lambda a: a.astype(jnp.int32)
    return (cast(tile), cast(e), cast(lo), cast(hi), cast(flag)), starts


def _combine_kernel(pos_ref, h1_ref, gate_ref, ys_ref, o_ref, buf, sem):
    tb = h1_ref.shape[0]

    def issue(t, c):
        for k in range(TOP_K):
            p = pos_ref[t * TOP_K + k]
            pltpu.make_async_copy(ys_ref.at[pl.ds(p, 1), :], buf.at[k, pl.ds(t, 1), :],
                                  sem).start()
        return c

    lax.fori_loop(0, tb, issue, 0)
    for k in range(TOP_K):
        pltpu.make_async_copy(ys_ref.at[pl.ds(0, tb), :], buf.at[k], sem).wait()
    g = gate_ref[...]
    acc = h1_ref[...]
    for k in range(TOP_K):
        acc = acc + g[:, k:k + 1] * buf[k]
    o_ref[...] = acc


def _combine_call(pos_flat, h1, gates, ys):
    n, d = h1.shape
    tb = TB_ROWS
    return pl.pallas_call(
        _combine_kernel,
        grid=(n // tb,),
        in_specs=[pl.BlockSpec((tb * TOP_K,), lambda i: (i,), memory_space=pltpu.SMEM),
                  pl.BlockSpec((tb, d), lambda i: (i, 0)),
                  pl.BlockSpec((tb, TOP_K), lambda i: (i, 0)),
                  pl.BlockSpec(memory_space=pl.ANY)],
        out_specs=pl.BlockSpec((tb, d), lambda i: (i, 0)),
        out_shape=jax.ShapeDtypeStruct((n, d), F32),
        scratch_shapes=[pltpu.VMEM((TOP_K, tb, d), F32), pltpu.SemaphoreType.DMA],
        compiler_params=pltpu.CompilerParams(
            dimension_semantics=("arbitrary",), vmem_limit_bytes=VMEM_LIMIT),
        name="moe_combine",
    )(pos_flat, h1, gates, ys)


def _rotate_half_perm():
    half = QK_ROPE // 2
    perm = np.concatenate([np.arange(half, QK_ROPE), np.arange(0, half)])
    sign = np.concatenate([-np.ones(half), np.ones(half)]).astype(np.float32)
    return perm, sign


def _layer(h, l, norm_mix, w_in, conv_w, q_norm, w_uq, kv_norm, w_ukv, q_head_norm,
           k_head_norm, w_o, norm_ffn, w_router, b_router, w_gate_up, b_gate_up,
           w_down, b_down):
    b, s, d = h.shape
    n = b * s
    perm, sign = _rotate_half_perm()

    kpe_cols = w_in[l][:, OFF_KPE:OFF_KPE + QK_ROPE]
    win = jnp.concatenate([w_in[l], kpe_cols[:, perm] * sign], axis=1).astype(BF16)
    wq = w_uq[l].reshape(Q_LORA, MLA_HEADS, QK_HEAD)
    wq_rope = wq[:, :, QK_NOPE:]
    wuq = jnp.concatenate(
        [wq[:, :, :QK_NOPE].reshape(Q_LORA, MLA_HEADS * QK_NOPE),
         jnp.concatenate([wq_rope, wq_rope[:, :, perm] * sign], axis=2)
         .reshape(Q_LORA, MLA_HEADS * 2 * QK_ROPE)], axis=1).astype(BF16)
    gq, gk = q_head_norm[l], k_head_norm[l]
    gqn, gkn = gq[None, :QK_NOPE], gk[None, :QK_NOPE]
    gqr = jnp.concatenate([gq[QK_NOPE:], gq[QK_NOPE:][perm]])[None, :]
    gkr = jnp.concatenate([gk[QK_NOPE:], gk[QK_NOPE:][perm]])[None, :]

    half = QK_ROPE // 2
    inv_freq = ROPE_THETA ** (-jnp.arange(half, dtype=F32) / half)
    ang = jnp.arange(s).astype(F32)[:, None] * inv_freq[None, :]
    cos, sin = jnp.cos(ang), jnp.sin(ang)
    tbl = jnp.concatenate([cos, cos, sin, sin], axis=1)

    conv, q, k, v = _pre_call(
        h, norm_mix[l][None, :], win, conv_w[l], q_norm[l][None, :], wuq,
        kv_norm[l][None, :], w_ukv[l].astype(BF16), gqn, gqr, gkn, gkr, tbl)
    attn = _attn_call(q, k, v)

    tri = jnp.tril(jnp.ones((TM_POST, TM_POST), BF16), -1)
    h1, hp, idx, rank, gates, counts = _post_call(
        h.reshape(n, d), conv.reshape(n, CONV_WIDTH), attn.reshape(n, CONV_WIDTH),
        w_o[l].astype(BF16), norm_ffn[l][None, :], w_router[l].astype(BF16),
        b_router[l][None, :], tri)

    meta, starts = _expert_items(counts[0], n * TOP_K, TM_EXP)
    pos_flat = (starts[idx] + rank).reshape(n * TOP_K).astype(jnp.int32)
    xs = _scatter_call(pos_flat, hp)
    ys = _expert_call(meta, xs, w_gate_up[l], b_gate_up[l][:, None, :],
                      w_down[l], b_down[l][:, None, :])
    out = _combine_call(pos_flat, h1, gates, ys)
    return out.reshape(b, s, d)


def kernel(x, norm_mix, w_in, conv_w, q_norm, w_uq, kv_norm, w_ukv, q_head_norm, k_head_norm,
           w_o, norm_ffn, w_router, b_router, w_gate_up, b_gate_up, w_down, b_down):
    h = x
    for l in range(norm_mix.shape[0]):
        h = _layer(h, l, norm_mix, w_in, conv_w, q_norm, w_uq, kv_norm, w_ukv, q_head_norm,
                   k_head_norm, w_o, norm_ffn, w_router, b_router, w_gate_up, b_gate_up,
                   w_down, b_down)
    return h
```

```python
import functools
import math

import jax
import jax.numpy as jnp
import numpy as np
from jax import lax
from jax.experimental import pallas as pl
from jax.experimental.pallas import tpu as pltpu

F32 = jnp.float32
BF16 = jnp.bfloat16

D_MODEL = 1024
CONV_WIDTH = 512
MLA_HEADS = 4
QK_NOPE = 128
QK_ROPE = 64
QK_HEAD = QK_NOPE + QK_ROPE
QK_PAD = 256
V_HEAD = 128
Q_LORA = 384
KV_LORA = 256
ROPE_THETA = 10000.0
CHUNK = 64
N_EXPERTS = 32
TOP_K = 4
D_FF = 1024
SWIGLU_LIMIT = 7.0
SWIGLU_ALPHA = 1.702
EPS = 1e-6

OFF_B, OFF_C, OFF_X, OFF_CQ, OFF_CKV, OFF_KPE = 0, 512, 1024, 1536, 1920, 2176
IN_PROJ_EXT = 2304

VMEM_LIMIT = 56 * 1024 * 1024

TS_PRE = 512
TQ = 256
TM_POST = 512
TB_ROWS = 256
TM_EXP = 512
HALO = 8


def _rms(x, g):
    return x * lax.rsqrt(jnp.mean(x * x, axis=-1, keepdims=True) + EPS) * g


def _pre_kernel(x_ref, nm_ref, win_ref, convw_ref, qn_ref, wuq_ref, kvn_ref, wukv_ref,
                gqn_ref, gqr_ref, gkn_ref, gkr_ref, tbl_ref,
                conv_ref, q_ref, k_ref, vt_ref, ext_ref):
    s = pl.program_id(1)
    ts = x_ref.shape[1]
    hn = _rms(x_ref[0], nm_ref[...])
    proj = jnp.dot(hn.astype(BF16), win_ref[...], preferred_element_type=F32)

    @pl.when(s == 0)
    def _():
        ext_ref[0:HALO, :] = jnp.zeros((HALO, CONV_WIDTH), F32)

    @pl.when(s != 0)
    def _():
        ext_ref[0:HALO, :] = ext_ref[ts:ts + HALO, :]

    u = proj[:, OFF_C:OFF_C + CONV_WIDTH] * proj[:, OFF_X:OFF_X + CONV_WIDTH]
    ext_ref[HALO:HALO + ts, :] = u
    y = (convw_ref[2:3, :] * u
         + convw_ref[1:2, :] * ext_ref[HALO - 1:HALO - 1 + ts, :]
         + convw_ref[0:1, :] * ext_ref[HALO - 2:HALO - 2 + ts, :])
    conv_ref[0] = (proj[:, OFF_B:OFF_B + CONV_WIDTH] * y).astype(BF16)

    hq = _rms(proj[:, OFF_CQ:OFF_CQ + Q_LORA], qn_ref[...]).astype(BF16)
    qp = jnp.dot(hq, wuq_ref[...], preferred_element_type=F32)
    hkv = _rms(proj[:, OFF_CKV:OFF_CKV + KV_LORA], kvn_ref[...]).astype(BF16)
    kvp = jnp.dot(hkv, wukv_ref[...], preferred_element_type=F32)

    tbl = tbl_ref[...]
    lane = lax.broadcasted_iota(jnp.int32, (ts, 128), 1)
    first_half = lane < QK_ROPE

    def rope_block(grp, gain):
        r = grp * tbl * gain
        r = r + pltpu.roll(r, QK_ROPE, axis=1)
        return jnp.where(first_half, r, 0.0)

    kpe = proj[:, OFF_KPE:OFF_KPE + 128]
    ss_kpe = jnp.sum(jnp.where(first_half, kpe * kpe, 0.0), axis=-1, keepdims=True)
    k_rope = rope_block(kpe, gkr_ref[...])
    scale = 1.0 / math.sqrt(QK_HEAD)
    for h in range(MLA_HEADS):
        qn = qp[:, 128 * h:128 * (h + 1)]
        qg = qp[:, 512 + 128 * h:512 + 128 * (h + 1)]
        ss = (jnp.sum(qn * qn, axis=-1, keepdims=True)
              + jnp.sum(jnp.where(first_half, qg * qg, 0.0), axis=-1, keepdims=True))
        rs = lax.rsqrt(ss * (1.0 / QK_HEAD) + EPS) * scale
        q_ref[0, h, :, 0:128] = (qn * rs * gqn_ref[...]).astype(BF16)
        q_ref[0, h, :, 128:256] = (rope_block(qg, gqr_ref[...]) * rs).astype(BF16)

        kn = kvp[:, 256 * h:256 * h + 128]
        ssk = jnp.sum(kn * kn, axis=-1, keepdims=True) + ss_kpe
        rsk = lax.rsqrt(ssk * (1.0 / QK_HEAD) + EPS)
        k_ref[0, h, :, 0:128] = (kn * rsk * gkn_ref[...]).astype(BF16)
        k_ref[0, h, :, 128:256] = (k_rope * rsk).astype(BF16)
        v = kvp[:, 256 * h + 128:256 * (h + 1)]
        for c in range(ts // TQ):
            vt_ref[0, h, c] = v[c * TQ:(c + 1) * TQ, :].T.astype(BF16)


def _pre_call(x, nm, win, convw, qn, wuq, kvn, wukv, gqn, gqr, gkn, gkr, tbl):
    b, s, d = x.shape
    ts = TS_PRE
    const = lambda shape: pl.BlockSpec(shape, lambda bi, si: (0,) * len(shape))
    return pl.pallas_call(
        _pre_kernel,
        grid=(b, s // ts),
        in_specs=[
            pl.BlockSpec((1, ts, d), lambda bi, si: (bi, si, 0)),
            const((1, d)), const((d, IN_PROJ_EXT)), const((3, CONV_WIDTH)),
            const((1, Q_LORA)), const((Q_LORA, 1024)),
            const((1, KV_LORA)), const((KV_LORA, 1024)),
            const((1, 128)), const((1, 128)), const((1, 128)), const((1, 128)),
            pl.BlockSpec((ts, 128), lambda bi, si: (si, 0)),
        ],
        out_specs=[
            pl.BlockSpec((1, ts, CONV_WIDTH), lambda bi, si: (bi, si, 0)),
            pl.BlockSpec((1, MLA_HEADS, ts, QK_PAD), lambda bi, si: (bi, 0, si, 0)),
            pl.BlockSpec((1, MLA_HEADS, ts, QK_PAD), lambda bi, si: (bi, 0, si, 0)),
            pl.BlockSpec((1, MLA_HEADS, ts // TQ, V_HEAD, TQ), lambda bi, si: (bi, 0, si, 0, 0)),
        ],
        out_shape=[
            jax.ShapeDtypeStruct((b, s, CONV_WIDTH), BF16),
            jax.ShapeDtypeStruct((b, MLA_HEADS, s, QK_PAD), BF16),
            jax.ShapeDtypeStruct((b, MLA_HEADS, s, QK_PAD), BF16),
            jax.ShapeDtypeStruct((b, MLA_HEADS, s // TQ, V_HEAD, TQ), BF16),
        ],
        scratch_shapes=[pltpu.VMEM((ts + HALO, CONV_WIDTH), F32)],
        compiler_params=pltpu.CompilerParams(
            dimension_semantics=("arbitrary", "arbitrary"), vmem_limit_bytes=VMEM_LIMIT),
        name="pre_mixer",
    )(x, nm, win, convw, qn, wuq, kvn, wukv, gqn, gqr, gkn, gkr, tbl)


def _attn_kernel(q_ref, k_ref, vt_ref, o_ref, m_ref, l_ref, acc_ref):
    i = pl.program_id(1)
    tq = q_ref.shape[2]
    key_c = lax.broadcasted_iota(jnp.int32, (tq, tq), 0) // CHUNK
    qry_c = lax.broadcasted_iota(jnp.int32, (tq, tq), 1) // CHUNK
    allowed = key_c <= qry_c
    m_ref[...] = jnp.full(m_ref.shape, -1e30, F32)
    l_ref[...] = jnp.zeros(l_ref.shape, F32)
    acc_ref[...] = jnp.zeros(acc_ref.shape, F32)

    def key_block(j, masked):
        start = pl.multiple_of(j * tq, tq)
        scores = []
        for h in range(MLA_HEADS):
            kb = k_ref[0, h, pl.ds(start, tq), :]
            st = lax.dot_general(kb, q_ref[0, h], (((1,), (1,)), ((), ())),
                                 preferred_element_type=F32)
            scores.append(jnp.where(allowed, st, -1e30) if masked else st)
        probs, alphas = [], []
        for h in range(MLA_HEADS):
            m = m_ref[h]
            m_new = jnp.maximum(m, jnp.max(scores[h], axis=0, keepdims=True))
            alpha = jnp.exp(m - m_new)
            p = jnp.exp(scores[h] - m_new)
            l_ref[h] = alpha * l_ref[h] + jnp.sum(p, axis=0, keepdims=True)
            m_ref[h] = m_new
            probs.append(p.astype(BF16))
            alphas.append(alpha)
        for h in range(MLA_HEADS):
            acc_ref[h] = alphas[h] * acc_ref[h] + jnp.dot(vt_ref[0, h, j], probs[h],
                                                           preferred_element_type=F32)

    def body(j, c):
        key_block(j, False)
        return c

    lax.fori_loop(0, i, body, 0)
    key_block(i, True)
    for h in range(MLA_HEADS):
        o = acc_ref[h] / l_ref[h]
        o_ref[0, :, V_HEAD * h:V_HEAD * (h + 1)] = o.T.astype(BF16)


def _attn_call(q, k, vt):
    b, nh, s, _ = q.shape
    return pl.pallas_call(
        _attn_kernel,
        grid=(b, s // TQ),
        in_specs=[
            pl.BlockSpec((1, nh, TQ, QK_PAD), lambda bi, qi: (bi, 0, qi, 0)),
            pl.BlockSpec((1, nh, s, QK_PAD), lambda bi, qi: (bi, 0, 0, 0)),
            pl.BlockSpec((1, nh, s // TQ, V_HEAD, TQ), lambda bi, qi: (bi, 0, 0, 0, 0)),
        ],
        out_specs=pl.BlockSpec((1, TQ, nh * V_HEAD), lambda bi, qi: (bi, qi, 0)),
        out_shape=jax.ShapeDtypeStruct((b, s, nh * V_HEAD), BF16),
        scratch_shapes=[pltpu.VMEM((nh, 1, TQ), F32), pltpu.VMEM((nh, 1, TQ), F32),
                        pltpu.VMEM((nh, V_HEAD, TQ), F32)],
        compiler_params=pltpu.CompilerParams(
            dimension_semantics=("arbitrary", "arbitrary"), vmem_limit_bytes=VMEM_LIMIT),
        name="mla_attention",
    )(q, k, vt)


def _post_kernel(x_ref, conv_ref, attn_ref, wo_ref, g_ref, wr_ref, br_ref, tri_ref,
                 h1_ref, hp_ref, idx_ref, rank_ref, gate_ref, cnt_ref, carry_ref):
    step = pl.program_id(0)
    tm = x_ref.shape[0]

    @pl.when(step == 0)
    def _():
        carry_ref[...] = jnp.zeros_like(carry_ref)

    h1 = (x_ref[...]
          + jnp.dot(conv_ref[...], wo_ref[0:CONV_WIDTH, :], preferred_element_type=F32)
          + jnp.dot(attn_ref[...], wo_ref[CONV_WIDTH:, :], preferred_element_type=F32))
    h1_ref[...] = h1
    hb = _rms(h1, g_ref[...]).astype(BF16)
    half = D_MODEL // 2
    lo = lax.bitcast_convert_type(hb[:, :half].astype(F32), jnp.uint32) >> 16
    hi = lax.bitcast_convert_type(hb[:, half:].astype(F32), jnp.uint32) & jnp.uint32(0xFFFF0000)
    hp_ref[...] = lo | hi

    logits = jnp.dot(hb, wr_ref[...], preferred_element_type=F32) + br_ref[...]
    lane = lax.broadcasted_iota(jnp.int32, (tm, N_EXPERTS), 1).astype(F32)
    lane4 = lax.broadcasted_iota(jnp.int32, (tm, TOP_K), 1)
    member = jnp.zeros((tm, N_EXPERTS), F32)
    vals, idxs, hots = [], [], []
    lg = logits
    for _ in range(TOP_K):
        m = jnp.max(lg, axis=-1, keepdims=True)
        ix = jnp.min(jnp.where(lg == m, lane, float(N_EXPERTS)), axis=-1, keepdims=True)
        hot = lane == ix
        lg = jnp.where(hot, -jnp.inf, lg)
        member = member + hot.astype(F32)
        vals.append(m)
        idxs.append(ix)
        hots.append(hot)

    before = jnp.dot(tri_ref[...], member.astype(BF16), preferred_element_type=F32)
    before = before + carry_ref[...]
    ranks = [jnp.sum(jnp.where(hot, before, 0.0), axis=-1, keepdims=True) for hot in hots]
    carry_ref[...] = carry_ref[...] + jnp.sum(member, axis=0, keepdims=True)
    cnt_ref[...] = carry_ref[...].astype(jnp.int32)

    exps = [jnp.exp(v - vals[0]) for v in vals]
    denom = exps[0] + exps[1] + exps[2] + exps[3]

    def cols(c):
        return jnp.where(lane4 == 0, c[0], jnp.where(lane4 == 1, c[1],
                         jnp.where(lane4 == 2, c[2], c[3])))

    idx_ref[...] = cols(idxs).astype(jnp.int32)
    rank_ref[...] = cols(ranks).astype(jnp.int32)
    gate_ref[...] = cols([e / denom for e in exps])


def _post_call(xf, conv, attn, wo, g, wr, br, tri):
    n, d = xf.shape
    tm = TM_POST
    const = lambda shape: pl.BlockSpec(shape, lambda i: (0,) * len(shape))
    row = lambda w: pl.BlockSpec((tm, w), lambda i: (i, 0))
    return pl.pallas_call(
        _post_kernel,
        grid=(n // tm,),
        in_specs=[row(d), row(CONV_WIDTH), row(CONV_WIDTH), const((d, d)), const((1, d)),
                  const((d, N_EXPERTS)), const((1, N_EXPERTS)), const((tm, tm))],
        out_specs=[row(d), row(d // 2), row(TOP_K), row(TOP_K), row(TOP_K),
                   const((1, N_EXPERTS))],
        out_shape=[
            jax.ShapeDtypeStruct((n, d), F32),
            jax.ShapeDtypeStruct((n, d // 2), jnp.uint32),
            jax.ShapeDtypeStruct((n, TOP_K), jnp.int32),
            jax.ShapeDtypeStruct((n, TOP_K), jnp.int32),
            jax.ShapeDtypeStruct((n, TOP_K), F32),
            jax.ShapeDtypeStruct((1, N_EXPERTS), jnp.int32),
        ],
        scratch_shapes=[pltpu.VMEM((1, N_EXPERTS), F32)],
        compiler_params=pltpu.CompilerParams(
            dimension_semantics=("arbitrary",), vmem_limit_bytes=VMEM_LIMIT),
        name="post_mixer_router",
    )(xf, conv, attn, wo, g, wr, br, tri)


def _scatter_kernel(pos_ref, hp_ref, xs_ref, sem):
    tb = hp_ref.shape[0]

    def issue(t, c):
        for k in range(TOP_K):
            p = pos_ref[t * TOP_K + k]
            pltpu.make_async_copy(hp_ref.at[pl.ds(t, 1), :], xs_ref.at[pl.ds(p, 1), :],
                                  sem).start()
        return c

    lax.fori_loop(0, tb, issue, 0)
    for k in range(TOP_K):
        pltpu.make_async_copy(hp_ref, xs_ref.at[pl.ds(0, tb), :], sem).wait()


def _scatter_call(pos_flat, hp):
    n, w = hp.shape
    tb = TB_ROWS
    return pl.pallas_call(
        _scatter_kernel,
        grid=(n // tb,),
        in_specs=[pl.BlockSpec((tb * TOP_K,), lambda i: (i,), memory_space=pltpu.SMEM),
                  pl.BlockSpec((tb, w), lambda i: (i, 0))],
        out_specs=pl.BlockSpec(memory_space=pl.ANY),
        out_shape=jax.ShapeDtypeStruct((n * TOP_K, w), hp.dtype),
        scratch_shapes=[pltpu.SemaphoreType.DMA],
        compiler_params=pltpu.CompilerParams(
            dimension_semantics=("arbitrary",), vmem_limit_bytes=VMEM_LIMIT),
        name="dispatch_scatter",
    )(pos_flat, hp)


def _expert_kernel(tile_ref, exp_ref, lo_ref, hi_ref, flag_ref,
                   xs_ref, wgu_ref, bgu_ref, wd_ref, bd_ref, o_ref, wgu_bf, wd_bf):
    i = pl.program_id(0)
    tm = xs_ref.shape[0]
    lo, hi, flag = lo_ref[i], hi_ref[i], flag_ref[i]
    cast_rows = 128

    @pl.when((flag & 2) != 0)
    def _():
        def cast(r, c):
            rows = pl.ds(pl.multiple_of(r * cast_rows, cast_rows), cast_rows)
            wgu_bf[rows, :] = wgu_ref[0, rows, :].astype(BF16)
            wd_bf[rows, :] = wd_ref[0, rows, :].astype(BF16)
            return c
        lax.fori_loop(0, D_MODEL // cast_rows, cast, 0)

    @pl.when(hi > lo)
    def _():
        xp = xs_ref[...]
        half = D_MODEL // 2
        x_lo = lax.bitcast_convert_type(xp << 16, F32).astype(BF16)
        x_hi = lax.bitcast_convert_type(xp & jnp.uint32(0xFFFF0000), F32).astype(BF16)
        gu = (jnp.dot(x_lo, wgu_bf[0:half, :], preferred_element_type=F32)
              + jnp.dot(x_hi, wgu_bf[half:, :], preferred_element_type=F32)
              + bgu_ref[0])
        gate = jnp.minimum(gu[:, :D_FF], SWIGLU_LIMIT)
        up = jnp.clip(gu[:, D_FF:], -SWIGLU_LIMIT, SWIGLU_LIMIT)
        act = (up + 1.0) * (gate * (1.0 / (1.0 + jnp.exp(-SWIGLU_ALPHA * gate))))
        out = jnp.dot(act.astype(BF16), wd_bf[...], preferred_element_type=F32) + bd_ref[0]
        row = lax.broadcasted_iota(jnp.int32, (tm, 1), 0)
        mine = (row >= lo) & (row < hi)

        @pl.when((flag & 1) != 0)
        def _():
            o_ref[...] = jnp.where(mine, out, 0.0)

        @pl.when((flag & 1) == 0)
        def _():
            o_ref[...] = jnp.where(mine, out, o_ref[...])


def _expert_call(meta, xs, wgu, bgu, wd, bd):
    rows, w = xs.shape
    tm = TM_EXP
    n_items = meta[0].shape[0]
    return pl.pallas_call(
        _expert_kernel,
        grid_spec=pltpu.PrefetchScalarGridSpec(
            num_scalar_prefetch=5,
            grid=(n_items,),
            in_specs=[
                pl.BlockSpec((tm, w), lambda i, t, e, lo, hi, f: (t[i], 0)),
                pl.BlockSpec((1, D_MODEL, 2 * D_FF), lambda i, t, e, lo, hi, f: (e[i], 0, 0)),
                pl.BlockSpec((1, 1, 2 * D_FF), lambda i, t, e, lo, hi, f: (e[i], 0, 0)),
                pl.BlockSpec((1, D_FF, D_MODEL), lambda i, t, e, lo, hi, f: (e[i], 0, 0)),
                pl.BlockSpec((1, 1, D_MODEL), lambda i, t, e, lo, hi, f: (e[i], 0, 0)),
            ],
            out_specs=pl.BlockSpec((tm, D_MODEL), lambda i, t, e, lo, hi, f: (t[i], 0)),
            scratch_shapes=[pltpu.VMEM((D_MODEL, 2 * D_FF), BF16),
                            pltpu.VMEM((D_FF, D_MODEL), BF16)],
        ),
        out_shape=jax.ShapeDtypeStruct((rows, D_MODEL), F32),
        compiler_params=pltpu.CompilerParams(
            dimension_semantics=("arbitrary",), vmem_limit_bytes=VMEM_LIMIT),
        name="moe_experts",
    )(*meta, xs, wgu, bgu, wd, bd)


def _expert_items(counts, rows, tm):
    n_tiles = rows // tm
    n_items = n_tiles + N_EXPERTS - 1
    ends = jnp.cumsum(counts)
    starts = ends - counts
    first_tile = starts // tm
    last_tile = jnp.maximum(ends - 1, 0) // tm
    per_exp = jnp.where(counts > 0, last_tile - first_tile + 1, 0)
    item_end = jnp.cumsum(per_exp)
    item_start = item_end - per_exp
    total = item_end[-1]
    i = jnp.arange(n_items, dtype=jnp.int32)
    valid = i < total
    e = jnp.sum((item_end[None, :] <= jnp.minimum(i, total - 1)[:, None]).astype(jnp.int32), axis=1)
    hot = e[:, None] == jnp.arange(N_EXPERTS, dtype=jnp.int32)[None, :]
    pick = lambda tab: jnp.sum(jnp.where(hot, tab[None, :], 0), axis=1)
    e_start, e_end, e_item0 = pick(starts), pick(ends), pick(item_start)
    tile = jnp.where(valid, e_start // tm + (i - e_item0), n_tiles - 1)
    lo = jnp.where(valid, jnp.maximum(e_start, tile * tm) - tile * tm, 0)
    hi = jnp.where(valid, jnp.minimum(e_end, (tile + 1) * tm) - tile * tm, 0)
    flag = jnp.where(valid, (lo == 0).astype(jnp.int32) + 2 * (i == e_item0), 0)
    cast = lambda a: a.astype(jnp.int32)
    return (cast(tile), cast(e), cast(lo), cast(hi), cast(flag)), starts


def _combine_kernel(pos_ref, h1_ref, gate_ref, ys_ref, o_ref, buf, sem):
    tb = h1_ref.shape[0]

    def issue(t, c):
        for k in range(TOP_K):
            p = pos_ref[t * TOP_K + k]
            pltpu.make_async_copy(ys_ref.at[pl.ds(p, 1), :], buf.at[k, pl.ds(t, 1), :],
                                  sem).start()
        return c

    lax.fori_loop(0, tb, issue, 0)
    for k in range(TOP_K):
        pltpu.make_async_copy(ys_ref.at[pl.ds(0, tb), :], buf.at[k], sem).wait()
    g = gate_ref[...]
    acc = h1_ref[...]
    for k in range(TOP_K):
        acc = acc + g[:, k:k + 1] * buf[k]
    o_ref[...] = acc


def _combine_call(pos_flat, h1, gates, ys):
    n, d = h1.shape
    tb = TB_ROWS
    return pl.pallas_call(
        _combine_kernel,
        grid=(n // tb,),
        in_specs=[pl.BlockSpec((tb * TOP_K,), lambda i: (i,), memory_space=pltpu.SMEM),
                  pl.BlockSpec((tb, d), lambda i: (i, 0)),
                  pl.BlockSpec((tb, TOP_K), lambda i: (i, 0)),
                  pl.BlockSpec(memory_space=pl.ANY)],
        out_specs=pl.BlockSpec((tb, d), lambda i: (i, 0)),
        out_shape=jax.ShapeDtypeStruct((n, d), F32),
        scratch_shapes=[pltpu.VMEM((TOP_K, tb, d), F32), pltpu.SemaphoreType.DMA],
        compiler_params=pltpu.CompilerParams(
            dimension_semantics=("arbitrary",), vmem_limit_bytes=VMEM_LIMIT),
        name="moe_combine",
    )(pos_flat, h1, gates, ys)


def _rotate_half_perm():
    half = QK_ROPE // 2
    perm = np.concatenate([np.arange(half, QK_ROPE), np.arange(0, half)])
    sign = np.concatenate([-np.ones(half), np.ones(half)]).astype(np.float32)
    return perm, sign


def _layer(h, l, norm_mix, w_in, conv_w, q_norm, w_uq, kv_norm, w_ukv, q_head_norm,
           k_head_norm, w_o, norm_ffn, w_router, b_router, w_gate_up, b_gate_up,
           w_down, b_down):
    b, s, d = h.shape
    n = b * s
    perm, sign = _rotate_half_perm()

    kpe_cols = w_in[l][:, OFF_KPE:OFF_KPE + QK_ROPE]
    win = jnp.concatenate([w_in[l], kpe_cols[:, perm] * sign], axis=1).astype(BF16)
    wq = w_uq[l].reshape(Q_LORA, MLA_HEADS, QK_HEAD)
    wq_rope = wq[:, :, QK_NOPE:]
    wuq = jnp.concatenate(
        [wq[:, :, :QK_NOPE].reshape(Q_LORA, MLA_HEADS * QK_NOPE),
         jnp.concatenate([wq_rope, wq_rope[:, :, perm] * sign], axis=2)
         .reshape(Q_LORA, MLA_HEADS * 2 * QK_ROPE)], axis=1).astype(BF16)
    gq, gk = q_head_norm[l], k_head_norm[l]
    gqn, gkn = gq[None, :QK_NOPE], gk[None, :QK_NOPE]
    gqr = jnp.concatenate([gq[QK_NOPE:], gq[QK_NOPE:][perm]])[None, :]
    gkr = jnp.concatenate([gk[QK_NOPE:], gk[QK_NOPE:][perm]])[None, :]

    half = QK_ROPE // 2
    inv_freq = ROPE_THETA ** (-jnp.arange(half, dtype=F32) / half)
    ang = jnp.arange(s).astype(F32)[:, None] * inv_freq[None, :]
    cos, sin = jnp.cos(ang), jnp.sin(ang)
    tbl = jnp.concatenate([cos, cos, sin, sin], axis=1)

    conv, q, k, v = _pre_call(
        h, norm_mix[l][None, :], win, conv_w[l], q_norm[l][None, :], wuq,
        kv_norm[l][None, :], w_ukv[l].astype(BF16), gqn, gqr, gkn, gkr, tbl)
    attn = _attn_call(q, k, v)

    tri = jnp.tril(jnp.ones((TM_POST, TM_POST), BF16), -1)
    h1, hp, idx, rank, gates, counts = _post_call(
        h.reshape(n, d), conv.reshape(n, CONV_WIDTH), attn.reshape(n, CONV_WIDTH),
        w_o[l].astype(BF16), norm_ffn[l][None, :], w_router[l].astype(BF16),
        b_router[l][None, :], tri)

    meta, starts = _expert_items(counts[0], n * TOP_K, TM_EXP)
    idx_hot = idx[:, :, None] == jnp.arange(N_EXPERTS, dtype=jnp.int32)[None, None, :]
    group_start = jnp.sum(jnp.where(idx_hot, starts[None, None, :], 0), axis=-1)
    pos_flat = (group_start + rank).reshape(n * TOP_K).astype(jnp.int32)
    xs = _scatter_call(pos_flat, hp)
    ys = _expert_call(meta, xs, w_gate_up[l], b_gate_up[l][:, None, :],
                      w_down[l], b_down[l][:, None, :])
    out = _combine_call(pos_flat, h1, gates, ys)
    return out.reshape(b, s, d)


def kernel(x, norm_mix, w_in, conv_w, q_norm, w_uq, kv_norm, w_ukv, q_head_norm, k_head_norm,
           w_o, norm_ffn, w_router, b_router, w_gate_up, b_gate_up, w_down, b_down):
    h = x
    for l in range(norm_mix.shape[0]):
        h = _layer(h, l, norm_mix, w_in, conv_w, q_norm, w_uq, kv_norm, w_ukv, q_head_norm,
                   k_head_norm, w_o, norm_ffn, w_router, b_router, w_gate_up, b_gate_up,
                   w_down, b_down)
    return h
```

```python
import functools
import math

import jax
import jax.numpy as jnp
import numpy as np
from jax import lax
from jax.experimental import pallas as pl
from jax.experimental.pallas import tpu as pltpu

F32 = jnp.float32
BF16 = jnp.bfloat16

D_MODEL = 1024
CONV_WIDTH = 512
MLA_HEADS = 4
QK_NOPE = 128
QK_ROPE = 64
QK_HEAD = QK_NOPE + QK_ROPE
QK_PAD = 256
V_HEAD = 128
Q_LORA = 384
KV_LORA = 256
ROPE_THETA = 10000.0
CHUNK = 64
N_EXPERTS = 32
TOP_K = 4
D_FF = 1024
SWIGLU_LIMIT = 7.0
SWIGLU_ALPHA = 1.702
EPS = 1e-6

OFF_B, OFF_C, OFF_X, OFF_CQ, OFF_CKV, OFF_KPE = 0, 512, 1024, 1536, 1920, 2176
IN_PROJ_EXT = 2304

VMEM_LIMIT = 56 * 1024 * 1024

TS_PRE = 512
TQ = 256
TM_POST = 512
TB_ROWS = 256
TM_EXP = 512
HALO = 8


def _rms(x, g):
    return x * lax.rsqrt(jnp.mean(x * x, axis=-1, keepdims=True) + EPS) * g


def _pre_kernel(x_ref, nm_ref, win_ref, convw_ref, qn_ref, wuq_ref, kvn_ref, wukv_ref,
                gqn_ref, gqr_ref, gkn_ref, gkr_ref, tbl_ref,
                conv_ref, q_ref, k_ref, vt_ref, ext_ref):
    s = pl.program_id(1)
    ts = x_ref.shape[1]
    hn = _rms(x_ref[0], nm_ref[...])
    proj = jnp.dot(hn.astype(BF16), win_ref[...], preferred_element_type=F32)

    @pl.when(s == 0)
    def _():
        ext_ref[0:HALO, :] = jnp.zeros((HALO, CONV_WIDTH), F32)

    @pl.when(s != 0)
    def _():
        ext_ref[0:HALO, :] = ext_ref[ts:ts + HALO, :]

    u = proj[:, OFF_C:OFF_C + CONV_WIDTH] * proj[:, OFF_X:OFF_X + CONV_WIDTH]
    ext_ref[HALO:HALO + ts, :] = u
    y = (convw_ref[2:3, :] * u
         + convw_ref[1:2, :] * ext_ref[HALO - 1:HALO - 1 + ts, :]
         + convw_ref[0:1, :] * ext_ref[HALO - 2:HALO - 2 + ts, :])
    conv_ref[0] = (proj[:, OFF_B:OFF_B + CONV_WIDTH] * y).astype(BF16)

    hq = _rms(proj[:, OFF_CQ:OFF_CQ + Q_LORA], qn_ref[...]).astype(BF16)
    qp = jnp.dot(hq, wuq_ref[...], preferred_element_type=F32)
    hkv = _rms(proj[:, OFF_CKV:OFF_CKV + KV_LORA], kvn_ref[...]).astype(BF16)
    kvp = jnp.dot(hkv, wukv_ref[...], preferred_element_type=F32)

    tbl = tbl_ref[...]
    lane = lax.broadcasted_iota(jnp.int32, (ts, 128), 1)
    first_half = lane < QK_ROPE

    def rope_block(grp, gain):
        r = grp * tbl * gain
        r = r + pltpu.roll(r, QK_ROPE, axis=1)
        return jnp.where(first_half, r, 0.0)

    kpe = proj[:, OFF_KPE:OFF_KPE + 128]
    ss_kpe = jnp.sum(jnp.where(first_half, kpe * kpe, 0.0), axis=-1, keepdims=True)
    k_rope = rope_block(kpe, gkr_ref[...])
    scale = 1.0 / math.sqrt(QK_HEAD)
    for h in range(MLA_HEADS):
        qn = qp[:, 128 * h:128 * (h + 1)]
        qg = qp[:, 512 + 128 * h:512 + 128 * (h + 1)]
        ss = (jnp.sum(qn * qn, axis=-1, keepdims=True)
              + jnp.sum(jnp.where(first_half, qg * qg, 0.0), axis=-1, keepdims=True))
        rs = lax.rsqrt(ss * (1.0 / QK_HEAD) + EPS) * scale
        q_ref[0, h, :, 0:128] = (qn * rs * gqn_ref[...]).astype(BF16)
        q_ref[0, h, :, 128:256] = (rope_block(qg, gqr_ref[...]) * rs).astype(BF16)

        kn = kvp[:, 256 * h:256 * h + 128]
        ssk = jnp.sum(kn * kn, axis=-1, keepdims=True) + ss_kpe
        rsk = lax.rsqrt(ssk * (1.0 / QK_HEAD) + EPS)
        k_ref[0, h, :, 0:128] = (kn * rsk * gkn_ref[...]).astype(BF16)
        k_ref[0, h, :, 128:256] = (k_rope * rsk).astype(BF16)
        v = kvp[:, 256 * h + 128:256 * (h + 1)]
        for c in range(ts // TQ):
            vt_ref[0, h, c] = v[c * TQ:(c + 1) * TQ, :].T.astype(BF16)


def _pre_call(x, nm, win, convw, qn, wuq, kvn, wukv, gqn, gqr, gkn, gkr, tbl):
    b, s, d = x.shape
    ts = TS_PRE
    const = lambda shape: pl.BlockSpec(shape, lambda bi, si: (0,) * len(shape))
    return pl.pallas_call(
        _pre_kernel,
        grid=(b, s // ts),
        in_specs=[
            pl.BlockSpec((1, ts, d), lambda bi, si: (bi, si, 0)),
            const((1, d)), const((d, IN_PROJ_EXT)), const((3, CONV_WIDTH)),
            const((1, Q_LORA)), const((Q_LORA, 1024)),
            const((1, KV_LORA)), const((KV_LORA, 1024)),
            const((1, 128)), const((1, 128)), const((1, 128)), const((1, 128)),
            pl.BlockSpec((ts, 128), lambda bi, si: (si, 0)),
        ],
        out_specs=[
            pl.BlockSpec((1, ts, CONV_WIDTH), lambda bi, si: (bi, si, 0)),
            pl.BlockSpec((1, MLA_HEADS, ts, QK_PAD), lambda bi, si: (bi, 0, si, 0)),
            pl.BlockSpec((1, MLA_HEADS, ts, QK_PAD), lambda bi, si: (bi, 0, si, 0)),
            pl.BlockSpec((1, MLA_HEADS, ts // TQ, V_HEAD, TQ), lambda bi, si: (bi, 0, si, 0, 0)),
        ],
        out_shape=[
            jax.ShapeDtypeStruct((b, s, CONV_WIDTH), BF16),
            jax.ShapeDtypeStruct((b, MLA_HEADS, s, QK_PAD), BF16),
            jax.ShapeDtypeStruct((b, MLA_HEADS, s, QK_PAD), BF16),
            jax.ShapeDtypeStruct((b, MLA_HEADS, s // TQ, V_HEAD, TQ), BF16),
        ],
        scratch_shapes=[pltpu.VMEM((ts + HALO, CONV_WIDTH), F32)],
        compiler_params=pltpu.CompilerParams(
            dimension_semantics=("arbitrary", "arbitrary"), vmem_limit_bytes=VMEM_LIMIT),
        name="pre_mixer",
    )(x, nm, win, convw, qn, wuq, kvn, wukv, gqn, gqr, gkn, gkr, tbl)


def _attn_kernel(q_ref, k_ref, vt_ref, o_ref, m_ref, l_ref, acc_ref):
    i = pl.program_id(1)
    tq = q_ref.shape[2]
    key_c = lax.broadcasted_iota(jnp.int32, (tq, tq), 0) // CHUNK
    qry_c = lax.broadcasted_iota(jnp.int32, (tq, tq), 1) // CHUNK
    allowed = key_c <= qry_c
    m_ref[...] = jnp.full(m_ref.shape, -1e30, F32)
    l_ref[...] = jnp.zeros(l_ref.shape, F32)
    acc_ref[...] = jnp.zeros(acc_ref.shape, F32)

    def key_block(j, masked):
        start = pl.multiple_of(j * tq, tq)
        scores = []
        for h in range(MLA_HEADS):
            kb = k_ref[0, h, pl.ds(start, tq), :]
            st = lax.dot_general(kb, q_ref[0, h], (((1,), (1,)), ((), ())),
                                 preferred_element_type=F32)
            scores.append(jnp.where(allowed, st, -1e30) if masked else st)
        probs, alphas = [], []
        for h in range(MLA_HEADS):
            m = m_ref[h]
            m_new = jnp.maximum(m, jnp.max(scores[h], axis=0, keepdims=True))
            alpha = jnp.exp(m - m_new)
            p = jnp.exp(scores[h] - m_new)
            l_ref[h] = alpha * l_ref[h] + jnp.sum(p, axis=0, keepdims=True)
            m_ref[h] = m_new
            probs.append(p.astype(BF16))
            alphas.append(alpha)
        for h in range(MLA_HEADS):
            acc_ref[h] = alphas[h] * acc_ref[h] + jnp.dot(vt_ref[0, h, j], probs[h],
                                                           preferred_element_type=F32)

    def body(j, c):
        key_block(j, False)
        return c

    lax.fori_loop(0, i, body, 0)
    key_block(i, True)
    for h in range(MLA_HEADS):
        o = acc_ref[h] / l_ref[h]
        o_ref[0, :, V_HEAD * h:V_HEAD * (h + 1)] = o.T.astype(BF16)


def _attn_call(q, k, vt):
    b, nh, s, _ = q.shape
    return pl.pallas_call(
        _attn_kernel,
        grid=(b, s // TQ),
        in_specs=[
            pl.BlockSpec((1, nh, TQ, QK_PAD), lambda bi, qi: (bi, 0, qi, 0)),
            pl.BlockSpec((1, nh, s, QK_PAD), lambda bi, qi: (bi, 0, 0, 0)),
            pl.BlockSpec((1, nh, s // TQ, V_HEAD, TQ), lambda bi, qi: (bi, 0, 0, 0, 0)),
        ],
        out_specs=pl.BlockSpec((1, TQ, nh * V_HEAD), lambda bi, qi: (bi, qi, 0)),
        out_shape=jax.ShapeDtypeStruct((b, s, nh * V_HEAD), BF16),
        scratch_shapes=[pltpu.VMEM((nh, 1, TQ), F32), pltpu.VMEM((nh, 1, TQ), F32),
                        pltpu.VMEM((nh, V_HEAD, TQ), F32)],
        compiler_params=pltpu.CompilerParams(
            dimension_semantics=("arbitrary", "arbitrary"), vmem_limit_bytes=VMEM_LIMIT),
        name="mla_attention",
    )(q, k, vt)


def _post_kernel(x_ref, conv_ref, attn_ref, wo_ref, g_ref, wr_ref, br_ref, tri_ref,
                 h1_ref, hp_ref, idx_ref, rank_ref, gate_ref, cnt_ref, carry_ref):
    step = pl.program_id(0)
    tm = x_ref.shape[0]

    @pl.when(step == 0)
    def _():
        carry_ref[...] = jnp.zeros_like(carry_ref)

    h1 = (x_ref[...]
          + jnp.dot(conv_ref[...], wo_ref[0:CONV_WIDTH, :], preferred_element_type=F32)
          + jnp.dot(attn_ref[...], wo_ref[CONV_WIDTH:, :], preferred_element_type=F32))
    h1_ref[...] = h1
    hb = _rms(h1, g_ref[...]).astype(BF16)
    half = D_MODEL // 2
    lo = lax.bitcast_convert_type(hb[:, :half].astype(F32), jnp.uint32) >> 16
    hi = lax.bitcast_convert_type(hb[:, half:].astype(F32), jnp.uint32) & jnp.uint32(0xFFFF0000)
    hp_ref[...] = lo | hi

    logits = jnp.dot(hb, wr_ref[...], preferred_element_type=F32) + br_ref[...]
    lane = lax.broadcasted_iota(jnp.int32, (tm, N_EXPERTS), 1).astype(F32)
    lane4 = lax.broadcasted_iota(jnp.int32, (tm, TOP_K), 1)
    member = jnp.zeros((tm, N_EXPERTS), F32)
    vals, idxs, hots = [], [], []
    lg = logits
    for _ in range(TOP_K):
        m = jnp.max(lg, axis=-1, keepdims=True)
        ix = jnp.min(jnp.where(lg == m, lane, float(N_EXPERTS)), axis=-1, keepdims=True)
        hot = lane == ix
        lg = jnp.where(hot, -jnp.inf, lg)
        member = member + hot.astype(F32)
        vals.append(m)
        idxs.append(ix)
        hots.append(hot)

    before = jnp.dot(tri_ref[...], member.astype(BF16), preferred_element_type=F32)
    before = before + carry_ref[...]
    ranks = [jnp.sum(jnp.where(hot, before, 0.0), axis=-1, keepdims=True) for hot in hots]
    carry_ref[...] = carry_ref[...] + jnp.sum(member, axis=0, keepdims=True)
    cnt_ref[...] = carry_ref[...].astype(jnp.int32)

    exps = [jnp.exp(v - vals[0]) for v in vals]
    denom = exps[0] + exps[1] + exps[2] + exps[3]

    def cols(c):
        return jnp.where(lane4 == 0, c[0], jnp.where(lane4 == 1, c[1],
                         jnp.where(lane4 == 2, c[2], c[3])))

    idx_ref[...] = cols(idxs).astype(jnp.int32)
    rank_ref[...] = cols(ranks).astype(jnp.int32)
    gate_ref[...] = cols([e / denom for e in exps])


def _post_call(xf, conv, attn, wo, g, wr, br, tri):
    n, d = xf.shape
    tm = TM_POST
    const = lambda shape: pl.BlockSpec(shape, lambda i: (0,) * len(shape))
    row = lambda w: pl.BlockSpec((tm, w), lambda i: (i, 0))
    return pl.pallas_call(
        _post_kernel,
        grid=(n // tm,),
        in_specs=[row(d), row(CONV_WIDTH), row(CONV_WIDTH), const((d, d)), const((1, d)),
                  const((d, N_EXPERTS)), const((1, N_EXPERTS)), const((tm, tm))],
        out_specs=[row(d), row(d // 2), row(TOP_K), row(TOP_K), row(TOP_K),
                   const((1, N_EXPERTS))],
        out_shape=[
            jax.ShapeDtypeStruct((n, d), F32),
            jax.ShapeDtypeStruct((n, d // 2), jnp.uint32),
            jax.ShapeDtypeStruct((n, TOP_K), jnp.int32),
            jax.ShapeDtypeStruct((n, TOP_K), jnp.int32),
            jax.ShapeDtypeStruct((n, TOP_K), F32),
            jax.ShapeDtypeStruct((1, N_EXPERTS), jnp.int32),
        ],
        scratch_shapes=[pltpu.VMEM((1, N_EXPERTS), F32)],
        compiler_params=pltpu.CompilerParams(
            dimension_semantics=("arbitrary",), vmem_limit_bytes=VMEM_LIMIT),
        name="post_mixer_router",
    )(xf, conv, attn, wo, g, wr, br, tri)


def _expert_kernel(exp_ref, flag_ref, total_ref,
                   src_cur_ref, src_next_ref, dst_ref, hp_ref,
                   wgu_ref, bgu_ref, wd_ref, bd_ref, y_ref,
                   xbuf, obuf, wgu_bf, wd_bf, gsem, ssem):
    i = pl.program_id(0)
    n = pl.num_programs(0)
    tm = xbuf.shape[1]
    total = total_ref[0]
    slot = i % 2
    cast_rows = 128

    def start_gather(src_ref, s):
        def issue(r, c):
            tok = src_ref[0, 0, r]
            pltpu.make_async_copy(hp_ref.at[pl.ds(tok, 1), :], xbuf.at[s, pl.ds(r, 1), :],
                                  gsem.at[s]).start()
            return c
        lax.fori_loop(0, tm, issue, 0, unroll=8)

    def wait_gather(s):
        pltpu.make_async_copy(hp_ref.at[pl.ds(0, tm), :], xbuf.at[s], gsem.at[s]).wait()

    def wait_scatter(s):
        pltpu.make_async_copy(obuf.at[s], y_ref.at[pl.ds(0, tm), :], ssem.at[s]).wait()

    @pl.when(i == 0)
    def _():
        start_gather(src_cur_ref, 0)
        spare0 = y_ref.shape[0] - 2 * tm
        obuf[...] = jnp.zeros(obuf.shape, F32)
        fills = [pltpu.make_async_copy(obuf.at[s], y_ref.at[pl.ds(spare0 + s * tm, tm), :],
                                       ssem.at[s]) for s in range(2)]
        for f in fills:
            f.start()
        for f in fills:
            f.wait()

    @pl.when(i < total)
    def _():
        wait_gather(slot)

        @pl.when(i + 1 < total)
        def _():
            start_gather(src_next_ref, 1 - slot)

        @pl.when(flag_ref[i] != 0)
        def _():
            def cast(r, c):
                rows = pl.ds(pl.multiple_of(r * cast_rows, cast_rows), cast_rows)
                wgu_bf[rows, :] = wgu_ref[0, rows, :].astype(BF16)
                wd_bf[rows, :] = wd_ref[0, rows, :].astype(BF16)
                return c
            lax.fori_loop(0, D_MODEL // cast_rows, cast, 0)

        xp = xbuf[slot]
        half = D_MODEL // 2
        x_lo = lax.bitcast_convert_type(xp << 16, F32).astype(BF16)
        x_hi = lax.bitcast_convert_type(xp & jnp.uint32(0xFFFF0000), F32).astype(BF16)
        gu = (jnp.dot(x_lo, wgu_bf[0:half, :], preferred_element_type=F32)
              + jnp.dot(x_hi, wgu_bf[half:, :], preferred_element_type=F32)
              + bgu_ref[0])
        gate = jnp.minimum(gu[:, :D_FF], SWIGLU_LIMIT)
        up = jnp.clip(gu[:, D_FF:], -SWIGLU_LIMIT, SWIGLU_LIMIT)
        act = (up + 1.0) * (gate * (1.0 / (1.0 + jnp.exp(-SWIGLU_ALPHA * gate))))
        out = jnp.dot(act.astype(BF16), wd_bf[...], preferred_element_type=F32) + bd_ref[0]

        @pl.when(i >= 2)
        def _():
            wait_scatter(slot)

        obuf[slot] = out

        def issue(r, c):
            d = dst_ref[0, 0, r]
            pltpu.make_async_copy(obuf.at[slot, pl.ds(r, 1), :], y_ref.at[pl.ds(d, 1), :],
                                  ssem.at[slot]).start()
            return c
        lax.fori_loop(0, tm, issue, 0, unroll=8)

    @pl.when(i == n - 1)
    def _():
        wait_scatter(0)
        wait_scatter(1)


def _expert_call(exp, flag, total, src, dst, hp, wgu, bgu, wd, bd, y_rows):
    n_items, _, tm = src.shape
    w = hp.shape[1]
    smem_row = lambda fn: pl.BlockSpec((1, 1, tm), fn, memory_space=pltpu.SMEM)
    by_exp = lambda shape: pl.BlockSpec(shape, lambda i, e, f, t: (e[i], 0, 0))
    return pl.pallas_call(
        _expert_kernel,
        grid_spec=pltpu.PrefetchScalarGridSpec(
            num_scalar_prefetch=3,
            grid=(n_items,),
            in_specs=[
                smem_row(lambda i, e, f, t: (i, 0, 0)),
                smem_row(lambda i, e, f, t: (jnp.minimum(i + 1, n_items - 1), 0, 0)),
                smem_row(lambda i, e, f, t: (i, 0, 0)),
                pl.BlockSpec(memory_space=pl.ANY),
                by_exp((1, D_MODEL, 2 * D_FF)), by_exp((1, 1, 2 * D_FF)),
                by_exp((1, D_FF, D_MODEL)), by_exp((1, 1, D_MODEL)),
            ],
            out_specs=pl.BlockSpec(memory_space=pl.ANY),
            scratch_shapes=[pltpu.VMEM((2, tm, w), jnp.uint32),
                            pltpu.VMEM((2, tm, D_MODEL), F32),
                            pltpu.VMEM((D_MODEL, 2 * D_FF), BF16),
                            pltpu.VMEM((D_FF, D_MODEL), BF16),
                            pltpu.SemaphoreType.DMA((2,)),
                            pltpu.SemaphoreType.DMA((2,))],
        ),
        out_shape=jax.ShapeDtypeStruct((y_rows, D_MODEL), F32),
        compiler_params=pltpu.CompilerParams(
            dimension_semantics=("arbitrary",), vmem_limit_bytes=VMEM_LIMIT),
        name="moe_experts",
    )(exp, flag, total, src, src, dst, hp, wgu, bgu, wd, bd)


def _expert_items(counts, idx, rank, n_tok, tm):
    n_pairs = n_tok * TOP_K
    n_items = n_pairs // tm + N_EXPERTS
    per_exp = (counts + tm - 1) // tm
    item_end = jnp.cumsum(per_exp)
    item_start = item_end - per_exp
    total = item_end[-1]
    i = jnp.arange(n_items, dtype=jnp.int32)
    experts = jnp.arange(N_EXPERTS, dtype=jnp.int32)
    e = jnp.sum((item_end[None, :] <= jnp.minimum(i, total - 1)[:, None]).astype(jnp.int32), axis=1)
    e_item0 = jnp.sum(jnp.where(e[:, None] == experts[None, :], item_start[None, :], 0), axis=1)
    flag = ((i == e_item0) & (i < total)).astype(jnp.int32)

    base = jnp.sum(jnp.where(idx[:, :, None] == experts[None, None, :],
                             (item_start * tm)[None, None, :], 0), axis=-1)
    slot_of_pair = (base + rank).reshape(n_pairs)
    pair_of_slot = jnp.full((n_items * tm,), -1, jnp.int32).at[slot_of_pair].set(
        jnp.arange(n_pairs, dtype=jnp.int32), unique_indices=True).reshape(n_items, tm)
    used = pair_of_slot >= 0
    spare = n_pairs + (i % 2)[:, None] * tm + jnp.arange(tm, dtype=jnp.int32)[None, :]
    src = jnp.where(used, pair_of_slot >> 2, 0)
    dst = jnp.where(used, (pair_of_slot & 3) * n_tok + (pair_of_slot >> 2), spare)
    cast = lambda a: a.astype(jnp.int32)
    return (cast(e), flag, cast(total).reshape(1),
            cast(src).reshape(n_items, 1, tm), cast(dst).reshape(n_items, 1, tm))


def _combine_kernel(h1_ref, gate_ref, y0_ref, y1_ref, y2_ref, y3_ref, o_ref):
    g = gate_ref[...]
    acc = h1_ref[...]
    for k, y_ref in enumerate((y0_ref, y1_ref, y2_ref, y3_ref)):
        acc = acc + g[:, k:k + 1] * y_ref[...]
    o_ref[...] = acc


def _combine_call(h1, gates, y):
    n, d = h1.shape
    tb = TB_ROWS
    blocks_per_slot = n // tb
    y_spec = lambda k: pl.BlockSpec((tb, d), lambda i: (k * blocks_per_slot + i, 0))
    return pl.pallas_call(
        _combine_kernel,
        grid=(n // tb,),
        in_specs=[pl.BlockSpec((tb, d), lambda i: (i, 0)),
                  pl.BlockSpec((tb, TOP_K), lambda i: (i, 0)),
                  y_spec(0), y_spec(1), y_spec(2), y_spec(3)],
        out_specs=pl.BlockSpec((tb, d), lambda i: (i, 0)),
        out_shape=jax.ShapeDtypeStruct((n, d), F32),
        compiler_params=pltpu.CompilerParams(
            dimension_semantics=("arbitrary",), vmem_limit_bytes=VMEM_LIMIT),
        name="moe_combine",
    )(h1, gates, y, y, y, y)


def _rotate_half_perm():
    half = QK_ROPE // 2
    perm = np.concatenate([np.arange(half, QK_ROPE), np.arange(0, half)])
    sign = np.concatenate([-np.ones(half), np.ones(half)]).astype(np.float32)
    return perm, sign


def _layer(h, l, norm_mix, w_in, conv_w, q_norm, w_uq, kv_norm, w_ukv, q_head_norm,
           k_head_norm, w_o, norm_ffn, w_router, b_router, w_gate_up, b_gate_up,
           w_down, b_down):
    b, s, d = h.shape
    n = b * s
    perm, sign = _rotate_half_perm()

    kpe_cols = w_in[l][:, OFF_KPE:OFF_KPE + QK_ROPE]
    win = jnp.concatenate([w_in[l], kpe_cols[:, perm] * sign], axis=1).astype(BF16)
    wq = w_uq[l].reshape(Q_LORA, MLA_HEADS, QK_HEAD)
    wq_rope = wq[:, :, QK_NOPE:]
    wuq = jnp.concatenate(
        [wq[:, :, :QK_NOPE].reshape(Q_LORA, MLA_HEADS * QK_NOPE),
         jnp.concatenate([wq_rope, wq_rope[:, :, perm] * sign], axis=2)
         .reshape(Q_LORA, MLA_HEADS * 2 * QK_ROPE)], axis=1).astype(BF16)
    gq, gk = q_head_norm[l], k_head_norm[l]
    gqn, gkn = gq[None, :QK_NOPE], gk[None, :QK_NOPE]
    gqr = jnp.concatenate([gq[QK_NOPE:], gq[QK_NOPE:][perm]])[None, :]
    gkr = jnp.concatenate([gk[QK_NOPE:], gk[QK_NOPE:][perm]])[None, :]

    half = QK_ROPE // 2
    inv_freq = ROPE_THETA ** (-jnp.arange(half, dtype=F32) / half)
    ang = jnp.arange(s).astype(F32)[:, None] * inv_freq[None, :]
    cos, sin = jnp.cos(ang), jnp.sin(ang)
    tbl = jnp.concatenate([cos, cos, sin, sin], axis=1)

    conv, q, k, v = _pre_call(
        h, norm_mix[l][None, :], win, conv_w[l], q_norm[l][None, :], wuq,
        kv_norm[l][None, :], w_ukv[l].astype(BF16), gqn, gqr, gkn, gkr, tbl)
    attn = _attn_call(q, k, v)

    tri = jnp.tril(jnp.ones((TM_POST, TM_POST), BF16), -1)
    h1, hp, idx, rank, gates, counts = _post_call(
        h.reshape(n, d), conv.reshape(n, CONV_WIDTH), attn.reshape(n, CONV_WIDTH),
        w_o[l].astype(BF16), norm_ffn[l][None, :], w_router[l].astype(BF16),
        b_router[l][None, :], tri)

    exp, flag, total, src, dst = _expert_items(counts[0], idx, rank, n, TM_EXP)
    y = _expert_call(exp, flag, total, src, dst, hp, w_gate_up[l], b_gate_up[l][:, None, :],
                     w_down[l], b_down[l][:, None, :], n * TOP_K + 2 * TM_EXP)
    out = _combine_call(h1, gates, y)
    return out.reshape(b, s, d)


def kernel(x, norm_mix, w_in, conv_w, q_norm, w_uq, kv_norm, w_ukv, q_head_norm, k_head_norm,
           w_o, norm_ffn, w_router, b_router, w_gate_up, b_gate_up, w_down, b_down):
    h = x
    for l in range(norm_mix.shape[0]):
        h = _layer(h, l, norm_mix, w_in, conv_w, q_norm, w_uq, kv_norm, w_ukv, q_head_norm,
                   k_head_norm, w_o, norm_ffn, w_router, b_router, w_gate_up, b_gate_up,
                   w_down, b_down)
    return h
```

```python
import math

import jax
import jax.numpy as jnp
import numpy as np
from jax import lax
from jax.experimental import pallas as pl
from jax.experimental.pallas import tpu as pltpu

F32 = jnp.float32
BF16 = jnp.bfloat16

D_MODEL = 1024
CONV_WIDTH = 512
MLA_HEADS = 4
QK_NOPE = 128
QK_ROPE = 64
QK_HEAD = QK_NOPE + QK_ROPE
QK_PAD = 256
V_HEAD = 128
Q_LORA = 384
KV_LORA = 256
ROPE_THETA = 10000.0
CHUNK = 64
N_EXPERTS = 32
TOP_K = 4
D_FF = 1024
SWIGLU_LIMIT = 7.0
SWIGLU_ALPHA = 1.702
EPS = 1e-6

OFF_B, OFF_C, OFF_X, OFF_CQ, OFF_CKV, OFF_KPE = 0, 512, 1024, 1536, 1920, 2176
IN_PROJ_EXT = 2304

VMEM_LIMIT = 56 * 1024 * 1024

TS_PRE = 512
TQ = 256
TM_POST = 512
TB_ROWS = 256
TM_EXP = 512
HALO = 8


def _rms(x, g):
    return x * lax.rsqrt(jnp.mean(x * x, axis=-1, keepdims=True) + EPS) * g


ROW_LINES = D_MODEL // 128


def _load_row_tiles(ref, rows, lead=()):
    return jnp.concatenate(
        [ref[lead + (pl.ds(c, rows, stride=ROW_LINES), slice(None))] for c in range(ROW_LINES)],
        axis=1)


def _store_row_tiles(ref, val):
    rows = val.shape[0]
    for c in range(ROW_LINES):
        ref[pl.ds(c, rows, stride=ROW_LINES), :] = val[:, 128 * c:128 * (c + 1)]


def _pre_kernel(x_ref, nm_ref, win_ref, convw_ref, qn_ref, wuq_ref, kvn_ref, wukv_ref,
                gqn_ref, gqr_ref, gkn_ref, gkr_ref, tbl_ref,
                conv_ref, q_ref, k_ref, vt_ref, ext_ref):
    s = pl.program_id(1)
    ts = x_ref.shape[1]
    hn = _rms(x_ref[0], nm_ref[...])
    proj = jnp.dot(hn.astype(BF16), win_ref[...], preferred_element_type=F32)

    @pl.when(s == 0)
    def _():
        ext_ref[0:HALO, :] = jnp.zeros((HALO, CONV_WIDTH), F32)

    @pl.when(s != 0)
    def _():
        ext_ref[0:HALO, :] = ext_ref[ts:ts + HALO, :]

    u = proj[:, OFF_C:OFF_C + CONV_WIDTH] * proj[:, OFF_X:OFF_X + CONV_WIDTH]
    ext_ref[HALO:HALO + ts, :] = u
    y = (convw_ref[2:3, :] * u
         + convw_ref[1:2, :] * ext_ref[HALO - 1:HALO - 1 + ts, :]
         + convw_ref[0:1, :] * ext_ref[HALO - 2:HALO - 2 + ts, :])
    conv_ref[0] = (proj[:, OFF_B:OFF_B + CONV_WIDTH] * y).astype(BF16)

    hq = _rms(proj[:, OFF_CQ:OFF_CQ + Q_LORA], qn_ref[...]).astype(BF16)
    qp = jnp.dot(hq, wuq_ref[...], preferred_element_type=F32)
    hkv = _rms(proj[:, OFF_CKV:OFF_CKV + KV_LORA], kvn_ref[...]).astype(BF16)
    kvp = jnp.dot(hkv, wukv_ref[...], preferred_element_type=F32)

    tbl = tbl_ref[...]
    lane = lax.broadcasted_iota(jnp.int32, (ts, 128), 1)
    first_half = lane < QK_ROPE

    def rope_block(grp, gain):
        r = grp * tbl * gain
        r = r + pltpu.roll(r, QK_ROPE, axis=1)
        return jnp.where(first_half, r, 0.0)

    kpe = proj[:, OFF_KPE:OFF_KPE + 128]
    ss_kpe = jnp.sum(jnp.where(first_half, kpe * kpe, 0.0), axis=-1, keepdims=True)
    k_rope = rope_block(kpe, gkr_ref[...])
    scale = 1.0 / math.sqrt(QK_HEAD)
    for h in range(MLA_HEADS):
        qn = qp[:, 128 * h:128 * (h + 1)]
        qg = qp[:, 512 + 128 * h:512 + 128 * (h + 1)]
        ss = (jnp.sum(qn * qn, axis=-1, keepdims=True)
              + jnp.sum(jnp.where(first_half, qg * qg, 0.0), axis=-1, keepdims=True))
        rs = lax.rsqrt(ss * (1.0 / QK_HEAD) + EPS) * scale
        q_ref[0, h, :, 0:128] = (qn * rs * gqn_ref[...]).astype(BF16)
        q_ref[0, h, :, 128:256] = (rope_block(qg, gqr_ref[...]) * rs).astype(BF16)

        kn = kvp[:, 256 * h:256 * h + 128]
        ssk = jnp.sum(kn * kn, axis=-1, keepdims=True) + ss_kpe
        rsk = lax.rsqrt(ssk * (1.0 / QK_HEAD) + EPS)
        k_ref[0, h, :, 0:128] = (kn * rsk * gkn_ref[...]).astype(BF16)
        k_ref[0, h, :, 128:256] = (k_rope * rsk).astype(BF16)
        v = kvp[:, 256 * h + 128:256 * (h + 1)]
        for c in range(ts // TQ):
            vt_ref[0, h, c] = v[c * TQ:(c + 1) * TQ, :].T.astype(BF16)


def _pre_call(x, nm, win, convw, qn, wuq, kvn, wukv, gqn, gqr, gkn, gkr, tbl):
    b, s, d = x.shape
    ts = TS_PRE
    const = lambda shape: pl.BlockSpec(shape, lambda bi, si: (0,) * len(shape))
    return pl.pallas_call(
        _pre_kernel,
        grid=(b, s // ts),
        in_specs=[
            pl.BlockSpec((1, ts, d), lambda bi, si: (bi, si, 0)),
            const((1, d)), const((d, IN_PROJ_EXT)), const((3, CONV_WIDTH)),
            const((1, Q_LORA)), const((Q_LORA, 1024)),
            const((1, KV_LORA)), const((KV_LORA, 1024)),
            const((1, 128)), const((1, 128)), const((1, 128)), const((1, 128)),
            pl.BlockSpec((ts, 128), lambda bi, si: (si, 0)),
        ],
        out_specs=[
            pl.BlockSpec((1, ts, CONV_WIDTH), lambda bi, si: (bi, si, 0)),
            pl.BlockSpec((1, MLA_HEADS, ts, QK_PAD), lambda bi, si: (bi, 0, si, 0)),
            pl.BlockSpec((1, MLA_HEADS, ts, QK_PAD), lambda bi, si: (bi, 0, si, 0)),
            pl.BlockSpec((1, MLA_HEADS, ts // TQ, V_HEAD, TQ), lambda bi, si: (bi, 0, si, 0, 0)),
        ],
        out_shape=[
            jax.ShapeDtypeStruct((b, s, CONV_WIDTH), BF16),
            jax.ShapeDtypeStruct((b, MLA_HEADS, s, QK_PAD), BF16),
            jax.ShapeDtypeStruct((b, MLA_HEADS, s, QK_PAD), BF16),
            jax.ShapeDtypeStruct((b, MLA_HEADS, s // TQ, V_HEAD, TQ), BF16),
        ],
        scratch_shapes=[pltpu.VMEM((ts + HALO, CONV_WIDTH), F32)],
        compiler_params=pltpu.CompilerParams(
            dimension_semantics=("arbitrary", "arbitrary"), vmem_limit_bytes=VMEM_LIMIT),
        name="pre_mixer",
    )(x, nm, win, convw, qn, wuq, kvn, wukv, gqn, gqr, gkn, gkr, tbl)


def _attn_kernel(q_ref, k_ref, vt_ref, o_ref, m_ref, l_ref, acc_ref):
    i = pl.program_id(1)
    tq = q_ref.shape[2]
    key_c = lax.broadcasted_iota(jnp.int32, (tq, tq), 0) // CHUNK
    qry_c = lax.broadcasted_iota(jnp.int32, (tq, tq), 1) // CHUNK
    allowed = key_c <= qry_c
    m_ref[...] = jnp.full(m_ref.shape, -1e30, F32)
    l_ref[...] = jnp.zeros(l_ref.shape, F32)
    acc_ref[...] = jnp.zeros(acc_ref.shape, F32)

    def key_block(j, masked):
        start = pl.multiple_of(j * tq, tq)
        scores = []
        for h in range(MLA_HEADS):
            kb = k_ref[0, h, pl.ds(start, tq), :]
            st = lax.dot_general(kb, q_ref[0, h], (((1,), (1,)), ((), ())),
                                 preferred_element_type=F32)
            scores.append(jnp.where(allowed, st, -1e30) if masked else st)
        probs, alphas = [], []
        for h in range(MLA_HEADS):
            m = m_ref[h]
            m_new = jnp.maximum(m, jnp.max(scores[h], axis=0, keepdims=True))
            alpha = jnp.exp(m - m_new)
            p = jnp.exp(scores[h] - m_new)
            l_ref[h] = alpha * l_ref[h] + jnp.sum(p, axis=0, keepdims=True)
            m_ref[h] = m_new
            probs.append(p.astype(BF16))
            alphas.append(alpha)
        for h in range(MLA_HEADS):
            acc_ref[h] = alphas[h] * acc_ref[h] + jnp.dot(vt_ref[0, h, j], probs[h],
                                                           preferred_element_type=F32)

    def body(j, c):
        key_block(j, False)
        return c

    lax.fori_loop(0, i, body, 0)
    key_block(i, True)
    for h in range(MLA_HEADS):
        o = acc_ref[h] / l_ref[h]
        o_ref[0, :, V_HEAD * h:V_HEAD * (h + 1)] = o.T.astype(BF16)


def _attn_call(q, k, vt):
    b, nh, s, _ = q.shape
    return pl.pallas_call(
        _attn_kernel,
        grid=(b, s // TQ),
        in_specs=[
            pl.BlockSpec((1, nh, TQ, QK_PAD), lambda bi, qi: (bi, 0, qi, 0)),
            pl.BlockSpec((1, nh, s, QK_PAD), lambda bi, qi: (bi, 0, 0, 0)),
            pl.BlockSpec((1, nh, s // TQ, V_HEAD, TQ), lambda bi, qi: (bi, 0, 0, 0, 0)),
        ],
        out_specs=pl.BlockSpec((1, TQ, nh * V_HEAD), lambda bi, qi: (bi, qi, 0)),
        out_shape=jax.ShapeDtypeStruct((b, s, nh * V_HEAD), BF16),
        scratch_shapes=[pltpu.VMEM((nh, 1, TQ), F32), pltpu.VMEM((nh, 1, TQ), F32),
                        pltpu.VMEM((nh, V_HEAD, TQ), F32)],
        compiler_params=pltpu.CompilerParams(
            dimension_semantics=("arbitrary", "arbitrary"), vmem_limit_bytes=VMEM_LIMIT),
        name="mla_attention",
    )(q, k, vt)


def _post_kernel(x_ref, conv_ref, attn_ref, wo_ref, g_ref, wr_ref, br_ref, tri_ref,
                 h1_ref, hp_ref, idx_ref, rank_ref, gate_ref, cnt_ref, carry_ref):
    step = pl.program_id(0)
    tm = x_ref.shape[0]

    @pl.when(step == 0)
    def _():
        carry_ref[...] = jnp.zeros_like(carry_ref)

    h1 = (x_ref[...]
          + jnp.dot(conv_ref[...], wo_ref[0:CONV_WIDTH, :], preferred_element_type=F32)
          + jnp.dot(attn_ref[...], wo_ref[CONV_WIDTH:, :], preferred_element_type=F32))
    h1_ref[...] = h1
    hb = _rms(h1, g_ref[...]).astype(BF16)
    _store_row_tiles(hp_ref, hb.astype(F32))

    logits = jnp.dot(hb, wr_ref[...], preferred_element_type=F32) + br_ref[...]
    lane = lax.broadcasted_iota(jnp.int32, (tm, N_EXPERTS), 1).astype(F32)
    lane4 = lax.broadcasted_iota(jnp.int32, (tm, TOP_K), 1)
    member = jnp.zeros((tm, N_EXPERTS), F32)
    vals, idxs, hots = [], [], []
    lg = logits
    for _ in range(TOP_K):
        m = jnp.max(lg, axis=-1, keepdims=True)
        ix = jnp.min(jnp.where(lg == m, lane, float(N_EXPERTS)), axis=-1, keepdims=True)
        hot = lane == ix
        lg = jnp.where(hot, -jnp.inf, lg)
        member = member + hot.astype(F32)
        vals.append(m)
        idxs.append(ix)
        hots.append(hot)

    before = jnp.dot(tri_ref[...], member.astype(BF16), preferred_element_type=F32)
    before = before + carry_ref[...]
    ranks = [jnp.sum(jnp.where(hot, before, 0.0), axis=-1, keepdims=True) for hot in hots]
    carry_ref[...] = carry_ref[...] + jnp.sum(member, axis=0, keepdims=True)
    cnt_ref[...] = carry_ref[...].astype(jnp.int32)

    exps = [jnp.exp(v - vals[0]) for v in vals]
    denom = exps[0] + exps[1] + exps[2] + exps[3]

    def cols(c):
        return jnp.where(lane4 == 0, c[0], jnp.where(lane4 == 1, c[1],
                         jnp.where(lane4 == 2, c[2], c[3])))

    idx_ref[...] = cols(idxs).astype(jnp.int32)
    rank_ref[...] = cols(ranks).astype(jnp.int32)
    gate_ref[...] = cols([e / denom for e in exps])


def _post_call(xf, conv, attn, wo, g, wr, br, tri):
    n, d = xf.shape
    tm = TM_POST
    const = lambda shape: pl.BlockSpec(shape, lambda i: (0,) * len(shape))
    row = lambda w: pl.BlockSpec((tm, w), lambda i: (i, 0))
    return pl.pallas_call(
        _post_kernel,
        grid=(n // tm,),
        in_specs=[row(d), row(CONV_WIDTH), row(CONV_WIDTH), const((d, d)), const((1, d)),
                  const((d, N_EXPERTS)), const((1, N_EXPERTS)), const((tm, tm))],
        out_specs=[row(d), pl.BlockSpec((tm * ROW_LINES, 128), lambda i: (i, 0)),
                   row(TOP_K), row(TOP_K), row(TOP_K), const((1, N_EXPERTS))],
        out_shape=[
            jax.ShapeDtypeStruct((n, d), F32),
            jax.ShapeDtypeStruct((n * ROW_LINES, 128), F32),
            jax.ShapeDtypeStruct((n, TOP_K), jnp.int32),
            jax.ShapeDtypeStruct((n, TOP_K), jnp.int32),
            jax.ShapeDtypeStruct((n, TOP_K), F32),
            jax.ShapeDtypeStruct((1, N_EXPERTS), jnp.int32),
        ],
        scratch_shapes=[pltpu.VMEM((1, N_EXPERTS), F32)],
        compiler_params=pltpu.CompilerParams(
            dimension_semantics=("arbitrary",), vmem_limit_bytes=VMEM_LIMIT),
        name="post_mixer_router",
    )(xf, conv, attn, wo, g, wr, br, tri)


def _row_tile(ref, r):
    return ref.at[pl.ds(pl.multiple_of(r * ROW_LINES, ROW_LINES), ROW_LINES), :]


def _scatter_kernel(pos_ref, hp_ref, xs_ref, sem):
    tb = hp_ref.shape[0] // ROW_LINES

    def issue(t, c):
        for k in range(TOP_K):
            p = pos_ref[t * TOP_K + k]
            pltpu.make_async_copy(_row_tile(hp_ref, t), _row_tile(xs_ref, p), sem).start()
        return c

    lax.fori_loop(0, tb, issue, 0, unroll=4)
    for k in range(TOP_K):
        pltpu.make_async_copy(hp_ref, xs_ref.at[pl.ds(0, tb * ROW_LINES), :], sem).wait()


def _scatter_call(pos_flat, hp):
    lines, w = hp.shape
    tb = TB_ROWS
    return pl.pallas_call(
        _scatter_kernel,
        grid=(lines // (tb * ROW_LINES),),
        in_specs=[pl.BlockSpec((tb * TOP_K,), lambda i: (i,), memory_space=pltpu.SMEM),
                  pl.BlockSpec((tb * ROW_LINES, w), lambda i: (i, 0))],
        out_specs=pl.BlockSpec(memory_space=pl.ANY),
        out_shape=jax.ShapeDtypeStruct((lines * TOP_K, w), hp.dtype),
        scratch_shapes=[pltpu.SemaphoreType.DMA],
        compiler_params=pltpu.CompilerParams(
            dimension_semantics=("arbitrary",), vmem_limit_bytes=VMEM_LIMIT),
        name="dispatch_scatter",
    )(pos_flat, hp)


def _expert_kernel(tile_ref, exp_ref, lo_ref, hi_ref, flag_ref,
                   xs_ref, wgu_ref, bgu_ref, wd_ref, bd_ref, o_ref, wgu_bf, wd_bf):
    i = pl.program_id(0)
    tm = xs_ref.shape[0] // ROW_LINES
    lo, hi, flag = lo_ref[i], hi_ref[i], flag_ref[i]
    cast_rows = 128

    @pl.when((flag & 2) != 0)
    def _():
        def cast(r, c):
            rows = pl.ds(pl.multiple_of(r * cast_rows, cast_rows), cast_rows)
            wgu_bf[rows, :] = wgu_ref[0, rows, :].astype(BF16)
            wd_bf[rows, :] = wd_ref[0, rows, :].astype(BF16)
            return c
        lax.fori_loop(0, D_MODEL // cast_rows, cast, 0)

    @pl.when(hi > lo)
    def _():
        x = _load_row_tiles(xs_ref, tm).astype(BF16)
        gu = jnp.dot(x, wgu_bf[...], preferred_element_type=F32) + bgu_ref[0]
        gate = jnp.minimum(gu[:, :D_FF], SWIGLU_LIMIT)
        up = jnp.clip(gu[:, D_FF:], -SWIGLU_LIMIT, SWIGLU_LIMIT)
        act = (up + 1.0) * (gate * (1.0 / (1.0 + jnp.exp(-SWIGLU_ALPHA * gate))))
        out = jnp.dot(act.astype(BF16), wd_bf[...], preferred_element_type=F32) + bd_ref[0]
        row = lax.broadcasted_iota(jnp.int32, (tm, 1), 0)
        mine = (row >= lo) & (row < hi)

        @pl.when((flag & 1) != 0)
        def _():
            _store_row_tiles(o_ref, jnp.where(mine, out, 0.0))

        @pl.when((flag & 1) == 0)
        def _():
            _store_row_tiles(o_ref, jnp.where(mine, out, _load_row_tiles(o_ref, tm)))


def _expert_call(meta, xs, wgu, bgu, wd, bd):
    lines, w = xs.shape
    tm = TM_EXP
    n_items = meta[0].shape[0]
    return pl.pallas_call(
        _expert_kernel,
        grid_spec=pltpu.PrefetchScalarGridSpec(
            num_scalar_prefetch=5,
            grid=(n_items,),
            in_specs=[
                pl.BlockSpec((tm * ROW_LINES, w), lambda i, t, e, lo, hi, f: (t[i], 0)),
                pl.BlockSpec((1, D_MODEL, 2 * D_FF), lambda i, t, e, lo, hi, f: (e[i], 0, 0)),
                pl.BlockSpec((1, 1, 2 * D_FF), lambda i, t, e, lo, hi, f: (e[i], 0, 0)),
                pl.BlockSpec((1, D_FF, D_MODEL), lambda i, t, e, lo, hi, f: (e[i], 0, 0)),
                pl.BlockSpec((1, 1, D_MODEL), lambda i, t, e, lo, hi, f: (e[i], 0, 0)),
            ],
            out_specs=pl.BlockSpec((tm * ROW_LINES, w), lambda i, t, e, lo, hi, f: (t[i], 0)),
            scratch_shapes=[pltpu.VMEM((D_MODEL, 2 * D_FF), BF16),
                            pltpu.VMEM((D_FF, D_MODEL), BF16)],
        ),
        out_shape=jax.ShapeDtypeStruct((lines, w), F32),
        compiler_params=pltpu.CompilerParams(
            dimension_semantics=("arbitrary",), vmem_limit_bytes=VMEM_LIMIT),
        name="moe_experts",
    )(*meta, xs, wgu, bgu, wd, bd)


def _expert_items(counts, rows, tm):
    n_tiles = rows // tm
    n_items = n_tiles + N_EXPERTS - 1
    ends = jnp.cumsum(counts)
    starts = ends - counts
    first_tile = starts // tm
    last_tile = jnp.maximum(ends - 1, 0) // tm
    per_exp = jnp.where(counts > 0, last_tile - first_tile + 1, 0)
    item_end = jnp.cumsum(per_exp)
    item_start = item_end - per_exp
    total = item_end[-1]
    i = jnp.arange(n_items, dtype=jnp.int32)
    valid = i < total
    e = jnp.sum((item_end[None, :] <= jnp.minimum(i, total - 1)[:, None]).astype(jnp.int32), axis=1)
    hot = e[:, None] == jnp.arange(N_EXPERTS, dtype=jnp.int32)[None, :]
    pick = lambda tab: jnp.sum(jnp.where(hot, tab[None, :], 0), axis=1)
    e_start, e_end, e_item0 = pick(starts), pick(ends), pick(item_start)
    tile = jnp.where(valid, e_start // tm + (i - e_item0), n_tiles - 1)
    lo = jnp.where(valid, jnp.maximum(e_start, tile * tm) - tile * tm, 0)
    hi = jnp.where(valid, jnp.minimum(e_end, (tile + 1) * tm) - tile * tm, 0)
    flag = jnp.where(valid, (lo == 0).astype(jnp.int32) + 2 * (i == e_item0), 0)
    cast = lambda a: a.astype(jnp.int32)
    return (cast(tile), cast(e), cast(lo), cast(hi), cast(flag)), starts


def _combine_kernel(pos_ref, h1_ref, gate_ref, ys_ref, o_ref, buf, sem):
    tb = h1_ref.shape[0]

    def issue(t, c):
        for k in range(TOP_K):
            p = pos_ref[t * TOP_K + k]
            pltpu.make_async_copy(_row_tile(ys_ref, p), _row_tile(buf.at[k], t), sem).start()
        return c

    lax.fori_loop(0, tb, issue, 0, unroll=4)
    for k in range(TOP_K):
        pltpu.make_async_copy(ys_ref.at[pl.ds(0, tb * ROW_LINES), :], buf.at[k], sem).wait()
    g = gate_ref[...]
    acc = h1_ref[...]
    for k in range(TOP_K):
        acc = acc + g[:, k:k + 1] * _load_row_tiles(buf, tb, lead=(k,))
    o_ref[...] = acc


def _combine_call(pos_flat, h1, gates, ys):
    n, d = h1.shape
    tb = TB_ROWS
    return pl.pallas_call(
        _combine_kernel,
        grid=(n // tb,),
        in_specs=[pl.BlockSpec((tb * TOP_K,), lambda i: (i,), memory_space=pltpu.SMEM),
                  pl.BlockSpec((tb, d), lambda i: (i, 0)),
                  pl.BlockSpec((tb, TOP_K), lambda i: (i, 0)),
                  pl.BlockSpec(memory_space=pl.ANY)],
        out_specs=pl.BlockSpec((tb, d), lambda i: (i, 0)),
        out_shape=jax.ShapeDtypeStruct((n, d), F32),
        scratch_shapes=[pltpu.VMEM((TOP_K, tb * ROW_LINES, 128), F32), pltpu.SemaphoreType.DMA],
        compiler_params=pltpu.CompilerParams(
            dimension_semantics=("arbitrary",), vmem_limit_bytes=VMEM_LIMIT),
        name="moe_combine",
    )(pos_flat, h1, gates, ys)


def _rotate_half_perm():
    half = QK_ROPE // 2
    perm = np.concatenate([np.arange(half, QK_ROPE), np.arange(0, half)])
    sign = np.concatenate([-np.ones(half), np.ones(half)]).astype(np.float32)
    return perm, sign


def _layer(h, l, norm_mix, w_in, conv_w, q_norm, w_uq, kv_norm, w_ukv, q_head_norm,
           k_head_norm, w_o, norm_ffn, w_router, b_router, w_gate_up, b_gate_up,
           w_down, b_down):
    b, s, d = h.shape
    n = b * s
    perm, sign = _rotate_half_perm()

    kpe_cols = w_in[l][:, OFF_KPE:OFF_KPE + QK_ROPE]
    win = jnp.concatenate([w_in[l], kpe_cols[:, perm] * sign], axis=1).astype(BF16)
    wq = w_uq[l].reshape(Q_LORA, MLA_HEADS, QK_HEAD)
    wq_rope = wq[:, :, QK_NOPE:]
    wuq = jnp.concatenate(
        [wq[:, :, :QK_NOPE].reshape(Q_LORA, MLA_HEADS * QK_NOPE),
         jnp.concatenate([wq_rope, wq_rope[:, :, perm] * sign], axis=2)
         .reshape(Q_LORA, MLA_HEADS * 2 * QK_ROPE)], axis=1).astype(BF16)
    gq, gk = q_head_norm[l], k_head_norm[l]
    gqn, gkn = gq[None, :QK_NOPE], gk[None, :QK_NOPE]
    gqr = jnp.concatenate([gq[QK_NOPE:], gq[QK_NOPE:][perm]])[None, :]
    gkr = jnp.concatenate([gk[QK_NOPE:], gk[QK_NOPE:][perm]])[None, :]

    half = QK_ROPE // 2
    inv_freq = ROPE_THETA ** (-jnp.arange(half, dtype=F32) / half)
    ang = jnp.arange(s).astype(F32)[:, None] * inv_freq[None, :]
    cos, sin = jnp.cos(ang), jnp.sin(ang)
    tbl = jnp.concatenate([cos, cos, sin, sin], axis=1)

    conv, q, k, vt = _pre_call(
        h, norm_mix[l][None, :], win, conv_w[l], q_norm[l][None, :], wuq,
        kv_norm[l][None, :], w_ukv[l].astype(BF16), gqn, gqr, gkn, gkr, tbl)
    attn = _attn_call(q, k, vt)

    tri = jnp.tril(jnp.ones((TM_POST, TM_POST), BF16), -1)
    h1, hp, idx, rank, gates, counts = _post_call(
        h.reshape(n, d), conv.reshape(n, CONV_WIDTH), attn.reshape(n, CONV_WIDTH),
        w_o[l].astype(BF16), norm_ffn[l][None, :], w_router[l].astype(BF16),
        b_router[l][None, :], tri)

    meta, starts = _expert_items(counts[0], n * TOP_K, TM_EXP)
    idx_hot = idx[:, :, None] == jnp.arange(N_EXPERTS, dtype=jnp.int32)[None, None, :]
    group_start = jnp.sum(jnp.where(idx_hot, starts[None, None, :], 0), axis=-1)
    pos_flat = (group_start + rank).reshape(n * TOP_K).astype(jnp.int32)
    xs = _scatter_call(pos_flat, hp)
    ys = _expert_call(meta, xs, w_gate_up[l], b_gate_up[l][:, None, :],
                      w_down[l], b_down[l][:, None, :])
    out = _combine_call(pos_flat, h1, gates, ys)
    return out.reshape(b, s, d)


def kernel(x, norm_mix, w_in, conv_w, q_norm, w_uq, kv_norm, w_ukv, q_head_norm, k_head_norm,
           w_o, norm_ffn, w_router, b_router, w_gate_up, b_gate_up, w_down, b_down):
    h = x
    for l in range(norm_mix.shape[0]):
        h = _layer(h, l, norm_mix, w_in, conv_w, q_norm, w_uq, kv_norm, w_ukv, q_head_norm,
                   k_head_norm, w_o, norm_ffn, w_router, b_router, w_gate_up, b_gate_up,
                   w_down, b_down)
    return h
```

```python
import math

import jax
import jax.numpy as jnp
import numpy as np
from jax import lax
from jax.experimental import pallas as pl
from jax.experimental.pallas import tpu as pltpu

F32 = jnp.float32
BF16 = jnp.bfloat16

D_MODEL = 1024
CONV_WIDTH = 512
MLA_HEADS = 4
QK_NOPE = 128
QK_ROPE = 64
QK_HEAD = QK_NOPE + QK_ROPE
QK_PAD = 256
V_HEAD = 128
Q_LORA = 384
KV_LORA = 256
ROPE_THETA = 10000.0
CHUNK = 64
N_EXPERTS = 32
TOP_K = 4
D_FF = 1024
SWIGLU_LIMIT = 7.0
SWIGLU_ALPHA = 1.702
EPS = 1e-6

OFF_B, OFF_C, OFF_X, OFF_CQ, OFF_CKV, OFF_KPE = 0, 512, 1024, 1536, 1920, 2176
IN_PROJ_EXT = 2304

VMEM_LIMIT = 56 * 1024 * 1024

TS_PRE = 512
TQ = 256
TM_POST = 512
TB_ROWS = 256
TM_EXP = 512
HALO = 8


def _rms(x, g):
    return x * lax.rsqrt(jnp.mean(x * x, axis=-1, keepdims=True) + EPS) * g


ROW_LINES = D_MODEL // 128


def _load_row_tiles(ref, rows, lead=()):
    return jnp.concatenate(
        [ref[lead + (pl.ds(c, rows, stride=ROW_LINES), slice(None))] for c in range(ROW_LINES)],
        axis=1)


def _store_row_tiles(ref, val):
    rows = val.shape[0]
    for c in range(ROW_LINES):
        ref[pl.ds(c, rows, stride=ROW_LINES), :] = val[:, 128 * c:128 * (c + 1)]


def _pre_kernel(x_ref, nm_ref, win_ref, convw_ref, qn_ref, wuq_ref, kvn_ref, wukv_ref,
                gqn_ref, gqr_ref, gkn_ref, gkr_ref, tbl_ref,
                conv_ref, q_ref, k_ref, vt_ref, ext_ref):
    s = pl.program_id(1)
    ts = x_ref.shape[1]
    hn = _rms(x_ref[0], nm_ref[...])
    proj = jnp.dot(hn.astype(BF16), win_ref[...], preferred_element_type=F32)

    @pl.when(s == 0)
    def _():
        ext_ref[0:HALO, :] = jnp.zeros((HALO, CONV_WIDTH), F32)

    @pl.when(s != 0)
    def _():
        ext_ref[0:HALO, :] = ext_ref[ts:ts + HALO, :]

    u = proj[:, OFF_C:OFF_C + CONV_WIDTH] * proj[:, OFF_X:OFF_X + CONV_WIDTH]
    ext_ref[HALO:HALO + ts, :] = u
    y = (convw_ref[2:3, :] * u
         + convw_ref[1:2, :] * ext_ref[HALO - 1:HALO - 1 + ts, :]
         + convw_ref[0:1, :] * ext_ref[HALO - 2:HALO - 2 + ts, :])
    conv_ref[0] = (proj[:, OFF_B:OFF_B + CONV_WIDTH] * y).astype(BF16)

    hq = _rms(proj[:, OFF_CQ:OFF_CQ + Q_LORA], qn_ref[...]).astype(BF16)
    qp = jnp.dot(hq, wuq_ref[...], preferred_element_type=F32)
    hkv = _rms(proj[:, OFF_CKV:OFF_CKV + KV_LORA], kvn_ref[...]).astype(BF16)
    kvp = jnp.dot(hkv, wukv_ref[...], preferred_element_type=F32)

    tbl = tbl_ref[...]
    lane = lax.broadcasted_iota(jnp.int32, (ts, 128), 1)
    first_half = lane < QK_ROPE

    def rope_block(grp, gain):
        r = grp * tbl * gain
        r = r + pltpu.roll(r, QK_ROPE, axis=1)
        return jnp.where(first_half, r, 0.0)

    kpe = proj[:, OFF_KPE:OFF_KPE + 128]
    ss_kpe = jnp.sum(jnp.where(first_half, kpe * kpe, 0.0), axis=-1, keepdims=True)
    k_rope = rope_block(kpe, gkr_ref[...])
    scale = 1.0 / math.sqrt(QK_HEAD)
    for h in range(MLA_HEADS):
        qn = qp[:, 128 * h:128 * (h + 1)]
        qg = qp[:, 512 + 128 * h:512 + 128 * (h + 1)]
        ss = (jnp.sum(qn * qn, axis=-1, keepdims=True)
              + jnp.sum(jnp.where(first_half, qg * qg, 0.0), axis=-1, keepdims=True))
        rs = lax.rsqrt(ss * (1.0 / QK_HEAD) + EPS) * scale
        q_ref[0, h, :, 0:128] = (qn * rs * gqn_ref[...]).astype(BF16)
        q_ref[0, h, :, 128:256] = (rope_block(qg, gqr_ref[...]) * rs).astype(BF16)

        kn = kvp[:, 256 * h:256 * h + 128]
        ssk = jnp.sum(kn * kn, axis=-1, keepdims=True) + ss_kpe
        rsk = lax.rsqrt(ssk * (1.0 / QK_HEAD) + EPS)
        k_ref[0, h, :, 0:128] = (kn * rsk * gkn_ref[...]).astype(BF16)
        k_ref[0, h, :, 128:256] = (k_rope * rsk).astype(BF16)
        v = kvp[:, 256 * h + 128:256 * (h + 1)]
        for c in range(ts // TQ):
            vt_ref[0, h, c] = v[c * TQ:(c + 1) * TQ, :].T.astype(BF16)


def _pre_call(x, nm, win, convw, qn, wuq, kvn, wukv, gqn, gqr, gkn, gkr, tbl):
    b, s, d = x.shape
    ts = TS_PRE
    const = lambda shape: pl.BlockSpec(shape, lambda bi, si: (0,) * len(shape))
    return pl.pallas_call(
        _pre_kernel,
        grid=(b, s // ts),
        in_specs=[
            pl.BlockSpec((1, ts, d), lambda bi, si: (bi, si, 0)),
            const((1, d)), const((d, IN_PROJ_EXT)), const((3, CONV_WIDTH)),
            const((1, Q_LORA)), const((Q_LORA, 1024)),
            const((1, KV_LORA)), const((KV_LORA, 1024)),
            const((1, 128)), const((1, 128)), const((1, 128)), const((1, 128)),
            pl.BlockSpec((ts, 128), lambda bi, si: (si, 0)),
        ],
        out_specs=[
            pl.BlockSpec((1, ts, CONV_WIDTH), lambda bi, si: (bi, si, 0)),
            pl.BlockSpec((1, MLA_HEADS, ts, QK_PAD), lambda bi, si: (bi, 0, si, 0)),
            pl.BlockSpec((1, MLA_HEADS, ts, QK_PAD), lambda bi, si: (bi, 0, si, 0)),
            pl.BlockSpec((1, MLA_HEADS, ts // TQ, V_HEAD, TQ), lambda bi, si: (bi, 0, si, 0, 0)),
        ],
        out_shape=[
            jax.ShapeDtypeStruct((b, s, CONV_WIDTH), BF16),
            jax.ShapeDtypeStruct((b, MLA_HEADS, s, QK_PAD), BF16),
            jax.ShapeDtypeStruct((b, MLA_HEADS, s, QK_PAD), BF16),
            jax.ShapeDtypeStruct((b, MLA_HEADS, s // TQ, V_HEAD, TQ), BF16),
        ],
        scratch_shapes=[pltpu.VMEM((ts + HALO, CONV_WIDTH), F32)],
        compiler_params=pltpu.CompilerParams(
            dimension_semantics=("arbitrary", "arbitrary"), vmem_limit_bytes=VMEM_LIMIT),
        name="pre_mixer",
    )(x, nm, win, convw, qn, wuq, kvn, wukv, gqn, gqr, gkn, gkr, tbl)


def _attn_kernel(q_ref, k_ref, vt_ref, o_ref, m_ref, l_ref, acc_ref):
    i = pl.program_id(1)
    tq = q_ref.shape[2]
    key_c = lax.broadcasted_iota(jnp.int32, (tq, tq), 0) // CHUNK
    qry_c = lax.broadcasted_iota(jnp.int32, (tq, tq), 1) // CHUNK
    allowed = key_c <= qry_c
    m_ref[...] = jnp.full(m_ref.shape, -1e30, F32)
    l_ref[...] = jnp.zeros(l_ref.shape, F32)
    acc_ref[...] = jnp.zeros(acc_ref.shape, F32)

    def key_block(j, masked):
        start = pl.multiple_of(j * tq, tq)
        scores = []
        for h in range(MLA_HEADS):
            kb = k_ref[0, h, pl.ds(start, tq), :]
            st = lax.dot_general(kb, q_ref[0, h], (((1,), (1,)), ((), ())),
                                 preferred_element_type=F32)
            scores.append(jnp.where(allowed, st, -1e30) if masked else st)
        probs, alphas = [], []
        for h in range(MLA_HEADS):
            m = m_ref[h]
            m_new = jnp.maximum(m, jnp.max(scores[h], axis=0, keepdims=True))
            alpha = jnp.exp(m - m_new)
            p = jnp.exp(scores[h] - m_new)
            l_ref[h] = alpha * l_ref[h] + jnp.sum(p, axis=0, keepdims=True)
            m_ref[h] = m_new
            probs.append(p.astype(BF16))
            alphas.append(alpha)
        for h in range(MLA_HEADS):
            acc_ref[h] = alphas[h] * acc_ref[h] + jnp.dot(vt_ref[0, h, j], probs[h],
                                                           preferred_element_type=F32)

    def body(j, c):
        key_block(j, False)
        return c

    lax.fori_loop(0, i, body, 0)
    key_block(i, True)
    for h in range(MLA_HEADS):
        o = acc_ref[h] / l_ref[h]
        o_ref[0, :, V_HEAD * h:V_HEAD * (h + 1)] = o.T.astype(BF16)


def _attn_call(q, k, vt):
    b, nh, s, _ = q.shape
    return pl.pallas_call(
        _attn_kernel,
        grid=(b, s // TQ),
        in_specs=[
            pl.BlockSpec((1, nh, TQ, QK_PAD), lambda bi, qi: (bi, 0, qi, 0)),
            pl.BlockSpec((1, nh, s, QK_PAD), lambda bi, qi: (bi, 0, 0, 0)),
            pl.BlockSpec((1, nh, s // TQ, V_HEAD, TQ), lambda bi, qi: (bi, 0, 0, 0, 0)),
        ],
        out_specs=pl.BlockSpec((1, TQ, nh * V_HEAD), lambda bi, qi: (bi, qi, 0)),
        out_shape=jax.ShapeDtypeStruct((b, s, nh * V_HEAD), BF16),
        scratch_shapes=[pltpu.VMEM((nh, 1, TQ), F32), pltpu.VMEM((nh, 1, TQ), F32),
                        pltpu.VMEM((nh, V_HEAD, TQ), F32)],
        compiler_params=pltpu.CompilerParams(
            dimension_semantics=("arbitrary", "arbitrary"), vmem_limit_bytes=VMEM_LIMIT),
        name="mla_attention",
    )(q, k, vt)


def _post_kernel(x_ref, conv_ref, attn_ref, wo_ref, g_ref, wr_ref, br_ref, tri_ref,
                 h1_ref, hp_ref, idx_ref, rank_ref, gate_ref, cnt_ref, carry_ref):
    step = pl.program_id(0)
    tm = x_ref.shape[0]

    @pl.when(step == 0)
    def _():
        carry_ref[...] = jnp.zeros_like(carry_ref)

    h1 = (x_ref[...]
          + jnp.dot(conv_ref[...], wo_ref[0:CONV_WIDTH, :], preferred_element_type=F32)
          + jnp.dot(attn_ref[...], wo_ref[CONV_WIDTH:, :], preferred_element_type=F32))
    h1_ref[...] = h1
    hb = _rms(h1, g_ref[...]).astype(BF16)
    _store_row_tiles(hp_ref, hb.astype(F32))

    logits = jnp.dot(hb, wr_ref[...], preferred_element_type=F32) + br_ref[...]
    lane = lax.broadcasted_iota(jnp.int32, (tm, N_EXPERTS), 1).astype(F32)
    lane4 = lax.broadcasted_iota(jnp.int32, (tm, TOP_K), 1)
    member = jnp.zeros((tm, N_EXPERTS), F32)
    vals, idxs, hots = [], [], []
    lg = logits
    for _ in range(TOP_K):
        m = jnp.max(lg, axis=-1, keepdims=True)
        ix = jnp.min(jnp.where(lg == m, lane, float(N_EXPERTS)), axis=-1, keepdims=True)
        hot = lane == ix
        lg = jnp.where(hot, -jnp.inf, lg)
        member = member + hot.astype(F32)
        vals.append(m)
        idxs.append(ix)
        hots.append(hot)

    before = jnp.dot(tri_ref[...], member.astype(BF16), preferred_element_type=F32)
    before = before + carry_ref[...]
    ranks = [jnp.sum(jnp.where(hot, before, 0.0), axis=-1, keepdims=True) for hot in hots]
    carry_ref[...] = carry_ref[...] + jnp.sum(member, axis=0, keepdims=True)
    cnt_ref[...] = carry_ref[...].astype(jnp.int32)

    exps = [jnp.exp(v - vals[0]) for v in vals]
    denom = exps[0] + exps[1] + exps[2] + exps[3]

    def cols(c):
        return jnp.where(lane4 == 0, c[0], jnp.where(lane4 == 1, c[1],
                         jnp.where(lane4 == 2, c[2], c[3])))

    idx_ref[...] = cols(idxs).astype(jnp.int32)
    rank_ref[...] = cols(ranks).astype(jnp.int32)
    gate_ref[...] = cols([e / denom for e in exps])


def _post_call(xf, conv, attn, wo, g, wr, br, tri):
    n, d = xf.shape
    tm = TM_POST
    const = lambda shape: pl.BlockSpec(shape, lambda i: (0,) * len(shape))
    row = lambda w: pl.BlockSpec((tm, w), lambda i: (i, 0))
    return pl.pallas_call(
        _post_kernel,
        grid=(n // tm,),
        in_specs=[row(d), row(CONV_WIDTH), row(CONV_WIDTH), const((d, d)), const((1, d)),
                  const((d, N_EXPERTS)), const((1, N_EXPERTS)), const((tm, tm))],
        out_specs=[row(d), pl.BlockSpec((tm * ROW_LINES, 128), lambda i: (i, 0)),
                   row(TOP_K), row(TOP_K), row(TOP_K), const((1, N_EXPERTS))],
        out_shape=[
            jax.ShapeDtypeStruct((n, d), F32),
            jax.ShapeDtypeStruct((n * ROW_LINES, 128), F32),
            jax.ShapeDtypeStruct((n, TOP_K), jnp.int32),
            jax.ShapeDtypeStruct((n, TOP_K), jnp.int32),
            jax.ShapeDtypeStruct((n, TOP_K), F32),
            jax.ShapeDtypeStruct((1, N_EXPERTS), jnp.int32),
        ],
        scratch_shapes=[pltpu.VMEM((1, N_EXPERTS), F32)],
        compiler_params=pltpu.CompilerParams(
            dimension_semantics=("arbitrary",), vmem_limit_bytes=VMEM_LIMIT),
        name="post_mixer_router",
    )(xf, conv, attn, wo, g, wr, br, tri)


def _row_tile(ref, r):
    return ref.at[pl.ds(pl.multiple_of(r * ROW_LINES, ROW_LINES), ROW_LINES), :]


def _scatter_kernel(pos_ref, hp_ref, xs_ref, sem):
    tb = hp_ref.shape[0] // ROW_LINES

    def issue(t, c):
        for k in range(TOP_K):
            p = pos_ref[t * TOP_K + k]
            pltpu.make_async_copy(_row_tile(hp_ref, t), _row_tile(xs_ref, p),
                                  sem).start(priority=k % 2)
        return c

    lax.fori_loop(0, tb, issue, 0, unroll=4)
    for k in range(TOP_K):
        pltpu.make_async_copy(hp_ref, xs_ref.at[pl.ds(0, tb * ROW_LINES), :], sem).wait()


def _scatter_call(pos_flat, hp):
    lines, w = hp.shape
    tb = TB_ROWS
    return pl.pallas_call(
        _scatter_kernel,
        grid=(lines // (tb * ROW_LINES),),
        in_specs=[pl.BlockSpec((tb * TOP_K,), lambda i: (i,), memory_space=pltpu.SMEM),
                  pl.BlockSpec((tb * ROW_LINES, w), lambda i: (i, 0))],
        out_specs=pl.BlockSpec(memory_space=pl.ANY),
        out_shape=jax.ShapeDtypeStruct((lines * TOP_K, w), hp.dtype),
        scratch_shapes=[pltpu.SemaphoreType.DMA],
        compiler_params=pltpu.CompilerParams(
            dimension_semantics=("arbitrary",), vmem_limit_bytes=VMEM_LIMIT),
        name="dispatch_scatter",
    )(pos_flat, hp)


def _expert_kernel(tile_ref, exp_ref, lo_ref, hi_ref, flag_ref,
                   xs_ref, wgu_ref, bgu_ref, wd_ref, bd_ref, o_ref, wgu_bf, wd_bf):
    i = pl.program_id(0)
    tm = xs_ref.shape[0] // ROW_LINES
    lo, hi, flag = lo_ref[i], hi_ref[i], flag_ref[i]
    cast_rows = 128

    @pl.when((flag & 2) != 0)
    def _():
        def cast(r, c):
            rows = pl.ds(pl.multiple_of(r * cast_rows, cast_rows), cast_rows)
            wgu_bf[rows, :] = wgu_ref[0, rows, :].astype(BF16)
            wd_bf[rows, :] = wd_ref[0, rows, :].astype(BF16)
            return c
        lax.fori_loop(0, D_MODEL // cast_rows, cast, 0)

    @pl.when(hi > lo)
    def _():
        x = _load_row_tiles(xs_ref, tm).astype(BF16)
        gu = jnp.dot(x, wgu_bf[...], preferred_element_type=F32) + bgu_ref[0]
        gate = jnp.minimum(gu[:, :D_FF], SWIGLU_LIMIT)
        up = jnp.clip(gu[:, D_FF:], -SWIGLU_LIMIT, SWIGLU_LIMIT)
        act = (up + 1.0) * (gate * (1.0 / (1.0 + jnp.exp(-SWIGLU_ALPHA * gate))))
        out = jnp.dot(act.astype(BF16), wd_bf[...], preferred_element_type=F32) + bd_ref[0]
        row = lax.broadcasted_iota(jnp.int32, (tm, 1), 0)
        mine = (row >= lo) & (row < hi)

        @pl.when((flag & 1) != 0)
        def _():
            _store_row_tiles(o_ref, jnp.where(mine, out, 0.0))

        @pl.when((flag & 1) == 0)
        def _():
            _store_row_tiles(o_ref, jnp.where(mine, out, _load_row_tiles(o_ref, tm)))


def _expert_call(meta, xs, wgu, bgu, wd, bd):
    lines, w = xs.shape
    tm = TM_EXP
    n_items = meta[0].shape[0]
    return pl.pallas_call(
        _expert_kernel,
        grid_spec=pltpu.PrefetchScalarGridSpec(
            num_scalar_prefetch=5,
            grid=(n_items,),
            in_specs=[
                pl.BlockSpec((tm * ROW_LINES, w), lambda i, t, e, lo, hi, f: (t[i], 0)),
                pl.BlockSpec((1, D_MODEL, 2 * D_FF), lambda i, t, e, lo, hi, f: (e[i], 0, 0)),
                pl.BlockSpec((1, 1, 2 * D_FF), lambda i, t, e, lo, hi, f: (e[i], 0, 0)),
                pl.BlockSpec((1, D_FF, D_MODEL), lambda i, t, e, lo, hi, f: (e[i], 0, 0)),
                pl.BlockSpec((1, 1, D_MODEL), lambda i, t, e, lo, hi, f: (e[i], 0, 0)),
            ],
            out_specs=pl.BlockSpec((tm * ROW_LINES, w), lambda i, t, e, lo, hi, f: (t[i], 0)),
            scratch_shapes=[pltpu.VMEM((D_MODEL, 2 * D_FF), BF16),
                            pltpu.VMEM((D_FF, D_MODEL), BF16)],
        ),
        out_shape=jax.ShapeDtypeStruct((lines, w), F32),
        compiler_params=pltpu.CompilerParams(
            dimension_semantics=("arbitrary",), vmem_limit_bytes=VMEM_LIMIT),
        name="moe_experts",
    )(*meta, xs, wgu, bgu, wd, bd)


def _expert_items(counts, rows, tm):
    n_tiles = rows // tm
    n_items = n_tiles + N_EXPERTS - 1
    ends = jnp.cumsum(counts)
    starts = ends - counts
    first_tile = starts // tm
    last_tile = jnp.maximum(ends - 1, 0) // tm
    per_exp = jnp.where(counts > 0, last_tile - first_tile + 1, 0)
    item_end = jnp.cumsum(per_exp)
    item_start = item_end - per_exp
    total = item_end[-1]
    i = jnp.arange(n_items, dtype=jnp.int32)
    valid = i < total
    e = jnp.sum((item_end[None, :] <= jnp.minimum(i, total - 1)[:, None]).astype(jnp.int32), axis=1)
    hot = e[:, None] == jnp.arange(N_EXPERTS, dtype=jnp.int32)[None, :]
    pick = lambda tab: jnp.sum(jnp.where(hot, tab[None, :], 0), axis=1)
    e_start, e_end, e_item0 = pick(starts), pick(ends), pick(item_start)
    tile = jnp.where(valid, e_start // tm + (i - e_item0), n_tiles - 1)
    lo = jnp.where(valid, jnp.maximum(e_start, tile * tm) - tile * tm, 0)
    hi = jnp.where(valid, jnp.minimum(e_end, (tile + 1) * tm) - tile * tm, 0)
    flag = jnp.where(valid, (lo == 0).astype(jnp.int32) + 2 * (i == e_item0), 0)
    cast = lambda a: a.astype(jnp.int32)
    return (cast(tile), cast(e), cast(lo), cast(hi), cast(flag)), starts


def _combine_kernel(pos_ref, h1_ref, gate_ref, ys_ref, o_ref, buf, sem):
    tb = h1_ref.shape[0]

    def issue(t, c):
        for k in range(TOP_K):
            p = pos_ref[t * TOP_K + k]
            pltpu.make_async_copy(_row_tile(ys_ref, p), _row_tile(buf.at[k], t),
                                  sem).start(priority=k % 2)
        return c

    lax.fori_loop(0, tb, issue, 0, unroll=4)
    for k in range(TOP_K):
        pltpu.make_async_copy(ys_ref.at[pl.ds(0, tb * ROW_LINES), :], buf.at[k], sem).wait()
    g = gate_ref[...]
    acc = h1_ref[...]
    for k in range(TOP_K):
        acc = acc + g[:, k:k + 1] * _load_row_tiles(buf, tb, lead=(k,))
    o_ref[...] = acc


def _combine_call(pos_flat, h1, gates, ys):
    n, d = h1.shape
    tb = TB_ROWS
    return pl.pallas_call(
        _combine_kernel,
        grid=(n // tb,),
        in_specs=[pl.BlockSpec((tb * TOP_K,), lambda i: (i,), memory_space=pltpu.SMEM),
                  pl.BlockSpec((tb, d), lambda i: (i, 0)),
                  pl.BlockSpec((tb, TOP_K), lambda i: (i, 0)),
                  pl.BlockSpec(memory_space=pl.ANY)],
        out_specs=pl.BlockSpec((tb, d), lambda i: (i, 0)),
        out_shape=jax.ShapeDtypeStruct((n, d), F32),
        scratch_shapes=[pltpu.VMEM((TOP_K, tb * ROW_LINES, 128), F32), pltpu.SemaphoreType.DMA],
        compiler_params=pltpu.CompilerParams(
            dimension_semantics=("arbitrary",), vmem_limit_bytes=VMEM_LIMIT),
        name="moe_combine",
    )(pos_flat, h1, gates, ys)


def _rotate_half_perm():
    half = QK_ROPE // 2
    perm = np.concatenate([np.arange(half, QK_ROPE), np.arange(0, half)])
    sign = np.concatenate([-np.ones(half), np.ones(half)]).astype(np.float32)
    return perm, sign


def _layer(h, l, norm_mix, w_in, conv_w, q_norm, w_uq, kv_norm, w_ukv, q_head_norm,
           k_head_norm, w_o, norm_ffn, w_router, b_router, w_gate_up, b_gate_up,
           w_down, b_down):
    b, s, d = h.shape
    n = b * s
    perm, sign = _rotate_half_perm()

    kpe_cols = w_in[l][:, OFF_KPE:OFF_KPE + QK_ROPE]
    win = jnp.concatenate([w_in[l], kpe_cols[:, perm] * sign], axis=1).astype(BF16)
    wq = w_uq[l].reshape(Q_LORA, MLA_HEADS, QK_HEAD)
    wq_rope = wq[:, :, QK_NOPE:]
    wuq = jnp.concatenate(
        [wq[:, :, :QK_NOPE].reshape(Q_LORA, MLA_HEADS * QK_NOPE),
         jnp.concatenate([wq_rope, wq_rope[:, :, perm] * sign], axis=2)
         .reshape(Q_LORA, MLA_HEADS * 2 * QK_ROPE)], axis=1).astype(BF16)
    gq, gk = q_head_norm[l], k_head_norm[l]
    gqn, gkn = gq[None, :QK_NOPE], gk[None, :QK_NOPE]
    gqr = jnp.concatenate([gq[QK_NOPE:], gq[QK_NOPE:][perm]])[None, :]
    gkr = jnp.concatenate([gk[QK_NOPE:], gk[QK_NOPE:][perm]])[None, :]

    half = QK_ROPE // 2
    inv_freq = ROPE_THETA ** (-jnp.arange(half, dtype=F32) / half)
    ang = jnp.arange(s).astype(F32)[:, None] * inv_freq[None, :]
    cos, sin = jnp.cos(ang), jnp.sin(ang)
    tbl = jnp.concatenate([cos, cos, sin, sin], axis=1)

    conv, q, k, vt = _pre_call(
        h, norm_mix[l][None, :], win, conv_w[l], q_norm[l][None, :], wuq,
        kv_norm[l][None, :], w_ukv[l].astype(BF16), gqn, gqr, gkn, gkr, tbl)
    attn = _attn_call(q, k, vt)

    tri = jnp.tril(jnp.ones((TM_POST, TM_POST), BF16), -1)
    h1, hp, idx, rank, gates, counts = _post_call(
        h.reshape(n, d), conv.reshape(n, CONV_WIDTH), attn.reshape(n, CONV_WIDTH),
        w_o[l].astype(BF16), norm_ffn[l][None, :], w_router[l].astype(BF16),
        b_router[l][None, :], tri)

    meta, starts = _expert_items(counts[0], n * TOP_K, TM_EXP)
    idx_hot = idx[:, :, None] == jnp.arange(N_EXPERTS, dtype=jnp.int32)[None, None, :]
    group_start = jnp.sum(jnp.where(idx_hot, starts[None, None, :], 0), axis=-1)
    pos_flat = (group_start + rank).reshape(n * TOP_K).astype(jnp.int32)
    xs = _scatter_call(pos_flat, hp)
    ys = _expert_call(meta, xs, w_gate_up[l], b_gate_up[l][:, None, :],
                      w_down[l], b_down[l][:, None, :])
    out = _combine_call(pos_flat, h1, gates, ys)
    return out.reshape(b, s, d)


def kernel(x, norm_mix, w_in, conv_w, q_norm, w_uq, kv_norm, w_ukv, q_head_norm, k_head_norm,
           w_o, norm_ffn, w_router, b_router, w_gate_up, b_gate_up, w_down, b_down):
    h = x
    for l in range(norm_mix.shape[0]):
        h = _layer(h, l, norm_mix, w_in, conv_w, q_norm, w_uq, kv_norm, w_ukv, q_head_norm,
                   k_head_norm, w_o, norm_ffn, w_router, b_router, w_gate_up, b_gate_up,
                   w_down, b_down)
    return h
```

```python
import math

import jax
import jax.numpy as jnp
import numpy as np
from jax import lax
from jax.experimental import pallas as pl
from jax.experimental.pallas import tpu as pltpu

F32 = jnp.float32
BF16 = jnp.bfloat16

D_MODEL = 1024
CONV_WIDTH = 512
MLA_HEADS = 4
QK_NOPE = 128
QK_ROPE = 64
QK_HEAD = QK_NOPE + QK_ROPE
QK_PAD = 256
V_HEAD = 128
Q_LORA = 384
KV_LORA = 256
ROPE_THETA = 10000.0
CHUNK = 64
N_EXPERTS = 32
TOP_K = 4
D_FF = 1024
SWIGLU_LIMIT = 7.0
SWIGLU_ALPHA = 1.702
EPS = 1e-6

OFF_B, OFF_C, OFF_X, OFF_CQ, OFF_CKV, OFF_KPE = 0, 512, 1024, 1536, 1920, 2176
IN_PROJ_EXT = 2304

VMEM_LIMIT = 56 * 1024 * 1024

TS_PRE = 512
TQ = 256
TM_POST = 512
TB_ROWS = 256
TM_EXP = 512
HALO = 8


def _rms(x, g):
    return x * lax.rsqrt(jnp.mean(x * x, axis=-1, keepdims=True) + EPS) * g


ROW_LINES = D_MODEL // 128


def _load_row_tiles(ref, rows, lead=()):
    return jnp.concatenate(
        [ref[lead + (pl.ds(c, rows, stride=ROW_LINES), slice(None))] for c in range(ROW_LINES)],
        axis=1)


def _store_row_tiles(ref, val):
    rows = val.shape[0]
    for c in range(ROW_LINES):
        ref[pl.ds(c, rows, stride=ROW_LINES), :] = val[:, 128 * c:128 * (c + 1)]


def _pre_kernel(x_ref, nm_ref, win_ref, convw_ref, qn_ref, wuq_ref, kvn_ref, wukv_ref,
                gqn_ref, gqr_ref, gkn_ref, gkr_ref, tbl_ref,
                conv_ref, q_ref, k_ref, vt_ref, ext_ref):
    s = pl.program_id(1)
    ts = x_ref.shape[1]
    sub = TQ
    starts = tuple(range(0, ts, sub))

    @pl.when(s == 0)
    def _():
        ext_ref[0:HALO, :] = jnp.zeros((HALO, CONV_WIDTH), F32)

    @pl.when(s != 0)
    def _():
        ext_ref[0:HALO, :] = ext_ref[ts:ts + HALO, :]

    projs = [jnp.dot(_rms(x_ref[0, r0:r0 + sub, :], nm_ref[...]).astype(BF16), win_ref[...],
                     preferred_element_type=F32) for r0 in starts]

    lane = lax.broadcasted_iota(jnp.int32, (sub, 128), 1)
    first_half = lane < QK_ROPE
    scale = 1.0 / math.sqrt(QK_HEAD)

    for r0, proj in zip(starts, projs):
        rows = slice(r0, r0 + sub)
        u = proj[:, OFF_C:OFF_C + CONV_WIDTH] * proj[:, OFF_X:OFF_X + CONV_WIDTH]
        ext_ref[HALO + r0:HALO + r0 + sub, :] = u
        y = (convw_ref[2:3, :] * u
             + convw_ref[1:2, :] * ext_ref[HALO + r0 - 1:HALO + r0 - 1 + sub, :]
             + convw_ref[0:1, :] * ext_ref[HALO + r0 - 2:HALO + r0 - 2 + sub, :])
        conv_ref[0, rows, :] = (proj[:, OFF_B:OFF_B + CONV_WIDTH] * y).astype(BF16)

        hq = _rms(proj[:, OFF_CQ:OFF_CQ + Q_LORA], qn_ref[...]).astype(BF16)
        qp = jnp.dot(hq, wuq_ref[...], preferred_element_type=F32)
        hkv = _rms(proj[:, OFF_CKV:OFF_CKV + KV_LORA], kvn_ref[...]).astype(BF16)
        kvp = jnp.dot(hkv, wukv_ref[...], preferred_element_type=F32)

        tbl = tbl_ref[rows, :]

        def rope_block(grp, gain):
            r = grp * tbl * gain
            r = r + pltpu.roll(r, QK_ROPE, axis=1)
            return jnp.where(first_half, r, 0.0)

        kpe = proj[:, OFF_KPE:OFF_KPE + 128]
        ss_kpe = jnp.sum(jnp.where(first_half, kpe * kpe, 0.0), axis=-1, keepdims=True)
        k_rope = rope_block(kpe, gkr_ref[...])
        for h in range(MLA_HEADS):
            qn = qp[:, 128 * h:128 * (h + 1)]
            qg = qp[:, 512 + 128 * h:512 + 128 * (h + 1)]
            ss = (jnp.sum(qn * qn, axis=-1, keepdims=True)
                  + jnp.sum(jnp.where(first_half, qg * qg, 0.0), axis=-1, keepdims=True))
            rs = lax.rsqrt(ss * (1.0 / QK_HEAD) + EPS) * scale
            q_ref[0, h, rows, 0:128] = (qn * rs * gqn_ref[...]).astype(BF16)
            q_ref[0, h, rows, 128:256] = (rope_block(qg, gqr_ref[...]) * rs).astype(BF16)

            kn = kvp[:, 256 * h:256 * h + 128]
            ssk = jnp.sum(kn * kn, axis=-1, keepdims=True) + ss_kpe
            rsk = lax.rsqrt(ssk * (1.0 / QK_HEAD) + EPS)
            k_ref[0, h, rows, 0:128] = (kn * rsk * gkn_ref[...]).astype(BF16)
            k_ref[0, h, rows, 128:256] = (k_rope * rsk).astype(BF16)
            v = kvp[:, 256 * h + 128:256 * (h + 1)]
            vt_ref[0, h, r0 // TQ] = v.T.astype(BF16)


def _pre_call(x, nm, win, convw, qn, wuq, kvn, wukv, gqn, gqr, gkn, gkr, tbl):
    b, s, d = x.shape
    ts = TS_PRE
    const = lambda shape: pl.BlockSpec(shape, lambda bi, si: (0,) * len(shape))
    return pl.pallas_call(
        _pre_kernel,
        grid=(b, s // ts),
        in_specs=[
            pl.BlockSpec((1, ts, d), lambda bi, si: (bi, si, 0)),
            const((1, d)), const((d, IN_PROJ_EXT)), const((3, CONV_WIDTH)),
            const((1, Q_LORA)), const((Q_LORA, 1024)),
            const((1, KV_LORA)), const((KV_LORA, 1024)),
            const((1, 128)), const((1, 128)), const((1, 128)), const((1, 128)),
            pl.BlockSpec((ts, 128), lambda bi, si: (si, 0)),
        ],
        out_specs=[
            pl.BlockSpec((1, ts, CONV_WIDTH), lambda bi, si: (bi, si, 0)),
            pl.BlockSpec((1, MLA_HEADS, ts, QK_PAD), lambda bi, si: (bi, 0, si, 0)),
            pl.BlockSpec((1, MLA_HEADS, ts, QK_PAD), lambda bi, si: (bi, 0, si, 0)),
            pl.BlockSpec((1, MLA_HEADS, ts // TQ, V_HEAD, TQ), lambda bi, si: (bi, 0, si, 0, 0)),
        ],
        out_shape=[
            jax.ShapeDtypeStruct((b, s, CONV_WIDTH), BF16),
            jax.ShapeDtypeStruct((b, MLA_HEADS, s, QK_PAD), BF16),
            jax.ShapeDtypeStruct((b, MLA_HEADS, s, QK_PAD), BF16),
            jax.ShapeDtypeStruct((b, MLA_HEADS, s // TQ, V_HEAD, TQ), BF16),
        ],
        scratch_shapes=[pltpu.VMEM((ts + HALO, CONV_WIDTH), F32)],
        compiler_params=pltpu.CompilerParams(
            dimension_semantics=("arbitrary", "arbitrary"), vmem_limit_bytes=VMEM_LIMIT),
        name="pre_mixer",
    )(x, nm, win, convw, qn, wuq, kvn, wukv, gqn, gqr, gkn, gkr, tbl)


def _attn_kernel(q_ref, k_ref, vt_ref, o_ref, m_ref, l_ref, acc_ref):
    i = pl.program_id(1)
    tq = q_ref.shape[2]
    key_c = lax.broadcasted_iota(jnp.int32, (tq, tq), 0) // CHUNK
    qry_c = lax.broadcasted_iota(jnp.int32, (tq, tq), 1) // CHUNK
    allowed = key_c <= qry_c
    m_ref[...] = jnp.full(m_ref.shape, -1e30, F32)
    l_ref[...] = jnp.zeros(l_ref.shape, F32)
    acc_ref[...] = jnp.zeros(acc_ref.shape, F32)

    def key_block(j, masked):
        start = pl.multiple_of(j * tq, tq)
        scores = []
        for h in range(MLA_HEADS):
            kb = k_ref[0, h, pl.ds(start, tq), :]
            st = lax.dot_general(kb, q_ref[0, h], (((1,), (1,)), ((), ())),
                                 preferred_element_type=F32)
            scores.append(jnp.where(allowed, st, -1e30) if masked else st)
        probs, alphas = [], []
        for h in range(MLA_HEADS):
            m = m_ref[h]
            m_new = jnp.maximum(m, jnp.max(scores[h], axis=0, keepdims=True))
            alpha = jnp.exp(m - m_new)
            p = jnp.exp(scores[h] - m_new)
            l_ref[h] = alpha * l_ref[h] + jnp.sum(p, axis=0, keepdims=True)
            m_ref[h] = m_new
            probs.append(p.astype(BF16))
            alphas.append(alpha)
        for h in range(MLA_HEADS):
            acc_ref[h] = alphas[h] * acc_ref[h] + jnp.dot(vt_ref[0, h, j], probs[h],
                                                           preferred_element_type=F32)

    def body(j, c):
        key_block(j, False)
        return c

    lax.fori_loop(0, i, body, 0)
    key_block(i, True)
    for h in range(MLA_HEADS):
        o = acc_ref[h] / l_ref[h]
        o_ref[0, :, V_HEAD * h:V_HEAD * (h + 1)] = o.T.astype(BF16)


def _attn_call(q, k, vt):
    b, nh, s, _ = q.shape
    return pl.pallas_call(
        _attn_kernel,
        grid=(b, s // TQ),
        in_specs=[
            pl.BlockSpec((1, nh, TQ, QK_PAD), lambda bi, qi: (bi, 0, qi, 0)),
            pl.BlockSpec((1, nh, s, QK_PAD), lambda bi, qi: (bi, 0, 0, 0)),
            pl.BlockSpec((1, nh, s // TQ, V_HEAD, TQ), lambda bi, qi: (bi, 0, 0, 0, 0)),
        ],
        out_specs=pl.BlockSpec((1, TQ, nh * V_HEAD), lambda bi, qi: (bi, qi, 0)),
        out_shape=jax.ShapeDtypeStruct((b, s, nh * V_HEAD), BF16),
        scratch_shapes=[pltpu.VMEM((nh, 1, TQ), F32), pltpu.VMEM((nh, 1, TQ), F32),
                        pltpu.VMEM((nh, V_HEAD, TQ), F32)],
        compiler_params=pltpu.CompilerParams(
            dimension_semantics=("arbitrary", "arbitrary"), vmem_limit_bytes=VMEM_LIMIT),
        name="mla_attention",
    )(q, k, vt)


def _post_kernel(x_ref, conv_ref, attn_ref, wo_ref, g_ref, wr_ref, br_ref, tri_ref,
                 h1_ref, hb_ref, lpos_ref, gate_ref, tile_n_ref, tile_carry_ref, cnt_ref,
                 carry_ref):
    step = pl.program_id(0)
    tm = x_ref.shape[0]

    @pl.when(step == 0)
    def _():
        carry_ref[...] = jnp.zeros_like(carry_ref)

    h1 = (x_ref[...]
          + jnp.dot(conv_ref[...], wo_ref[0:CONV_WIDTH, :], preferred_element_type=F32)
          + jnp.dot(attn_ref[...], wo_ref[CONV_WIDTH:, :], preferred_element_type=F32))
    h1_ref[...] = h1
    hb = _rms(h1, g_ref[...]).astype(BF16)
    hb_ref[...] = hb

    logits = jnp.dot(hb, wr_ref[...], preferred_element_type=F32) + br_ref[...]
    lane = lax.broadcasted_iota(jnp.int32, (tm, N_EXPERTS), 1).astype(F32)
    lane4 = lax.broadcasted_iota(jnp.int32, (tm, TOP_K), 1)
    member = jnp.zeros((tm, N_EXPERTS), F32)
    vals, idxs, hots = [], [], []
    lg = logits
    for _ in range(TOP_K):
        m = jnp.max(lg, axis=-1, keepdims=True)
        ix = jnp.min(jnp.where(lg == m, lane, float(N_EXPERTS)), axis=-1, keepdims=True)
        hot = lane == ix
        lg = jnp.where(hot, -jnp.inf, lg)
        member = member + hot.astype(F32)
        vals.append(m)
        idxs.append(ix)
        hots.append(hot)

    before = jnp.dot(tri_ref[...], member.astype(BF16), preferred_element_type=F32)
    tile_n = jnp.sum(member, axis=0, keepdims=True)
    lpos = [jnp.sum(jnp.where(hot, before, 0.0) + jnp.where(lane < ix, tile_n, 0.0),
                    axis=-1, keepdims=True) for hot, ix in zip(hots, idxs)]
    tile_n_ref[0] = tile_n.astype(jnp.int32)
    tile_carry_ref[0] = carry_ref[...].astype(jnp.int32)
    carry_ref[...] = carry_ref[...] + tile_n
    cnt_ref[...] = carry_ref[...].astype(jnp.int32)

    exps = [jnp.exp(v - vals[0]) for v in vals]
    denom = exps[0] + exps[1] + exps[2] + exps[3]

    def cols(c):
        return jnp.where(lane4 == 0, c[0], jnp.where(lane4 == 1, c[1],
                         jnp.where(lane4 == 2, c[2], c[3])))

    lpos_ref[...] = cols(lpos).astype(jnp.int32)
    gate_ref[...] = cols([e / denom for e in exps])


def _post_call(xf, conv, attn, wo, g, wr, br, tri):
    n, d = xf.shape
    tm = TM_POST
    const = lambda shape: pl.BlockSpec(shape, lambda i: (0,) * len(shape))
    row = lambda w: pl.BlockSpec((tm, w), lambda i: (i, 0))
    return pl.pallas_call(
        _post_kernel,
        grid=(n // tm,),
        in_specs=[row(d), row(CONV_WIDTH), row(CONV_WIDTH), const((d, d)), const((1, d)),
                  const((d, N_EXPERTS)), const((1, N_EXPERTS)), const((tm, tm))],
        out_specs=[row(d), row(d), row(TOP_K), row(TOP_K),
                   pl.BlockSpec((1, 1, N_EXPERTS), lambda i: (i, 0, 0)),
                   pl.BlockSpec((1, 1, N_EXPERTS), lambda i: (i, 0, 0)),
                   const((1, N_EXPERTS))],
        out_shape=[
            jax.ShapeDtypeStruct((n, d), F32),
            jax.ShapeDtypeStruct((n, d), BF16),
            jax.ShapeDtypeStruct((n, TOP_K), jnp.int32),
            jax.ShapeDtypeStruct((n, TOP_K), F32),
            jax.ShapeDtypeStruct((n // tm, 1, N_EXPERTS), jnp.int32),
            jax.ShapeDtypeStruct((n // tm, 1, N_EXPERTS), jnp.int32),
            jax.ShapeDtypeStruct((1, N_EXPERTS), jnp.int32),
        ],
        scratch_shapes=[pltpu.VMEM((1, N_EXPERTS), F32)],
        compiler_params=pltpu.CompilerParams(
            dimension_semantics=("arbitrary",), vmem_limit_bytes=VMEM_LIMIT),
        name="post_mixer_router",
    )(xf, conv, attn, wo, g, wr, br, tri)


RUN_BITS = 10


def _rows(ref, first_row, n_rows):
    return ref.at[pl.ds(pl.multiple_of(first_row * ROW_LINES, ROW_LINES), n_rows * ROW_LINES), :]


def _for_each_run_piece(n_ref, far_ref, fn):
    def per_expert(e, local0):
        n_e = n_ref[0, 0, e]
        far0 = far_ref[0, 0, e]
        for b in reversed(range(RUN_BITS)):
            done = (n_e >> (b + 1)) << (b + 1)

            @pl.when(((n_e >> b) & 1) == 1)
            def _():
                fn(local0 + done, far0 + done, 1 << b)
        return local0 + n_e

    lax.fori_loop(0, N_EXPERTS, per_expert, 0)


def _scatter_kernel(n_ref, far_ref, hb_ref, lpos_ref, xs_ref, xl, sem):
    i = pl.program_id(0)
    steps = pl.num_programs(0)
    tm = hb_ref.shape[0]
    pairs = tm * TOP_K
    slot = i % 2

    def wait_slot(s):
        pltpu.make_async_copy(xl.at[s], _rows(xs_ref, 0, pairs), sem.at[s]).wait()

    @pl.when(i >= 2)
    def _():
        wait_slot(slot)

    lane = lax.broadcasted_iota(jnp.int32, (tm, 128), 1)
    lp = lpos_ref[...].astype(F32)
    cols = jnp.zeros((tm, 128), F32)
    for k in range(TOP_K):
        cols = jnp.where(lane == k, lp[:, k:k + 1], cols)
    rows_t = cols.T
    j = lax.broadcasted_iota(jnp.int32, (pairs, tm), 0).astype(F32)
    hit = j == rows_t[0:1, :]
    for k in range(1, TOP_K):
        hit = hit | (j == rows_t[k:k + 1, :])
    onehot = jnp.where(hit, 1.0, 0.0).astype(BF16)
    _store_row_tiles(xl.at[slot], jnp.dot(onehot, hb_ref[...], preferred_element_type=F32))

    def send(local_row, far_row, size):
        pltpu.make_async_copy(_rows(xl.at[slot], local_row, size), _rows(xs_ref, far_row, size),
                              sem.at[slot]).start()

    _for_each_run_piece(n_ref, far_ref, send)

    @pl.when(i == steps - 1)
    def _():
        @pl.when(steps >= 2)
        def _():
            wait_slot(1 - slot)
        wait_slot(slot)


def _scatter_call(tile_n, tile_far, hb, lpos):
    n, d = hb.shape
    tm = TM_POST
    table = pl.BlockSpec((1, 1, N_EXPERTS), lambda i: (i, 0, 0), memory_space=pltpu.SMEM)
    return pl.pallas_call(
        _scatter_kernel,
        grid=(n // tm,),
        in_specs=[table, table,
                  pl.BlockSpec((tm, d), lambda i: (i, 0)),
                  pl.BlockSpec((tm, TOP_K), lambda i: (i, 0))],
        out_specs=pl.BlockSpec(memory_space=pl.ANY),
        out_shape=jax.ShapeDtypeStruct((n * TOP_K * ROW_LINES, 128), F32),
        scratch_shapes=[pltpu.VMEM((2, tm * TOP_K * ROW_LINES, 128), F32),
                        pltpu.SemaphoreType.DMA((2,))],
        compiler_params=pltpu.CompilerParams(
            dimension_semantics=("arbitrary",), vmem_limit_bytes=VMEM_LIMIT),
        name="dispatch_scatter",
    )(tile_n, tile_far, hb, lpos)


def _expert_kernel(tile_ref, exp_ref, lo_ref, hi_ref, flag_ref,
                   xs_ref, wgu_ref, bgu_ref, wd_ref, bd_ref, o_ref, wgu_bf, wd_bf):
    i = pl.program_id(0)
    tm = xs_ref.shape[0] // ROW_LINES
    lo, hi, flag = lo_ref[i], hi_ref[i], flag_ref[i]
    cast_rows = 128

    @pl.when((flag & 2) != 0)
    def _():
        def cast(r, c):
            rows = pl.ds(pl.multiple_of(r * cast_rows, cast_rows), cast_rows)
            wgu_bf[rows, :] = wgu_ref[0, rows, :].astype(BF16)
            wd_bf[rows, :] = wd_ref[0, rows, :].astype(BF16)
            return c
        lax.fori_loop(0, D_MODEL // cast_rows, cast, 0)

    @pl.when(hi > lo)
    def _():
        x = _load_row_tiles(xs_ref, tm).astype(BF16)

        gu = jnp.dot(x, wgu_bf[...], preferred_element_type=F32) + bgu_ref[0]
        gate = jnp.minimum(gu[:, :D_FF], SWIGLU_LIMIT)
        up = jnp.clip(gu[:, D_FF:], -SWIGLU_LIMIT, SWIGLU_LIMIT)
        act = (up + 1.0) * (gate * (1.0 / (1.0 + jnp.exp(-SWIGLU_ALPHA * gate))))
        out = jnp.dot(act.astype(BF16), wd_bf[...], preferred_element_type=F32) + bd_ref[0]
        row = lax.broadcasted_iota(jnp.int32, (tm, 1), 0)
        mine = (row >= lo) & (row < hi)

        @pl.when((flag & 1) != 0)
        def _():
            _store_row_tiles(o_ref, jnp.where(mine, out, 0.0))

        @pl.when((flag & 1) == 0)
        def _():
            _store_row_tiles(o_ref, jnp.where(mine, out, _load_row_tiles(o_ref, tm)))


def _expert_call(meta, xs, wgu, bgu, wd, bd):
    lines, w = xs.shape
    tm = TM_EXP
    n_items = meta[0].shape[0]
    return pl.pallas_call(
        _expert_kernel,
        grid_spec=pltpu.PrefetchScalarGridSpec(
            num_scalar_prefetch=5,
            grid=(n_items,),
            in_specs=[
                pl.BlockSpec((tm * ROW_LINES, w), lambda i, t, e, lo, hi, f: (t[i], 0)),
                pl.BlockSpec((1, D_MODEL, 2 * D_FF), lambda i, t, e, lo, hi, f: (e[i], 0, 0)),
                pl.BlockSpec((1, 1, 2 * D_FF), lambda i, t, e, lo, hi, f: (e[i], 0, 0)),
                pl.BlockSpec((1, D_FF, D_MODEL), lambda i, t, e, lo, hi, f: (e[i], 0, 0)),
                pl.BlockSpec((1, 1, D_MODEL), lambda i, t, e, lo, hi, f: (e[i], 0, 0)),
            ],
            out_specs=pl.BlockSpec((tm * ROW_LINES, w), lambda i, t, e, lo, hi, f: (t[i], 0)),
            scratch_shapes=[pltpu.VMEM((D_MODEL, 2 * D_FF), BF16),
                            pltpu.VMEM((D_FF, D_MODEL), BF16)],
        ),
        out_shape=jax.ShapeDtypeStruct((lines, w), F32),
        compiler_params=pltpu.CompilerParams(
            dimension_semantics=("arbitrary",), vmem_limit_bytes=VMEM_LIMIT),
        name="moe_experts",
    )(*meta, xs, wgu, bgu, wd, bd)


def _expert_items(counts, rows, tm):
    n_tiles = rows // tm
    n_items = n_tiles + N_EXPERTS - 1
    ends = jnp.cumsum(counts)
    starts = ends - counts
    first_tile = starts // tm
    last_tile = jnp.maximum(ends - 1, 0) // tm
    per_exp = jnp.where(counts > 0, last_tile - first_tile + 1, 0)
    item_end = jnp.cumsum(per_exp)
    item_start = item_end - per_exp
    total = item_end[-1]
    i = jnp.arange(n_items, dtype=jnp.int32)
    valid = i < total
    e = jnp.sum((item_end[None, :] <= jnp.minimum(i, total - 1)[:, None]).astype(jnp.int32), axis=1)
    hot = e[:, None] == jnp.arange(N_EXPERTS, dtype=jnp.int32)[None, :]
    pick = lambda tab: jnp.sum(jnp.where(hot, tab[None, :], 0), axis=1)
    e_start, e_end, e_item0 = pick(starts), pick(ends), pick(item_start)
    tile = jnp.where(valid, e_start // tm + (i - e_item0), n_tiles - 1)
    lo = jnp.where(valid, jnp.maximum(e_start, tile * tm) - tile * tm, 0)
    hi = jnp.where(valid, jnp.minimum(e_end, (tile + 1) * tm) - tile * tm, 0)
    flag = jnp.where(valid, (lo == 0).astype(jnp.int32) + 2 * (i == e_item0), 0)
    cast = lambda a: a.astype(jnp.int32)
    return (cast(tile), cast(e), cast(lo), cast(hi), cast(flag)), starts


def _combine_kernel(n_ref, far_ref, n_next_ref, far_next_ref, h1_ref, lpos_ref, gate_ref, ys_ref,
                    o_ref, yl, sem):
    i = pl.program_id(0)
    steps = pl.num_programs(0)
    tm = h1_ref.shape[0]
    pairs = tm * TOP_K
    slot = i % 2

    def fetch(nr, fr, s):
        def recv(local_row, far_row, size):
            pltpu.make_async_copy(_rows(ys_ref, far_row, size), _rows(yl.at[s], local_row, size),
                                  sem.at[s]).start()
        _for_each_run_piece(nr, fr, recv)

    @pl.when(i == 0)
    def _():
        fetch(n_ref, far_ref, 0)

    @pl.when(i + 1 < steps)
    def _():
        fetch(n_next_ref, far_next_ref, 1 - slot)

    pltpu.make_async_copy(_rows(ys_ref, 0, pairs), yl.at[slot], sem.at[slot]).wait()

    j = lax.broadcasted_iota(jnp.int32, (tm, pairs), 1)
    lp = lpos_ref[...]
    g = gate_ref[...]
    w = jnp.zeros((tm, pairs), F32)
    for k in range(TOP_K):
        w = jnp.where(j == lp[:, k:k + 1], g[:, k:k + 1], w)
    y = _load_row_tiles(yl, pairs, lead=(slot,)).astype(BF16)
    o_ref[...] = h1_ref[...] + jnp.dot(w.astype(BF16), y, preferred_element_type=F32)


def _combine_call(tile_n, tile_far, h1, lpos, gates, ys):
    n, d = h1.shape
    tm = TM_POST
    steps = n // tm
    table = lambda fn: pl.BlockSpec((1, 1, N_EXPERTS), fn, memory_space=pltpu.SMEM)
    cur = lambda i: (i, 0, 0)
    nxt = lambda i: (jnp.minimum(i + 1, steps - 1), 0, 0)
    return pl.pallas_call(
        _combine_kernel,
        grid=(steps,),
        in_specs=[table(cur), table(cur), table(nxt), table(nxt),
                  pl.BlockSpec((tm, d), lambda i: (i, 0)),
                  pl.BlockSpec((tm, TOP_K), lambda i: (i, 0)),
                  pl.BlockSpec((tm, TOP_K), lambda i: (i, 0)),
                  pl.BlockSpec(memory_space=pl.ANY)],
        out_specs=pl.BlockSpec((tm, d), lambda i: (i, 0)),
        out_shape=jax.ShapeDtypeStruct((n, d), F32),
        scratch_shapes=[pltpu.VMEM((2, tm * TOP_K * ROW_LINES, 128), F32),
                        pltpu.SemaphoreType.DMA((2,))],
        compiler_params=pltpu.CompilerParams(
            dimension_semantics=("arbitrary",), vmem_limit_bytes=VMEM_LIMIT),
        name="moe_combine",
    )(tile_n, tile_far, tile_n, tile_far, h1, lpos, gates, ys)


def _rotate_half_perm():
    half = QK_ROPE // 2
    perm = np.concatenate([np.arange(half, QK_ROPE), np.arange(0, half)])
    sign = np.concatenate([-np.ones(half), np.ones(half)]).astype(np.float32)
    return perm, sign


def _layer(h, l, norm_mix, w_in, conv_w, q_norm, w_uq, kv_norm, w_ukv, q_head_norm,
           k_head_norm, w_o, norm_ffn, w_router, b_router, w_gate_up, b_gate_up,
           w_down, b_down):
    b, s, d = h.shape
    n = b * s
    perm, sign = _rotate_half_perm()

    kpe_cols = w_in[l][:, OFF_KPE:OFF_KPE + QK_ROPE]
    win = jnp.concatenate([w_in[l], kpe_cols[:, perm] * sign], axis=1).astype(BF16)
    wq = w_uq[l].reshape(Q_LORA, MLA_HEADS, QK_HEAD)
    wq_rope = wq[:, :, QK_NOPE:]
    wuq = jnp.concatenate(
        [wq[:, :, :QK_NOPE].reshape(Q_LORA, MLA_HEADS * QK_NOPE),
         jnp.concatenate([wq_rope, wq_rope[:, :, perm] * sign], axis=2)
         .reshape(Q_LORA, MLA_HEADS * 2 * QK_ROPE)], axis=1).astype(BF16)
    gq, gk = q_head_norm[l], k_head_norm[l]
    gqn, gkn = gq[None, :QK_NOPE], gk[None, :QK_NOPE]
    gqr = jnp.concatenate([gq[QK_NOPE:], gq[QK_NOPE:][perm]])[None, :]
    gkr = jnp.concatenate([gk[QK_NOPE:], gk[QK_NOPE:][perm]])[None, :]

    half = QK_ROPE // 2
    inv_freq = ROPE_THETA ** (-jnp.arange(half, dtype=F32) / half)
    ang = jnp.arange(s).astype(F32)[:, None] * inv_freq[None, :]
    cos, sin = jnp.cos(ang), jnp.sin(ang)
    tbl = jnp.concatenate([cos, cos, sin, sin], axis=1)

    conv, q, k, vt = _pre_call(
        h, norm_mix[l][None, :], win, conv_w[l], q_norm[l][None, :], wuq,
        kv_norm[l][None, :], w_ukv[l].astype(BF16), gqn, gqr, gkn, gkr, tbl)
    attn = _attn_call(q, k, vt)

    tri = jnp.tril(jnp.ones((TM_POST, TM_POST), BF16), -1)
    h1, hb, lpos, gates, tile_n, tile_carry, counts = _post_call(
        h.reshape(n, d), conv.reshape(n, CONV_WIDTH), attn.reshape(n, CONV_WIDTH),
        w_o[l].astype(BF16), norm_ffn[l][None, :], w_router[l].astype(BF16),
        b_router[l][None, :], tri)

    meta, starts = _expert_items(counts[0], n * TOP_K, TM_EXP)
    tile_far = (tile_carry + starts[None, None, :]).astype(jnp.int32)
    xs = _scatter_call(tile_n, tile_far, hb, lpos)
    ys = _expert_call(meta, xs, w_gate_up[l], b_gate_up[l][:, None, :],
                      w_down[l], b_down[l][:, None, :])
    out = _combine_call(tile_n, tile_far, h1, lpos, gates, ys)
    return out.reshape(b, s, d)


def kernel(x, norm_mix, w_in, conv_w, q_norm, w_uq, kv_norm, w_ukv, q_head_norm, k_head_norm,
           w_o, norm_ffn, w_router, b_router, w_gate_up, b_gate_up, w_down, b_down):
    h = x
    for l in range(norm_mix.shape[0]):
        h = _layer(h, l, norm_mix, w_in, conv_w, q_norm, w_uq, kv_norm, w_ukv, q_head_norm,
                   k_head_norm, w_o, norm_ffn, w_router, b_router, w_gate_up, b_gate_up,
                   w_down, b_down)
    return h
```

```python
import math

import jax
import jax.numpy as jnp
import numpy as np
from jax import lax
from jax.experimental import pallas as pl
from jax.experimental.pallas import tpu as pltpu

F32 = jnp.float32
BF16 = jnp.bfloat16

D_MODEL = 1024
CONV_WIDTH = 512
MLA_HEADS = 4
QK_NOPE = 128
QK_ROPE = 64
QK_HEAD = QK_NOPE + QK_ROPE
QK_PAD = 256
V_HEAD = 128
Q_LORA = 384
KV_LORA = 256
ROPE_THETA = 10000.0
CHUNK = 64
N_EXPERTS = 32
TOP_K = 4
D_FF = 1024
SWIGLU_LIMIT = 7.0
SWIGLU_ALPHA = 1.702
EPS = 1e-6

OFF_B, OFF_C, OFF_X, OFF_CQ, OFF_CKV, OFF_KPE = 0, 512, 1024, 1536, 1920, 2176
IN_PROJ_EXT = 2304

VMEM_LIMIT = 56 * 1024 * 1024

TS_PRE = 512
TQ = 256
TM_POST = 512
TB_ROWS = 256
TM_EXP = 512
HALO = 8


def _rms(x, g):
    return x * lax.rsqrt(jnp.mean(x * x, axis=-1, keepdims=True) + EPS) * g


ROW_LINES = D_MODEL // 128


def _load_row_tiles(ref, rows, lead=()):
    return jnp.concatenate(
        [ref[lead + (pl.ds(c, rows, stride=ROW_LINES), slice(None))] for c in range(ROW_LINES)],
        axis=1)


def _store_row_tiles(ref, val):
    rows = val.shape[0]
    for c in range(ROW_LINES):
        ref[pl.ds(c, rows, stride=ROW_LINES), :] = val[:, 128 * c:128 * (c + 1)]


def _pre_kernel(x_ref, nm_ref, win_ref, convw_ref, qn_ref, wuq_ref, kvn_ref, wukv_ref,
                gqn_ref, gqr_ref, gkn_ref, gkr_ref, tbl_ref,
                conv_ref, q_ref, k_ref, vt_ref, ext_ref):
    s = pl.program_id(1)
    ts = x_ref.shape[1]
    sub = TQ
    starts = tuple(range(0, ts, sub))

    @pl.when(s == 0)
    def _():
        ext_ref[0:HALO, :] = jnp.zeros((HALO, CONV_WIDTH), F32)

    @pl.when(s != 0)
    def _():
        ext_ref[0:HALO, :] = ext_ref[ts:ts + HALO, :]

    projs = [jnp.dot(_rms(x_ref[0, r0:r0 + sub, :], nm_ref[...]).astype(BF16), win_ref[...],
                     preferred_element_type=F32) for r0 in starts]

    lane = lax.broadcasted_iota(jnp.int32, (sub, 128), 1)
    first_half = lane < QK_ROPE
    scale = 1.0 / math.sqrt(QK_HEAD)

    for r0, proj in zip(starts, projs):
        rows = slice(r0, r0 + sub)
        u = proj[:, OFF_C:OFF_C + CONV_WIDTH] * proj[:, OFF_X:OFF_X + CONV_WIDTH]
        ext_ref[HALO + r0:HALO + r0 + sub, :] = u
        y = (convw_ref[2:3, :] * u
             + convw_ref[1:2, :] * ext_ref[HALO + r0 - 1:HALO + r0 - 1 + sub, :]
             + convw_ref[0:1, :] * ext_ref[HALO + r0 - 2:HALO + r0 - 2 + sub, :])
        conv_ref[0, rows, :] = (proj[:, OFF_B:OFF_B + CONV_WIDTH] * y).astype(BF16)

        hq = _rms(proj[:, OFF_CQ:OFF_CQ + Q_LORA], qn_ref[...]).astype(BF16)
        qp = jnp.dot(hq, wuq_ref[...], preferred_element_type=F32)
        hkv = _rms(proj[:, OFF_CKV:OFF_CKV + KV_LORA], kvn_ref[...]).astype(BF16)
        kvp = jnp.dot(hkv, wukv_ref[...], preferred_element_type=F32)

        tbl = tbl_ref[rows, :]

        def rope_block(grp, gain):
            r = grp * tbl * gain
            r = r + pltpu.roll(r, QK_ROPE, axis=1)
            return jnp.where(first_half, r, 0.0)

        kpe = proj[:, OFF_KPE:OFF_KPE + 128]
        ss_kpe = jnp.sum(jnp.where(first_half, kpe * kpe, 0.0), axis=-1, keepdims=True)
        k_rope = rope_block(kpe, gkr_ref[...])
        for h in range(MLA_HEADS):
            qn = qp[:, 128 * h:128 * (h + 1)]
            qg = qp[:, 512 + 128 * h:512 + 128 * (h + 1)]
            ss = (jnp.sum(qn * qn, axis=-1, keepdims=True)
                  + jnp.sum(jnp.where(first_half, qg * qg, 0.0), axis=-1, keepdims=True))
            rs = lax.rsqrt(ss * (1.0 / QK_HEAD) + EPS) * scale
            q_ref[0, h, rows, 0:128] = (qn * rs * gqn_ref[...]).astype(BF16)
            q_ref[0, h, rows, 128:256] = (rope_block(qg, gqr_ref[...]) * rs).astype(BF16)

            kn = kvp[:, 256 * h:256 * h + 128]
            ssk = jnp.sum(kn * kn, axis=-1, keepdims=True) + ss_kpe
            rsk = lax.rsqrt(ssk * (1.0 / QK_HEAD) + EPS)
            k_ref[0, h, rows, 0:128] = (kn * rsk * gkn_ref[...]).astype(BF16)
            k_ref[0, h, rows, 128:256] = (k_rope * rsk).astype(BF16)
            v = kvp[:, 256 * h + 128:256 * (h + 1)]
            vt_ref[0, h, r0 // TQ] = v.T.astype(BF16)


def _pre_call(x, nm, win, convw, qn, wuq, kvn, wukv, gqn, gqr, gkn, gkr, tbl):
    b, s, d = x.shape
    ts = TS_PRE
    const = lambda shape: pl.BlockSpec(shape, lambda bi, si: (0,) * len(shape))
    return pl.pallas_call(
        _pre_kernel,
        grid=(b, s // ts),
        in_specs=[
            pl.BlockSpec((1, ts, d), lambda bi, si: (bi, si, 0)),
            const((1, d)), const((d, IN_PROJ_EXT)), const((3, CONV_WIDTH)),
            const((1, Q_LORA)), const((Q_LORA, 1024)),
            const((1, KV_LORA)), const((KV_LORA, 1024)),
            const((1, 128)), const((1, 128)), const((1, 128)), const((1, 128)),
            pl.BlockSpec((ts, 128), lambda bi, si: (si, 0)),
        ],
        out_specs=[
            pl.BlockSpec((1, ts, CONV_WIDTH), lambda bi, si: (bi, si, 0)),
            pl.BlockSpec((1, MLA_HEADS, ts, QK_PAD), lambda bi, si: (bi, 0, si, 0)),
            pl.BlockSpec((1, MLA_HEADS, ts, QK_PAD), lambda bi, si: (bi, 0, si, 0)),
            pl.BlockSpec((1, MLA_HEADS, ts // TQ, V_HEAD, TQ), lambda bi, si: (bi, 0, si, 0, 0)),
        ],
        out_shape=[
            jax.ShapeDtypeStruct((b, s, CONV_WIDTH), BF16),
            jax.ShapeDtypeStruct((b, MLA_HEADS, s, QK_PAD), BF16),
            jax.ShapeDtypeStruct((b, MLA_HEADS, s, QK_PAD), BF16),
            jax.ShapeDtypeStruct((b, MLA_HEADS, s // TQ, V_HEAD, TQ), BF16),
        ],
        scratch_shapes=[pltpu.VMEM((ts + HALO, CONV_WIDTH), F32)],
        compiler_params=pltpu.CompilerParams(
            dimension_semantics=("arbitrary", "arbitrary"), vmem_limit_bytes=VMEM_LIMIT),
        name="pre_mixer",
    )(x, nm, win, convw, qn, wuq, kvn, wukv, gqn, gqr, gkn, gkr, tbl)


def _attn_kernel(q_ref, k_ref, vt_ref, o_ref, m_ref, l_ref, acc_ref, s_ref):
    i = pl.program_id(1)
    tq = q_ref.shape[2]
    key_c = lax.broadcasted_iota(jnp.int32, (tq, tq), 0) // CHUNK
    qry_c = lax.broadcasted_iota(jnp.int32, (tq, tq), 1) // CHUNK
    allowed = key_c <= qry_c
    m_ref[...] = jnp.full(m_ref.shape, -1e30, F32)
    l_ref[...] = jnp.zeros(l_ref.shape, F32)
    acc_ref[...] = jnp.zeros(acc_ref.shape, F32)

    def scores(j):
        start = pl.multiple_of(j * tq, tq)
        keep = allowed | (j < i)
        out = []
        for h in range(MLA_HEADS):
            kb = k_ref[0, h, pl.ds(start, tq), :]
            st = lax.dot_general(kb, q_ref[0, h], (((1,), (1,)), ((), ())),
                                 preferred_element_type=F32)
            out.append(jnp.where(keep, st, -1e30))
        return out

    def update(j, sts):
        probs, alphas = [], []
        for h in range(MLA_HEADS):
            m = m_ref[h]
            m_new = jnp.maximum(m, jnp.max(sts[h], axis=0, keepdims=True))
            alpha = jnp.exp(m - m_new)
            p = jnp.exp(sts[h] - m_new)
            l_ref[h] = alpha * l_ref[h] + jnp.sum(p, axis=0, keepdims=True)
            m_ref[h] = m_new
            probs.append(p.astype(BF16))
            alphas.append(alpha)
        for h in range(MLA_HEADS):
            acc_ref[h] = alphas[h] * acc_ref[h] + jnp.dot(vt_ref[0, h, j], probs[h],
                                                           preferred_element_type=F32)

    for h, st in enumerate(scores(0)):
        s_ref[h] = st

    def body(j, c):
        nxt = scores(j + 1)
        update(j, [s_ref[h] for h in range(MLA_HEADS)])
        for h in range(MLA_HEADS):
            s_ref[h] = nxt[h]
        return c

    lax.fori_loop(0, i, body, 0)
    update(i, [s_ref[h] for h in range(MLA_HEADS)])
    for h in range(MLA_HEADS):
        o = acc_ref[h] / l_ref[h]
        o_ref[0, :, V_HEAD * h:V_HEAD * (h + 1)] = o.T.astype(BF16)


def _attn_call(q, k, vt):
    b, nh, s, _ = q.shape
    return pl.pallas_call(
        _attn_kernel,
        grid=(b, s // TQ),
        in_specs=[
            pl.BlockSpec((1, nh, TQ, QK_PAD), lambda bi, qi: (bi, 0, qi, 0)),
            pl.BlockSpec((1, nh, s, QK_PAD), lambda bi, qi: (bi, 0, 0, 0)),
            pl.BlockSpec((1, nh, s // TQ, V_HEAD, TQ), lambda bi, qi: (bi, 0, 0, 0, 0)),
        ],
        out_specs=pl.BlockSpec((1, TQ, nh * V_HEAD), lambda bi, qi: (bi, qi, 0)),
        out_shape=jax.ShapeDtypeStruct((b, s, nh * V_HEAD), BF16),
        scratch_shapes=[pltpu.VMEM((nh, 1, TQ), F32), pltpu.VMEM((nh, 1, TQ), F32),
                        pltpu.VMEM((nh, V_HEAD, TQ), F32), pltpu.VMEM((nh, TQ, TQ), F32)],
        compiler_params=pltpu.CompilerParams(
            dimension_semantics=("arbitrary", "arbitrary"), vmem_limit_bytes=VMEM_LIMIT),
        name="mla_attention",
    )(q, k, vt)


def _post_kernel(x_ref, conv_ref, attn_ref, wo_ref, g_ref, wr_ref, br_ref, tri_ref,
                 h1_ref, hb_ref, lpos_ref, gate_ref, tile_n_ref, tile_carry_ref, cnt_ref,
                 carry_ref):
    step = pl.program_id(0)
    tm = x_ref.shape[0]

    @pl.when(step == 0)
    def _():
        carry_ref[...] = jnp.zeros_like(carry_ref)

    h1 = (x_ref[...]
          + jnp.dot(conv_ref[...], wo_ref[0:CONV_WIDTH, :], preferred_element_type=F32)
          + jnp.dot(attn_ref[...], wo_ref[CONV_WIDTH:, :], preferred_element_type=F32))
    h1_ref[...] = h1
    hb = _rms(h1, g_ref[...]).astype(BF16)
    hb_ref[...] = hb

    logits = jnp.dot(hb, wr_ref[...], preferred_element_type=F32) + br_ref[...]
    lane = lax.broadcasted_iota(jnp.int32, (tm, N_EXPERTS), 1).astype(F32)
    lane4 = lax.broadcasted_iota(jnp.int32, (tm, TOP_K), 1)
    member = jnp.zeros((tm, N_EXPERTS), F32)
    vals, idxs, hots = [], [], []
    lg = logits
    for _ in range(TOP_K):
        m = jnp.max(lg, axis=-1, keepdims=True)
        ix = jnp.min(jnp.where(lg == m, lane, float(N_EXPERTS)), axis=-1, keepdims=True)
        hot = lane == ix
        lg = jnp.where(hot, -jnp.inf, lg)
        member = member + hot.astype(F32)
        vals.append(m)
        idxs.append(ix)
        hots.append(hot)

    before = jnp.dot(tri_ref[...], member.astype(BF16), preferred_element_type=F32)
    tile_n = jnp.sum(member, axis=0, keepdims=True)
    lpos = [jnp.sum(jnp.where(hot, before, 0.0) + jnp.where(lane < ix, tile_n, 0.0),
                    axis=-1, keepdims=True) for hot, ix in zip(hots, idxs)]
    tile_n_ref[0] = tile_n.astype(jnp.int32)
    tile_carry_ref[0] = carry_ref[...].astype(jnp.int32)
    carry_ref[...] = carry_ref[...] + tile_n
    cnt_ref[...] = carry_ref[...].astype(jnp.int32)

    exps = [jnp.exp(v - vals[0]) for v in vals]
    denom = exps[0] + exps[1] + exps[2] + exps[3]

    def cols(c):
        return jnp.where(lane4 == 0, c[0], jnp.where(lane4 == 1, c[1],
                         jnp.where(lane4 == 2, c[2], c[3])))

    lpos_ref[...] = cols(lpos).astype(jnp.int32)
    gate_ref[...] = cols([e / denom for e in exps])


def _post_call(xf, conv, attn, wo, g, wr, br, tri):
    n, d = xf.shape
    tm = TM_POST
    const = lambda shape: pl.BlockSpec(shape, lambda i: (0,) * len(shape))
    row = lambda w: pl.BlockSpec((tm, w), lambda i: (i, 0))
    return pl.pallas_call(
        _post_kernel,
        grid=(n // tm,),
        in_specs=[row(d), row(CONV_WIDTH), row(CONV_WIDTH), const((d, d)), const((1, d)),
                  const((d, N_EXPERTS)), const((1, N_EXPERTS)), const((tm, tm))],
        out_specs=[row(d), row(d), row(TOP_K), row(TOP_K),
                   pl.BlockSpec((1, 1, N_EXPERTS), lambda i: (i, 0, 0)),
                   pl.BlockSpec((1, 1, N_EXPERTS), lambda i: (i, 0, 0)),
                   const((1, N_EXPERTS))],
        out_shape=[
            jax.ShapeDtypeStruct((n, d), F32),
            jax.ShapeDtypeStruct((n, d), BF16),
            jax.ShapeDtypeStruct((n, TOP_K), jnp.int32),
            jax.ShapeDtypeStruct((n, TOP_K), F32),
            jax.ShapeDtypeStruct((n // tm, 1, N_EXPERTS), jnp.int32),
            jax.ShapeDtypeStruct((n // tm, 1, N_EXPERTS), jnp.int32),
            jax.ShapeDtypeStruct((1, N_EXPERTS), jnp.int32),
        ],
        scratch_shapes=[pltpu.VMEM((1, N_EXPERTS), F32)],
        compiler_params=pltpu.CompilerParams(
            dimension_semantics=("arbitrary",), vmem_limit_bytes=VMEM_LIMIT),
        name="post_mixer_router",
    )(xf, conv, attn, wo, g, wr, br, tri)


RUN_BITS = 10


def _rows(ref, first_row, n_rows):
    return ref.at[pl.ds(pl.multiple_of(first_row * ROW_LINES, ROW_LINES), n_rows * ROW_LINES), :]


def _for_each_run_piece(n_ref, far_ref, fn, bits=RUN_BITS):
    def per_expert(e, local0):
        n_e = n_ref[0, 0, e]
        far0 = far_ref[0, 0, e]
        for b in reversed(range(bits)):
            done = (n_e >> (b + 1)) << (b + 1)

            @pl.when(((n_e >> b) & 1) == 1)
            def _():
                fn(local0 + done, far0 + done, 1 << b)
        return local0 + n_e

    lax.fori_loop(0, N_EXPERTS, per_expert, 0)


PAD_BITS = 9


def _scatter_kernel(n_ref, far_ref, pad_n_ref, pad_far_ref, used_ref, hb_ref, lpos_ref, xs_ref,
                    xl, zeros, sem, zsem):
    i = pl.program_id(0)
    steps = pl.num_programs(0)
    tm = hb_ref.shape[0]
    pairs = tm * TOP_K
    slot = i % 2

    def wait_slot(s):
        pltpu.make_async_copy(xl.at[s], _rows(xs_ref, 0, pairs), sem.at[s]).wait()

    @pl.when(i == 0)
    def _():
        zeros[...] = jnp.zeros(zeros.shape, F32)

        def fill(_, far_row, size):
            return pltpu.make_async_copy(_rows(zeros, 0, size), _rows(xs_ref, far_row, size), zsem)

        _for_each_run_piece(pad_n_ref, pad_far_ref, lambda *a: fill(*a).start(), PAD_BITS)
        _for_each_run_piece(pad_n_ref, pad_far_ref, lambda *a: fill(*a).wait(), PAD_BITS)

        tile_rows = zeros.shape[0] // ROW_LINES
        n_tiles = xs_ref.shape[0] // zeros.shape[0]

        def tail(op):
            def body(t, c):
                op(fill(None, t * tile_rows, tile_rows))
                return c
            lax.fori_loop(used_ref[0], n_tiles, body, 0)

        tail(lambda cp: cp.start())
        tail(lambda cp: cp.wait())

    @pl.when(i >= 2)
    def _():
        wait_slot(slot)

    lane = lax.broadcasted_iota(jnp.int32, (tm, 128), 1)
    lp = lpos_ref[...].astype(F32)
    cols = jnp.zeros((tm, 128), F32)
    for k in range(TOP_K):
        cols = jnp.where(lane == k, lp[:, k:k + 1], cols)
    rows_t = cols.T
    j = lax.broadcasted_iota(jnp.int32, (pairs, tm), 0).astype(F32)
    hit = j == rows_t[0:1, :]
    for k in range(1, TOP_K):
        hit = hit | (j == rows_t[k:k + 1, :])
    onehot = jnp.where(hit, 1.0, 0.0).astype(BF16)
    _store_row_tiles(xl.at[slot], jnp.dot(onehot, hb_ref[...], preferred_element_type=F32))

    def send(local_row, far_row, size):
        pltpu.make_async_copy(_rows(xl.at[slot], local_row, size), _rows(xs_ref, far_row, size),
                              sem.at[slot]).start()

    _for_each_run_piece(n_ref, far_ref, send)

    @pl.when(i == steps - 1)
    def _():
        @pl.when(steps >= 2)
        def _():
            wait_slot(1 - slot)
        wait_slot(slot)


def _scatter_call(tile_n, tile_far, pad_n, pad_far, used_tiles, hb, lpos, sorted_rows):
    n, d = hb.shape
    tm = TM_POST
    table = pl.BlockSpec((1, 1, N_EXPERTS), lambda i: (i, 0, 0), memory_space=pltpu.SMEM)
    whole = pl.BlockSpec((1, 1, N_EXPERTS), lambda i: (0, 0, 0), memory_space=pltpu.SMEM)
    return pl.pallas_call(
        _scatter_kernel,
        grid=(n // tm,),
        in_specs=[table, table, whole, whole, pl.BlockSpec(memory_space=pltpu.SMEM),
                  pl.BlockSpec((tm, d), lambda i: (i, 0)),
                  pl.BlockSpec((tm, TOP_K), lambda i: (i, 0))],
        out_specs=pl.BlockSpec(memory_space=pl.ANY),
        out_shape=jax.ShapeDtypeStruct((sorted_rows * ROW_LINES, 128), F32),
        scratch_shapes=[pltpu.VMEM((2, tm * TOP_K * ROW_LINES, 128), F32),
                        pltpu.VMEM((TM_EXP * ROW_LINES, 128), F32),
                        pltpu.SemaphoreType.DMA((2,)), pltpu.SemaphoreType.DMA],
        compiler_params=pltpu.CompilerParams(
            dimension_semantics=("arbitrary",), vmem_limit_bytes=VMEM_LIMIT),
        name="dispatch_scatter",
    )(tile_n, tile_far, pad_n, pad_far, used_tiles, hb, lpos)


def _expert_kernel(tile_ref, exp_ref, flag_ref,
                   xs_ref, wgu_ref, bgu_ref, wd_ref, bd_ref, o_ref, wgu_bf, wd_bf):
    i = pl.program_id(0)
    tm = xs_ref.shape[0] // ROW_LINES
    flag = flag_ref[i]
    cast_rows = 128

    @pl.when((flag & 2) != 0)
    def _():
        def cast(r, c):
            rows = pl.ds(pl.multiple_of(r * cast_rows, cast_rows), cast_rows)
            wgu_bf[rows, :] = wgu_ref[0, rows, :].astype(BF16)
            wd_bf[rows, :] = wd_ref[0, rows, :].astype(BF16)
            return c
        lax.fori_loop(0, D_MODEL // cast_rows, cast, 0)

    @pl.when((flag & 1) != 0)
    def _():
        x = _load_row_tiles(xs_ref, tm).astype(BF16)
        gu = jnp.dot(x, wgu_bf[...], preferred_element_type=F32) + bgu_ref[0]
        gate = jnp.minimum(gu[:, :D_FF], SWIGLU_LIMIT)
        up = jnp.clip(gu[:, D_FF:], -SWIGLU_LIMIT, SWIGLU_LIMIT)
        act = (up + 1.0) * (gate * (1.0 / (1.0 + jnp.exp(-SWIGLU_ALPHA * gate))))
        out = jnp.dot(act.astype(BF16), wd_bf[...], preferred_element_type=F32) + bd_ref[0]
        _store_row_tiles(o_ref, out)

    @pl.when((flag & 1) == 0)
    def _():
        o_ref[...] = jnp.zeros(o_ref.shape, F32)


def _expert_call(meta, xs, wgu, bgu, wd, bd):
    lines, w = xs.shape
    tm = TM_EXP
    n_items = meta[0].shape[0]
    return pl.pallas_call(
        _expert_kernel,
        grid_spec=pltpu.PrefetchScalarGridSpec(
            num_scalar_prefetch=3,
            grid=(n_items,),
            in_specs=[
                pl.BlockSpec((tm * ROW_LINES, w), lambda i, t, e, f: (t[i], 0)),
                pl.BlockSpec((1, D_MODEL, 2 * D_FF), lambda i, t, e, f: (e[i], 0, 0)),
                pl.BlockSpec((1, 1, 2 * D_FF), lambda i, t, e, f: (e[i], 0, 0)),
                pl.BlockSpec((1, D_FF, D_MODEL), lambda i, t, e, f: (e[i], 0, 0)),
                pl.BlockSpec((1, 1, D_MODEL), lambda i, t, e, f: (e[i], 0, 0)),
            ],
            out_specs=pl.BlockSpec((tm * ROW_LINES, w), lambda i, t, e, f: (t[i], 0)),
            scratch_shapes=[pltpu.VMEM((D_MODEL, 2 * D_FF), BF16),
                            pltpu.VMEM((D_FF, D_MODEL), BF16)],
        ),
        out_shape=jax.ShapeDtypeStruct((lines, w), F32),
        compiler_params=pltpu.CompilerParams(
            dimension_semantics=("arbitrary",), vmem_limit_bytes=VMEM_LIMIT),
        name="moe_experts",
    )(*meta, xs, wgu, bgu, wd, bd)


def _expert_layout(counts, n_pairs, tm):
    n_tiles = n_pairs // tm + N_EXPERTS
    per_exp = (counts + tm - 1) // tm
    tile_end = jnp.cumsum(per_exp)
    tile0 = tile_end - per_exp
    used = tile_end[-1]
    tile = jnp.arange(n_tiles, dtype=jnp.int32)
    e = jnp.sum((tile_end[None, :] <= jnp.minimum(tile, used - 1)[:, None]).astype(jnp.int32),
                axis=1)
    e_tile0 = jnp.sum(jnp.where(e[:, None] == jnp.arange(N_EXPERTS, dtype=jnp.int32)[None, :],
                                tile0[None, :], 0), axis=1)
    flag = jnp.where(tile < used, 1 + 2 * (tile == e_tile0), 0)
    cast = lambda a: a.astype(jnp.int32)
    group_start = tile0 * tm
    pad_n = per_exp * tm - counts
    pad_start = group_start + counts
    return ((tile, cast(e), cast(flag)), cast(used).reshape(1), cast(group_start), cast(pad_n),
            cast(pad_start))


def _combine_kernel(n_ref, far_ref, n_next_ref, far_next_ref, h1_ref, lpos_ref, gate_ref, ys_ref,
                    o_ref, yl, sem):
    i = pl.program_id(0)
    steps = pl.num_programs(0)
    tm = h1_ref.shape[0]
    pairs = tm * TOP_K
    slot = i % 2

    def fetch(nr, fr, s):
        def recv(local_row, far_row, size):
            pltpu.make_async_copy(_rows(ys_ref, far_row, size), _rows(yl.at[s], local_row, size),
                                  sem.at[s]).start()
        _for_each_run_piece(nr, fr, recv)

    @pl.when(i == 0)
    def _():
        fetch(n_ref, far_ref, 0)

    @pl.when(i + 1 < steps)
    def _():
        fetch(n_next_ref, far_next_ref, 1 - slot)

    pltpu.make_async_copy(_rows(ys_ref, 0, pairs), yl.at[slot], sem.at[slot]).wait()

    j = lax.broadcasted_iota(jnp.int32, (tm, pairs), 1)
    lp = lpos_ref[...]
    g = gate_ref[...]
    w = jnp.zeros((tm, pairs), F32)
    for k in range(TOP_K):
        w = jnp.where(j == lp[:, k:k + 1], g[:, k:k + 1], w)
    y = _load_row_tiles(yl, pairs, lead=(slot,)).astype(BF16)
    o_ref[...] = h1_ref[...] + jnp.dot(w.astype(BF16), y, preferred_element_type=F32)


def _combine_call(tile_n, tile_far, h1, lpos, gates, ys):
    n, d = h1.shape
    tm = TM_POST
    steps = n // tm
    table = lambda fn: pl.BlockSpec((1, 1, N_EXPERTS), fn, memory_space=pltpu.SMEM)
    cur = lambda i: (i, 0, 0)
    nxt = lambda i: (jnp.minimum(i + 1, steps - 1), 0, 0)
    return pl.pallas_call(
        _combine_kernel,
        grid=(steps,),
        in_specs=[table(cur), table(cur), table(nxt), table(nxt),
                  pl.BlockSpec((tm, d), lambda i: (i, 0)),
                  pl.BlockSpec((tm, TOP_K), lambda i: (i, 0)),
                  pl.BlockSpec((tm, TOP_K), lambda i: (i, 0)),
                  pl.BlockSpec(memory_space=pl.ANY)],
        out_specs=pl.BlockSpec((tm, d), lambda i: (i, 0)),
        out_shape=jax.ShapeDtypeStruct((n, d), F32),
        scratch_shapes=[pltpu.VMEM((2, tm * TOP_K * ROW_LINES, 128), F32),
                        pltpu.SemaphoreType.DMA((2,))],
        compiler_params=pltpu.CompilerParams(
            dimension_semantics=("arbitrary",), vmem_limit_bytes=VMEM_LIMIT),
        name="moe_combine",
    )(tile_n, tile_far, tile_n, tile_far, h1, lpos, gates, ys)


def _rotate_half_perm():
    half = QK_ROPE // 2
    perm = np.concatenate([np.arange(half, QK_ROPE), np.arange(0, half)])
    sign = np.concatenate([-np.ones(half), np.ones(half)]).astype(np.float32)
    return perm, sign


def _layer(h, l, norm_mix, w_in, conv_w, q_norm, w_uq, kv_norm, w_ukv, q_head_norm,
           k_head_norm, w_o, norm_ffn, w_router, b_router, w_gate_up, b_gate_up,
           w_down, b_down):
    b, s, d = h.shape
    n = b * s
    perm, sign = _rotate_half_perm()

    kpe_cols = w_in[l][:, OFF_KPE:OFF_KPE + QK_ROPE]
    win = jnp.concatenate([w_in[l], kpe_cols[:, perm] * sign], axis=1).astype(BF16)
    wq = w_uq[l].reshape(Q_LORA, MLA_HEADS, QK_HEAD)
    wq_rope = wq[:, :, QK_NOPE:]
    wuq = jnp.concatenate(
        [wq[:, :, :QK_NOPE].reshape(Q_LORA, MLA_HEADS * QK_NOPE),
         jnp.concatenate([wq_rope, wq_rope[:, :, perm] * sign], axis=2)
         .reshape(Q_LORA, MLA_HEADS * 2 * QK_ROPE)], axis=1).astype(BF16)
    gq, gk = q_head_norm[l], k_head_norm[l]
    gqn, gkn = gq[None, :QK_NOPE], gk[None, :QK_NOPE]
    gqr = jnp.concatenate([gq[QK_NOPE:], gq[QK_NOPE:][perm]])[None, :]
    gkr = jnp.concatenate([gk[QK_NOPE:], gk[QK_NOPE:][perm]])[None, :]

    half = QK_ROPE // 2
    inv_freq = ROPE_THETA ** (-jnp.arange(half, dtype=F32) / half)
    ang = jnp.arange(s).astype(F32)[:, None] * inv_freq[None, :]
    cos, sin = jnp.cos(ang), jnp.sin(ang)
    tbl = jnp.concatenate([cos, cos, sin, sin], axis=1)

    conv, q, k, vt = _pre_call(
        h, norm_mix[l][None, :], win, conv_w[l], q_norm[l][None, :], wuq,
        kv_norm[l][None, :], w_ukv[l].astype(BF16), gqn, gqr, gkn, gkr, tbl)
    attn = _attn_call(q, k, vt)

    tri = jnp.tril(jnp.ones((TM_POST, TM_POST), BF16), -1)
    h1, hb, lpos, gates, tile_n, tile_carry, counts = _post_call(
        h.reshape(n, d), conv.reshape(n, CONV_WIDTH), attn.reshape(n, CONV_WIDTH),
        w_o[l].astype(BF16), norm_ffn[l][None, :], w_router[l].astype(BF16),
        b_router[l][None, :], tri)

    meta, used_tiles, group_start, pad_n, pad_start = _expert_layout(counts[0], n * TOP_K, TM_EXP)
    sorted_rows = n * TOP_K + N_EXPERTS * TM_EXP
    tile_far = (tile_carry + group_start[None, None, :]).astype(jnp.int32)
    xs = _scatter_call(tile_n, tile_far, pad_n[None, None, :], pad_start[None, None, :],
                       used_tiles, hb, lpos, sorted_rows)
    ys = _expert_call(meta, xs, w_gate_up[l], b_gate_up[l][:, None, :],
                      w_down[l], b_down[l][:, None, :])
    out = _combine_call(tile_n, tile_far, h1, lpos, gates, ys)
    return out.reshape(b, s, d)


def kernel(x, norm_mix, w_in, conv_w, q_norm, w_uq, kv_norm, w_ukv, q_head_norm, k_head_norm,
           w_o, norm_ffn, w_router, b_router, w_gate_up, b_gate_up, w_down, b_down):
    h = x
    for l in range(norm_mix.shape[0]):
        h = _layer(h, l, norm_mix, w_in, conv_w, q_norm, w_uq, kv_norm, w_ukv, q_head_norm,
                   k_head_norm, w_o, norm_ffn, w_router, b_router, w_gate_up, b_gate_up,
                   w_down, b_down)
    return h
```

```python
import math

import jax
import jax.numpy as jnp
import numpy as np
from jax import lax
from jax.experimental import pallas as pl
from jax.experimental.pallas import tpu as pltpu

F32 = jnp.float32
BF16 = jnp.bfloat16

D_MODEL = 1024
CONV_WIDTH = 512
MLA_HEADS = 4
QK_NOPE = 128
QK_ROPE = 64
QK_HEAD = QK_NOPE + QK_ROPE
QK_PAD = 256
V_HEAD = 128
Q_LORA = 384
KV_LORA = 256
ROPE_THETA = 10000.0
CHUNK = 64
N_EXPERTS = 32
TOP_K = 4
D_FF = 1024
SWIGLU_LIMIT = 7.0
SWIGLU_ALPHA = 1.702
EPS = 1e-6

OFF_B, OFF_C, OFF_X, OFF_CQ, OFF_CKV, OFF_KPE = 0, 512, 1024, 1536, 1920, 2176
IN_PROJ_EXT = 2304

VMEM_LIMIT = 56 * 1024 * 1024

TS_PRE = 1024
TQ = 256
TM_POST = 512
TB_ROWS = 256
TM_EXP = 512
HALO = 8


def _rms(x, g):
    return x * lax.rsqrt(jnp.mean(x * x, axis=-1, keepdims=True) + EPS) * g


ROW_LINES = D_MODEL // 128


def _load_row_tiles(ref, rows, lead=()):
    return jnp.concatenate(
        [ref[lead + (pl.ds(c, rows, stride=ROW_LINES), slice(None))] for c in range(ROW_LINES)],
        axis=1)


def _store_row_tiles(ref, val):
    rows = val.shape[0]
    for c in range(ROW_LINES):
        ref[pl.ds(c, rows, stride=ROW_LINES), :] = val[:, 128 * c:128 * (c + 1)]


def _pre_kernel(x_ref, nm_ref, win_ref, convw_ref, qn_ref, wuq_ref, kvn_ref, wukv_ref,
                gqn_ref, gqr_ref, gkn_ref, gkr_ref, tbl_ref,
                conv_ref, q_ref, k_ref, vt_ref, ext_ref):
    s = pl.program_id(1)
    ts = x_ref.shape[1]
    sub = TQ
    starts = tuple(range(0, ts, sub))

    @pl.when(s == 0)
    def _():
        ext_ref[0:HALO, :] = jnp.zeros((HALO, CONV_WIDTH), F32)

    @pl.when(s != 0)
    def _():
        ext_ref[0:HALO, :] = ext_ref[ts:ts + HALO, :]

    projs = [jnp.dot(_rms(x_ref[0, r0:r0 + sub, :], nm_ref[...]).astype(BF16), win_ref[...],
                     preferred_element_type=F32) for r0 in starts]

    lane = lax.broadcasted_iota(jnp.int32, (sub, 128), 1)
    first_half = lane < QK_ROPE
    scale = math.log2(math.e) / math.sqrt(QK_HEAD)

    for r0, proj in zip(starts, projs):
        rows = slice(r0, r0 + sub)
        u = proj[:, OFF_C:OFF_C + CONV_WIDTH] * proj[:, OFF_X:OFF_X + CONV_WIDTH]
        ext_ref[HALO + r0:HALO + r0 + sub, :] = u
        y = (convw_ref[2:3, :] * u
             + convw_ref[1:2, :] * ext_ref[HALO + r0 - 1:HALO + r0 - 1 + sub, :]
             + convw_ref[0:1, :] * ext_ref[HALO + r0 - 2:HALO + r0 - 2 + sub, :])
        conv_ref[0, rows, :] = (proj[:, OFF_B:OFF_B + CONV_WIDTH] * y).astype(BF16)

        hq = _rms(proj[:, OFF_CQ:OFF_CQ + Q_LORA], qn_ref[...]).astype(BF16)
        qp = jnp.dot(hq, wuq_ref[...], preferred_element_type=F32)
        hkv = _rms(proj[:, OFF_CKV:OFF_CKV + KV_LORA], kvn_ref[...]).astype(BF16)
        kvp = jnp.dot(hkv, wukv_ref[...], preferred_element_type=F32)

        tbl = tbl_ref[rows, :]

        def rope_block(grp, gain):
            r = grp * tbl * gain
            r = r + pltpu.roll(r, QK_ROPE, axis=1)
            return jnp.where(first_half, r, 0.0)

        kpe = proj[:, OFF_KPE:OFF_KPE + 128]
        ss_kpe = jnp.sum(jnp.where(first_half, kpe * kpe, 0.0), axis=-1, keepdims=True)
        k_rope = rope_block(kpe, gkr_ref[...])
        for h in range(MLA_HEADS):
            qn = qp[:, 128 * h:128 * (h + 1)]
            qg = qp[:, 512 + 128 * h:512 + 128 * (h + 1)]
            ss = (jnp.sum(qn * qn, axis=-1, keepdims=True)
                  + jnp.sum(jnp.where(first_half, qg * qg, 0.0), axis=-1, keepdims=True))
            rs = lax.rsqrt(ss * (1.0 / QK_HEAD) + EPS) * scale
            q_ref[0, h, rows, 0:128] = (qn * rs * gqn_ref[...]).astype(BF16)
            q_ref[0, h, rows, 128:256] = (rope_block(qg, gqr_ref[...]) * rs).astype(BF16)

            kn = kvp[:, 256 * h:256 * h + 128]
            ssk = jnp.sum(kn * kn, axis=-1, keepdims=True) + ss_kpe
            rsk = lax.rsqrt(ssk * (1.0 / QK_HEAD) + EPS)
            k_ref[0, h, rows, 0:128] = (kn * rsk * gkn_ref[...]).astype(BF16)
            k_ref[0, h, rows, 128:256] = (k_rope * rsk).astype(BF16)
            v = kvp[:, 256 * h + 128:256 * (h + 1)]
            vt_ref[0, h, r0 // TQ] = v.T.astype(BF16)


def _pre_call(x, nm, win, convw, qn, wuq, kvn, wukv, gqn, gqr, gkn, gkr, tbl):
    b, s, d = x.shape
    ts = TS_PRE
    const = lambda shape: pl.BlockSpec(shape, lambda bi, si: (0,) * len(shape))
    return pl.pallas_call(
        _pre_kernel,
        grid=(b, s // ts),
        in_specs=[
            pl.BlockSpec((1, ts, d), lambda bi, si: (bi, si, 0)),
            const((1, d)), const((d, IN_PROJ_EXT)), const((3, CONV_WIDTH)),
            const((1, Q_LORA)), const((Q_LORA, 1024)),
            const((1, KV_LORA)), const((KV_LORA, 1024)),
            const((1, 128)), const((1, 128)), const((1, 128)), const((1, 128)),
            pl.BlockSpec((ts, 128), lambda bi, si: (si, 0)),
        ],
        out_specs=[
            pl.BlockSpec((1, ts, CONV_WIDTH), lambda bi, si: (bi, si, 0)),
            pl.BlockSpec((1, MLA_HEADS, ts, QK_PAD), lambda bi, si: (bi, 0, si, 0)),
            pl.BlockSpec((1, MLA_HEADS, ts, QK_PAD), lambda bi, si: (bi, 0, si, 0)),
            pl.BlockSpec((1, MLA_HEADS, ts // TQ, V_HEAD, TQ), lambda bi, si: (bi, 0, si, 0, 0)),
        ],
        out_shape=[
            jax.ShapeDtypeStruct((b, s, CONV_WIDTH), BF16),
            jax.ShapeDtypeStruct((b, MLA_HEADS, s, QK_PAD), BF16),
            jax.ShapeDtypeStruct((b, MLA_HEADS, s, QK_PAD), BF16),
            jax.ShapeDtypeStruct((b, MLA_HEADS, s // TQ, V_HEAD, TQ), BF16),
        ],
        scratch_shapes=[pltpu.VMEM((ts + HALO, CONV_WIDTH), F32)],
        compiler_params=pltpu.CompilerParams(
            dimension_semantics=("arbitrary", "arbitrary"), vmem_limit_bytes=VMEM_LIMIT),
        name="pre_mixer",
    )(x, nm, win, convw, qn, wuq, kvn, wukv, gqn, gqr, gkn, gkr, tbl)


def _attn_kernel(q_ref, k_ref, vt_ref, o_ref, m_ref, l_ref, acc_ref, s_ref):
    i = pl.program_id(1)
    tq = q_ref.shape[2]
    key_c = lax.broadcasted_iota(jnp.int32, (tq, tq), 0) // CHUNK
    qry_c = lax.broadcasted_iota(jnp.int32, (tq, tq), 1) // CHUNK
    allowed = key_c <= qry_c
    m_ref[...] = jnp.full(m_ref.shape, -1e30, F32)
    l_ref[...] = jnp.zeros(l_ref.shape, F32)
    acc_ref[...] = jnp.zeros(acc_ref.shape, F32)

    def scores(j, keep):
        start = pl.multiple_of(j * tq, tq)
        out = []
        for h in range(MLA_HEADS):
            kb = k_ref[0, h, pl.ds(start, tq), :]
            st = lax.dot_general(kb, q_ref[0, h], (((1,), (1,)), ((), ())),
                                 preferred_element_type=F32)
            out.append(st if keep is None else jnp.where(keep, st, -1e30))
        return out

    def update(j):
        probs, alphas = [], []
        for h in range(MLA_HEADS):
            st = s_ref[h]
            m = m_ref[h]
            m_new = jnp.maximum(m, jnp.max(st, axis=0, keepdims=True))
            alpha = jnp.exp2(m - m_new)
            p = jnp.exp2(st - m_new)
            l_ref[h] = alpha * l_ref[h] + jnp.sum(p, axis=0, keepdims=True)
            m_ref[h] = m_new
            probs.append(p.astype(BF16))
            alphas.append(alpha)
        for h in range(MLA_HEADS):
            acc_ref[h] = alphas[h] * acc_ref[h] + jnp.dot(vt_ref[0, h, j], probs[h],
                                                           preferred_element_type=F32)

    def stage(sts):
        for h in range(MLA_HEADS):
            s_ref[h] = sts[h]

    stage(scores(0, allowed | (i > 0)))

    def body(j, c):
        nxt = scores(j + 1, None)
        update(j)
        stage(nxt)
        return c

    lax.fori_loop(0, i - 1, body, 0)

    @pl.when(i > 0)
    def _():
        nxt = scores(i, allowed)
        update(i - 1)
        stage(nxt)

    update(i)
    for h in range(MLA_HEADS):
        o = acc_ref[h] / l_ref[h]
        o_ref[0, :, V_HEAD * h:V_HEAD * (h + 1)] = o.T.astype(BF16)


def _attn_call(q, k, vt):
    b, nh, s, _ = q.shape
    return pl.pallas_call(
        _attn_kernel,
        grid=(b, s // TQ),
        in_specs=[
            pl.BlockSpec((1, nh, TQ, QK_PAD), lambda bi, qi: (bi, 0, qi, 0)),
            pl.BlockSpec((1, nh, s, QK_PAD), lambda bi, qi: (bi, 0, 0, 0)),
            pl.BlockSpec((1, nh, s // TQ, V_HEAD, TQ), lambda bi, qi: (bi, 0, 0, 0, 0)),
        ],
        out_specs=pl.BlockSpec((1, TQ, nh * V_HEAD), lambda bi, qi: (bi, qi, 0)),
        out_shape=jax.ShapeDtypeStruct((b, s, nh * V_HEAD), BF16),
        scratch_shapes=[pltpu.VMEM((nh, 1, TQ), F32), pltpu.VMEM((nh, 1, TQ), F32),
                        pltpu.VMEM((nh, V_HEAD, TQ), F32), pltpu.VMEM((nh, TQ, TQ), F32)],
        compiler_params=pltpu.CompilerParams(
            dimension_semantics=("arbitrary", "arbitrary"), vmem_limit_bytes=VMEM_LIMIT),
        name="mla_attention",
    )(q, k, vt)


def _post_kernel(x_ref, conv_ref, attn_ref, wo_ref, g_ref, wr_ref, br_ref, tri_ref,
                 h1_ref, hb_ref, lpos_ref, gate_ref, tile_n_ref, tile_carry_ref, cnt_ref,
                 carry_ref):
    step = pl.program_id(0)
    tm = x_ref.shape[0]

    @pl.when(step == 0)
    def _():
        carry_ref[...] = jnp.zeros_like(carry_ref)

    h1 = (x_ref[...]
          + jnp.dot(conv_ref[...], wo_ref[0:CONV_WIDTH, :], preferred_element_type=F32)
          + jnp.dot(attn_ref[...], wo_ref[CONV_WIDTH:, :], preferred_element_type=F32))
    h1_ref[...] = h1
    hb = _rms(h1, g_ref[...]).astype(BF16)
    hb_ref[...] = hb

    logits = jnp.dot(hb, wr_ref[...], preferred_element_type=F32) + br_ref[...]
    lane = lax.broadcasted_iota(jnp.int32, (tm, N_EXPERTS), 1).astype(F32)
    lane4 = lax.broadcasted_iota(jnp.int32, (tm, TOP_K), 1)
    member = jnp.zeros((tm, N_EXPERTS), F32)
    vals, idxs, hots = [], [], []
    lg = logits
    for _ in range(TOP_K):
        m = jnp.max(lg, axis=-1, keepdims=True)
        ix = jnp.min(jnp.where(lg == m, lane, float(N_EXPERTS)), axis=-1, keepdims=True)
        hot = lane == ix
        lg = jnp.where(hot, -jnp.inf, lg)
        member = member + hot.astype(F32)
        vals.append(m)
        idxs.append(ix)
        hots.append(hot)

    before = jnp.dot(tri_ref[...], member.astype(BF16), preferred_element_type=F32)
    tile_n = jnp.sum(member, axis=0, keepdims=True)
    lpos = [jnp.sum(jnp.where(hot, before, 0.0) + jnp.where(lane < ix, tile_n, 0.0),
                    axis=-1, keepdims=True) for hot, ix in zip(hots, idxs)]
    tile_n_ref[0] = tile_n.astype(jnp.int32)
    tile_carry_ref[0] = carry_ref[...].astype(jnp.int32)
    carry_ref[...] = carry_ref[...] + tile_n
    cnt_ref[...] = carry_ref[...].astype(jnp.int32)

    exps = [jnp.exp(v - vals[0]) for v in vals]
    denom = exps[0] + exps[1] + exps[2] + exps[3]

    def cols(c):
        return jnp.where(lane4 == 0, c[0], jnp.where(lane4 == 1, c[1],
                         jnp.where(lane4 == 2, c[2], c[3])))

    lpos_ref[...] = cols(lpos).astype(jnp.int32)
    gate_ref[...] = cols([e / denom for e in exps])


def _post_call(xf, conv, attn, wo, g, wr, br, tri):
    n, d = xf.shape
    tm = TM_POST
    const = lambda shape: pl.BlockSpec(shape, lambda i: (0,) * len(shape))
    row = lambda w: pl.BlockSpec((tm, w), lambda i: (i, 0))
    return pl.pallas_call(
        _post_kernel,
        grid=(n // tm,),
        in_specs=[row(d), row(CONV_WIDTH), row(CONV_WIDTH), const((d, d)), const((1, d)),
                  const((d, N_EXPERTS)), const((1, N_EXPERTS)), const((tm, tm))],
        out_specs=[row(d), row(d), row(TOP_K), row(TOP_K),
                   pl.BlockSpec((1, 1, N_EXPERTS), lambda i: (i, 0, 0)),
                   pl.BlockSpec((1, 1, N_EXPERTS), lambda i: (i, 0, 0)),
                   const((1, N_EXPERTS))],
        out_shape=[
            jax.ShapeDtypeStruct((n, d), F32),
            jax.ShapeDtypeStruct((n, d), BF16),
            jax.ShapeDtypeStruct((n, TOP_K), jnp.int32),
            jax.ShapeDtypeStruct((n, TOP_K), F32),
            jax.ShapeDtypeStruct((n // tm, 1, N_EXPERTS), jnp.int32),
            jax.ShapeDtypeStruct((n // tm, 1, N_EXPERTS), jnp.int32),
            jax.ShapeDtypeStruct((1, N_EXPERTS), jnp.int32),
        ],
        scratch_shapes=[pltpu.VMEM((1, N_EXPERTS), F32)],
        compiler_params=pltpu.CompilerParams(
            dimension_semantics=("arbitrary",), vmem_limit_bytes=VMEM_LIMIT),
        name="post_mixer_router",
    )(xf, conv, attn, wo, g, wr, br, tri)


RUN_BITS = 10


def _rows(ref, first_row, n_rows):
    return ref.at[pl.ds(pl.multiple_of(first_row * ROW_LINES, ROW_LINES), n_rows * ROW_LINES), :]


def _for_each_run_piece(n_ref, far_ref, fn, bits=RUN_BITS):
    def per_expert(e, local0):
        n_e = n_ref[0, 0, e]
        far0 = far_ref[0, 0, e]
        for b in reversed(range(bits)):
            done = (n_e >> (b + 1)) << (b + 1)

            @pl.when(((n_e >> b) & 1) == 1)
            def _():
                fn(local0 + done, far0 + done, 1 << b)
        return local0 + n_e

    lax.fori_loop(0, N_EXPERTS, per_expert, 0)


PAD_BITS = 9


def _scatter_kernel(n_ref, far_ref, pad_n_ref, pad_far_ref, used_ref, hb_ref, lpos_ref, xs_ref,
                    xl, zeros, sem, zsem):
    i = pl.program_id(0)
    steps = pl.num_programs(0)
    tm = hb_ref.shape[0]
    pairs = tm * TOP_K
    slot = i % 2

    def wait_slot(s):
        pltpu.make_async_copy(xl.at[s], _rows(xs_ref, 0, pairs), sem.at[s]).wait()

    @pl.when(i == 0)
    def _():
        zeros[...] = jnp.zeros(zeros.shape, F32)

        def fill(_, far_row, size):
            return pltpu.make_async_copy(_rows(zeros, 0, size), _rows(xs_ref, far_row, size), zsem)

        _for_each_run_piece(pad_n_ref, pad_far_ref, lambda *a: fill(*a).start(), PAD_BITS)
        _for_each_run_piece(pad_n_ref, pad_far_ref, lambda *a: fill(*a).wait(), PAD_BITS)

        tile_rows = zeros.shape[0] // ROW_LINES
        n_tiles = xs_ref.shape[0] // zeros.shape[0]

        def tail(op):
            def body(t, c):
                op(fill(None, t * tile_rows, tile_rows))
                return c
            lax.fori_loop(used_ref[0], n_tiles, body, 0)

        tail(lambda cp: cp.start())
        tail(lambda cp: cp.wait())

    @pl.when(i >= 2)
    def _():
        wait_slot(slot)

    lane = lax.broadcasted_iota(jnp.int32, (tm, 128), 1)
    lp = lpos_ref[...].astype(F32)
    cols = jnp.zeros((tm, 128), F32)
    for k in range(TOP_K):
        cols = jnp.where(lane == k, lp[:, k:k + 1], cols)
    rows_t = cols.T
    j = lax.broadcasted_iota(jnp.int32, (pairs, tm), 0).astype(F32)
    hit = j == rows_t[0:1, :]
    for k in range(1, TOP_K):
        hit = hit | (j == rows_t[k:k + 1, :])
    onehot = jnp.where(hit, 1.0, 0.0).astype(BF16)
    _store_row_tiles(xl.at[slot], jnp.dot(onehot, hb_ref[...], preferred_element_type=F32))

    def send(local_row, far_row, size):
        pltpu.make_async_copy(_rows(xl.at[slot], local_row, size), _rows(xs_ref, far_row, size),
                              sem.at[slot]).start()

    _for_each_run_piece(n_ref, far_ref, send)

    @pl.when(i == steps - 1)
    def _():
        @pl.when(steps >= 2)
        def _():
            wait_slot(1 - slot)
        wait_slot(slot)


def _scatter_call(tile_n, tile_far, pad_n, pad_far, used_tiles, hb, lpos, sorted_rows):
    n, d = hb.shape
    tm = TM_POST
    table = pl.BlockSpec((1, 1, N_EXPERTS), lambda i: (i, 0, 0), memory_space=pltpu.SMEM)
    whole = pl.BlockSpec((1, 1, N_EXPERTS), lambda i: (0, 0, 0), memory_space=pltpu.SMEM)
    return pl.pallas_call(
        _scatter_kernel,
        grid=(n // tm,),
        in_specs=[table, table, whole, whole, pl.BlockSpec(memory_space=pltpu.SMEM),
                  pl.BlockSpec((tm, d), lambda i: (i, 0)),
                  pl.BlockSpec((tm, TOP_K), lambda i: (i, 0))],
        out_specs=pl.BlockSpec(memory_space=pl.ANY),
        out_shape=jax.ShapeDtypeStruct((sorted_rows * ROW_LINES, 128), F32),
        scratch_shapes=[pltpu.VMEM((2, tm * TOP_K * ROW_LINES, 128), F32),
                        pltpu.VMEM((TM_EXP * ROW_LINES, 128), F32),
                        pltpu.SemaphoreType.DMA((2,)), pltpu.SemaphoreType.DMA],
        compiler_params=pltpu.CompilerParams(
            dimension_semantics=("arbitrary",), vmem_limit_bytes=VMEM_LIMIT),
        name="dispatch_scatter",
    )(tile_n, tile_far, pad_n, pad_far, used_tiles, hb, lpos)


def _expert_kernel(tile_ref, exp_ref, flag_ref,
                   xs_ref, wgu_ref, bgu_ref, wd_ref, bd_ref, o_ref, wgu_bf, wd_bf):
    i = pl.program_id(0)
    tm = xs_ref.shape[0] // ROW_LINES
    flag = flag_ref[i]
    cast_rows = 128

    @pl.when((flag & 2) != 0)
    def _():
        def cast(r, c):
            rows = pl.ds(pl.multiple_of(r * cast_rows, cast_rows), cast_rows)
            wgu_bf[rows, :] = wgu_ref[0, rows, :].astype(BF16)
            wd_bf[rows, :] = wd_ref[0, rows, :].astype(BF16)
            return c
        lax.fori_loop(0, D_MODEL // cast_rows, cast, 0)

    @pl.when((flag & 1) != 0)
    def _():
        x = _load_row_tiles(xs_ref, tm).astype(BF16)
        gu = jnp.dot(x, wgu_bf[...], preferred_element_type=F32) + bgu_ref[0]
        gate = jnp.minimum(gu[:, :D_FF], SWIGLU_LIMIT)
        up = jnp.clip(gu[:, D_FF:], -SWIGLU_LIMIT, SWIGLU_LIMIT)
        act = (up + 1.0) * (gate * (1.0 / (1.0 + jnp.exp(-SWIGLU_ALPHA * gate))))
        out = jnp.dot(act.astype(BF16), wd_bf[...], preferred_element_type=F32) + bd_ref[0]
        _store_row_tiles(o_ref, out)

    @pl.when((flag & 1) == 0)
    def _():
        o_ref[...] = jnp.zeros(o_ref.shape, F32)


def _expert_call(meta, xs, wgu, bgu, wd, bd):
    lines, w = xs.shape
    tm = TM_EXP
    n_items = meta[0].shape[0]
    return pl.pallas_call(
        _expert_kernel,
        grid_spec=pltpu.PrefetchScalarGridSpec(
            num_scalar_prefetch=3,
            grid=(n_items,),
            in_specs=[
                pl.BlockSpec((tm * ROW_LINES, w), lambda i, t, e, f: (t[i], 0)),
                pl.BlockSpec((1, D_MODEL, 2 * D_FF), lambda i, t, e, f: (e[i], 0, 0)),
                pl.BlockSpec((1, 1, 2 * D_FF), lambda i, t, e, f: (e[i], 0, 0)),
                pl.BlockSpec((1, D_FF, D_MODEL), lambda i, t, e, f: (e[i], 0, 0)),
                pl.BlockSpec((1, 1, D_MODEL), lambda i, t, e, f: (e[i], 0, 0)),
            ],
            out_specs=pl.BlockSpec((tm * ROW_LINES, w), lambda i, t, e, f: (t[i], 0)),
            scratch_shapes=[pltpu.VMEM((D_MODEL, 2 * D_FF), BF16),
                            pltpu.VMEM((D_FF, D_MODEL), BF16)],
        ),
        out_shape=jax.ShapeDtypeStruct((lines, w), F32),
        compiler_params=pltpu.CompilerParams(
            dimension_semantics=("arbitrary",), vmem_limit_bytes=VMEM_LIMIT),
        name="moe_experts",
    )(*meta, xs, wgu, bgu, wd, bd)


def _expert_layout(counts, n_pairs, tm):
    n_tiles = n_pairs // tm + N_EXPERTS
    per_exp = (counts + tm - 1) // tm
    tile_end = jnp.cumsum(per_exp)
    tile0 = tile_end - per_exp
    used = tile_end[-1]
    tile = jnp.arange(n_tiles, dtype=jnp.int32)
    e = jnp.sum((tile_end[None, :] <= jnp.minimum(tile, used - 1)[:, None]).astype(jnp.int32),
                axis=1)
    e_tile0 = jnp.sum(jnp.where(e[:, None] == jnp.arange(N_EXPERTS, dtype=jnp.int32)[None, :],
                                tile0[None, :], 0), axis=1)
    flag = jnp.where(tile < used, 1 + 2 * (tile == e_tile0), 0)
    cast = lambda a: a.astype(jnp.int32)
    group_start = tile0 * tm
    pad_n = per_exp * tm - counts
    pad_start = group_start + counts
    return ((tile, cast(e), cast(flag)), cast(used).reshape(1), cast(group_start), cast(pad_n),
            cast(pad_start))


def _combine_kernel(n_ref, far_ref, n_next_ref, far_next_ref, h1_ref, lpos_ref, gate_ref, ys_ref,
                    o_ref, yl, sem):
    i = pl.program_id(0)
    steps = pl.num_programs(0)
    tm = h1_ref.shape[0]
    pairs = tm * TOP_K
    slot = i % 2

    def fetch(nr, fr, s):
        def recv(local_row, far_row, size):
            pltpu.make_async_copy(_rows(ys_ref, far_row, size), _rows(yl.at[s], local_row, size),
                                  sem.at[s]).start()
        _for_each_run_piece(nr, fr, recv)

    @pl.when(i == 0)
    def _():
        fetch(n_ref, far_ref, 0)

    @pl.when(i + 1 < steps)
    def _():
        fetch(n_next_ref, far_next_ref, 1 - slot)

    pltpu.make_async_copy(_rows(ys_ref, 0, pairs), yl.at[slot], sem.at[slot]).wait()

    j = lax.broadcasted_iota(jnp.int32, (tm, pairs), 1)
    lp = lpos_ref[...]
    g = gate_ref[...]
    w = jnp.zeros((tm, pairs), F32)
    for k in range(TOP_K):
        w = jnp.where(j == lp[:, k:k + 1], g[:, k:k + 1], w)
    y = _load_row_tiles(yl, pairs, lead=(slot,)).astype(BF16)
    o_ref[...] = h1_ref[...] + jnp.dot(w.astype(BF16), y, preferred_element_type=F32)


def _combine_call(tile_n, tile_far, h1, lpos, gates, ys):
    n, d = h1.shape
    tm = TM_POST
    steps = n // tm
    table = lambda fn: pl.BlockSpec((1, 1, N_EXPERTS), fn, memory_space=pltpu.SMEM)
    cur = lambda i: (i, 0, 0)
    nxt = lambda i: (jnp.minimum(i + 1, steps - 1), 0, 0)
    return pl.pallas_call(
        _combine_kernel,
        grid=(steps,),
        in_specs=[table(cur), table(cur), table(nxt), table(nxt),
                  pl.BlockSpec((tm, d), lambda i: (i, 0)),
                  pl.BlockSpec((tm, TOP_K), lambda i: (i, 0)),
                  pl.BlockSpec((tm, TOP_K), lambda i: (i, 0)),
                  pl.BlockSpec(memory_space=pl.ANY)],
        out_specs=pl.BlockSpec((tm, d), lambda i: (i, 0)),
        out_shape=jax.ShapeDtypeStruct((n, d), F32),
        scratch_shapes=[pltpu.VMEM((2, tm * TOP_K * ROW_LINES, 128), F32),
                        pltpu.SemaphoreType.DMA((2,))],
        compiler_params=pltpu.CompilerParams(
            dimension_semantics=("arbitrary",), vmem_limit_bytes=VMEM_LIMIT),
        name="moe_combine",
    )(tile_n, tile_far, tile_n, tile_far, h1, lpos, gates, ys)


def _rotate_half_perm():
    half = QK_ROPE // 2
    perm = np.concatenate([np.arange(half, QK_ROPE), np.arange(0, half)])
    sign = np.concatenate([-np.ones(half), np.ones(half)]).astype(np.float32)
    return perm, sign


def _layer(h, l, norm_mix, w_in, conv_w, q_norm, w_uq, kv_norm, w_ukv, q_head_norm,
           k_head_norm, w_o, norm_ffn, w_router, b_router, w_gate_up, b_gate_up,
           w_down, b_down):
    b, s, d = h.shape
    n = b * s
    perm, sign = _rotate_half_perm()

    kpe_cols = w_in[l][:, OFF_KPE:OFF_KPE + QK_ROPE]
    win = jnp.concatenate([w_in[l], kpe_cols[:, perm] * sign], axis=1).astype(BF16)
    wq = w_uq[l].reshape(Q_LORA, MLA_HEADS, QK_HEAD)
    wq_rope = wq[:, :, QK_NOPE:]
    wuq = jnp.concatenate(
        [wq[:, :, :QK_NOPE].reshape(Q_LORA, MLA_HEADS * QK_NOPE),
         jnp.concatenate([wq_rope, wq_rope[:, :, perm] * sign], axis=2)
         .reshape(Q_LORA, MLA_HEADS * 2 * QK_ROPE)], axis=1).astype(BF16)
    gq, gk = q_head_norm[l], k_head_norm[l]
    gqn, gkn = gq[None, :QK_NOPE], gk[None, :QK_NOPE]
    gqr = jnp.concatenate([gq[QK_NOPE:], gq[QK_NOPE:][perm]])[None, :]
    gkr = jnp.concatenate([gk[QK_NOPE:], gk[QK_NOPE:][perm]])[None, :]

    half = QK_ROPE // 2
    inv_freq = ROPE_THETA ** (-jnp.arange(half, dtype=F32) / half)
    ang = jnp.arange(s).astype(F32)[:, None] * inv_freq[None, :]
    cos, sin = jnp.cos(ang), jnp.sin(ang)
    tbl = jnp.concatenate([cos, cos, sin, sin], axis=1)

    conv, q, k, vt = _pre_call(
        h, norm_mix[l][None, :], win, conv_w[l], q_norm[l][None, :], wuq,
        kv_norm[l][None, :], w_ukv[l].astype(BF16), gqn, gqr, gkn, gkr, tbl)
    attn = _attn_call(q, k, vt)

    tri = jnp.tril(jnp.ones((TM_POST, TM_POST), BF16), -1)
    h1, hb, lpos, gates, tile_n, tile_carry, counts = _post_call(
        h.reshape(n, d), conv.reshape(n, CONV_WIDTH), attn.reshape(n, CONV_WIDTH),
        w_o[l].astype(BF16), norm_ffn[l][None, :], w_router[l].astype(BF16),
        b_router[l][None, :], tri)

    meta, used_tiles, group_start, pad_n, pad_start = _expert_layout(counts[0], n * TOP_K, TM_EXP)
    sorted_rows = n * TOP_K + N_EXPERTS * TM_EXP
    tile_far = (tile_carry + group_start[None, None, :]).astype(jnp.int32)
    xs = _scatter_call(tile_n, tile_far, pad_n[None, None, :], pad_start[None, None, :],
                       used_tiles, hb, lpos, sorted_rows)
    ys = _expert_call(meta, xs, w_gate_up[l], b_gate_up[l][:, None, :],
                      w_down[l], b_down[l][:, None, :])
    out = _combine_call(tile_n, tile_far, h1, lpos, gates, ys)
    return out.reshape(b, s, d)


def kernel(x, norm_mix, w_in, conv_w, q_norm, w_uq, kv_norm, w_ukv, q_head_norm, k_head_norm,
           w_o, norm_ffn, w_router, b_router, w_gate_up, b_gate_up, w_down, b_down):
    h = x
    for l in range(norm_mix.shape[0]):
        h = _layer(h, l, norm_mix, w_in, conv_w, q_norm, w_uq, kv_norm, w_ukv, q_head_norm,
                   k_head_norm, w_o, norm_ffn, w_router, b_router, w_gate_up, b_gate_up,
                   w_down, b_down)
    return h
```

```python
import math

import jax
import jax.numpy as jnp
import numpy as np
from jax import lax
from jax.experimental import pallas as pl
from jax.experimental.pallas import tpu as pltpu

F32 = jnp.float32
BF16 = jnp.bfloat16

D_MODEL = 1024
CONV_WIDTH = 512
MLA_HEADS = 4
QK_NOPE = 128
QK_ROPE = 64
QK_HEAD = QK_NOPE + QK_ROPE
QK_PAD = 256
V_HEAD = 128
Q_LORA = 384
KV_LORA = 256
ROPE_THETA = 10000.0
CHUNK = 64
N_EXPERTS = 32
TOP_K = 4
D_FF = 1024
SWIGLU_LIMIT = 7.0
SWIGLU_ALPHA = 1.702
EPS = 1e-6

OFF_B, OFF_C, OFF_X, OFF_CQ, OFF_CKV, OFF_KPE = 0, 512, 1024, 1536, 1920, 2176
IN_PROJ_EXT = 2304

VMEM_LIMIT = 56 * 1024 * 1024

TS_PRE = 1024
TQ = 256
TM_POST = 512
TB_ROWS = 256
TM_EXP = 512
HALO = 8


def _rms(x, g):
    return x * lax.rsqrt(jnp.mean(x * x, axis=-1, keepdims=True) + EPS) * g


ROW_LINES = D_MODEL // 128


def _load_row_tiles(ref, rows, lead=()):
    return jnp.concatenate(
        [ref[lead + (pl.ds(c, rows, stride=ROW_LINES), slice(None))] for c in range(ROW_LINES)],
        axis=1)


def _store_row_tiles(ref, val):
    rows = val.shape[0]
    for c in range(ROW_LINES):
        ref[pl.ds(c, rows, stride=ROW_LINES), :] = val[:, 128 * c:128 * (c + 1)]


def _pre_kernel(x_ref, nm_ref, win_ref, convw_ref, qn_ref, wuq_ref, kvn_ref, wukv_ref,
                gqn_ref, gqr_ref, gkn_ref, gkr_ref, tbl_ref,
                conv_ref, q_ref, k_ref, vt_ref, ext_ref):
    s = pl.program_id(1)
    ts = x_ref.shape[1]
    sub = TQ
    starts = tuple(range(0, ts, sub))

    @pl.when(s == 0)
    def _():
        ext_ref[0:HALO, :] = jnp.zeros((HALO, CONV_WIDTH), F32)

    @pl.when(s != 0)
    def _():
        ext_ref[0:HALO, :] = ext_ref[ts:ts + HALO, :]

    projs = [jnp.dot(_rms(x_ref[0, r0:r0 + sub, :], nm_ref[...]).astype(BF16), win_ref[...],
                     preferred_element_type=F32) for r0 in starts]

    lane = lax.broadcasted_iota(jnp.int32, (sub, 128), 1)
    first_half = lane < QK_ROPE
    scale = math.log2(math.e) / math.sqrt(QK_HEAD)

    for r0, proj in zip(starts, projs):
        rows = slice(r0, r0 + sub)
        u = proj[:, OFF_C:OFF_C + CONV_WIDTH] * proj[:, OFF_X:OFF_X + CONV_WIDTH]
        ext_ref[HALO + r0:HALO + r0 + sub, :] = u
        y = (convw_ref[2:3, :] * u
             + convw_ref[1:2, :] * ext_ref[HALO + r0 - 1:HALO + r0 - 1 + sub, :]
             + convw_ref[0:1, :] * ext_ref[HALO + r0 - 2:HALO + r0 - 2 + sub, :])
        conv_ref[0, rows, :] = (proj[:, OFF_B:OFF_B + CONV_WIDTH] * y).astype(BF16)

        hq = _rms(proj[:, OFF_CQ:OFF_CQ + Q_LORA], qn_ref[...]).astype(BF16)
        qp = jnp.dot(hq, wuq_ref[...], preferred_element_type=F32)
        hkv = _rms(proj[:, OFF_CKV:OFF_CKV + KV_LORA], kvn_ref[...]).astype(BF16)
        kvp = jnp.dot(hkv, wukv_ref[...], preferred_element_type=F32)

        tbl = tbl_ref[rows, :]

        def rope_block(grp, gain):
            r = grp * tbl * gain
            r = r + pltpu.roll(r, QK_ROPE, axis=1)
            return jnp.where(first_half, r, 0.0)

        kpe = proj[:, OFF_KPE:OFF_KPE + 128]
        ss_kpe = jnp.sum(jnp.where(first_half, kpe * kpe, 0.0), axis=-1, keepdims=True)
        k_rope = rope_block(kpe, gkr_ref[...])
        for h in range(MLA_HEADS):
            qn = qp[:, 128 * h:128 * (h + 1)]
            qg = qp[:, 512 + 128 * h:512 + 128 * (h + 1)]
            ss = (jnp.sum(qn * qn, axis=-1, keepdims=True)
                  + jnp.sum(jnp.where(first_half, qg * qg, 0.0), axis=-1, keepdims=True))
            rs = lax.rsqrt(ss * (1.0 / QK_HEAD) + EPS) * scale
            q_ref[0, h, rows, 0:128] = (qn * rs * gqn_ref[...]).astype(BF16)
            q_ref[0, h, rows, 128:256] = (rope_block(qg, gqr_ref[...]) * rs).astype(BF16)

            kn = kvp[:, 256 * h:256 * h + 128]
            ssk = jnp.sum(kn * kn, axis=-1, keepdims=True) + ss_kpe
            rsk = lax.rsqrt(ssk * (1.0 / QK_HEAD) + EPS)
            k_ref[0, h, rows, 0:128] = (kn * rsk * gkn_ref[...]).astype(BF16)
            k_ref[0, h, rows, 128:256] = (k_rope * rsk).astype(BF16)
            v = kvp[:, 256 * h + 128:256 * (h + 1)]
            vt_ref[0, h, r0 // TQ] = v.T.astype(BF16)


def _pre_call(x, nm, win, convw, qn, wuq, kvn, wukv, gqn, gqr, gkn, gkr, tbl):
    b, s, d = x.shape
    ts = TS_PRE
    const = lambda shape: pl.BlockSpec(shape, lambda bi, si: (0,) * len(shape))
    return pl.pallas_call(
        _pre_kernel,
        grid=(b, s // ts),
        in_specs=[
            pl.BlockSpec((1, ts, d), lambda bi, si: (bi, si, 0)),
            const((1, d)), const((d, IN_PROJ_EXT)), const((3, CONV_WIDTH)),
            const((1, Q_LORA)), const((Q_LORA, 1024)),
            const((1, KV_LORA)), const((KV_LORA, 1024)),
            const((1, 128)), const((1, 128)), const((1, 128)), const((1, 128)),
            pl.BlockSpec((ts, 128), lambda bi, si: (si, 0)),
        ],
        out_specs=[
            pl.BlockSpec((1, ts, CONV_WIDTH), lambda bi, si: (bi, si, 0)),
            pl.BlockSpec((1, MLA_HEADS, ts, QK_PAD), lambda bi, si: (bi, 0, si, 0)),
            pl.BlockSpec((1, MLA_HEADS, ts, QK_PAD), lambda bi, si: (bi, 0, si, 0)),
            pl.BlockSpec((1, MLA_HEADS, ts // TQ, V_HEAD, TQ), lambda bi, si: (bi, 0, si, 0, 0)),
        ],
        out_shape=[
            jax.ShapeDtypeStruct((b, s, CONV_WIDTH), BF16),
            jax.ShapeDtypeStruct((b, MLA_HEADS, s, QK_PAD), BF16),
            jax.ShapeDtypeStruct((b, MLA_HEADS, s, QK_PAD), BF16),
            jax.ShapeDtypeStruct((b, MLA_HEADS, s // TQ, V_HEAD, TQ), BF16),
        ],
        scratch_shapes=[pltpu.VMEM((ts + HALO, CONV_WIDTH), F32)],
        compiler_params=pltpu.CompilerParams(
            dimension_semantics=("arbitrary", "arbitrary"), vmem_limit_bytes=VMEM_LIMIT),
        name="pre_mixer",
    )(x, nm, win, convw, qn, wuq, kvn, wukv, gqn, gqr, gkn, gkr, tbl)


def _attn_kernel(q_ref, k_ref, vt_ref, o_ref, m_ref, l_ref, acc_ref, s_ref):
    i = pl.program_id(1)
    tq = q_ref.shape[2]
    key_c = lax.broadcasted_iota(jnp.int32, (tq, tq), 0) // CHUNK
    qry_c = lax.broadcasted_iota(jnp.int32, (tq, tq), 1) // CHUNK
    allowed = key_c <= qry_c
    m_ref[...] = jnp.full(m_ref.shape, -1e30, F32)
    l_ref[...] = jnp.zeros(l_ref.shape, F32)
    acc_ref[...] = jnp.zeros(acc_ref.shape, F32)

    def scores(j, keep):
        start = pl.multiple_of(j * tq, tq)
        out = []
        for h in range(MLA_HEADS):
            kb = k_ref[0, h, pl.ds(start, tq), :]
            st = lax.dot_general(kb, q_ref[0, h], (((1,), (1,)), ((), ())),
                                 preferred_element_type=F32)
            out.append(st if keep is None else jnp.where(keep, st, -1e30))
        return out

    def update(j):
        probs, alphas = [], []
        for h in range(MLA_HEADS):
            st = s_ref[h]
            m = m_ref[h]
            m_new = jnp.maximum(m, jnp.max(st, axis=0, keepdims=True))
            alpha = jnp.exp2(m - m_new)
            p = jnp.exp2(st - m_new)
            l_ref[h] = alpha * l_ref[h] + jnp.sum(p, axis=0, keepdims=True)
            m_ref[h] = m_new
            probs.append(p.astype(BF16))
            alphas.append(alpha)
        for h in range(MLA_HEADS):
            acc_ref[h] = alphas[h] * acc_ref[h] + jnp.dot(vt_ref[0, h, j], probs[h],
                                                           preferred_element_type=F32)

    def stage(sts):
        for h in range(MLA_HEADS):
            s_ref[h] = sts[h]

    stage(scores(0, allowed | (i > 0)))

    def body(j, c):
        nxt = scores(j + 1, None)
        update(j)
        stage(nxt)
        return c

    lax.fori_loop(0, i - 1, body, 0)

    @pl.when(i > 0)
    def _():
        nxt = scores(i, allowed)
        update(i - 1)
        stage(nxt)

    update(i)
    for h in range(MLA_HEADS):
        o = acc_ref[h] / l_ref[h]
        o_ref[0, :, V_HEAD * h:V_HEAD * (h + 1)] = o.T.astype(BF16)


def _attn_call(q, k, vt):
    b, nh, s, _ = q.shape
    return pl.pallas_call(
        _attn_kernel,
        grid=(b, s // TQ),
        in_specs=[
            pl.BlockSpec((1, nh, TQ, QK_PAD), lambda bi, qi: (bi, 0, qi, 0)),
            pl.BlockSpec((1, nh, s, QK_PAD), lambda bi, qi: (bi, 0, 0, 0)),
            pl.BlockSpec((1, nh, s // TQ, V_HEAD, TQ), lambda bi, qi: (bi, 0, 0, 0, 0)),
        ],
        out_specs=pl.BlockSpec((1, TQ, nh * V_HEAD), lambda bi, qi: (bi, qi, 0)),
        out_shape=jax.ShapeDtypeStruct((b, s, nh * V_HEAD), BF16),
        scratch_shapes=[pltpu.VMEM((nh, 1, TQ), F32), pltpu.VMEM((nh, 1, TQ), F32),
                        pltpu.VMEM((nh, V_HEAD, TQ), F32), pltpu.VMEM((nh, TQ, TQ), F32)],
        compiler_params=pltpu.CompilerParams(
            dimension_semantics=("arbitrary", "arbitrary"), vmem_limit_bytes=VMEM_LIMIT),
        name="mla_attention",
    )(q, k, vt)


def _post_kernel(x_ref, conv_ref, attn_ref, wo_ref, g_ref, wr_ref, br_ref, tri_ref,
                 h1_ref, hb_ref, lpos_ref, gate_ref, tile_n_ref, tile_carry_ref, cnt_ref,
                 carry_ref):
    step = pl.program_id(0)
    tm = x_ref.shape[0]

    @pl.when(step == 0)
    def _():
        carry_ref[...] = jnp.zeros_like(carry_ref)

    h1 = (x_ref[...]
          + jnp.dot(conv_ref[...], wo_ref[0:CONV_WIDTH, :], preferred_element_type=F32)
          + jnp.dot(attn_ref[...], wo_ref[CONV_WIDTH:, :], preferred_element_type=F32))
    h1_ref[...] = h1
    hb = _rms(h1, g_ref[...]).astype(BF16)
    hb_ref[...] = hb

    logits = jnp.dot(hb, wr_ref[...], preferred_element_type=F32) + br_ref[...]
    lane = lax.broadcasted_iota(jnp.int32, (tm, N_EXPERTS), 1).astype(F32)
    lane4 = lax.broadcasted_iota(jnp.int32, (tm, TOP_K), 1)
    member = jnp.zeros((tm, N_EXPERTS), F32)
    vals, idxs, hots = [], [], []
    lg = logits
    for _ in range(TOP_K):
        m = jnp.max(lg, axis=-1, keepdims=True)
        ix = jnp.min(jnp.where(lg == m, lane, float(N_EXPERTS)), axis=-1, keepdims=True)
        hot = lane == ix
        lg = jnp.where(hot, -jnp.inf, lg)
        member = member + hot.astype(F32)
        vals.append(m)
        idxs.append(ix)
        hots.append(hot)

    before = jnp.dot(tri_ref[...], member.astype(BF16), preferred_element_type=F32)
    tile_n = jnp.sum(member, axis=0, keepdims=True)
    lpos = [jnp.sum(jnp.where(hot, before, 0.0) + jnp.where(lane < ix, tile_n, 0.0),
                    axis=-1, keepdims=True) for hot, ix in zip(hots, idxs)]
    tile_n_ref[0] = tile_n.astype(jnp.int32)
    tile_carry_ref[0] = carry_ref[...].astype(jnp.int32)
    carry_ref[...] = carry_ref[...] + tile_n
    cnt_ref[...] = carry_ref[...].astype(jnp.int32)

    exps = [jnp.exp(v - vals[0]) for v in vals]
    denom = exps[0] + exps[1] + exps[2] + exps[3]

    def cols(c):
        return jnp.where(lane4 == 0, c[0], jnp.where(lane4 == 1, c[1],
                         jnp.where(lane4 == 2, c[2], c[3])))

    lpos_ref[...] = cols(lpos).astype(jnp.int32)
    gate_ref[...] = cols([e / denom for e in exps])


def _post_call(xf, conv, attn, wo, g, wr, br, tri):
    n, d = xf.shape
    tm = TM_POST
    const = lambda shape: pl.BlockSpec(shape, lambda i: (0,) * len(shape))
    row = lambda w: pl.BlockSpec((tm, w), lambda i: (i, 0))
    return pl.pallas_call(
        _post_kernel,
        grid=(n // tm,),
        in_specs=[row(d), row(CONV_WIDTH), row(CONV_WIDTH), const((d, d)), const((1, d)),
                  const((d, N_EXPERTS)), const((1, N_EXPERTS)), const((tm, tm))],
        out_specs=[row(d), row(d), row(TOP_K), row(TOP_K),
                   pl.BlockSpec((1, 1, N_EXPERTS), lambda i: (i, 0, 0)),
                   pl.BlockSpec((1, 1, N_EXPERTS), lambda i: (i, 0, 0)),
                   const((1, N_EXPERTS))],
        out_shape=[
            jax.ShapeDtypeStruct((n, d), F32),
            jax.ShapeDtypeStruct((n, d), BF16),
            jax.ShapeDtypeStruct((n, TOP_K), jnp.int32),
            jax.ShapeDtypeStruct((n, TOP_K), F32),
            jax.ShapeDtypeStruct((n // tm, 1, N_EXPERTS), jnp.int32),
            jax.ShapeDtypeStruct((n // tm, 1, N_EXPERTS), jnp.int32),
            jax.ShapeDtypeStruct((1, N_EXPERTS), jnp.int32),
        ],
        scratch_shapes=[pltpu.VMEM((1, N_EXPERTS), F32)],
        compiler_params=pltpu.CompilerParams(
            dimension_semantics=("arbitrary",), vmem_limit_bytes=VMEM_LIMIT),
        name="post_mixer_router",
    )(xf, conv, attn, wo, g, wr, br, tri)


RUN_BITS = 10


def _rows(ref, first_row, n_rows):
    return ref.at[pl.ds(pl.multiple_of(first_row * ROW_LINES, ROW_LINES), n_rows * ROW_LINES), :]


def _for_each_run_piece(n_ref, far_ref, fn, bits=RUN_BITS):
    def per_expert(e, local0):
        n_e = n_ref[0, 0, e]
        far0 = far_ref[0, 0, e]

        for b in reversed(range(bits)):
            done = (n_e >> (b + 1)) << (b + 1)

            @pl.when(((n_e >> b) & 1) == 1)
            def _():
                fn(local0 + done, far0 + done, 1 << b)
        return local0 + n_e

    lax.fori_loop(0, N_EXPERTS, per_expert, 0)


PAD_BITS = TM_EXP.bit_length() - 1
assert TM_EXP == 1 << PAD_BITS


def _scatter_kernel(n_ref, far_ref, pad_n_ref, pad_far_ref, used_ref, hb_ref, lpos_ref, xs_ref,
                    xl, zeros, sem, zsem):
    i = pl.program_id(0)
    steps = pl.num_programs(0)
    tm = hb_ref.shape[0]
    pairs = tm * TOP_K
    slot = i % 2

    def wait_slot(s):
        pltpu.make_async_copy(xl.at[s], _rows(xs_ref, 0, pairs), sem.at[s]).wait()

    @pl.when(i == 0)
    def _():
        zeros[...] = jnp.zeros(zeros.shape, F32)

        def fill(_, far_row, size):
            return pltpu.make_async_copy(_rows(zeros, 0, size), _rows(xs_ref, far_row, size), zsem)

        _for_each_run_piece(pad_n_ref, pad_far_ref, lambda *a: fill(*a).start(), PAD_BITS)
        _for_each_run_piece(pad_n_ref, pad_far_ref, lambda *a: fill(*a).wait(), PAD_BITS)

        tile_rows = zeros.shape[0] // ROW_LINES
        n_tiles = xs_ref.shape[0] // zeros.shape[0]

        def tail(op):
            def body(t, c):
                op(fill(None, t * tile_rows, tile_rows))
                return c
            lax.fori_loop(used_ref[0], n_tiles, body, 0)

        tail(lambda cp: cp.start())
        tail(lambda cp: cp.wait())

    @pl.when(i >= 2)
    def _():
        wait_slot(slot)

    lane = lax.broadcasted_iota(jnp.int32, (tm, 128), 1)
    lp = lpos_ref[...].astype(F32)
    cols = jnp.zeros((tm, 128), F32)
    for k in range(TOP_K):
        cols = jnp.where(lane == k, lp[:, k:k + 1], cols)
    rows_t = cols.T
    j = lax.broadcasted_iota(jnp.int32, (pairs, tm), 0).astype(F32)
    onehot = jnp.zeros((pairs, tm), F32)
    for k in range(TOP_K):
        onehot = jnp.where(j == rows_t[k:k + 1, :], 1.0, onehot)
    onehot = onehot.astype(BF16)
    _store_row_tiles(xl.at[slot], jnp.dot(onehot, hb_ref[...], preferred_element_type=F32))

    def send(local_row, far_row, size):
        pltpu.make_async_copy(_rows(xl.at[slot], local_row, size), _rows(xs_ref, far_row, size),
                              sem.at[slot]).start()

    _for_each_run_piece(n_ref, far_ref, send)

    @pl.when(i == steps - 1)
    def _():
        @pl.when(steps >= 2)
        def _():
            wait_slot(1 - slot)
        wait_slot(slot)


def _scatter_call(tile_n, tile_far, pad_n, pad_far, used_tiles, hb, lpos, sorted_rows):
    n, d = hb.shape
    tm = TM_POST
    table = pl.BlockSpec((1, 1, N_EXPERTS), lambda i: (i, 0, 0), memory_space=pltpu.SMEM)
    whole = pl.BlockSpec((1, 1, N_EXPERTS), lambda i: (0, 0, 0), memory_space=pltpu.SMEM)
    return pl.pallas_call(
        _scatter_kernel,
        grid=(n // tm,),
        in_specs=[table, table, whole, whole, pl.BlockSpec(memory_space=pltpu.SMEM),
                  pl.BlockSpec((tm, d), lambda i: (i, 0)),
                  pl.BlockSpec((tm, TOP_K), lambda i: (i, 0))],
        out_specs=pl.BlockSpec(memory_space=pl.ANY),
        out_shape=jax.ShapeDtypeStruct((sorted_rows * ROW_LINES, 128), F32),
        scratch_shapes=[pltpu.VMEM((2, tm * TOP_K * ROW_LINES, 128), F32),
                        pltpu.VMEM((TM_EXP * ROW_LINES, 128), F32),
                        pltpu.SemaphoreType.DMA((2,)), pltpu.SemaphoreType.DMA],
        compiler_params=pltpu.CompilerParams(
            dimension_semantics=("arbitrary",), vmem_limit_bytes=VMEM_LIMIT),
        name="dispatch_scatter",
    )(tile_n, tile_far, pad_n, pad_far, used_tiles, hb, lpos)


def _expert_kernel(tile_ref, exp_ref, flag_ref,
                   xs_ref, wgu_ref, bgu_ref, wd_ref, bd_ref, o_ref, wgu_bf, wd_bf):
    i = pl.program_id(0)
    tm = xs_ref.shape[0] // ROW_LINES
    flag = flag_ref[i]
    cast_rows = 128

    @pl.when((flag & 2) != 0)
    def _():
        def cast(r, c):
            rows = pl.ds(pl.multiple_of(r * cast_rows, cast_rows), cast_rows)
            wgu_bf[rows, :] = wgu_ref[0, rows, :].astype(BF16)
            wd_bf[rows, :] = wd_ref[0, rows, :].astype(BF16)
            return c
        lax.fori_loop(0, D_MODEL // cast_rows, cast, 0)

    @pl.when((flag & 1) != 0)
    def _():
        x = _load_row_tiles(xs_ref, tm).astype(BF16)
        gu = jnp.dot(x, wgu_bf[...], preferred_element_type=F32) + bgu_ref[0]
        gate = jnp.minimum(gu[:, :D_FF], SWIGLU_LIMIT)
        up = jnp.clip(gu[:, D_FF:], -SWIGLU_LIMIT, SWIGLU_LIMIT)
        act = (up + 1.0) * (gate * (1.0 / (1.0 + jnp.exp(-SWIGLU_ALPHA * gate))))
        out = jnp.dot(act.astype(BF16), wd_bf[...], preferred_element_type=F32) + bd_ref[0]
        _store_row_tiles(o_ref, out)

    @pl.when((flag & 1) == 0)
    def _():
        o_ref[...] = jnp.zeros(o_ref.shape, F32)


def _expert_call(meta, xs, wgu, bgu, wd, bd):
    lines, w = xs.shape
    tm = TM_EXP
    n_items = meta[0].shape[0]
    return pl.pallas_call(
        _expert_kernel,
        grid_spec=pltpu.PrefetchScalarGridSpec(
            num_scalar_prefetch=3,
            grid=(n_items,),
            in_specs=[
                pl.BlockSpec((tm * ROW_LINES, w), lambda i, t, e, f: (t[i], 0)),
                pl.BlockSpec((1, D_MODEL, 2 * D_FF), lambda i, t, e, f: (e[i], 0, 0)),
                pl.BlockSpec((1, 1, 2 * D_FF), lambda i, t, e, f: (e[i], 0, 0)),
                pl.BlockSpec((1, D_FF, D_MODEL), lambda i, t, e, f: (e[i], 0, 0)),
                pl.BlockSpec((1, 1, D_MODEL), lambda i, t, e, f: (e[i], 0, 0)),
            ],
            out_specs=pl.BlockSpec((tm * ROW_LINES, w), lambda i, t, e, f: (t[i], 0)),
            scratch_shapes=[pltpu.VMEM((D_MODEL, 2 * D_FF), BF16),
                            pltpu.VMEM((D_FF, D_MODEL), BF16)],
        ),
        out_shape=jax.ShapeDtypeStruct((lines, w), F32),
        compiler_params=pltpu.CompilerParams(
            dimension_semantics=("arbitrary",), vmem_limit_bytes=VMEM_LIMIT),
        name="moe_experts",
    )(*meta, xs, wgu, bgu, wd, bd)


def _expert_layout(counts, n_pairs, tm):
    n_tiles = n_pairs // tm + N_EXPERTS
    per_exp = (counts + tm - 1) // tm
    tile_end = jnp.cumsum(per_exp)
    tile0 = tile_end - per_exp
    used = tile_end[-1]
    tile = jnp.arange(n_tiles, dtype=jnp.int32)
    e = jnp.sum((tile_end[None, :] <= jnp.minimum(tile, used - 1)[:, None]).astype(jnp.int32),
                axis=1)
    e_tile0 = jnp.sum(jnp.where(e[:, None] == jnp.arange(N_EXPERTS, dtype=jnp.int32)[None, :],
                                tile0[None, :], 0), axis=1)
    flag = jnp.where(tile < used, 1 + 2 * (tile == e_tile0), 0)
    cast = lambda a: a.astype(jnp.int32)
    group_start = tile0 * tm
    pad_n = per_exp * tm - counts
    pad_start = group_start + counts
    return ((tile, cast(e), cast(flag)), cast(used).reshape(1), cast(group_start), cast(pad_n),
            cast(pad_start))


def _combine_kernel(n_ref, far_ref, n_next_ref, far_next_ref, h1_ref, lpos_ref, gate_ref, ys_ref,
                    o_ref, yl, sem):
    i = pl.program_id(0)
    steps = pl.num_programs(0)
    tm = h1_ref.shape[0]
    pairs = tm * TOP_K
    slot = i % 2

    def fetch(nr, fr, s):
        def recv(local_row, far_row, size):
            pltpu.make_async_copy(_rows(ys_ref, far_row, size), _rows(yl.at[s], local_row, size),
                                  sem.at[s]).start()
        _for_each_run_piece(nr, fr, recv)

    @pl.when(i == 0)
    def _():
        fetch(n_ref, far_ref, 0)

    @pl.when(i + 1 < steps)
    def _():
        fetch(n_next_ref, far_next_ref, 1 - slot)

    pltpu.make_async_copy(_rows(ys_ref, 0, pairs), yl.at[slot], sem.at[slot]).wait()

    j = lax.broadcasted_iota(jnp.int32, (tm, pairs), 1)
    lp = lpos_ref[...]
    g = gate_ref[...]
    w = jnp.zeros((tm, pairs), F32)
    for k in range(TOP_K):
        w = jnp.where(j == lp[:, k:k + 1], g[:, k:k + 1], w)
    y = _load_row_tiles(yl, pairs, lead=(slot,)).astype(BF16)
    o_ref[...] = h1_ref[...] + jnp.dot(w.astype(BF16), y, preferred_element_type=F32)


def _combine_call(tile_n, tile_far, h1, lpos, gates, ys):
    n, d = h1.shape
    tm = TM_POST
    steps = n // tm
    table = lambda fn: pl.BlockSpec((1, 1, N_EXPERTS), fn, memory_space=pltpu.SMEM)
    cur = lambda i: (i, 0, 0)
    nxt = lambda i: (jnp.minimum(i + 1, steps - 1), 0, 0)
    return pl.pallas_call(
        _combine_kernel,
        grid=(steps,),
        in_specs=[table(cur), table(cur), table(nxt), table(nxt),
                  pl.BlockSpec((tm, d), lambda i: (i, 0)),
                  pl.BlockSpec((tm, TOP_K), lambda i: (i, 0)),
                  pl.BlockSpec((tm, TOP_K), lambda i: (i, 0)),
                  pl.BlockSpec(memory_space=pl.ANY)],
        out_specs=pl.BlockSpec((tm, d), lambda i: (i, 0)),
        out_shape=jax.ShapeDtypeStruct((n, d), F32),
        scratch_shapes=[pltpu.VMEM((2, tm * TOP_K * ROW_LINES, 128), F32),
                        pltpu.SemaphoreType.DMA((2,))],
        compiler_params=pltpu.CompilerParams(
            dimension_semantics=("arbitrary",), vmem_limit_bytes=VMEM_LIMIT),
        name="moe_combine",
    )(tile_n, tile_far, tile_n, tile_far, h1, lpos, gates, ys)


def _rotate_half_perm():
    half = QK_ROPE // 2
    perm = np.concatenate([np.arange(half, QK_ROPE), np.arange(0, half)])
    sign = np.concatenate([-np.ones(half), np.ones(half)]).astype(np.float32)
    return perm, sign


def _layer(h, l, norm_mix, w_in, conv_w, q_norm, w_uq, kv_norm, w_ukv, q_head_norm,
           k_head_norm, w_o, norm_ffn, w_router, b_router, w_gate_up, b_gate_up,
           w_down, b_down):
    b, s, d = h.shape
    n = b * s
    perm, sign = _rotate_half_perm()

    kpe_cols = w_in[l][:, OFF_KPE:OFF_KPE + QK_ROPE]
    win = jnp.concatenate([w_in[l], kpe_cols[:, perm] * sign], axis=1).astype(BF16)
    wq = w_uq[l].reshape(Q_LORA, MLA_HEADS, QK_HEAD)
    wq_rope = wq[:, :, QK_NOPE:]
    wuq = jnp.concatenate(
        [wq[:, :, :QK_NOPE].reshape(Q_LORA, MLA_HEADS * QK_NOPE),
         jnp.concatenate([wq_rope, wq_rope[:, :, perm] * sign], axis=2)
         .reshape(Q_LORA, MLA_HEADS * 2 * QK_ROPE)], axis=1).astype(BF16)
    gq, gk = q_head_norm[l], k_head_norm[l]
    gqn, gkn = gq[None, :QK_NOPE], gk[None, :QK_NOPE]
    gqr = jnp.concatenate([gq[QK_NOPE:], gq[QK_NOPE:][perm]])[None, :]
    gkr = jnp.concatenate([gk[QK_NOPE:], gk[QK_NOPE:][perm]])[None, :]

    half = QK_ROPE // 2
    inv_freq = ROPE_THETA ** (-jnp.arange(half, dtype=F32) / half)
    ang = jnp.arange(s).astype(F32)[:, None] * inv_freq[None, :]
    cos, sin = jnp.cos(ang), jnp.sin(ang)
    tbl = jnp.concatenate([cos, cos, sin, sin], axis=1)

    conv, q, k, vt = _pre_call(
        h, norm_mix[l][None, :], win, conv_w[l], q_norm[l][None, :], wuq,
        kv_norm[l][None, :], w_ukv[l].astype(BF16), gqn, gqr, gkn, gkr, tbl)
    attn = _attn_call(q, k, vt)

    tri = jnp.tril(jnp.ones((TM_POST, TM_POST), BF16), -1)
    h1, hb, lpos, gates, tile_n, tile_carry, counts = _post_call(
        h.reshape(n, d), conv.reshape(n, CONV_WIDTH), attn.reshape(n, CONV_WIDTH),
        w_o[l].astype(BF16), norm_ffn[l][None, :], w_router[l].astype(BF16),
        b_router[l][None, :], tri)

    meta, used_tiles, group_start, pad_n, pad_start = _expert_layout(counts[0], n * TOP_K, TM_EXP)
    sorted_rows = n * TOP_K + N_EXPERTS * TM_EXP
    tile_far = (tile_carry + group_start[None, None, :]).astype(jnp.int32)
    xs = _scatter_call(tile_n, tile_far, pad_n[None, None, :], pad_start[None, None, :],
                       used_tiles, hb, lpos, sorted_rows)
    ys = _expert_call(meta, xs, w_gate_up[l], b_gate_up[l][:, None, :],
                      w_down[l], b_down[l][:, None, :])
    out = _combine_call(tile_n, tile_far, h1, lpos, gates, ys)
    return out.reshape(b, s, d)


def kernel(x, norm_mix, w_in, conv_w, q_norm, w_uq, kv_norm, w_ukv, q_head_norm, k_head_norm,
           w_o, norm_ffn, w_router, b_router, w_gate_up, b_gate_up, w_down, b_down):
    h = x
    for l in range(norm_mix.shape[0]):
        h = _layer(h, l, norm_mix, w_in, conv_w, q_norm, w_uq, kv_norm, w_ukv, q_head_norm,
                   k_head_norm, w_o, norm_ffn, w_router, b_router, w_gate_up, b_gate_up,
                   w_down, b_down)
    return h
```

```python
import math

import jax
import jax.numpy as jnp
import numpy as np
from jax import lax
from jax.experimental import pallas as pl
from jax.experimental.pallas import tpu as pltpu

F32 = jnp.float32
BF16 = jnp.bfloat16

D_MODEL = 1024
CONV_WIDTH = 512
MLA_HEADS = 4
QK_NOPE = 128
QK_ROPE = 64
QK_HEAD = QK_NOPE + QK_ROPE
QK_PAD = 256
V_HEAD = 128
Q_LORA = 384
KV_LORA = 256
ROPE_THETA = 10000.0
CHUNK = 64
N_EXPERTS = 32
TOP_K = 4
D_FF = 1024
SWIGLU_LIMIT = 7.0
SWIGLU_ALPHA = 1.702
EPS = 1e-6

OFF_B, OFF_C, OFF_X, OFF_CQ, OFF_CKV, OFF_KPE = 0, 512, 1024, 1536, 1920, 2176
IN_PROJ_EXT = 2304

VMEM_LIMIT = 56 * 1024 * 1024

TS_PRE = 1024
TQ = 256
TM_POST = 512
SUB_POST = 256
TB_ROWS = 256
TM_EXP = 512
HALO = 8


def _rms(x, g):
    return x * lax.rsqrt(jnp.mean(x * x, axis=-1, keepdims=True) + EPS) * g


ROW_LINES = D_MODEL // 128


def _load_row_tiles(ref, rows, lead=()):
    return jnp.concatenate(
        [ref[lead + (pl.ds(c, rows, stride=ROW_LINES), slice(None))] for c in range(ROW_LINES)],
        axis=1)


def _store_row_tiles(ref, val):
    rows = val.shape[0]
    for c in range(ROW_LINES):
        ref[pl.ds(c, rows, stride=ROW_LINES), :] = val[:, 128 * c:128 * (c + 1)]


def _pre_kernel(x_ref, nm_ref, win_ref, convw_ref, qn_ref, wuq_ref, kvn_ref, wukv_ref,
                gqn_ref, gqr_ref, gkn_ref, gkr_ref, tbl_ref,
                conv_ref, q_ref, k_ref, vt_ref, ext_ref):
    s = pl.program_id(1)
    ts = x_ref.shape[1]
    sub = TQ
    starts = tuple(range(0, ts, sub))

    @pl.when(s == 0)
    def _():
        ext_ref[0:HALO, :] = jnp.zeros((HALO, CONV_WIDTH), F32)

    @pl.when(s != 0)
    def _():
        ext_ref[0:HALO, :] = ext_ref[ts:ts + HALO, :]

    projs = [jnp.dot(_rms(x_ref[0, r0:r0 + sub, :], nm_ref[...]).astype(BF16), win_ref[...],
                     preferred_element_type=F32) for r0 in starts]

    lane = lax.broadcasted_iota(jnp.int32, (sub, 128), 1)
    first_half = lane < QK_ROPE
    scale = math.log2(math.e) / math.sqrt(QK_HEAD)

    for r0, proj in zip(starts, projs):
        rows = slice(r0, r0 + sub)
        u = proj[:, OFF_C:OFF_C + CONV_WIDTH] * proj[:, OFF_X:OFF_X + CONV_WIDTH]
        ext_ref[HALO + r0:HALO + r0 + sub, :] = u
        y = (convw_ref[2:3, :] * u
             + convw_ref[1:2, :] * ext_ref[HALO + r0 - 1:HALO + r0 - 1 + sub, :]
             + convw_ref[0:1, :] * ext_ref[HALO + r0 - 2:HALO + r0 - 2 + sub, :])
        conv_ref[0, rows, :] = (proj[:, OFF_B:OFF_B + CONV_WIDTH] * y).astype(BF16)

        hq = _rms(proj[:, OFF_CQ:OFF_CQ + Q_LORA], qn_ref[...]).astype(BF16)
        qp = jnp.dot(hq, wuq_ref[...], preferred_element_type=F32)
        hkv = _rms(proj[:, OFF_CKV:OFF_CKV + KV_LORA], kvn_ref[...]).astype(BF16)
        kvp = jnp.dot(hkv, wukv_ref[...], preferred_element_type=F32)

        tbl = tbl_ref[rows, :]

        def rope_block(grp, gain):
            r = grp * tbl * gain
            r = r + pltpu.roll(r, QK_ROPE, axis=1)
            return jnp.where(first_half, r, 0.0)

        kpe = proj[:, OFF_KPE:OFF_KPE + 128]
        ss_kpe = jnp.sum(jnp.where(first_half, kpe * kpe, 0.0), axis=-1, keepdims=True)
        k_rope = rope_block(kpe, gkr_ref[...])
        for h in range(MLA_HEADS):
            qn = qp[:, 128 * h:128 * (h + 1)]
            qg = qp[:, 512 + 128 * h:512 + 128 * (h + 1)]
            ss = (jnp.sum(qn * qn, axis=-1, keepdims=True)
                  + jnp.sum(jnp.where(first_half, qg * qg, 0.0), axis=-1, keepdims=True))
            rs = lax.rsqrt(ss * (1.0 / QK_HEAD) + EPS) * scale
            q_ref[0, h, rows, 0:128] = (qn * rs * gqn_ref[...]).astype(BF16)
            q_ref[0, h, rows, 128:256] = (rope_block(qg, gqr_ref[...]) * rs).astype(BF16)

            kn = kvp[:, 256 * h:256 * h + 128]
            ssk = jnp.sum(kn * kn, axis=-1, keepdims=True) + ss_kpe
            rsk = lax.rsqrt(ssk * (1.0 / QK_HEAD) + EPS)
            k_ref[0, h, rows, 0:128] = (kn * rsk * gkn_ref[...]).astype(BF16)
            k_ref[0, h, rows, 128:256] = (k_rope * rsk).astype(BF16)
            v = kvp[:, 256 * h + 128:256 * (h + 1)]
            vt_ref[0, h, r0 // TQ] = v.T.astype(BF16)


def _pre_call(x, nm, win, convw, qn, wuq, kvn, wukv, gqn, gqr, gkn, gkr, tbl):
    b, s, d = x.shape
    ts = TS_PRE
    const = lambda shape: pl.BlockSpec(shape, lambda bi, si: (0,) * len(shape))
    return pl.pallas_call(
        _pre_kernel,
        grid=(b, s // ts),
        in_specs=[
            pl.BlockSpec((1, ts, d), lambda bi, si: (bi, si, 0)),
            const((1, d)), const((d, IN_PROJ_EXT)), const((3, CONV_WIDTH)),
            const((1, Q_LORA)), const((Q_LORA, 1024)),
            const((1, KV_LORA)), const((KV_LORA, 1024)),
            const((1, 128)), const((1, 128)), const((1, 128)), const((1, 128)),
            pl.BlockSpec((ts, 128), lambda bi, si: (si, 0)),
        ],
        out_specs=[
            pl.BlockSpec((1, ts, CONV_WIDTH), lambda bi, si: (bi, si, 0)),
            pl.BlockSpec((1, MLA_HEADS, ts, QK_PAD), lambda bi, si: (bi, 0, si, 0)),
            pl.BlockSpec((1, MLA_HEADS, ts, QK_PAD), lambda bi, si: (bi, 0, si, 0)),
            pl.BlockSpec((1, MLA_HEADS, ts // TQ, V_HEAD, TQ), lambda bi, si: (bi, 0, si, 0, 0)),
        ],
        out_shape=[
            jax.ShapeDtypeStruct((b, s, CONV_WIDTH), BF16),
            jax.ShapeDtypeStruct((b, MLA_HEADS, s, QK_PAD), BF16),
            jax.ShapeDtypeStruct((b, MLA_HEADS, s, QK_PAD), BF16),
            jax.ShapeDtypeStruct((b, MLA_HEADS, s // TQ, V_HEAD, TQ), BF16),
        ],
        scratch_shapes=[pltpu.VMEM((ts + HALO, CONV_WIDTH), F32)],
        compiler_params=pltpu.CompilerParams(
            dimension_semantics=("arbitrary", "arbitrary"), vmem_limit_bytes=VMEM_LIMIT),
        name="pre_mixer",
    )(x, nm, win, convw, qn, wuq, kvn, wukv, gqn, gqr, gkn, gkr, tbl)


def _attn_kernel(q_ref, k_ref, vt_ref, o_ref, m_ref, l_ref, acc_ref, s_ref):
    i = pl.program_id(1)
    tq = q_ref.shape[2]
    key_c = lax.broadcasted_iota(jnp.int32, (tq, tq), 0) // CHUNK
    qry_c = lax.broadcasted_iota(jnp.int32, (tq, tq), 1) // CHUNK
    allowed = key_c <= qry_c
    m_ref[...] = jnp.full(m_ref.shape, -1e30, F32)
    l_ref[...] = jnp.zeros(l_ref.shape, F32)
    acc_ref[...] = jnp.zeros(acc_ref.shape, F32)

    def scores(j, keep):
        start = pl.multiple_of(j * tq, tq)
        out = []
        for h in range(MLA_HEADS):
            kb = k_ref[0, h, pl.ds(start, tq), :]
            st = lax.dot_general(kb, q_ref[0, h], (((1,), (1,)), ((), ())),
                                 preferred_element_type=F32)
            out.append(st if keep is None else jnp.where(keep, st, -1e30))
        return out

    def update(j):
        probs, alphas = [], []
        for h in range(MLA_HEADS):
            st = s_ref[h]
            m = m_ref[h]
            m_new = jnp.maximum(m, jnp.max(st, axis=0, keepdims=True))
            alpha = jnp.exp2(m - m_new)
            p = jnp.exp2(st - m_new)
            l_ref[h] = alpha * l_ref[h] + jnp.sum(p, axis=0, keepdims=True)
            m_ref[h] = m_new
            probs.append(p.astype(BF16))
            alphas.append(alpha)
        for h in range(MLA_HEADS):
            acc_ref[h] = alphas[h] * acc_ref[h] + jnp.dot(vt_ref[0, h, j], probs[h],
                                                           preferred_element_type=F32)

    def stage(sts):
        for h in range(MLA_HEADS):
            s_ref[h] = sts[h]

    stage(scores(0, allowed | (i > 0)))

    def body(j, c):
        nxt = scores(j + 1, None)
        update(j)
        stage(nxt)
        return c

    lax.fori_loop(0, i - 1, body, 0)

    @pl.when(i > 0)
    def _():
        nxt = scores(i, allowed)
        update(i - 1)
        stage(nxt)

    update(i)
    for h in range(MLA_HEADS):
        o = acc_ref[h] / l_ref[h]
        o_ref[0, :, V_HEAD * h:V_HEAD * (h + 1)] = o.T.astype(BF16)


def _attn_call(q, k, vt):
    b, nh, s, _ = q.shape
    return pl.pallas_call(
        _attn_kernel,
        grid=(b, s // TQ),
        in_specs=[
            pl.BlockSpec((1, nh, TQ, QK_PAD), lambda bi, qi: (bi, 0, qi, 0)),
            pl.BlockSpec((1, nh, s, QK_PAD), lambda bi, qi: (bi, 0, 0, 0)),
            pl.BlockSpec((1, nh, s // TQ, V_HEAD, TQ), lambda bi, qi: (bi, 0, 0, 0, 0)),
        ],
        out_specs=pl.BlockSpec((1, TQ, nh * V_HEAD), lambda bi, qi: (bi, qi, 0)),
        out_shape=jax.ShapeDtypeStruct((b, s, nh * V_HEAD), BF16),
        scratch_shapes=[pltpu.VMEM((nh, 1, TQ), F32), pltpu.VMEM((nh, 1, TQ), F32),
                        pltpu.VMEM((nh, V_HEAD, TQ), F32), pltpu.VMEM((nh, TQ, TQ), F32)],
        compiler_params=pltpu.CompilerParams(
            dimension_semantics=("arbitrary", "arbitrary"), vmem_limit_bytes=VMEM_LIMIT),
        name="mla_attention",
    )(q, k, vt)


def _post_kernel(x_ref, conv_ref, attn_ref, wo_ref, g_ref, wr_ref, br_ref, tri_ref,
                 h1_ref, hb_ref, lpos_ref, gate_ref, tile_n_ref, tile_carry_ref, cnt_ref,
                 carry_ref):
    step = pl.program_id(0)
    tm = x_ref.shape[0]

    @pl.when(step == 0)
    def _():
        carry_ref[...] = jnp.zeros_like(carry_ref)

    sub = tri_ref.shape[0]
    subs = [slice(r0, r0 + sub) for r0 in range(0, tm, sub)]
    h1s = [x_ref[rows, :]
           + jnp.dot(conv_ref[rows, :], wo_ref[0:CONV_WIDTH, :], preferred_element_type=F32)
           + jnp.dot(attn_ref[rows, :], wo_ref[CONV_WIDTH:, :], preferred_element_type=F32)
           for rows in subs]
    lane = lax.broadcasted_iota(jnp.int32, (sub, N_EXPERTS), 1).astype(F32)
    lane4 = lax.broadcasted_iota(jnp.int32, (sub, TOP_K), 1)

    def cols(c):
        return jnp.where(lane4 == 0, c[0], jnp.where(lane4 == 1, c[1],
                         jnp.where(lane4 == 2, c[2], c[3])))

    picks, members = [], []
    for rows, h1 in zip(subs, h1s):
        h1_ref[rows, :] = h1
        hb = _rms(h1, g_ref[...]).astype(BF16)
        hb_ref[rows, :] = hb
        lg = jnp.dot(hb, wr_ref[...], preferred_element_type=F32) + br_ref[...]
        member = jnp.zeros((sub, N_EXPERTS), F32)
        vals, idxs, hots = [], [], []
        for _ in range(TOP_K):
            m = jnp.max(lg, axis=-1, keepdims=True)
            ix = jnp.min(jnp.where(lg == m, lane, float(N_EXPERTS)), axis=-1, keepdims=True)
            hot = lane == ix
            lg = jnp.where(hot, -jnp.inf, lg)
            member = member + hot.astype(F32)
            vals.append(m)
            idxs.append(ix)
            hots.append(hot)
        exps = [jnp.exp(v - vals[0]) for v in vals]
        denom = exps[0] + exps[1] + exps[2] + exps[3]
        gate_ref[rows, :] = cols([e / denom for e in exps])
        picks.append((idxs, hots))
        members.append(member)

    sub_n = [jnp.sum(member, axis=0, keepdims=True) for member in members]
    tile_n = sub_n[0]
    for n_s in sub_n[1:]:
        tile_n = tile_n + n_s
    earlier = jnp.zeros((1, N_EXPERTS), F32)
    for rows, member, n_s, (idxs, hots) in zip(subs, members, sub_n, picks):
        before = jnp.dot(tri_ref[...], member.astype(BF16), preferred_element_type=F32) + earlier
        lpos = [jnp.sum(jnp.where(hot, before, 0.0) + jnp.where(lane < ix, tile_n, 0.0),
                        axis=-1, keepdims=True) for hot, ix in zip(hots, idxs)]
        lpos_ref[rows, :] = cols(lpos).astype(jnp.int32)
        earlier = earlier + n_s
    tile_n_ref[0] = tile_n.astype(jnp.int32)
    tile_carry_ref[0] = carry_ref[...].astype(jnp.int32)
    carry_ref[...] = carry_ref[...] + tile_n
    cnt_ref[...] = carry_ref[...].astype(jnp.int32)


def _post_call(xf, conv, attn, wo, g, wr, br, tri):
    n, d = xf.shape
    tm = TM_POST
    const = lambda shape: pl.BlockSpec(shape, lambda i: (0,) * len(shape))
    row = lambda w: pl.BlockSpec((tm, w), lambda i: (i, 0))
    return pl.pallas_call(
        _post_kernel,
        grid=(n // tm,),
        in_specs=[row(d), row(CONV_WIDTH), row(CONV_WIDTH), const((d, d)), const((1, d)),
                  const((d, N_EXPERTS)), const((1, N_EXPERTS)), const(tri.shape)],
        out_specs=[row(d), row(d), row(TOP_K), row(TOP_K),
                   pl.BlockSpec((1, 1, N_EXPERTS), lambda i: (i, 0, 0)),
                   pl.BlockSpec((1, 1, N_EXPERTS), lambda i: (i, 0, 0)),
                   const((1, N_EXPERTS))],
        out_shape=[
            jax.ShapeDtypeStruct((n, d), F32),
            jax.ShapeDtypeStruct((n, d), BF16),
            jax.ShapeDtypeStruct((n, TOP_K), jnp.int32),
            jax.ShapeDtypeStruct((n, TOP_K), F32),
            jax.ShapeDtypeStruct((n // tm, 1, N_EXPERTS), jnp.int32),
            jax.ShapeDtypeStruct((n // tm, 1, N_EXPERTS), jnp.int32),
            jax.ShapeDtypeStruct((1, N_EXPERTS), jnp.int32),
        ],
        scratch_shapes=[pltpu.VMEM((1, N_EXPERTS), F32)],
        compiler_params=pltpu.CompilerParams(
            dimension_semantics=("arbitrary",), vmem_limit_bytes=VMEM_LIMIT),
        name="post_mixer_router",
    )(xf, conv, attn, wo, g, wr, br, tri)


RUN_BITS = 10


def _rows(ref, first_row, n_rows):
    return ref.at[pl.ds(pl.multiple_of(first_row * ROW_LINES, ROW_LINES), n_rows * ROW_LINES), :]


def _for_each_run_piece(n_ref, far_ref, fn, bits=RUN_BITS):
    def per_expert(e, local0):
        n_e = n_ref[0, 0, e]
        far0 = far_ref[0, 0, e]

        for b in reversed(range(bits)):
            done = (n_e >> (b + 1)) << (b + 1)

            @pl.when(((n_e >> b) & 1) == 1)
            def _():
                fn(local0 + done, far0 + done, 1 << b)
        return local0 + n_e

    lax.fori_loop(0, N_EXPERTS, per_expert, 0)


PAD_BITS = TM_EXP.bit_length() - 1
assert TM_EXP == 1 << PAD_BITS


def _scatter_kernel(n_ref, far_ref, pad_n_ref, pad_far_ref, used_ref, hb_ref, lpos_ref, xs_ref,
                    xl, zeros, sem, zsem):
    i = pl.program_id(0)
    steps = pl.num_programs(0)
    tm = hb_ref.shape[0]
    pairs = tm * TOP_K
    slot = i % 2

    def wait_slot(s):
        pltpu.make_async_copy(xl.at[s], _rows(xs_ref, 0, pairs), sem.at[s]).wait()

    @pl.when(i == 0)
    def _():
        zeros[...] = jnp.zeros(zeros.shape, F32)

        def fill(_, far_row, size):
            return pltpu.make_async_copy(_rows(zeros, 0, size), _rows(xs_ref, far_row, size), zsem)

        _for_each_run_piece(pad_n_ref, pad_far_ref, lambda *a: fill(*a).start(), PAD_BITS)
        _for_each_run_piece(pad_n_ref, pad_far_ref, lambda *a: fill(*a).wait(), PAD_BITS)

        tile_rows = zeros.shape[0] // ROW_LINES
        n_tiles = xs_ref.shape[0] // zeros.shape[0]

        def tail(op):
            def body(t, c):
                op(fill(None, t * tile_rows, tile_rows))
                return c
            lax.fori_loop(used_ref[0], n_tiles, body, 0)

        tail(lambda cp: cp.start())
        tail(lambda cp: cp.wait())

    @pl.when(i >= 2)
    def _():
        wait_slot(slot)

    lane = lax.broadcasted_iota(jnp.int32, (tm, 128), 1)
    lp = lpos_ref[...].astype(F32)
    cols = jnp.zeros((tm, 128), F32)
    for k in range(TOP_K):
        cols = jnp.where(lane == k, lp[:, k:k + 1], cols)
    rows_t = cols.T
    j = lax.broadcasted_iota(jnp.int32, (pairs, tm), 0).astype(F32)
    onehot = jnp.zeros((pairs, tm), F32)
    for k in range(TOP_K):
        onehot = jnp.where(j == rows_t[k:k + 1, :], 1.0, onehot)
    onehot = onehot.astype(BF16)
    _store_row_tiles(xl.at[slot], jnp.dot(onehot, hb_ref[...], preferred_element_type=F32))

    def send(local_row, far_row, size):
        pltpu.make_async_copy(_rows(xl.at[slot], local_row, size), _rows(xs_ref, far_row, size),
                              sem.at[slot]).start()

    _for_each_run_piece(n_ref, far_ref, send)

    @pl.when(i == steps - 1)
    def _():
        @pl.when(steps >= 2)
        def _():
            wait_slot(1 - slot)
        wait_slot(slot)


def _scatter_call(tile_n, tile_far, pad_n, pad_far, used_tiles, hb, lpos, sorted_rows):
    n, d = hb.shape
    tm = TM_POST
    table = pl.BlockSpec((1, 1, N_EXPERTS), lambda i: (i, 0, 0), memory_space=pltpu.SMEM)
    whole = pl.BlockSpec((1, 1, N_EXPERTS), lambda i: (0, 0, 0), memory_space=pltpu.SMEM)
    return pl.pallas_call(
        _scatter_kernel,
        grid=(n // tm,),
        in_specs=[table, table, whole, whole, pl.BlockSpec(memory_space=pltpu.SMEM),
                  pl.BlockSpec((tm, d), lambda i: (i, 0)),
                  pl.BlockSpec((tm, TOP_K), lambda i: (i, 0))],
        out_specs=pl.BlockSpec(memory_space=pl.ANY),
        out_shape=jax.ShapeDtypeStruct((sorted_rows * ROW_LINES, 128), F32),
        scratch_shapes=[pltpu.VMEM((2, tm * TOP_K * ROW_LINES, 128), F32),
                        pltpu.VMEM((TM_EXP * ROW_LINES, 128), F32),
                        pltpu.SemaphoreType.DMA((2,)), pltpu.SemaphoreType.DMA],
        compiler_params=pltpu.CompilerParams(
            dimension_semantics=("arbitrary",), vmem_limit_bytes=VMEM_LIMIT),
        name="dispatch_scatter",
    )(tile_n, tile_far, pad_n, pad_far, used_tiles, hb, lpos)


def _expert_kernel(tile_ref, exp_ref, flag_ref,
                   xs_ref, wgu_ref, bgu_ref, wd_ref, bd_ref, o_ref, wgu_bf, wd_bf):
    i = pl.program_id(0)
    tm = xs_ref.shape[0] // ROW_LINES
    flag = flag_ref[i]
    cast_rows = 128

    @pl.when((flag & 2) != 0)
    def _():
        def cast(r, c):
            rows = pl.ds(pl.multiple_of(r * cast_rows, cast_rows), cast_rows)
            wgu_bf[rows, :] = wgu_ref[0, rows, :].astype(BF16)
            wd_bf[rows, :] = wd_ref[0, rows, :].astype(BF16)
            return c
        lax.fori_loop(0, D_MODEL // cast_rows, cast, 0)

    @pl.when((flag & 1) != 0)
    def _():
        x = _load_row_tiles(xs_ref, tm).astype(BF16)
        gu = jnp.dot(x, wgu_bf[...], preferred_element_type=F32) + bgu_ref[0]
        gate = jnp.minimum(gu[:, :D_FF], SWIGLU_LIMIT)
        up = jnp.clip(gu[:, D_FF:], -SWIGLU_LIMIT, SWIGLU_LIMIT)
        act = (up + 1.0) * (gate * (1.0 / (1.0 + jnp.exp(-SWIGLU_ALPHA * gate))))
        out = jnp.dot(act.astype(BF16), wd_bf[...], preferred_element_type=F32) + bd_ref[0]
        _store_row_tiles(o_ref, out)

    @pl.when((flag & 1) == 0)
    def _():
        o_ref[...] = jnp.zeros(o_ref.shape, F32)


def _expert_call(meta, xs, wgu, bgu, wd, bd):
    lines, w = xs.shape
    tm = TM_EXP
    n_items = meta[0].shape[0]
    return pl.pallas_call(
        _expert_kernel,
        grid_spec=pltpu.PrefetchScalarGridSpec(
            num_scalar_prefetch=3,
            grid=(n_items,),
            in_specs=[
                pl.BlockSpec((tm * ROW_LINES, w), lambda i, t, e, f: (t[i], 0)),
                pl.BlockSpec((1, D_MODEL, 2 * D_FF), lambda i, t, e, f: (e[i], 0, 0)),
                pl.BlockSpec((1, 1, 2 * D_FF), lambda i, t, e, f: (e[i], 0, 0)),
                pl.BlockSpec((1, D_FF, D_MODEL), lambda i, t, e, f: (e[i], 0, 0)),
                pl.BlockSpec((1, 1, D_MODEL), lambda i, t, e, f: (e[i], 0, 0)),
            ],
            out_specs=pl.BlockSpec((tm * ROW_LINES, w), lambda i, t, e, f: (t[i], 0)),
            scratch_shapes=[pltpu.VMEM((D_MODEL, 2 * D_FF), BF16),
                            pltpu.VMEM((D_FF, D_MODEL), BF16)],
        ),
        out_shape=jax.ShapeDtypeStruct((lines, w), F32),
        compiler_params=pltpu.CompilerParams(
            dimension_semantics=("arbitrary",), vmem_limit_bytes=VMEM_LIMIT),
        name="moe_experts",
    )(*meta, xs, wgu, bgu, wd, bd)


def _expert_layout(counts, n_pairs, tm):
    n_tiles = n_pairs // tm + N_EXPERTS
    per_exp = (counts + tm - 1) // tm
    tile_end = jnp.cumsum(per_exp)
    tile0 = tile_end - per_exp
    used = tile_end[-1]
    tile = jnp.arange(n_tiles, dtype=jnp.int32)
    e = jnp.sum((tile_end[None, :] <= jnp.minimum(tile, used - 1)[:, None]).astype(jnp.int32),
                axis=1)
    e_tile0 = jnp.sum(jnp.where(e[:, None] == jnp.arange(N_EXPERTS, dtype=jnp.int32)[None, :],
                                tile0[None, :], 0), axis=1)
    flag = jnp.where(tile < used, 1 + 2 * (tile == e_tile0), 0)
    cast = lambda a: a.astype(jnp.int32)
    group_start = tile0 * tm
    pad_n = per_exp * tm - counts
    pad_start = group_start + counts
    return ((tile, cast(e), cast(flag)), cast(used).reshape(1), cast(group_start), cast(pad_n),
            cast(pad_start))


def _combine_kernel(n_ref, far_ref, n_next_ref, far_next_ref, h1_ref, lpos_ref, gate_ref, ys_ref,
                    o_ref, yl, sem):
    i = pl.program_id(0)
    steps = pl.num_programs(0)
    tm = h1_ref.shape[0]
    pairs = tm * TOP_K
    slot = i % 2

    def fetch(nr, fr, s):
        def recv(local_row, far_row, size):
            pltpu.make_async_copy(_rows(ys_ref, far_row, size), _rows(yl.at[s], local_row, size),
                                  sem.at[s]).start()
        _for_each_run_piece(nr, fr, recv)

    @pl.when(i == 0)
    def _():
        fetch(n_ref, far_ref, 0)

    @pl.when(i + 1 < steps)
    def _():
        fetch(n_next_ref, far_next_ref, 1 - slot)

    pltpu.make_async_copy(_rows(ys_ref, 0, pairs), yl.at[slot], sem.at[slot]).wait()

    j = lax.broadcasted_iota(jnp.int32, (tm, pairs), 1)
    lp = lpos_ref[...]
    g = gate_ref[...]
    w = jnp.zeros((tm, pairs), F32)
    for k in range(TOP_K):
        w = jnp.where(j == lp[:, k:k + 1], g[:, k:k + 1], w)
    y = _load_row_tiles(yl, pairs, lead=(slot,)).astype(BF16)
    o_ref[...] = h1_ref[...] + jnp.dot(w.astype(BF16), y, preferred_element_type=F32)


def _combine_call(tile_n, tile_far, h1, lpos, gates, ys):
    n, d = h1.shape
    tm = TM_POST
    steps = n // tm
    table = lambda fn: pl.BlockSpec((1, 1, N_EXPERTS), fn, memory_space=pltpu.SMEM)
    cur = lambda i: (i, 0, 0)
    nxt = lambda i: (jnp.minimum(i + 1, steps - 1), 0, 0)
    return pl.pallas_call(
        _combine_kernel,
        grid=(steps,),
        in_specs=[table(cur), table(cur), table(nxt), table(nxt),
                  pl.BlockSpec((tm, d), lambda i: (i, 0)),
                  pl.BlockSpec((tm, TOP_K), lambda i: (i, 0)),
                  pl.BlockSpec((tm, TOP_K), lambda i: (i, 0)),
                  pl.BlockSpec(memory_space=pl.ANY)],
        out_specs=pl.BlockSpec((tm, d), lambda i: (i, 0)),
        out_shape=jax.ShapeDtypeStruct((n, d), F32),
        scratch_shapes=[pltpu.VMEM((2, tm * TOP_K * ROW_LINES, 128), F32),
                        pltpu.SemaphoreType.DMA((2,))],
        compiler_params=pltpu.CompilerParams(
            dimension_semantics=("arbitrary",), vmem_limit_bytes=VMEM_LIMIT),
        name="moe_combine",
    )(tile_n, tile_far, tile_n, tile_far, h1, lpos, gates, ys)


def _rotate_half_perm():
    half = QK_ROPE // 2
    perm = np.concatenate([np.arange(half, QK_ROPE), np.arange(0, half)])
    sign = np.concatenate([-np.ones(half), np.ones(half)]).astype(np.float32)
    return perm, sign


def _layer(h, l, norm_mix, w_in, conv_w, q_norm, w_uq, kv_norm, w_ukv, q_head_norm,
           k_head_norm, w_o, norm_ffn, w_router, b_router, w_gate_up, b_gate_up,
           w_down, b_down):
    b, s, d = h.shape
    n = b * s
    perm, sign = _rotate_half_perm()

    kpe_cols = w_in[l][:, OFF_KPE:OFF_KPE + QK_ROPE]
    win = jnp.concatenate([w_in[l], kpe_cols[:, perm] * sign], axis=1).astype(BF16)
    wq = w_uq[l].reshape(Q_LORA, MLA_HEADS, QK_HEAD)
    wq_rope = wq[:, :, QK_NOPE:]
    wuq = jnp.concatenate(
        [wq[:, :, :QK_NOPE].reshape(Q_LORA, MLA_HEADS * QK_NOPE),
         jnp.concatenate([wq_rope, wq_rope[:, :, perm] * sign], axis=2)
         .reshape(Q_LORA, MLA_HEADS * 2 * QK_ROPE)], axis=1).astype(BF16)
    gq, gk = q_head_norm[l], k_head_norm[l]
    gqn, gkn = gq[None, :QK_NOPE], gk[None, :QK_NOPE]
    gqr = jnp.concatenate([gq[QK_NOPE:], gq[QK_NOPE:][perm]])[None, :]
    gkr = jnp.concatenate([gk[QK_NOPE:], gk[QK_NOPE:][perm]])[None, :]

    half = QK_ROPE // 2
    inv_freq = ROPE_THETA ** (-jnp.arange(half, dtype=F32) / half)
    ang = jnp.arange(s).astype(F32)[:, None] * inv_freq[None, :]
    cos, sin = jnp.cos(ang), jnp.sin(ang)
    tbl = jnp.concatenate([cos, cos, sin, sin], axis=1)

    conv, q, k, vt = _pre_call(
        h, norm_mix[l][None, :], win, conv_w[l], q_norm[l][None, :], wuq,
        kv_norm[l][None, :], w_ukv[l].astype(BF16), gqn, gqr, gkn, gkr, tbl)
    attn = _attn_call(q, k, vt)

    tri = jnp.tril(jnp.ones((SUB_POST, SUB_POST), BF16), -1)
    h1, hb, lpos, gates, tile_n, tile_carry, counts = _post_call(
        h.reshape(n, d), conv.reshape(n, CONV_WIDTH), attn.reshape(n, CONV_WIDTH),
        w_o[l].astype(BF16), norm_ffn[l][None, :], w_router[l].astype(BF16),
        b_router[l][None, :], tri)

    meta, used_tiles, group_start, pad_n, pad_start = _expert_layout(counts[0], n * TOP_K, TM_EXP)
    sorted_rows = n * TOP_K + N_EXPERTS * TM_EXP
    tile_far = (tile_carry + group_start[None, None, :]).astype(jnp.int32)
    xs = _scatter_call(tile_n, tile_far, pad_n[None, None, :], pad_start[None, None, :],
                       used_tiles, hb, lpos, sorted_rows)
    ys = _expert_call(meta, xs, w_gate_up[l], b_gate_up[l][:, None, :],
                      w_down[l], b_down[l][:, None, :])
    out = _combine_call(tile_n, tile_far, h1, lpos, gates, ys)
    return out.reshape(b, s, d)


def kernel(x, norm_mix, w_in, conv_w, q_norm, w_uq, kv_norm, w_ukv, q_head_norm, k_head_norm,
           w_o, norm_ffn, w_router, b_router, w_gate_up, b_gate_up, w_down, b_down):
    h = x
    for l in range(norm_mix.shape[0]):
        h = _layer(h, l, norm_mix, w_in, conv_w, q_norm, w_uq, kv_norm, w_ukv, q_head_norm,
                   k_head_norm, w_o, norm_ffn, w_router, b_router, w_gate_up, b_gate_up,
                   w_down, b_down)
    return h
```

```python
import math

import jax
import jax.numpy as jnp
import numpy as np
from jax import lax
from jax.experimental import pallas as pl
from jax.experimental.pallas import tpu as pltpu

F32 = jnp.float32
BF16 = jnp.bfloat16

D_MODEL = 1024
CONV_WIDTH = 512
MLA_HEADS = 4
QK_NOPE = 128
QK_ROPE = 64
QK_HEAD = QK_NOPE + QK_ROPE
QK_PAD = 256
V_HEAD = 128
Q_LORA = 384
KV_LORA = 256
ROPE_THETA = 10000.0
CHUNK = 64
N_EXPERTS = 32
TOP_K = 4
D_FF = 1024
SWIGLU_LIMIT = 7.0
SWIGLU_ALPHA = 1.702
EPS = 1e-6

OFF_B, OFF_C, OFF_X, OFF_CQ, OFF_CKV, OFF_KPE = 0, 512, 1024, 1536, 1920, 2176
IN_PROJ_EXT = 2304

VMEM_LIMIT = 56 * 1024 * 1024

TS_PRE = 1024
TQ = 256
TM_POST = 512
SUB_POST = 256
TB_ROWS = 256
TM_EXP = 512
HALO = 8


def _rms(x, g):
    return x * lax.rsqrt(jnp.mean(x * x, axis=-1, keepdims=True) + EPS) * g


ROW_LINES = D_MODEL // 128


def _load_row_tiles(ref, rows, lead=()):
    return jnp.concatenate(
        [ref[lead + (pl.ds(c, rows, stride=ROW_LINES), slice(None))] for c in range(ROW_LINES)],
        axis=1)


def _store_row_tiles(ref, val):
    rows = val.shape[0]
    for c in range(ROW_LINES):
        ref[pl.ds(c, rows, stride=ROW_LINES), :] = val[:, 128 * c:128 * (c + 1)]


def _pre_kernel(x_ref, nm_ref, win_ref, convw_ref, qn_ref, wuq_ref, kvn_ref, wukv_ref,
                gqn_ref, gqr_ref, gkn_ref, gkr_ref, tbl_ref,
                conv_ref, q_ref, k_ref, vt_ref, ext_ref):
    s = pl.program_id(1)
    ts = x_ref.shape[1]
    sub = TQ
    starts = tuple(range(0, ts, sub))

    @pl.when(s == 0)
    def _():
        ext_ref[0:HALO, :] = jnp.zeros((HALO, CONV_WIDTH), F32)

    @pl.when(s != 0)
    def _():
        ext_ref[0:HALO, :] = ext_ref[ts:ts + HALO, :]

    projs = [jnp.dot(_rms(x_ref[0, r0:r0 + sub, :], nm_ref[...]).astype(BF16), win_ref[...],
                     preferred_element_type=F32) for r0 in starts]

    lane = lax.broadcasted_iota(jnp.int32, (sub, 128), 1)
    first_half = lane < QK_ROPE
    scale = math.log2(math.e) / math.sqrt(QK_HEAD)

    for r0, proj in zip(starts, projs):
        rows = slice(r0, r0 + sub)
        u = proj[:, OFF_C:OFF_C + CONV_WIDTH] * proj[:, OFF_X:OFF_X + CONV_WIDTH]
        ext_ref[HALO + r0:HALO + r0 + sub, :] = u
        y = (convw_ref[2:3, :] * u
             + convw_ref[1:2, :] * ext_ref[HALO + r0 - 1:HALO + r0 - 1 + sub, :]
             + convw_ref[0:1, :] * ext_ref[HALO + r0 - 2:HALO + r0 - 2 + sub, :])
        conv_ref[0, rows, :] = (proj[:, OFF_B:OFF_B + CONV_WIDTH] * y).astype(BF16)

        hq = _rms(proj[:, OFF_CQ:OFF_CQ + Q_LORA], qn_ref[...]).astype(BF16)
        qp = jnp.dot(hq, wuq_ref[...], preferred_element_type=F32)
        hkv = _rms(proj[:, OFF_CKV:OFF_CKV + KV_LORA], kvn_ref[...]).astype(BF16)
        kvp = jnp.dot(hkv, wukv_ref[...], preferred_element_type=F32)

        tbl = tbl_ref[rows, :]

        def rope_block(grp, gain):
            r = grp * tbl * gain
            r = r + pltpu.roll(r, QK_ROPE, axis=1)
            return jnp.where(first_half, r, 0.0)

        kpe = proj[:, OFF_KPE:OFF_KPE + 128]
        ss_kpe = jnp.sum(jnp.where(first_half, kpe * kpe, 0.0), axis=-1, keepdims=True)
        k_rope = rope_block(kpe, gkr_ref[...])
        for h in range(MLA_HEADS):
            qn = qp[:, 128 * h:128 * (h + 1)]
            qg = qp[:, 512 + 128 * h:512 + 128 * (h + 1)]
            ss = (jnp.sum(qn * qn, axis=-1, keepdims=True)
                  + jnp.sum(jnp.where(first_half, qg * qg, 0.0), axis=-1, keepdims=True))
            rs = lax.rsqrt(ss * (1.0 / QK_HEAD) + EPS) * scale
            q_ref[0, h, rows, 0:128] = (qn * rs * gqn_ref[...]).astype(BF16)
            q_ref[0, h, rows, 128:256] = (rope_block(qg, gqr_ref[...]) * rs).astype(BF16)

            kn = kvp[:, 256 * h:256 * h + 128]
            ssk = jnp.sum(kn * kn, axis=-1, keepdims=True) + ss_kpe
            rsk = lax.rsqrt(ssk * (1.0 / QK_HEAD) + EPS)
            k_ref[0, h, rows, 0:128] = (kn * rsk * gkn_ref[...]).astype(BF16)
            k_ref[0, h, rows, 128:256] = (k_rope * rsk).astype(BF16)
            v = kvp[:, 256 * h + 128:256 * (h + 1)]
            vt_ref[0, h, r0 // TQ] = v.T.astype(BF16)


def _pre_call(x, nm, win, convw, qn, wuq, kvn, wukv, gqn, gqr, gkn, gkr, tbl):
    b, s, d = x.shape
    ts = TS_PRE
    const = lambda shape: pl.BlockSpec(shape, lambda bi, si: (0,) * len(shape))
    return pl.pallas_call(
        _pre_kernel,
        grid=(b, s // ts),
        in_specs=[
            pl.BlockSpec((1, ts, d), lambda bi, si: (bi, si, 0)),
            const((1, d)), const((d, IN_PROJ_EXT)), const((3, CONV_WIDTH)),
            const((1, Q_LORA)), const((Q_LORA, 1024)),
            const((1, KV_LORA)), const((KV_LORA, 1024)),
            const((1, 128)), const((1, 128)), const((1, 128)), const((1, 128)),
            pl.BlockSpec((ts, 128), lambda bi, si: (si, 0)),
        ],
        out_specs=[
            pl.BlockSpec((1, ts, CONV_WIDTH), lambda bi, si: (bi, si, 0)),
            pl.BlockSpec((1, MLA_HEADS, ts, QK_PAD), lambda bi, si: (bi, 0, si, 0)),
            pl.BlockSpec((1, MLA_HEADS, ts, QK_PAD), lambda bi, si: (bi, 0, si, 0)),
            pl.BlockSpec((1, MLA_HEADS, ts // TQ, V_HEAD, TQ), lambda bi, si: (bi, 0, si, 0, 0)),
        ],
        out_shape=[
            jax.ShapeDtypeStruct((b, s, CONV_WIDTH), BF16),
            jax.ShapeDtypeStruct((b, MLA_HEADS, s, QK_PAD), BF16),
            jax.ShapeDtypeStruct((b, MLA_HEADS, s, QK_PAD), BF16),
            jax.ShapeDtypeStruct((b, MLA_HEADS, s // TQ, V_HEAD, TQ), BF16),
        ],
        scratch_shapes=[pltpu.VMEM((ts + HALO, CONV_WIDTH), F32)],
        compiler_params=pltpu.CompilerParams(
            dimension_semantics=("arbitrary", "arbitrary"), vmem_limit_bytes=VMEM_LIMIT),
        name="pre_mixer",
    )(x, nm, win, convw, qn, wuq, kvn, wukv, gqn, gqr, gkn, gkr, tbl)


def _attn_kernel(q_ref, k_ref, vt_ref, o_ref, m_ref, l_ref, acc_ref, s_ref):
    i = pl.program_id(1)
    tq = q_ref.shape[2]
    key_c = lax.broadcasted_iota(jnp.int32, (tq, tq), 0) // CHUNK
    qry_c = lax.broadcasted_iota(jnp.int32, (tq, tq), 1) // CHUNK
    allowed = key_c <= qry_c
    m_ref[...] = jnp.full(m_ref.shape, -1e30, F32)
    l_ref[...] = jnp.zeros(l_ref.shape, F32)
    acc_ref[...] = jnp.zeros(acc_ref.shape, F32)

    def scores(j, keep):
        start = pl.multiple_of(j * tq, tq)
        out = []
        for h in range(MLA_HEADS):
            kb = k_ref[0, h, pl.ds(start, tq), :]
            st = lax.dot_general(kb, q_ref[0, h], (((1,), (1,)), ((), ())),
                                 preferred_element_type=F32)
            out.append(st if keep is None else jnp.where(keep, st, -1e30))
        return out

    def update(j):
        probs, alphas = [], []
        for h in range(MLA_HEADS):
            st = s_ref[h]
            m = m_ref[h]
            m_new = jnp.maximum(m, jnp.max(st, axis=0, keepdims=True))
            alpha = jnp.exp2(m - m_new)
            p = jnp.exp2(st - m_new)
            l_ref[h] = alpha * l_ref[h] + jnp.sum(p, axis=0, keepdims=True)
            m_ref[h] = m_new
            probs.append(p.astype(BF16))
            alphas.append(alpha)
        for h in range(MLA_HEADS):
            acc_ref[h] = alphas[h] * acc_ref[h] + jnp.dot(vt_ref[0, h, j], probs[h],
                                                           preferred_element_type=F32)

    def stage(sts):
        for h in range(MLA_HEADS):
            s_ref[h] = sts[h]

    stage(scores(0, allowed | (i > 0)))

    def body(j, c):
        nxt = scores(j + 1, None)
        update(j)
        stage(nxt)
        return c

    lax.fori_loop(0, i - 1, body, 0)

    @pl.when(i > 0)
    def _():
        nxt = scores(i, allowed)
        update(i - 1)
        stage(nxt)

    update(i)
    for h in range(MLA_HEADS):
        o = acc_ref[h] / l_ref[h]
        o_ref[0, :, V_HEAD * h:V_HEAD * (h + 1)] = o.T.astype(BF16)


def _attn_call(q, k, vt):
    b, nh, s, _ = q.shape
    return pl.pallas_call(
        _attn_kernel,
        grid=(b, s // TQ),
        in_specs=[
            pl.BlockSpec((1, nh, TQ, QK_PAD), lambda bi, qi: (bi, 0, qi, 0)),
            pl.BlockSpec((1, nh, s, QK_PAD), lambda bi, qi: (bi, 0, 0, 0)),
            pl.BlockSpec((1, nh, s // TQ, V_HEAD, TQ), lambda bi, qi: (bi, 0, 0, 0, 0)),
        ],
        out_specs=pl.BlockSpec((1, TQ, nh * V_HEAD), lambda bi, qi: (bi, qi, 0)),
        out_shape=jax.ShapeDtypeStruct((b, s, nh * V_HEAD), BF16),
        scratch_shapes=[pltpu.VMEM((nh, 1, TQ), F32), pltpu.VMEM((nh, 1, TQ), F32),
                        pltpu.VMEM((nh, V_HEAD, TQ), F32), pltpu.VMEM((nh, TQ, TQ), F32)],
        compiler_params=pltpu.CompilerParams(
            dimension_semantics=("arbitrary", "arbitrary"), vmem_limit_bytes=VMEM_LIMIT),
        name="mla_attention",
    )(q, k, vt)


def _post_kernel(x_ref, conv_ref, attn_ref, wo_ref, g_ref, wr_ref, br_ref, tri_ref,
                 h1_ref, hb_ref, lpos_ref, gate_ref, tile_n_ref, tile_carry_ref, cnt_ref,
                 carry_ref):
    step = pl.program_id(0)
    tm = x_ref.shape[0]

    @pl.when(step == 0)
    def _():
        carry_ref[...] = jnp.zeros_like(carry_ref)

    sub = tri_ref.shape[0]
    subs = [slice(r0, r0 + sub) for r0 in range(0, tm, sub)]
    h1s = [x_ref[rows, :]
           + jnp.dot(conv_ref[rows, :], wo_ref[0:CONV_WIDTH, :], preferred_element_type=F32)
           + jnp.dot(attn_ref[rows, :], wo_ref[CONV_WIDTH:, :], preferred_element_type=F32)
           for rows in subs]
    lane = lax.broadcasted_iota(jnp.int32, (sub, N_EXPERTS), 1).astype(F32)
    lane4 = lax.broadcasted_iota(jnp.int32, (sub, TOP_K), 1)

    def cols(c):
        return jnp.where(lane4 == 0, c[0], jnp.where(lane4 == 1, c[1],
                         jnp.where(lane4 == 2, c[2], c[3])))

    picks, members = [], []
    for rows, h1 in zip(subs, h1s):
        h1_ref[rows, :] = h1
        hb = _rms(h1, g_ref[...]).astype(BF16)
        hb_ref[rows, :] = hb
        lg = jnp.dot(hb, wr_ref[...], preferred_element_type=F32) + br_ref[...]
        member = jnp.zeros((sub, N_EXPERTS), F32)
        vals, idxs, hots = [], [], []
        for _ in range(TOP_K):
            m = jnp.max(lg, axis=-1, keepdims=True)
            ix = jnp.min(jnp.where(lg == m, lane, float(N_EXPERTS)), axis=-1, keepdims=True)
            hot = lane == ix
            lg = jnp.where(hot, -jnp.inf, lg)
            member = member + hot.astype(F32)
            vals.append(m)
            idxs.append(ix)
            hots.append(hot)
        exps = [jnp.exp(v - vals[0]) for v in vals]
        denom = exps[0] + exps[1] + exps[2] + exps[3]
        gate_ref[rows, :] = cols([e / denom for e in exps])
        picks.append((idxs, hots))
        members.append(member)

    sub_n = [jnp.sum(member, axis=0, keepdims=True) for member in members]
    tile_n = sub_n[0]
    for n_s in sub_n[1:]:
        tile_n = tile_n + n_s
    earlier = jnp.zeros((1, N_EXPERTS), F32)
    for rows, member, n_s, (idxs, hots) in zip(subs, members, sub_n, picks):
        before = jnp.dot(tri_ref[...], member.astype(BF16), preferred_element_type=F32) + earlier
        lpos = [jnp.sum(jnp.where(hot, before, 0.0) + jnp.where(lane < ix, tile_n, 0.0),
                        axis=-1, keepdims=True) for hot, ix in zip(hots, idxs)]
        lpos_ref[rows, :] = cols(lpos).astype(jnp.int32)
        earlier = earlier + n_s
    tile_n_ref[0] = tile_n.astype(jnp.int32)
    tile_carry_ref[0] = carry_ref[...].astype(jnp.int32)
    carry_ref[...] = carry_ref[...] + tile_n
    cnt_ref[...] = carry_ref[...].astype(jnp.int32)


def _post_call(xf, conv, attn, wo, g, wr, br, tri):
    n, d = xf.shape
    tm = TM_POST
    const = lambda shape: pl.BlockSpec(shape, lambda i: (0,) * len(shape))
    row = lambda w: pl.BlockSpec((tm, w), lambda i: (i, 0))
    return pl.pallas_call(
        _post_kernel,
        grid=(n // tm,),
        in_specs=[row(d), row(CONV_WIDTH), row(CONV_WIDTH), const((d, d)), const((1, d)),
                  const((d, N_EXPERTS)), const((1, N_EXPERTS)), const(tri.shape)],
        out_specs=[row(d), row(d), row(TOP_K), row(TOP_K),
                   pl.BlockSpec((1, 1, N_EXPERTS), lambda i: (i, 0, 0)),
                   pl.BlockSpec((1, 1, N_EXPERTS), lambda i: (i, 0, 0)),
                   const((1, N_EXPERTS))],
        out_shape=[
            jax.ShapeDtypeStruct((n, d), F32),
            jax.ShapeDtypeStruct((n, d), BF16),
            jax.ShapeDtypeStruct((n, TOP_K), jnp.int32),
            jax.ShapeDtypeStruct((n, TOP_K), F32),
            jax.ShapeDtypeStruct((n // tm, 1, N_EXPERTS), jnp.int32),
            jax.ShapeDtypeStruct((n // tm, 1, N_EXPERTS), jnp.int32),
            jax.ShapeDtypeStruct((1, N_EXPERTS), jnp.int32),
        ],
        scratch_shapes=[pltpu.VMEM((1, N_EXPERTS), F32)],
        compiler_params=pltpu.CompilerParams(
            dimension_semantics=("arbitrary",), vmem_limit_bytes=VMEM_LIMIT),
        name="post_mixer_router",
    )(xf, conv, attn, wo, g, wr, br, tri)


RUN_BITS = 10


def _rows(ref, first_row, n_rows):
    return ref.at[pl.ds(pl.multiple_of(first_row * ROW_LINES, ROW_LINES), n_rows * ROW_LINES), :]


def _for_each_run_piece(n_ref, far_ref, fn, bits=RUN_BITS):
    def per_expert(e, local0):
        n_e = n_ref[0, 0, e]
        far0 = far_ref[0, 0, e]

        for b in reversed(range(bits)):
            done = (n_e >> (b + 1)) << (b + 1)

            @pl.when(((n_e >> b) & 1) == 1)
            def _():
                fn(local0 + done, far0 + done, 1 << b)
        return local0 + n_e

    lax.fori_loop(0, N_EXPERTS, per_expert, 0)


PAD_BITS = TM_EXP.bit_length() - 1
assert TM_EXP == 1 << PAD_BITS


def _scatter_kernel(n_ref, far_ref, pad_n_ref, pad_far_ref, used_ref, hb_ref, lpos_ref, xs_ref,
                    xl, zeros, sem, zsem):
    i = pl.program_id(0)
    steps = pl.num_programs(0)
    tm = hb_ref.shape[0]
    pairs = tm * TOP_K
    slot = i % 2

    def wait_slot(s):
        pltpu.make_async_copy(xl.at[s], _rows(xs_ref, 0, pairs), sem.at[s]).wait()

    @pl.when(i == 0)
    def _():
        zeros[...] = jnp.zeros(zeros.shape, F32)

        def fill(_, far_row, size):
            return pltpu.make_async_copy(_rows(zeros, 0, size), _rows(xs_ref, far_row, size), zsem)

        _for_each_run_piece(pad_n_ref, pad_far_ref, lambda *a: fill(*a).start(), PAD_BITS)
        _for_each_run_piece(pad_n_ref, pad_far_ref, lambda *a: fill(*a).wait(), PAD_BITS)

        tile_rows = zeros.shape[0] // ROW_LINES
        n_tiles = xs_ref.shape[0] // zeros.shape[0]

        def tail(op):
            def body(t, c):
                op(fill(None, t * tile_rows, tile_rows))
                return c
            lax.fori_loop(used_ref[0], n_tiles, body, 0)

        tail(lambda cp: cp.start())
        tail(lambda cp: cp.wait())

    @pl.when(i >= 2)
    def _():
        wait_slot(slot)

    lane = lax.broadcasted_iota(jnp.int32, (tm, 128), 1)
    lp = lpos_ref[...].astype(F32)
    cols = jnp.zeros((tm, 128), F32)
    for k in range(TOP_K):
        cols = jnp.where(lane == k, lp[:, k:k + 1], cols)
    rows_t = cols.T.astype(jnp.int16)
    j = lax.broadcasted_iota(jnp.int32, (pairs, tm), 0).astype(jnp.int16)
    onehot = jnp.zeros((pairs, tm), BF16)
    for k in range(TOP_K):
        onehot = jnp.where(j == rows_t[k:k + 1, :], jnp.ones((), BF16), onehot)
    _store_row_tiles(xl.at[slot], jnp.dot(onehot, hb_ref[...], preferred_element_type=F32))

    def send(local_row, far_row, size):
        pltpu.make_async_copy(_rows(xl.at[slot], local_row, size), _rows(xs_ref, far_row, size),
                              sem.at[slot]).start()

    _for_each_run_piece(n_ref, far_ref, send)

    @pl.when(i == steps - 1)
    def _():
        @pl.when(steps >= 2)
        def _():
            wait_slot(1 - slot)
        wait_slot(slot)


def _scatter_call(tile_n, tile_far, pad_n, pad_far, used_tiles, hb, lpos, sorted_rows):
    n, d = hb.shape
    tm = TM_POST
    table = pl.BlockSpec((1, 1, N_EXPERTS), lambda i: (i, 0, 0), memory_space=pltpu.SMEM)
    whole = pl.BlockSpec((1, 1, N_EXPERTS), lambda i: (0, 0, 0), memory_space=pltpu.SMEM)
    return pl.pallas_call(
        _scatter_kernel,
        grid=(n // tm,),
        in_specs=[table, table, whole, whole, pl.BlockSpec(memory_space=pltpu.SMEM),
                  pl.BlockSpec((tm, d), lambda i: (i, 0)),
                  pl.BlockSpec((tm, TOP_K), lambda i: (i, 0))],
        out_specs=pl.BlockSpec(memory_space=pl.ANY),
        out_shape=jax.ShapeDtypeStruct((sorted_rows * ROW_LINES, 128), F32),
        scratch_shapes=[pltpu.VMEM((2, tm * TOP_K * ROW_LINES, 128), F32),
                        pltpu.VMEM((TM_EXP * ROW_LINES, 128), F32),
                        pltpu.SemaphoreType.DMA((2,)), pltpu.SemaphoreType.DMA],
        compiler_params=pltpu.CompilerParams(
            dimension_semantics=("arbitrary",), vmem_limit_bytes=VMEM_LIMIT),
        name="dispatch_scatter",
    )(tile_n, tile_far, pad_n, pad_far, used_tiles, hb, lpos)


def _expert_kernel(tile_ref, exp_ref, flag_ref,
                   xs_ref, wgu_ref, bgu_ref, wd_ref, bd_ref, o_ref, wgu_bf, wd_bf):
    i = pl.program_id(0)
    tm = xs_ref.shape[0] // ROW_LINES
    flag = flag_ref[i]
    cast_rows = 128

    @pl.when((flag & 2) != 0)
    def _():
        def cast(r, c):
            rows = pl.ds(pl.multiple_of(r * cast_rows, cast_rows), cast_rows)
            wgu_bf[rows, :] = wgu_ref[0, rows, :].astype(BF16)
            wd_bf[rows, :] = wd_ref[0, rows, :].astype(BF16)
            return c
        lax.fori_loop(0, D_MODEL // cast_rows, cast, 0)

    @pl.when((flag & 1) != 0)
    def _():
        x = _load_row_tiles(xs_ref, tm).astype(BF16)
        gu = jnp.dot(x, wgu_bf[...], preferred_element_type=F32) + bgu_ref[0]
        gate = jnp.minimum(gu[:, :D_FF], SWIGLU_LIMIT)
        up = jnp.clip(gu[:, D_FF:], -SWIGLU_LIMIT, SWIGLU_LIMIT)
        act = (up + 1.0) * (gate * (1.0 / (1.0 + jnp.exp(-SWIGLU_ALPHA * gate))))
        out = jnp.dot(act.astype(BF16), wd_bf[...], preferred_element_type=F32) + bd_ref[0]
        _store_row_tiles(o_ref, out)

    @pl.when((flag & 1) == 0)
    def _():
        o_ref[...] = jnp.zeros(o_ref.shape, F32)


def _expert_call(meta, xs, wgu, bgu, wd, bd):
    lines, w = xs.shape
    tm = TM_EXP
    n_items = meta[0].shape[0]
    return pl.pallas_call(
        _expert_kernel,
        grid_spec=pltpu.PrefetchScalarGridSpec(
            num_scalar_prefetch=3,
            grid=(n_items,),
            in_specs=[
                pl.BlockSpec((tm * ROW_LINES, w), lambda i, t, e, f: (t[i], 0)),
                pl.BlockSpec((1, D_MODEL, 2 * D_FF), lambda i, t, e, f: (e[i], 0, 0)),
                pl.BlockSpec((1, 1, 2 * D_FF), lambda i, t, e, f: (e[i], 0, 0)),
                pl.BlockSpec((1, D_FF, D_MODEL), lambda i, t, e, f: (e[i], 0, 0)),
                pl.BlockSpec((1, 1, D_MODEL), lambda i, t, e, f: (e[i], 0, 0)),
            ],
            out_specs=pl.BlockSpec((tm * ROW_LINES, w), lambda i, t, e, f: (t[i], 0)),
            scratch_shapes=[pltpu.VMEM((D_MODEL, 2 * D_FF), BF16),
                            pltpu.VMEM((D_FF, D_MODEL), BF16)],
        ),
        out_shape=jax.ShapeDtypeStruct((lines, w), F32),
        compiler_params=pltpu.CompilerParams(
            dimension_semantics=("arbitrary",), vmem_limit_bytes=VMEM_LIMIT),
        name="moe_experts",
    )(*meta, xs, wgu, bgu, wd, bd)


def _expert_layout(counts, n_pairs, tm):
    n_tiles = n_pairs // tm + N_EXPERTS
    per_exp = (counts + tm - 1) // tm
    tile_end = jnp.cumsum(per_exp)
    tile0 = tile_end - per_exp
    used = tile_end[-1]
    tile = jnp.arange(n_tiles, dtype=jnp.int32)
    e = jnp.sum((tile_end[None, :] <= jnp.minimum(tile, used - 1)[:, None]).astype(jnp.int32),
                axis=1)
    e_tile0 = jnp.sum(jnp.where(e[:, None] == jnp.arange(N_EXPERTS, dtype=jnp.int32)[None, :],
                                tile0[None, :], 0), axis=1)
    flag = jnp.where(tile < used, 1 + 2 * (tile == e_tile0), 0)
    cast = lambda a: a.astype(jnp.int32)
    group_start = tile0 * tm
    pad_n = per_exp * tm - counts
    pad_start = group_start + counts
    return ((tile, cast(e), cast(flag)), cast(used).reshape(1), cast(group_start), cast(pad_n),
            cast(pad_start))


def _combine_kernel(n_ref, far_ref, n_next_ref, far_next_ref, h1_ref, lpos_ref, gate_ref, ys_ref,
                    o_ref, yl, sem):
    i = pl.program_id(0)
    steps = pl.num_programs(0)
    tm = h1_ref.shape[0]
    pairs = tm * TOP_K
    slot = i % 2

    def fetch(nr, fr, s):
        def recv(local_row, far_row, size):
            pltpu.make_async_copy(_rows(ys_ref, far_row, size), _rows(yl.at[s], local_row, size),
                                  sem.at[s]).start()
        _for_each_run_piece(nr, fr, recv)

    @pl.when(i == 0)
    def _():
        fetch(n_ref, far_ref, 0)

    @pl.when(i + 1 < steps)
    def _():
        fetch(n_next_ref, far_next_ref, 1 - slot)

    pltpu.make_async_copy(_rows(ys_ref, 0, pairs), yl.at[slot], sem.at[slot]).wait()

    j = lax.broadcasted_iota(jnp.int32, (tm, pairs), 1).astype(jnp.int16)
    lp = lpos_ref[...].astype(jnp.int16)
    g = gate_ref[...].astype(BF16)
    w = jnp.zeros((tm, pairs), BF16)
    for k in range(TOP_K):
        w = jnp.where(j == lp[:, k:k + 1], g[:, k:k + 1], w)
    y = _load_row_tiles(yl, pairs, lead=(slot,)).astype(BF16)
    o_ref[...] = h1_ref[...] + jnp.dot(w, y, preferred_element_type=F32)


def _combine_call(tile_n, tile_far, h1, lpos, gates, ys):
    n, d = h1.shape
    tm = TM_POST
    steps = n // tm
    table = lambda fn: pl.BlockSpec((1, 1, N_EXPERTS), fn, memory_space=pltpu.SMEM)
    cur = lambda i: (i, 0, 0)
    nxt = lambda i: (jnp.minimum(i + 1, steps - 1), 0, 0)
    return pl.pallas_call(
        _combine_kernel,
        grid=(steps,),
        in_specs=[table(cur), table(cur), table(nxt), table(nxt),
                  pl.BlockSpec((tm, d), lambda i: (i, 0)),
                  pl.BlockSpec((tm, TOP_K), lambda i: (i, 0)),
                  pl.BlockSpec((tm, TOP_K), lambda i: (i, 0)),
                  pl.BlockSpec(memory_space=pl.ANY)],
        out_specs=pl.BlockSpec((tm, d), lambda i: (i, 0)),
        out_shape=jax.ShapeDtypeStruct((n, d), F32),
        scratch_shapes=[pltpu.VMEM((2, tm * TOP_K * ROW_LINES, 128), F32),
                        pltpu.SemaphoreType.DMA((2,))],
        compiler_params=pltpu.CompilerParams(
            dimension_semantics=("arbitrary",), vmem_limit_bytes=VMEM_LIMIT),
        name="moe_combine",
    )(tile_n, tile_far, tile_n, tile_far, h1, lpos, gates, ys)


def _rotate_half_perm():
    half = QK_ROPE // 2
    perm = np.concatenate([np.arange(half, QK_ROPE), np.arange(0, half)])
    sign = np.concatenate([-np.ones(half), np.ones(half)]).astype(np.float32)
    return perm, sign


def _layer(h, l, norm_mix, w_in, conv_w, q_norm, w_uq, kv_norm, w_ukv, q_head_norm,
           k_head_norm, w_o, norm_ffn, w_router, b_router, w_gate_up, b_gate_up,
           w_down, b_down):
    b, s, d = h.shape
    n = b * s
    perm, sign = _rotate_half_perm()

    kpe_cols = w_in[l][:, OFF_KPE:OFF_KPE + QK_ROPE]
    win = jnp.concatenate([w_in[l], kpe_cols[:, perm] * sign], axis=1).astype(BF16)
    wq = w_uq[l].reshape(Q_LORA, MLA_HEADS, QK_HEAD)
    wq_rope = wq[:, :, QK_NOPE:]
    wuq = jnp.concatenate(
        [wq[:, :, :QK_NOPE].reshape(Q_LORA, MLA_HEADS * QK_NOPE),
         jnp.concatenate([wq_rope, wq_rope[:, :, perm] * sign], axis=2)
         .reshape(Q_LORA, MLA_HEADS * 2 * QK_ROPE)], axis=1).astype(BF16)
    gq, gk = q_head_norm[l], k_head_norm[l]
    gqn, gkn = gq[None, :QK_NOPE], gk[None, :QK_NOPE]
    gqr = jnp.concatenate([gq[QK_NOPE:], gq[QK_NOPE:][perm]])[None, :]
    gkr = jnp.concatenate([gk[QK_NOPE:], gk[QK_NOPE:][perm]])[None, :]

    half = QK_ROPE // 2
    inv_freq = ROPE_THETA ** (-jnp.arange(half, dtype=F32) / half)
    ang = jnp.arange(s).astype(F32)[:, None] * inv_freq[None, :]
    cos, sin = jnp.cos(ang), jnp.sin(ang)
    tbl = jnp.concatenate([cos, cos, sin, sin], axis=1)

    conv, q, k, vt = _pre_call(
        h, norm_mix[l][None, :], win, conv_w[l], q_norm[l][None, :], wuq,
        kv_norm[l][None, :], w_ukv[l].astype(BF16), gqn, gqr, gkn, gkr, tbl)
    attn = _attn_call(q, k, vt)

    tri = jnp.tril(jnp.ones((SUB_POST, SUB_POST), BF16), -1)
    h1, hb, lpos, gates, tile_n, tile_carry, counts = _post_call(
        h.reshape(n, d), conv.reshape(n, CONV_WIDTH), attn.reshape(n, CONV_WIDTH),
        w_o[l].astype(BF16), norm_ffn[l][None, :], w_router[l].astype(BF16),
        b_router[l][None, :], tri)

    meta, used_tiles, group_start, pad_n, pad_start = _expert_layout(counts[0], n * TOP_K, TM_EXP)
    sorted_rows = n * TOP_K + N_EXPERTS * TM_EXP
    tile_far = (tile_carry + group_start[None, None, :]).astype(jnp.int32)
    xs = _scatter_call(tile_n, tile_far, pad_n[None, None, :], pad_start[None, None, :],
                       used_tiles, hb, lpos, sorted_rows)
    ys = _expert_call(meta, xs, w_gate_up[l], b_gate_up[l][:, None, :],
                      w_down[l], b_down[l][:, None, :])
    out = _combine_call(tile_n, tile_far, h1, lpos, gates, ys)
    return out.reshape(b, s, d)


def kernel(x, norm_mix, w_in, conv_w, q_norm, w_uq, kv_norm, w_ukv, q_head_norm, k_head_norm,
           w_o, norm_ffn, w_router, b_router, w_gate_up, b_gate_up, w_down, b_down):
    h = x
    for l in range(norm_mix.shape[0]):
        h = _layer(h, l, norm_mix, w_in, conv_w, q_norm, w_uq, kv_norm, w_ukv, q_head_norm,
                   k_head_norm, w_o, norm_ffn, w_router, b_router, w_gate_up, b_gate_up,
                   w_down, b_down)
    return h
```

```python
import math

import jax
import jax.numpy as jnp
import numpy as np
from jax import lax
from jax.experimental import pallas as pl
from jax.experimental.pallas import tpu as pltpu

F32 = jnp.float32
BF16 = jnp.bfloat16

D_MODEL = 1024
CONV_WIDTH = 512
MLA_HEADS = 4
QK_NOPE = 128
QK_ROPE = 64
QK_HEAD = QK_NOPE + QK_ROPE
QK_PAD = 256
V_HEAD = 128
VT_ROWS = V_HEAD + 16
Q_LORA = 384
KV_LORA = 256
ROPE_THETA = 10000.0
CHUNK = 64
N_EXPERTS = 32
TOP_K = 4
D_FF = 1024
SWIGLU_LIMIT = 7.0
SWIGLU_ALPHA = 1.702
EPS = 1e-6

OFF_B, OFF_C, OFF_X, OFF_CQ, OFF_CKV, OFF_KPE = 0, 512, 1024, 1536, 1920, 2176
IN_PROJ_EXT = 2304

VMEM_LIMIT = 56 * 1024 * 1024

TS_PRE = 1024
TQ = 256
TM_POST = 512
SUB_POST = 256
TB_ROWS = 256
TM_EXP = 512
HALO = 8


def _rms(x, g):
    return x * lax.rsqrt(jnp.mean(x * x, axis=-1, keepdims=True) + EPS) * g


ROW_LINES = D_MODEL // 128


def _load_row_tiles(ref, rows, lead=()):
    return jnp.concatenate(
        [ref[lead + (pl.ds(c, rows, stride=ROW_LINES), slice(None))] for c in range(ROW_LINES)],
        axis=1)


def _store_row_tiles(ref, val):
    rows = val.shape[0]
    for c in range(ROW_LINES):
        ref[pl.ds(c, rows, stride=ROW_LINES), :] = val[:, 128 * c:128 * (c + 1)]


def _pre_kernel(x_ref, nm_ref, win_ref, convw_ref, qn_ref, wuq_ref, kvn_ref, wukv_ref,
                gqn_ref, gqr_ref, gkn_ref, gkr_ref, tbl_ref,
                conv_ref, q_ref, k_ref, vt_ref, ext_ref):
    s = pl.program_id(1)
    ts = x_ref.shape[1]
    sub = TQ
    starts = tuple(range(0, ts, sub))

    @pl.when(s == 0)
    def _():
        ext_ref[0:HALO, :] = jnp.zeros((HALO, CONV_WIDTH), F32)

    @pl.when(s != 0)
    def _():
        ext_ref[0:HALO, :] = ext_ref[ts:ts + HALO, :]

    projs = [jnp.dot(_rms(x_ref[0, r0:r0 + sub, :], nm_ref[...]).astype(BF16), win_ref[...],
                     preferred_element_type=F32) for r0 in starts]

    lane = lax.broadcasted_iota(jnp.int32, (sub, 128), 1)
    first_half = lane < QK_ROPE
    scale = math.log2(math.e) / math.sqrt(QK_HEAD)

    for r0, proj in zip(starts, projs):
        rows = slice(r0, r0 + sub)
        u = proj[:, OFF_C:OFF_C + CONV_WIDTH] * proj[:, OFF_X:OFF_X + CONV_WIDTH]
        ext_ref[HALO + r0:HALO + r0 + sub, :] = u
        y = (convw_ref[2:3, :] * u
             + convw_ref[1:2, :] * ext_ref[HALO + r0 - 1:HALO + r0 - 1 + sub, :]
             + convw_ref[0:1, :] * ext_ref[HALO + r0 - 2:HALO + r0 - 2 + sub, :])
        conv_ref[0, rows, :] = (proj[:, OFF_B:OFF_B + CONV_WIDTH] * y).astype(BF16)

        hq = _rms(proj[:, OFF_CQ:OFF_CQ + Q_LORA], qn_ref[...]).astype(BF16)
        qp = jnp.dot(hq, wuq_ref[...], preferred_element_type=F32)
        hkv = _rms(proj[:, OFF_CKV:OFF_CKV + KV_LORA], kvn_ref[...]).astype(BF16)
        kvp = jnp.dot(hkv, wukv_ref[...], preferred_element_type=F32)

        tbl = tbl_ref[rows, :]

        def rope_block(grp, gain):
            r = grp * tbl * gain
            r = r + pltpu.roll(r, QK_ROPE, axis=1)
            return jnp.where(first_half, r, 0.0)

        kpe = proj[:, OFF_KPE:OFF_KPE + 128]
        ss_kpe = jnp.sum(jnp.where(first_half, kpe * kpe, 0.0), axis=-1, keepdims=True)
        k_rope = rope_block(kpe, gkr_ref[...])
        for h in range(MLA_HEADS):
            qn = qp[:, 128 * h:128 * (h + 1)]
            qg = qp[:, 512 + 128 * h:512 + 128 * (h + 1)]
            ss = (jnp.sum(qn * qn, axis=-1, keepdims=True)
                  + jnp.sum(jnp.where(first_half, qg * qg, 0.0), axis=-1, keepdims=True))
            rs = lax.rsqrt(ss * (1.0 / QK_HEAD) + EPS) * scale
            q_ref[0, h, rows, 0:128] = (qn * rs * gqn_ref[...]).astype(BF16)
            q_ref[0, h, rows, 128:256] = (rope_block(qg, gqr_ref[...]) * rs).astype(BF16)

            kn = kvp[:, 256 * h:256 * h + 128]
            ssk = jnp.sum(kn * kn, axis=-1, keepdims=True) + ss_kpe
            rsk = lax.rsqrt(ssk * (1.0 / QK_HEAD) + EPS)
            k_ref[0, h, rows, 0:128] = (kn * rsk * gkn_ref[...]).astype(BF16)
            k_ref[0, h, rows, 128:256] = (k_rope * rsk).astype(BF16)
            v = kvp[:, 256 * h + 128:256 * (h + 1)]
            vt_ref[0, h, r0 // TQ, 0:V_HEAD, :] = v.T.astype(BF16)
            ones_row = lax.broadcasted_iota(jnp.int32, (VT_ROWS - V_HEAD, TQ), 0) == 0
            vt_ref[0, h, r0 // TQ, V_HEAD:VT_ROWS, :] = jnp.where(ones_row, 1.0, 0.0).astype(BF16)


def _pre_call(x, nm, win, convw, qn, wuq, kvn, wukv, gqn, gqr, gkn, gkr, tbl):
    b, s, d = x.shape
    ts = TS_PRE
    const = lambda shape: pl.BlockSpec(shape, lambda bi, si: (0,) * len(shape))
    return pl.pallas_call(
        _pre_kernel,
        grid=(b, s // ts),
        in_specs=[
            pl.BlockSpec((1, ts, d), lambda bi, si: (bi, si, 0)),
            const((1, d)), const((d, IN_PROJ_EXT)), const((3, CONV_WIDTH)),
            const((1, Q_LORA)), const((Q_LORA, 1024)),
            const((1, KV_LORA)), const((KV_LORA, 1024)),
            const((1, 128)), const((1, 128)), const((1, 128)), const((1, 128)),
            pl.BlockSpec((ts, 128), lambda bi, si: (si, 0)),
        ],
        out_specs=[
            pl.BlockSpec((1, ts, CONV_WIDTH), lambda bi, si: (bi, si, 0)),
            pl.BlockSpec((1, MLA_HEADS, ts, QK_PAD), lambda bi, si: (bi, 0, si, 0)),
            pl.BlockSpec((1, MLA_HEADS, ts, QK_PAD), lambda bi, si: (bi, 0, si, 0)),
            pl.BlockSpec((1, MLA_HEADS, ts // TQ, VT_ROWS, TQ), lambda bi, si: (bi, 0, si, 0, 0)),
        ],
        out_shape=[
            jax.ShapeDtypeStruct((b, s, CONV_WIDTH), BF16),
            jax.ShapeDtypeStruct((b, MLA_HEADS, s, QK_PAD), BF16),
            jax.ShapeDtypeStruct((b, MLA_HEADS, s, QK_PAD), BF16),
            jax.ShapeDtypeStruct((b, MLA_HEADS, s // TQ, VT_ROWS, TQ), BF16),
        ],
        scratch_shapes=[pltpu.VMEM((ts + HALO, CONV_WIDTH), F32)],
        compiler_params=pltpu.CompilerParams(
            dimension_semantics=("arbitrary", "arbitrary"), vmem_limit_bytes=VMEM_LIMIT),
        name="pre_mixer",
    )(x, nm, win, convw, qn, wuq, kvn, wukv, gqn, gqr, gkn, gkr, tbl)


def _attn_kernel(q_ref, k_ref, vt_ref, o_ref, m_ref, acc_ref, s_ref):
    i = pl.program_id(1)
    tq = q_ref.shape[2]
    key_c = lax.broadcasted_iota(jnp.int32, (tq, tq), 0) // CHUNK
    qry_c = lax.broadcasted_iota(jnp.int32, (tq, tq), 1) // CHUNK
    allowed = key_c <= qry_c
    m_ref[...] = jnp.full(m_ref.shape, -1e30, F32)
    acc_ref[...] = jnp.zeros(acc_ref.shape, F32)

    def scores(j, keep):
        start = pl.multiple_of(j * tq, tq)
        out = []
        for h in range(MLA_HEADS):
            kb = k_ref[0, h, pl.ds(start, tq), :]
            st = lax.dot_general(kb, q_ref[0, h], (((1,), (1,)), ((), ())),
                                 preferred_element_type=F32)
            out.append(st if keep is None else jnp.where(keep, st, -1e30))
        return out

    def update(j):
        probs, alphas = [], []
        for h in range(MLA_HEADS):
            st = s_ref[h]
            m = m_ref[h]
            m_new = jnp.maximum(m, jnp.max(st, axis=0, keepdims=True))
            alpha = jnp.exp2(m - m_new)
            p = jnp.exp2(st - m_new)
            m_ref[h] = m_new
            probs.append(p.astype(BF16))
            alphas.append(alpha)
        for h in range(MLA_HEADS):
            acc_ref[h] = alphas[h] * acc_ref[h] + jnp.dot(vt_ref[0, h, j], probs[h],
                                                           preferred_element_type=F32)

    def stage(sts):
        for h in range(MLA_HEADS):
            s_ref[h] = sts[h]

    stage(scores(0, allowed | (i > 0)))

    def body(j, c):
        nxt = scores(j + 1, None)
        update(j)
        stage(nxt)
        return c

    lax.fori_loop(0, i - 1, body, 0)

    @pl.when(i > 0)
    def _():
        nxt = scores(i, allowed)
        update(i - 1)
        stage(nxt)

    update(i)
    for h in range(MLA_HEADS):
        o = acc_ref[h, 0:V_HEAD, :] / acc_ref[h, V_HEAD:V_HEAD + 1, :]
        o_ref[0, :, V_HEAD * h:V_HEAD * (h + 1)] = o.T.astype(BF16)


def _attn_call(q, k, vt):
    b, nh, s, _ = q.shape
    return pl.pallas_call(
        _attn_kernel,
        grid=(b, s // TQ),
        in_specs=[
            pl.BlockSpec((1, nh, TQ, QK_PAD), lambda bi, qi: (bi, 0, qi, 0)),
            pl.BlockSpec((1, nh, s, QK_PAD), lambda bi, qi: (bi, 0, 0, 0)),
            pl.BlockSpec((1, nh, s // TQ, VT_ROWS, TQ), lambda bi, qi: (bi, 0, 0, 0, 0)),
        ],
        out_specs=pl.BlockSpec((1, TQ, nh * V_HEAD), lambda bi, qi: (bi, qi, 0)),
        out_shape=jax.ShapeDtypeStruct((b, s, nh * V_HEAD), BF16),
        scratch_shapes=[pltpu.VMEM((nh, 1, TQ), F32),
                        pltpu.VMEM((nh, VT_ROWS, TQ), F32), pltpu.VMEM((nh, TQ, TQ), F32)],
        compiler_params=pltpu.CompilerParams(
            dimension_semantics=("arbitrary", "arbitrary"), vmem_limit_bytes=VMEM_LIMIT),
        name="mla_attention",
    )(q, k, vt)


def _post_kernel(x_ref, conv_ref, attn_ref, wo_ref, g_ref, wr_ref, br_ref, tri_ref,
                 h1_ref, hb_ref, lpos_ref, gate_ref, tile_n_ref, tile_carry_ref, cnt_ref,
                 carry_ref):
    step = pl.program_id(0)
    tm = x_ref.shape[0]

    @pl.when(step == 0)
    def _():
        carry_ref[...] = jnp.zeros_like(carry_ref)

    sub = tri_ref.shape[0]
    subs = [slice(r0, r0 + sub) for r0 in range(0, tm, sub)]
    h1s = [x_ref[rows, :]
           + jnp.dot(conv_ref[rows, :], wo_ref[0:CONV_WIDTH, :], preferred_element_type=F32)
           + jnp.dot(attn_ref[rows, :], wo_ref[CONV_WIDTH:, :], preferred_element_type=F32)
           for rows in subs]
    lane = lax.broadcasted_iota(jnp.int32, (sub, N_EXPERTS), 1).astype(F32)
    lane4 = lax.broadcasted_iota(jnp.int32, (sub, TOP_K), 1)

    def cols(c):
        return jnp.where(lane4 == 0, c[0], jnp.where(lane4 == 1, c[1],
                         jnp.where(lane4 == 2, c[2], c[3])))

    picks, members = [], []
    for rows, h1 in zip(subs, h1s):
        h1_ref[rows, :] = h1
        hb = _rms(h1, g_ref[...]).astype(BF16)
        hb_ref[rows, :] = hb
        lg = jnp.dot(hb, wr_ref[...], preferred_element_type=F32) + br_ref[...]
        member = jnp.zeros((sub, N_EXPERTS), F32)
        vals, idxs, hots = [], [], []
        for _ in range(TOP_K):
            m = jnp.max(lg, axis=-1, keepdims=True)
            ix = jnp.min(jnp.where(lg == m, lane, float(N_EXPERTS)), axis=-1, keepdims=True)
            hot = lane == ix
            lg = jnp.where(hot, -jnp.inf, lg)
            member = member + hot.astype(F32)
            vals.append(m)
            idxs.append(ix)
            hots.append(hot)
        exps = [jnp.exp(v - vals[0]) for v in vals]
        denom = exps[0] + exps[1] + exps[2] + exps[3]
        gate_ref[rows, :] = cols([e / denom for e in exps])
        picks.append((idxs, hots))
        members.append(member)

    sub_n = [jnp.sum(member, axis=0, keepdims=True) for member in members]
    tile_n = sub_n[0]
    for n_s in sub_n[1:]:
        tile_n = tile_n + n_s
    earlier = jnp.zeros((1, N_EXPERTS), F32)
    for rows, member, n_s, (idxs, hots) in zip(subs, members, sub_n, picks):
        before = jnp.dot(tri_ref[...], member.astype(BF16), preferred_element_type=F32) + earlier
        lpos = [jnp.sum(jnp.where(hot, before, 0.0) + jnp.where(lane < ix, tile_n, 0.0),
                        axis=-1, keepdims=True) for hot, ix in zip(hots, idxs)]
        lpos_ref[rows, :] = cols(lpos).astype(jnp.int32)
        earlier = earlier + n_s
    tile_n_ref[0] = tile_n.astype(jnp.int32)
    tile_carry_ref[0] = carry_ref[...].astype(jnp.int32)
    carry_ref[...] = carry_ref[...] + tile_n
    cnt_ref[...] = carry_ref[...].astype(jnp.int32)


def _post_call(xf, conv, attn, wo, g, wr, br, tri):
    n, d = xf.shape
    tm = TM_POST
    const = lambda shape: pl.BlockSpec(shape, lambda i: (0,) * len(shape))
    row = lambda w: pl.BlockSpec((tm, w), lambda i: (i, 0))
    return pl.pallas_call(
        _post_kernel,
        grid=(n // tm,),
        in_specs=[row(d), row(CONV_WIDTH), row(CONV_WIDTH), const((d, d)), const((1, d)),
                  const((d, N_EXPERTS)), const((1, N_EXPERTS)), const(tri.shape)],
        out_specs=[row(d), row(d), row(TOP_K), row(TOP_K),
                   pl.BlockSpec((1, 1, N_EXPERTS), lambda i: (i, 0, 0)),
                   pl.BlockSpec((1, 1, N_EXPERTS), lambda i: (i, 0, 0)),
                   const((1, N_EXPERTS))],
        out_shape=[
            jax.ShapeDtypeStruct((n, d), F32),
            jax.ShapeDtypeStruct((n, d), BF16),
            jax.ShapeDtypeStruct((n, TOP_K), jnp.int32),
            jax.ShapeDtypeStruct((n, TOP_K), F32),
            jax.ShapeDtypeStruct((n // tm, 1, N_EXPERTS), jnp.int32),
            jax.ShapeDtypeStruct((n // tm, 1, N_EXPERTS), jnp.int32),
            jax.ShapeDtypeStruct((1, N_EXPERTS), jnp.int32),
        ],
        scratch_shapes=[pltpu.VMEM((1, N_EXPERTS), F32)],
        compiler_params=pltpu.CompilerParams(
            dimension_semantics=("arbitrary",), vmem_limit_bytes=VMEM_LIMIT),
        name="post_mixer_router",
    )(xf, conv, attn, wo, g, wr, br, tri)


RUN_BITS = 10


def _rows(ref, first_row, n_rows):
    return ref.at[pl.ds(pl.multiple_of(first_row * ROW_LINES, ROW_LINES), n_rows * ROW_LINES), :]


def _for_each_run_piece(n_ref, far_ref, fn, bits=RUN_BITS):
    def per_expert(e, local0):
        n_e = n_ref[0, 0, e]
        far0 = far_ref[0, 0, e]

        for b in reversed(range(bits)):
            done = (n_e >> (b + 1)) << (b + 1)

            @pl.when(((n_e >> b) & 1) == 1)
            def _():
                fn(local0 + done, far0 + done, 1 << b)
        return local0 + n_e

    lax.fori_loop(0, N_EXPERTS, per_expert, 0)


PAD_BITS = TM_EXP.bit_length() - 1
assert TM_EXP == 1 << PAD_BITS


def _scatter_kernel(n_ref, far_ref, pad_n_ref, pad_far_ref, used_ref, hb_ref, lpos_ref, xs_ref,
                    xl, zeros, sem, zsem):
    i = pl.program_id(0)
    steps = pl.num_programs(0)
    tm = hb_ref.shape[0]
    pairs = tm * TOP_K
    slot = i % 2

    def wait_slot(s):
        pltpu.make_async_copy(xl.at[s], _rows(xs_ref, 0, pairs), sem.at[s]).wait()

    @pl.when(i == 0)
    def _():
        zeros[...] = jnp.zeros(zeros.shape, F32)

        def fill(_, far_row, size):
            return pltpu.make_async_copy(_rows(zeros, 0, size), _rows(xs_ref, far_row, size), zsem)

        _for_each_run_piece(pad_n_ref, pad_far_ref, lambda *a: fill(*a).start(), PAD_BITS)
        _for_each_run_piece(pad_n_ref, pad_far_ref, lambda *a: fill(*a).wait(), PAD_BITS)

        tile_rows = zeros.shape[0] // ROW_LINES
        n_tiles = xs_ref.shape[0] // zeros.shape[0]

        def tail(op):
            def body(t, c):
                op(fill(None, t * tile_rows, tile_rows))
                return c
            lax.fori_loop(used_ref[0], n_tiles, body, 0)

        tail(lambda cp: cp.start())
        tail(lambda cp: cp.wait())

    @pl.when(i >= 2)
    def _():
        wait_slot(slot)

    lane = lax.broadcasted_iota(jnp.int32, (tm, 128), 1)
    lp = lpos_ref[...].astype(F32)
    cols = jnp.zeros((tm, 128), F32)
    for k in range(TOP_K):
        cols = jnp.where(lane == k, lp[:, k:k + 1], cols)
    rows_t = cols.T.astype(jnp.int16)
    j = lax.broadcasted_iota(jnp.int32, (pairs, tm), 0).astype(jnp.int16)
    onehot = jnp.zeros((pairs, tm), BF16)
    for k in range(TOP_K):
        onehot = jnp.where(j == rows_t[k:k + 1, :], jnp.ones((), BF16), onehot)
    _store_row_tiles(xl.at[slot], jnp.dot(onehot, hb_ref[...], preferred_element_type=F32))

    def send(local_row, far_row, size):
        pltpu.make_async_copy(_rows(xl.at[slot], local_row, size), _rows(xs_ref, far_row, size),
                              sem.at[slot]).start()

    _for_each_run_piece(n_ref, far_ref, send)

    @pl.when(i == steps - 1)
    def _():
        @pl.when(steps >= 2)
        def _():
            wait_slot(1 - slot)
        wait_slot(slot)


def _scatter_call(tile_n, tile_far, pad_n, pad_far, used_tiles, hb, lpos, sorted_rows):
    n, d = hb.shape
    tm = TM_POST
    table = pl.BlockSpec((1, 1, N_EXPERTS), lambda i: (i, 0, 0), memory_space=pltpu.SMEM)
    whole = pl.BlockSpec((1, 1, N_EXPERTS), lambda i: (0, 0, 0), memory_space=pltpu.SMEM)
    return pl.pallas_call(
        _scatter_kernel,
        grid=(n // tm,),
        in_specs=[table, table, whole, whole, pl.BlockSpec(memory_space=pltpu.SMEM),
                  pl.BlockSpec((tm, d), lambda i: (i, 0)),
                  pl.BlockSpec((tm, TOP_K), lambda i: (i, 0))],
        out_specs=pl.BlockSpec(memory_space=pl.ANY),
        out_shape=jax.ShapeDtypeStruct((sorted_rows * ROW_LINES, 128), F32),
        scratch_shapes=[pltpu.VMEM((2, tm * TOP_K * ROW_LINES, 128), F32),
                        pltpu.VMEM((TM_EXP * ROW_LINES, 128), F32),
                        pltpu.SemaphoreType.DMA((2,)), pltpu.SemaphoreType.DMA],
        compiler_params=pltpu.CompilerParams(
            dimension_semantics=("arbitrary",), vmem_limit_bytes=VMEM_LIMIT),
        name="dispatch_scatter",
    )(tile_n, tile_far, pad_n, pad_far, used_tiles, hb, lpos)


def _expert_kernel(tile_ref, exp_ref, flag_ref,
                   xs_ref, wgu_ref, bgu_ref, wd_ref, bd_ref, o_ref, wgu_bf, wd_bf):
    i = pl.program_id(0)
    tm = xs_ref.shape[0] // ROW_LINES
    flag = flag_ref[i]
    cast_rows = 128

    @pl.when((flag & 2) != 0)
    def _():
        def cast(r, c):
            rows = pl.ds(pl.multiple_of(r * cast_rows, cast_rows), cast_rows)
            wgu_bf[rows, :] = wgu_ref[0, rows, :].astype(BF16)
            wd_bf[rows, :] = wd_ref[0, rows, :].astype(BF16)
            return c
        lax.fori_loop(0, D_MODEL // cast_rows, cast, 0)

    @pl.when((flag & 1) != 0)
    def _():
        x = _load_row_tiles(xs_ref, tm).astype(BF16)
        gu = jnp.dot(x, wgu_bf[...], preferred_element_type=F32) + bgu_ref[0]
        gate = jnp.minimum(gu[:, :D_FF], SWIGLU_LIMIT)
        up = jnp.clip(gu[:, D_FF:], -SWIGLU_LIMIT, SWIGLU_LIMIT)
        act = (up + 1.0) * (gate * (1.0 / (1.0 + jnp.exp(-SWIGLU_ALPHA * gate))))
        out = jnp.dot(act.astype(BF16), wd_bf[...], preferred_element_type=F32) + bd_ref[0]
        _store_row_tiles(o_ref, out)


def _expert_call(meta, xs, wgu, bgu, wd, bd):
    lines, w = xs.shape
    tm = TM_EXP
    n_items = meta[0].shape[0]
    return pl.pallas_call(
        _expert_kernel,
        grid_spec=pltpu.PrefetchScalarGridSpec(
            num_scalar_prefetch=3,
            grid=(n_items,),
            in_specs=[
                pl.BlockSpec((tm * ROW_LINES, w), lambda i, t, e, f: (t[i], 0)),
                pl.BlockSpec((1, D_MODEL, 2 * D_FF), lambda i, t, e, f: (e[i], 0, 0)),
                pl.BlockSpec((1, 1, 2 * D_FF), lambda i, t, e, f: (e[i], 0, 0)),
                pl.BlockSpec((1, D_FF, D_MODEL), lambda i, t, e, f: (e[i], 0, 0)),
                pl.BlockSpec((1, 1, D_MODEL), lambda i, t, e, f: (e[i], 0, 0)),
            ],
            out_specs=pl.BlockSpec((tm * ROW_LINES, w), lambda i, t, e, f: (t[i], 0)),
            scratch_shapes=[pltpu.VMEM((D_MODEL, 2 * D_FF), BF16),
                            pltpu.VMEM((D_FF, D_MODEL), BF16)],
        ),
        out_shape=jax.ShapeDtypeStruct((lines, w), F32),
        input_output_aliases={len(meta): 0},
        compiler_params=pltpu.CompilerParams(
            dimension_semantics=("arbitrary",), vmem_limit_bytes=VMEM_LIMIT),
        name="moe_experts",
    )(*meta, xs, wgu, bgu, wd, bd)


def _expert_layout(counts, n_pairs, tm):
    n_tiles = n_pairs // tm + N_EXPERTS
    per_exp = (counts + tm - 1) // tm
    tile_end = jnp.cumsum(per_exp)
    tile0 = tile_end - per_exp
    used = tile_end[-1]
    step = jnp.arange(n_tiles, dtype=jnp.int32)
    tile = jnp.minimum(step, used - 1)
    e = jnp.sum((tile_end[None, :] <= tile[:, None]).astype(jnp.int32), axis=1)
    e_tile0 = jnp.sum(jnp.where(e[:, None] == jnp.arange(N_EXPERTS, dtype=jnp.int32)[None, :],
                                tile0[None, :], 0), axis=1)
    flag = jnp.where(step < used, 1 + 2 * (step == e_tile0), 0)
    cast = lambda a: a.astype(jnp.int32)
    group_start = tile0 * tm
    pad_n = per_exp * tm - counts
    pad_start = group_start + counts
    return ((tile, cast(e), cast(flag)), cast(used).reshape(1), cast(group_start), cast(pad_n),
            cast(pad_start))


def _combine_kernel(n_ref, far_ref, n_next_ref, far_next_ref, h1_ref, lpos_ref, gate_ref, ys_ref,
                    o_ref, yl, sem):
    i = pl.program_id(0)
    steps = pl.num_programs(0)
    tm = h1_ref.shape[0]
    pairs = tm * TOP_K
    slot = i % 2

    def fetch(nr, fr, s):
        def recv(local_row, far_row, size):
            pltpu.make_async_copy(_rows(ys_ref, far_row, size), _rows(yl.at[s], local_row, size),
                                  sem.at[s]).start()
        _for_each_run_piece(nr, fr, recv)

    @pl.when(i == 0)
    def _():
        fetch(n_ref, far_ref, 0)

    @pl.when(i + 1 < steps)
    def _():
        fetch(n_next_ref, far_next_ref, 1 - slot)

    pltpu.make_async_copy(_rows(ys_ref, 0, pairs), yl.at[slot], sem.at[slot]).wait()

    j = lax.broadcasted_iota(jnp.int32, (tm, pairs), 1).astype(jnp.int16)
    lp = lpos_ref[...].astype(jnp.int16)
    g = gate_ref[...].astype(BF16)
    w = jnp.zeros((tm, pairs), BF16)
    for k in range(TOP_K):
        w = jnp.where(j == lp[:, k:k + 1], g[:, k:k + 1], w)
    y = _load_row_tiles(yl, pairs, lead=(slot,)).astype(BF16)
    o_ref[...] = h1_ref[...] + jnp.dot(w, y, preferred_element_type=F32)


def _combine_call(tile_n, tile_far, h1, lpos, gates, ys):
    n, d = h1.shape
    tm = TM_POST
    steps = n // tm
    table = lambda fn: pl.BlockSpec((1, 1, N_EXPERTS), fn, memory_space=pltpu.SMEM)
    cur = lambda i: (i, 0, 0)
    nxt = lambda i: (jnp.minimum(i + 1, steps - 1), 0, 0)
    return pl.pallas_call(
        _combine_kernel,
        grid=(steps,),
        in_specs=[table(cur), table(cur), table(nxt), table(nxt),
                  pl.BlockSpec((tm, d), lambda i: (i, 0)),
                  pl.BlockSpec((tm, TOP_K), lambda i: (i, 0)),
                  pl.BlockSpec((tm, TOP_K), lambda i: (i, 0)),
                  pl.BlockSpec(memory_space=pl.ANY)],
        out_specs=pl.BlockSpec((tm, d), lambda i: (i, 0)),
        out_shape=jax.ShapeDtypeStruct((n, d), F32),
        scratch_shapes=[pltpu.VMEM((2, tm * TOP_K * ROW_LINES, 128), F32),
                        pltpu.SemaphoreType.DMA((2,))],
        compiler_params=pltpu.CompilerParams(
            dimension_semantics=("arbitrary",), vmem_limit_bytes=VMEM_LIMIT),
        name="moe_combine",
    )(tile_n, tile_far, tile_n, tile_far, h1, lpos, gates, ys)


def _rotate_half_perm():
    half = QK_ROPE // 2
    perm = np.concatenate([np.arange(half, QK_ROPE), np.arange(0, half)])
    sign = np.concatenate([-np.ones(half), np.ones(half)]).astype(np.float32)
    return perm, sign


def _layer(h, l, norm_mix, w_in, conv_w, q_norm, w_uq, kv_norm, w_ukv, q_head_norm,
           k_head_norm, w_o, norm_ffn, w_router, b_router, w_gate_up, b_gate_up,
           w_down, b_down):
    b, s, d = h.shape
    n = b * s
    perm, sign = _rotate_half_perm()

    kpe_cols = w_in[l][:, OFF_KPE:OFF_KPE + QK_ROPE]
    win = jnp.concatenate([w_in[l], kpe_cols[:, perm] * sign], axis=1).astype(BF16)
    wq = w_uq[l].reshape(Q_LORA, MLA_HEADS, QK_HEAD)
    wq_rope = wq[:, :, QK_NOPE:]
    wuq = jnp.concatenate(
        [wq[:, :, :QK_NOPE].reshape(Q_LORA, MLA_HEADS * QK_NOPE),
         jnp.concatenate([wq_rope, wq_rope[:, :, perm] * sign], axis=2)
         .reshape(Q_LORA, MLA_HEADS * 2 * QK_ROPE)], axis=1).astype(BF16)
    gq, gk = q_head_norm[l], k_head_norm[l]
    gqn, gkn = gq[None, :QK_NOPE], gk[None, :QK_NOPE]
    gqr = jnp.concatenate([gq[QK_NOPE:], gq[QK_NOPE:][perm]])[None, :]
    gkr = jnp.concatenate([gk[QK_NOPE:], gk[QK_NOPE:][perm]])[None, :]

    half = QK_ROPE // 2
    inv_freq = ROPE_THETA ** (-jnp.arange(half, dtype=F32) / half)
    ang = jnp.arange(s).astype(F32)[:, None] * inv_freq[None, :]
    cos, sin = jnp.cos(ang), jnp.sin(ang)
    tbl = jnp.concatenate([cos, cos, sin, sin], axis=1)

    conv, q, k, vt = _pre_call(
        h, norm_mix[l][None, :], win, conv_w[l], q_norm[l][None, :], wuq,
        kv_norm[l][None, :], w_ukv[l].astype(BF16), gqn, gqr, gkn, gkr, tbl)
    attn = _attn_call(q, k, vt)

    tri = jnp.tril(jnp.ones((SUB_POST, SUB_POST), BF16), -1)
    h1, hb, lpos, gates, tile_n, tile_carry, counts = _post_call(
        h.reshape(n, d), conv.reshape(n, CONV_WIDTH), attn.reshape(n, CONV_WIDTH),
        w_o[l].astype(BF16), norm_ffn[l][None, :], w_router[l].astype(BF16),
        b_router[l][None, :], tri)

    meta, used_tiles, group_start, pad_n, pad_start = _expert_layout(counts[0], n * TOP_K, TM_EXP)
    sorted_rows = n * TOP_K + N_EXPERTS * TM_EXP
    tile_far = (tile_carry + group_start[None, None, :]).astype(jnp.int32)
    xs = _scatter_call(tile_n, tile_far, pad_n[None, None, :], pad_start[None, None, :],
                       used_tiles, hb, lpos, sorted_rows)
    ys = _expert_call(meta, xs, w_gate_up[l], b_gate_up[l][:, None, :],
                      w_down[l], b_down[l][:, None, :])
    out = _combine_call(tile_n, tile_far, h1, lpos, gates, ys)
    return out.reshape(b, s, d)


def kernel(x, norm_mix, w_in, conv_w, q_norm, w_uq, kv_norm, w_ukv, q_head_norm, k_head_norm,
           w_o, norm_ffn, w_router, b_router, w_gate_up, b_gate_up, w_down, b_down):
    h = x
    for l in range(norm_mix.shape[0]):
        h = _layer(h, l, norm_mix, w_in, conv_w, q_norm, w_uq, kv_norm, w_ukv, q_head_norm,
                   k_head_norm, w_o, norm_ffn, w_router, b_router, w_gate_up, b_gate_up,
                   w_down, b_down)
    return h
```

```python
import math

import jax
import jax.numpy as jnp
import numpy as np
from jax import lax
from jax.experimental import pallas as pl
from jax.experimental.pallas import tpu as pltpu

F32 = jnp.float32
BF16 = jnp.bfloat16

D_MODEL = 1024
CONV_WIDTH = 512
MLA_HEADS = 4
QK_NOPE = 128
QK_ROPE = 64
QK_HEAD = QK_NOPE + QK_ROPE
QK_PAD = 256
V_HEAD = 128
Q_LORA = 384
KV_LORA = 256
ROPE_THETA = 10000.0
CHUNK = 64
N_EXPERTS = 32
TOP_K = 4
D_FF = 1024
SWIGLU_LIMIT = 7.0
SWIGLU_ALPHA = 1.702
EPS = 1e-6

OFF_B, OFF_C, OFF_X, OFF_CQ, OFF_CKV, OFF_KPE = 0, 512, 1024, 1536, 1920, 2176
IN_PROJ_EXT = 2304

VMEM_LIMIT = 56 * 1024 * 1024

TS_PRE = 1024
TQ = 256
TM_POST = 512
SUB_POST = 256
TB_ROWS = 256
TM_EXP = 512
HALO = 8


def _rms(x, g):
    return x * lax.rsqrt(jnp.mean(x * x, axis=-1, keepdims=True) + EPS) * g


ROW_LINES = D_MODEL // 128


def _load_row_tiles(ref, rows, lead=()):
    return jnp.concatenate(
        [ref[lead + (pl.ds(c, rows, stride=ROW_LINES), slice(None))] for c in range(ROW_LINES)],
        axis=1)


def _store_row_tiles(ref, val):
    rows = val.shape[0]
    for c in range(ROW_LINES):
        ref[pl.ds(c, rows, stride=ROW_LINES), :] = val[:, 128 * c:128 * (c + 1)]


def _pre_kernel(x_ref, nm_ref, win_ref, convw_ref, qn_ref, wuq_ref, kvn_ref, wukv_ref,
                gqn_ref, gqr_ref, gkn_ref, gkr_ref, tbl_ref,
                conv_ref, q_ref, k_ref, vt_ref, ext_ref):
    s = pl.program_id(1)
    ts = x_ref.shape[1]
    sub = TQ
    starts = tuple(range(0, ts, sub))

    @pl.when(s == 0)
    def _():
        ext_ref[0:HALO, :] = jnp.zeros((HALO, CONV_WIDTH), F32)

    @pl.when(s != 0)
    def _():
        ext_ref[0:HALO, :] = ext_ref[ts:ts + HALO, :]

    projs = [jnp.dot(_rms(x_ref[0, r0:r0 + sub, :], nm_ref[...]).astype(BF16), win_ref[...],
                     preferred_element_type=F32) for r0 in starts]

    lane = lax.broadcasted_iota(jnp.int32, (sub, 128), 1)
    first_half = lane < QK_ROPE
    scale = math.log2(math.e) / math.sqrt(QK_HEAD)

    for r0, proj in zip(starts, projs):
        rows = slice(r0, r0 + sub)
        u = proj[:, OFF_C:OFF_C + CONV_WIDTH] * proj[:, OFF_X:OFF_X + CONV_WIDTH]
        ext_ref[HALO + r0:HALO + r0 + sub, :] = u
        y = (convw_ref[2:3, :] * u
             + convw_ref[1:2, :] * ext_ref[HALO + r0 - 1:HALO + r0 - 1 + sub, :]
             + convw_ref[0:1, :] * ext_ref[HALO + r0 - 2:HALO + r0 - 2 + sub, :])
        conv_ref[0, rows, :] = (proj[:, OFF_B:OFF_B + CONV_WIDTH] * y).astype(BF16)

        hq = _rms(proj[:, OFF_CQ:OFF_CQ + Q_LORA], qn_ref[...]).astype(BF16)
        qp = jnp.dot(hq, wuq_ref[...], preferred_element_type=F32)
        hkv = _rms(proj[:, OFF_CKV:OFF_CKV + KV_LORA], kvn_ref[...]).astype(BF16)
        kvp = jnp.dot(hkv, wukv_ref[...], preferred_element_type=F32)

        tbl = tbl_ref[rows, :]

        def rope_block(grp, gain):
            r = grp * tbl * gain
            r = r + pltpu.roll(r, QK_ROPE, axis=1)
            return jnp.where(first_half, r, 0.0)

        kpe = proj[:, OFF_KPE:OFF_KPE + 128]
        ss_kpe = jnp.sum(jnp.where(first_half, kpe * kpe, 0.0), axis=-1, keepdims=True)
        k_rope = rope_block(kpe, gkr_ref[...])
        for h in range(MLA_HEADS):
            qn = qp[:, 128 * h:128 * (h + 1)]
            qg = qp[:, 512 + 128 * h:512 + 128 * (h + 1)]
            ss = (jnp.sum(qn * qn, axis=-1, keepdims=True)
                  + jnp.sum(jnp.where(first_half, qg * qg, 0.0), axis=-1, keepdims=True))
            rs = lax.rsqrt(ss * (1.0 / QK_HEAD) + EPS) * scale
            q_ref[0, h, rows, 0:128] = (qn * rs * gqn_ref[...]).astype(BF16)
            q_ref[0, h, rows, 128:256] = (rope_block(qg, gqr_ref[...]) * rs).astype(BF16)

            kn = kvp[:, 256 * h:256 * h + 128]
            ssk = jnp.sum(kn * kn, axis=-1, keepdims=True) + ss_kpe
            rsk = lax.rsqrt(ssk * (1.0 / QK_HEAD) + EPS)
            k_ref[0, h, rows, 0:128] = (kn * rsk * gkn_ref[...]).astype(BF16)
            k_ref[0, h, rows, 128:256] = (k_rope * rsk).astype(BF16)
            v = kvp[:, 256 * h + 128:256 * (h + 1)]
            vt_ref[0, h, r0 // TQ] = v.T.astype(BF16)


def _pre_call(x, nm, win, convw, qn, wuq, kvn, wukv, gqn, gqr, gkn, gkr, tbl):
    b, s, d = x.shape
    ts = TS_PRE
    const = lambda shape: pl.BlockSpec(shape, lambda bi, si: (0,) * len(shape))
    return pl.pallas_call(
        _pre_kernel,
        grid=(b, s // ts),
        in_specs=[
            pl.BlockSpec((1, ts, d), lambda bi, si: (bi, si, 0)),
            const((1, d)), const((d, IN_PROJ_EXT)), const((3, CONV_WIDTH)),
            const((1, Q_LORA)), const((Q_LORA, 1024)),
            const((1, KV_LORA)), const((KV_LORA, 1024)),
            const((1, 128)), const((1, 128)), const((1, 128)), const((1, 128)),
            pl.BlockSpec((ts, 128), lambda bi, si: (si, 0)),
        ],
        out_specs=[
            pl.BlockSpec((1, ts, CONV_WIDTH), lambda bi, si: (bi, si, 0)),
            pl.BlockSpec((1, MLA_HEADS, ts, QK_PAD), lambda bi, si: (bi, 0, si, 0)),
            pl.BlockSpec((1, MLA_HEADS, ts, QK_PAD), lambda bi, si: (bi, 0, si, 0)),
            pl.BlockSpec((1, MLA_HEADS, ts // TQ, V_HEAD, TQ), lambda bi, si: (bi, 0, si, 0, 0)),
        ],
        out_shape=[
            jax.ShapeDtypeStruct((b, s, CONV_WIDTH), BF16),
            jax.ShapeDtypeStruct((b, MLA_HEADS, s, QK_PAD), BF16),
            jax.ShapeDtypeStruct((b, MLA_HEADS, s, QK_PAD), BF16),
            jax.ShapeDtypeStruct((b, MLA_HEADS, s // TQ, V_HEAD, TQ), BF16),
        ],
        scratch_shapes=[pltpu.VMEM((ts + HALO, CONV_WIDTH), F32)],
        compiler_params=pltpu.CompilerParams(
            dimension_semantics=("arbitrary", "arbitrary"), vmem_limit_bytes=VMEM_LIMIT),
        name="pre_mixer",
    )(x, nm, win, convw, qn, wuq, kvn, wukv, gqn, gqr, gkn, gkr, tbl)


def _attn_kernel(q_ref, k_ref, vt_ref, o_ref, m_ref, l_ref, acc_ref, s_ref):
    i = pl.program_id(1)
    tq = q_ref.shape[2]
    key_c = lax.broadcasted_iota(jnp.int32, (tq, tq), 0) // CHUNK
    qry_c = lax.broadcasted_iota(jnp.int32, (tq, tq), 1) // CHUNK
    allowed = key_c <= qry_c
    m_ref[...] = jnp.full(m_ref.shape, -1e30, F32)
    l_ref[...] = jnp.zeros(l_ref.shape, F32)
    acc_ref[...] = jnp.zeros(acc_ref.shape, F32)

    def scores(j, keep):
        start = pl.multiple_of(j * tq, tq)
        out = []
        for h in range(MLA_HEADS):
            kb = k_ref[0, h, pl.ds(start, tq), :]
            st = lax.dot_general(kb, q_ref[0, h], (((1,), (1,)), ((), ())),
                                 preferred_element_type=F32)
            out.append(st if keep is None else jnp.where(keep, st, -1e30))
        return out

    def update(j):
        probs, alphas = [], []
        for h in range(MLA_HEADS):
            st = s_ref[h]
            m = m_ref[h]
            m_new = jnp.maximum(m, jnp.max(st, axis=0, keepdims=True))
            alpha = jnp.exp2(m - m_new)
            p = jnp.exp2(st - m_new)
            l_ref[h] = alpha * l_ref[h] + jnp.sum(p, axis=0, keepdims=True)
            m_ref[h] = m_new
            probs.append(p.astype(BF16))
            alphas.append(alpha)
        for h in range(MLA_HEADS):
            acc_ref[h] = alphas[h] * acc_ref[h] + jnp.dot(vt_ref[0, h, j], probs[h],
                                                           preferred_element_type=F32)

    def stage(sts):
        for h in range(MLA_HEADS):
            s_ref[h] = sts[h]

    stage(scores(0, allowed | (i > 0)))

    def body(j, c):
        nxt = scores(j + 1, None)
        update(j)
        stage(nxt)
        return c

    lax.fori_loop(0, i - 1, body, 0)

    @pl.when(i > 0)
    def _():
        nxt = scores(i, allowed)
        update(i - 1)
        stage(nxt)

    update(i)
    for h in range(MLA_HEADS):
        o = acc_ref[h] / l_ref[h]
        o_ref[0, :, V_HEAD * h:V_HEAD * (h + 1)] = o.T.astype(BF16)


def _attn_call(q, k, vt):
    b, nh, s, _ = q.shape
    return pl.pallas_call(
        _attn_kernel,
        grid=(b, s // TQ),
        in_specs=[
            pl.BlockSpec((1, nh, TQ, QK_PAD), lambda bi, qi: (bi, 0, qi, 0)),
            pl.BlockSpec((1, nh, s, QK_PAD), lambda bi, qi: (bi, 0, 0, 0)),
            pl.BlockSpec((1, nh, s // TQ, V_HEAD, TQ), lambda bi, qi: (bi, 0, 0, 0, 0)),
        ],
        out_specs=pl.BlockSpec((1, TQ, nh * V_HEAD), lambda bi, qi: (bi, qi, 0)),
        out_shape=jax.ShapeDtypeStruct((b, s, nh * V_HEAD), BF16),
        scratch_shapes=[pltpu.VMEM((nh, 1, TQ), F32), pltpu.VMEM((nh, 1, TQ), F32),
                        pltpu.VMEM((nh, V_HEAD, TQ), F32), pltpu.VMEM((nh, TQ, TQ), F32)],
        compiler_params=pltpu.CompilerParams(
            dimension_semantics=("arbitrary", "arbitrary"), vmem_limit_bytes=VMEM_LIMIT),
        name="mla_attention",
    )(q, k, vt)


def _post_kernel(x_ref, conv_ref, attn_ref, wo_ref, g_ref, wr_ref, br_ref, tri_ref,
                 h1_ref, hb_ref, lpos_ref, gate_ref, tile_n_ref, tile_carry_ref, cnt_ref,
                 carry_ref):
    step = pl.program_id(0)
    tm = x_ref.shape[0]

    @pl.when(step == 0)
    def _():
        carry_ref[...] = jnp.zeros_like(carry_ref)

    sub = tri_ref.shape[0]
    subs = [slice(r0, r0 + sub) for r0 in range(0, tm, sub)]
    h1s = [x_ref[rows, :]
           + jnp.dot(conv_ref[rows, :], wo_ref[0:CONV_WIDTH, :], preferred_element_type=F32)
           + jnp.dot(attn_ref[rows, :], wo_ref[CONV_WIDTH:, :], preferred_element_type=F32)
           for rows in subs]
    lane = lax.broadcasted_iota(jnp.int32, (sub, N_EXPERTS), 1).astype(F32)
    lane4 = lax.broadcasted_iota(jnp.int32, (sub, TOP_K), 1)

    def cols(c):
        return jnp.where(lane4 == 0, c[0], jnp.where(lane4 == 1, c[1],
                         jnp.where(lane4 == 2, c[2], c[3])))

    picks, members = [], []
    for rows, h1 in zip(subs, h1s):
        h1_ref[rows, :] = h1
        hb = _rms(h1, g_ref[...]).astype(BF16)
        hb_ref[rows, :] = hb
        lg = jnp.dot(hb, wr_ref[...], preferred_element_type=F32) + br_ref[...]
        member = jnp.zeros((sub, N_EXPERTS), F32)
        vals, idxs, hots = [], [], []
        for _ in range(TOP_K):
            m = jnp.max(lg, axis=-1, keepdims=True)
            ix = jnp.min(jnp.where(lg == m, lane, float(N_EXPERTS)), axis=-1, keepdims=True)
            hot = lane == ix
            lg = jnp.where(hot, -jnp.inf, lg)
            member = member + hot.astype(F32)
            vals.append(m)
            idxs.append(ix)
            hots.append(hot)
        exps = [jnp.exp(v - vals[0]) for v in vals]
        denom = exps[0] + exps[1] + exps[2] + exps[3]
        gate_ref[rows, :] = cols([e / denom for e in exps])
        picks.append((idxs, hots))
        members.append(member)

    sub_n = [jnp.sum(member, axis=0, keepdims=True) for member in members]
    tile_n = sub_n[0]
    for n_s in sub_n[1:]:
        tile_n = tile_n + n_s
    earlier = jnp.zeros((1, N_EXPERTS), F32)
    for rows, member, n_s, (idxs, hots) in zip(subs, members, sub_n, picks):
        before = jnp.dot(tri_ref[...], member.astype(BF16), preferred_element_type=F32) + earlier
        lpos = [jnp.sum(jnp.where(hot, before, 0.0) + jnp.where(lane < ix, tile_n, 0.0),
                        axis=-1, keepdims=True) for hot, ix in zip(hots, idxs)]
        lpos_ref[rows, :] = cols(lpos).astype(jnp.int32)
        earlier = earlier + n_s
    tile_n_ref[0] = tile_n.astype(jnp.int32)
    tile_carry_ref[0] = carry_ref[...].astype(jnp.int32)
    carry_ref[...] = carry_ref[...] + tile_n
    cnt_ref[...] = carry_ref[...].astype(jnp.int32)


def _post_call(xf, conv, attn, wo, g, wr, br, tri):
    n, d = xf.shape
    tm = TM_POST
    const = lambda shape: pl.BlockSpec(shape, lambda i: (0,) * len(shape))
    row = lambda w: pl.BlockSpec((tm, w), lambda i: (i, 0))
    return pl.pallas_call(
        _post_kernel,
        grid=(n // tm,),
        in_specs=[row(d), row(CONV_WIDTH), row(CONV_WIDTH), const((d, d)), const((1, d)),
                  const((d, N_EXPERTS)), const((1, N_EXPERTS)), const(tri.shape)],
        out_specs=[row(d), row(d), row(TOP_K), row(TOP_K),
                   pl.BlockSpec((1, 1, N_EXPERTS), lambda i: (i, 0, 0)),
                   pl.BlockSpec((1, 1, N_EXPERTS), lambda i: (i, 0, 0)),
                   const((1, N_EXPERTS))],
        out_shape=[
            jax.ShapeDtypeStruct((n, d), F32),
            jax.ShapeDtypeStruct((n, d), BF16),
            jax.ShapeDtypeStruct((n, TOP_K), jnp.int32),
            jax.ShapeDtypeStruct((n, TOP_K), F32),
            jax.ShapeDtypeStruct((n // tm, 1, N_EXPERTS), jnp.int32),
            jax.ShapeDtypeStruct((n // tm, 1, N_EXPERTS), jnp.int32),
            jax.ShapeDtypeStruct((1, N_EXPERTS), jnp.int32),
        ],
        scratch_shapes=[pltpu.VMEM((1, N_EXPERTS), F32)],
        compiler_params=pltpu.CompilerParams(
            dimension_semantics=("arbitrary",), vmem_limit_bytes=VMEM_LIMIT),
        name="post_mixer_router",
    )(xf, conv, attn, wo, g, wr, br, tri)


RUN_BITS = 10


def _rows(ref, first_row, n_rows):
    return ref.at[pl.ds(pl.multiple_of(first_row * ROW_LINES, ROW_LINES), n_rows * ROW_LINES), :]


def _for_each_run_piece(n_ref, far_ref, fn, bits=RUN_BITS):
    def per_expert(e, local0):
        n_e = n_ref[0, 0, e]
        far0 = far_ref[0, 0, e]

        for b in reversed(range(bits)):
            done = (n_e >> (b + 1)) << (b + 1)

            @pl.when(((n_e >> b) & 1) == 1)
            def _():
                fn(local0 + done, far0 + done, 1 << b)
        return local0 + n_e

    lax.fori_loop(0, N_EXPERTS, per_expert, 0)


PAD_BITS = TM_EXP.bit_length() - 1
assert TM_EXP == 1 << PAD_BITS


def _scatter_kernel(n_ref, far_ref, pad_n_ref, pad_far_ref, used_ref, hb_ref, lpos_ref, xs_ref,
                    xl, zeros, sem, zsem):
    i = pl.program_id(0)
    steps = pl.num_programs(0)
    tm = hb_ref.shape[0]
    pairs = tm * TOP_K
    slot = i % 2

    def wait_slot(s):
        pltpu.make_async_copy(xl.at[s], _rows(xs_ref, 0, pairs), sem.at[s]).wait()

    @pl.when(i == 0)
    def _():
        zeros[...] = jnp.zeros(zeros.shape, F32)

        def fill(_, far_row, size):
            return pltpu.make_async_copy(_rows(zeros, 0, size), _rows(xs_ref, far_row, size), zsem)

        _for_each_run_piece(pad_n_ref, pad_far_ref, lambda *a: fill(*a).start(), PAD_BITS)
        _for_each_run_piece(pad_n_ref, pad_far_ref, lambda *a: fill(*a).wait(), PAD_BITS)

        tile_rows = zeros.shape[0] // ROW_LINES
        n_tiles = xs_ref.shape[0] // zeros.shape[0]

        def tail(op):
            def body(t, c):
                op(fill(None, t * tile_rows, tile_rows))
                return c
            lax.fori_loop(used_ref[0], n_tiles, body, 0)

        tail(lambda cp: cp.start())
        tail(lambda cp: cp.wait())

    @pl.when(i >= 2)
    def _():
        wait_slot(slot)

    lane = lax.broadcasted_iota(jnp.int32, (tm, 128), 1)
    lp = lpos_ref[...].astype(F32)
    cols = jnp.zeros((tm, 128), F32)
    for k in range(TOP_K):
        cols = jnp.where(lane == k, lp[:, k:k + 1], cols)
    rows_t = cols.T.astype(jnp.int16)
    j = lax.broadcasted_iota(jnp.int32, (pairs, tm), 0).astype(jnp.int16)
    onehot = jnp.zeros((pairs, tm), BF16)
    for k in range(TOP_K):
        onehot = jnp.where(j == rows_t[k:k + 1, :], jnp.ones((), BF16), onehot)
    _store_row_tiles(xl.at[slot], jnp.dot(onehot, hb_ref[...], preferred_element_type=F32))

    def send(local_row, far_row, size):
        pltpu.make_async_copy(_rows(xl.at[slot], local_row, size), _rows(xs_ref, far_row, size),
                              sem.at[slot]).start()

    _for_each_run_piece(n_ref, far_ref, send)

    @pl.when(i == steps - 1)
    def _():
        @pl.when(steps >= 2)
        def _():
            wait_slot(1 - slot)
        wait_slot(slot)


def _scatter_call(tile_n, tile_far, pad_n, pad_far, used_tiles, hb, lpos, sorted_rows):
    n, d = hb.shape
    tm = TM_POST
    table = pl.BlockSpec((1, 1, N_EXPERTS), lambda i: (i, 0, 0), memory_space=pltpu.SMEM)
    whole = pl.BlockSpec((1, 1, N_EXPERTS), lambda i: (0, 0, 0), memory_space=pltpu.SMEM)
    return pl.pallas_call(
        _scatter_kernel,
        grid=(n // tm,),
        in_specs=[table, table, whole, whole, pl.BlockSpec(memory_space=pltpu.SMEM),
                  pl.BlockSpec((tm, d), lambda i: (i, 0)),
                  pl.BlockSpec((tm, TOP_K), lambda i: (i, 0))],
        out_specs=pl.BlockSpec(memory_space=pl.ANY),
        out_shape=jax.ShapeDtypeStruct((sorted_rows * ROW_LINES, 128), F32),
        scratch_shapes=[pltpu.VMEM((2, tm * TOP_K * ROW_LINES, 128), F32),
                        pltpu.VMEM((TM_EXP * ROW_LINES, 128), F32),
                        pltpu.SemaphoreType.DMA((2,)), pltpu.SemaphoreType.DMA],
        compiler_params=pltpu.CompilerParams(
            dimension_semantics=("arbitrary",), vmem_limit_bytes=VMEM_LIMIT),
        name="dispatch_scatter",
    )(tile_n, tile_far, pad_n, pad_far, used_tiles, hb, lpos)


def _expert_kernel(tile_ref, exp_ref, flag_ref,
                   xs_ref, wgu_ref, bgu_ref, wd_ref, bd_ref, o_ref, wgu_bf, wd_bf):
    i = pl.program_id(0)
    tm = xs_ref.shape[0] // ROW_LINES
    flag = flag_ref[i]
    cast_rows = 128

    @pl.when((flag & 2) != 0)
    def _():
        def cast(r, c):
            rows = pl.ds(pl.multiple_of(r * cast_rows, cast_rows), cast_rows)
            wgu_bf[rows, :] = wgu_ref[0, rows, :].astype(BF16)
            wd_bf[rows, :] = wd_ref[0, rows, :].astype(BF16)
            return c
        lax.fori_loop(0, D_MODEL // cast_rows, cast, 0)

    def run(rows):
        x = _load_row_tiles(xs_ref, rows).astype(BF16)
        gu = jnp.dot(x, wgu_bf[...], preferred_element_type=F32) + bgu_ref[0]
        gate = jnp.minimum(gu[:, :D_FF], SWIGLU_LIMIT)
        up = jnp.clip(gu[:, D_FF:], -SWIGLU_LIMIT, SWIGLU_LIMIT)
        act = (up + 1.0) * (gate * (1.0 / (1.0 + jnp.exp(-SWIGLU_ALPHA * gate))))
        out = jnp.dot(act.astype(BF16), wd_bf[...], preferred_element_type=F32) + bd_ref[0]
        _store_row_tiles(o_ref, out)

    @pl.when((flag & 5) == 1)
    def _():
        run(tm)

    @pl.when((flag & 5) == 5)
    def _():
        run(tm // 2)
        o_ref[tm // 2 * ROW_LINES:, :] = jnp.zeros((tm // 2 * ROW_LINES, 128), F32)


def _expert_call(meta, xs, wgu, bgu, wd, bd):
    lines, w = xs.shape
    tm = TM_EXP
    n_items = meta[0].shape[0]
    return pl.pallas_call(
        _expert_kernel,
        grid_spec=pltpu.PrefetchScalarGridSpec(
            num_scalar_prefetch=3,
            grid=(n_items,),
            in_specs=[
                pl.BlockSpec((tm * ROW_LINES, w), lambda i, t, e, f: (t[i], 0)),
                pl.BlockSpec((1, D_MODEL, 2 * D_FF), lambda i, t, e, f: (e[i], 0, 0)),
                pl.BlockSpec((1, 1, 2 * D_FF), lambda i, t, e, f: (e[i], 0, 0)),
                pl.BlockSpec((1, D_FF, D_MODEL), lambda i, t, e, f: (e[i], 0, 0)),
                pl.BlockSpec((1, 1, D_MODEL), lambda i, t, e, f: (e[i], 0, 0)),
            ],
            out_specs=pl.BlockSpec((tm * ROW_LINES, w), lambda i, t, e, f: (t[i], 0)),
            scratch_shapes=[pltpu.VMEM((D_MODEL, 2 * D_FF), BF16),
                            pltpu.VMEM((D_FF, D_MODEL), BF16)],
        ),
        out_shape=jax.ShapeDtypeStruct((lines, w), F32),
        input_output_aliases={len(meta): 0},
        compiler_params=pltpu.CompilerParams(
            dimension_semantics=("arbitrary",), vmem_limit_bytes=VMEM_LIMIT),
        name="moe_experts",
    )(*meta, xs, wgu, bgu, wd, bd)


def _expert_layout(counts, n_pairs, tm):
    n_tiles = n_pairs // tm + N_EXPERTS
    per_exp = (counts + tm - 1) // tm
    tile_end = jnp.cumsum(per_exp)
    tile0 = tile_end - per_exp
    used = tile_end[-1]
    step = jnp.arange(n_tiles, dtype=jnp.int32)
    tile = jnp.minimum(step, used - 1)
    e = jnp.sum((tile_end[None, :] <= tile[:, None]).astype(jnp.int32), axis=1)
    hot = e[:, None] == jnp.arange(N_EXPERTS, dtype=jnp.int32)[None, :]
    pick = lambda tab: jnp.sum(jnp.where(hot, tab[None, :], 0), axis=1)
    e_tile0 = pick(tile0)
    rows_used = pick(counts) - (step - e_tile0) * tm
    flag = jnp.where(step < used, 1 + 2 * (step == e_tile0) + 4 * (rows_used <= tm // 2), 0)
    cast = lambda a: a.astype(jnp.int32)
    group_start = tile0 * tm
    pad_n = per_exp * tm - counts
    pad_start = group_start + counts
    return ((tile, cast(e), cast(flag)), cast(used).reshape(1), cast(group_start), cast(pad_n),
            cast(pad_start))


def _combine_kernel(n_ref, far_ref, n_next_ref, far_next_ref, h1_ref, lpos_ref, gate_ref, ys_ref,
                    o_ref, yl, sem):
    i = pl.program_id(0)
    steps = pl.num_programs(0)
    tm = h1_ref.shape[0]
    pairs = tm * TOP_K
    slot = i % 2

    def fetch(nr, fr, s):
        def recv(local_row, far_row, size):
            pltpu.make_async_copy(_rows(ys_ref, far_row, size), _rows(yl.at[s], local_row, size),
                                  sem.at[s]).start()
        _for_each_run_piece(nr, fr, recv)

    @pl.when(i == 0)
    def _():
        fetch(n_ref, far_ref, 0)

    @pl.when(i + 1 < steps)
    def _():
        fetch(n_next_ref, far_next_ref, 1 - slot)

    pltpu.make_async_copy(_rows(ys_ref, 0, pairs), yl.at[slot], sem.at[slot]).wait()

    j = lax.broadcasted_iota(jnp.int32, (tm, pairs), 1).astype(jnp.int16)
    lp = lpos_ref[...].astype(jnp.int16)
    g = gate_ref[...].astype(BF16)
    w = jnp.zeros((tm, pairs), BF16)
    for k in range(TOP_K):
        w = jnp.where(j == lp[:, k:k + 1], g[:, k:k + 1], w)
    y = _load_row_tiles(yl, pairs, lead=(slot,)).astype(BF16)
    o_ref[...] = h1_ref[...] + jnp.dot(w, y, preferred_element_type=F32)


def _combine_call(tile_n, tile_far, h1, lpos, gates, ys):
    n, d = h1.shape
    tm = TM_POST
    steps = n // tm
    table = lambda fn: pl.BlockSpec((1, 1, N_EXPERTS), fn, memory_space=pltpu.SMEM)
    cur = lambda i: (i, 0, 0)
    nxt = lambda i: (jnp.minimum(i + 1, steps - 1), 0, 0)
    return pl.pallas_call(
        _combine_kernel,
        grid=(steps,),
        in_specs=[table(cur), table(cur), table(nxt), table(nxt),
                  pl.BlockSpec((tm, d), lambda i: (i, 0)),
                  pl.BlockSpec((tm, TOP_K), lambda i: (i, 0)),
                  pl.BlockSpec((tm, TOP_K), lambda i: (i, 0)),
                  pl.BlockSpec(memory_space=pl.ANY)],
        out_specs=pl.BlockSpec((tm, d), lambda i: (i, 0)),
        out_shape=jax.ShapeDtypeStruct((n, d), F32),
        scratch_shapes=[pltpu.VMEM((2, tm * TOP_K * ROW_LINES, 128), F32),
                        pltpu.SemaphoreType.DMA((2,))],
        compiler_params=pltpu.CompilerParams(
            dimension_semantics=("arbitrary",), vmem_limit_bytes=VMEM_LIMIT),
        name="moe_combine",
    )(tile_n, tile_far, tile_n, tile_far, h1, lpos, gates, ys)


def _rotate_half_perm():
    half = QK_ROPE // 2
    perm = np.concatenate([np.arange(half, QK_ROPE), np.arange(0, half)])
    sign = np.concatenate([-np.ones(half), np.ones(half)]).astype(np.float32)
    return perm, sign


def _layer(h, l, norm_mix, w_in, conv_w, q_norm, w_uq, kv_norm, w_ukv, q_head_norm,
           k_head_norm, w_o, norm_ffn, w_router, b_router, w_gate_up, b_gate_up,
           w_down, b_down):
    b, s, d = h.shape
    n = b * s
    perm, sign = _rotate_half_perm()

    kpe_cols = w_in[l][:, OFF_KPE:OFF_KPE + QK_ROPE]
    win = jnp.concatenate([w_in[l], kpe_cols[:, perm] * sign], axis=1).astype(BF16)
    wq = w_uq[l].reshape(Q_LORA, MLA_HEADS, QK_HEAD)
    wq_rope = wq[:, :, QK_NOPE:]
    wuq = jnp.concatenate(
        [wq[:, :, :QK_NOPE].reshape(Q_LORA, MLA_HEADS * QK_NOPE),
         jnp.concatenate([wq_rope, wq_rope[:, :, perm] * sign], axis=2)
         .reshape(Q_LORA, MLA_HEADS * 2 * QK_ROPE)], axis=1).astype(BF16)
    gq, gk = q_head_norm[l], k_head_norm[l]
    gqn, gkn = gq[None, :QK_NOPE], gk[None, :QK_NOPE]
    gqr = jnp.concatenate([gq[QK_NOPE:], gq[QK_NOPE:][perm]])[None, :]
    gkr = jnp.concatenate([gk[QK_NOPE:], gk[QK_NOPE:][perm]])[None, :]

    half = QK_ROPE // 2
    inv_freq = ROPE_THETA ** (-jnp.arange(half, dtype=F32) / half)
    ang = jnp.arange(s).astype(F32)[:, None] * inv_freq[None, :]
    cos, sin = jnp.cos(ang), jnp.sin(ang)
    tbl = jnp.concatenate([cos, cos, sin, sin], axis=1)

    conv, q, k, vt = _pre_call(
        h, norm_mix[l][None, :], win, conv_w[l], q_norm[l][None, :], wuq,
        kv_norm[l][None, :], w_ukv[l].astype(BF16), gqn, gqr, gkn, gkr, tbl)
    attn = _attn_call(q, k, vt)

    tri = jnp.tril(jnp.ones((SUB_POST, SUB_POST), BF16), -1)
    h1, hb, lpos, gates, tile_n, tile_carry, counts = _post_call(
        h.reshape(n, d), conv.reshape(n, CONV_WIDTH), attn.reshape(n, CONV_WIDTH),
        w_o[l].astype(BF16), norm_ffn[l][None, :], w_router[l].astype(BF16),
        b_router[l][None, :], tri)

    meta, used_tiles, group_start, pad_n, pad_start = _expert_layout(counts[0], n * TOP_K, TM_EXP)
    sorted_rows = n * TOP_K + N_EXPERTS * TM_EXP
    tile_far = (tile_carry + group_start[None, None, :]).astype(jnp.int32)
    xs = _scatter_call(tile_n, tile_far, pad_n[None, None, :], pad_start[None, None, :],
                       used_tiles, hb, lpos, sorted_rows)
    ys = _expert_call(meta, xs, w_gate_up[l], b_gate_up[l][:, None, :],
                      w_down[l], b_down[l][:, None, :])
    out = _combine_call(tile_n, tile_far, h1, lpos, gates, ys)
    return out.reshape(b, s, d)


def kernel(x, norm_mix, w_in, conv_w, q_norm, w_uq, kv_norm, w_ukv, q_head_norm, k_head_norm,
           w_o, norm_ffn, w_router, b_router, w_gate_up, b_gate_up, w_down, b_down):
    h = x
    for l in range(norm_mix.shape[0]):
        h = _layer(h, l, norm_mix, w_in, conv_w, q_norm, w_uq, kv_norm, w_ukv, q_head_norm,
                   k_head_norm, w_o, norm_ffn, w_router, b_router, w_gate_up, b_gate_up,
                   w_down, b_down)
    return h
```

```python
import math

import jax
import jax.numpy as jnp
import numpy as np
from jax import lax
from jax.experimental import pallas as pl
from jax.experimental.pallas import tpu as pltpu

F32 = jnp.float32
BF16 = jnp.bfloat16

D_MODEL = 1024
CONV_WIDTH = 512
MLA_HEADS = 4
QK_NOPE = 128
QK_ROPE = 64
QK_HEAD = QK_NOPE + QK_ROPE
QK_PAD = 256
V_HEAD = 128
Q_LORA = 384
KV_LORA = 256
ROPE_THETA = 10000.0
CHUNK = 64
N_EXPERTS = 32
TOP_K = 4
D_FF = 1024
SWIGLU_LIMIT = 7.0
SWIGLU_ALPHA = 1.702
EPS = 1e-6

OFF_B, OFF_C, OFF_X, OFF_CQ, OFF_CKV, OFF_KPE = 0, 512, 1024, 1536, 1920, 2176
IN_PROJ_EXT = 2304

VMEM_LIMIT = 56 * 1024 * 1024

TS_PRE = 1024
TQ = 256
TM_POST = 512
SUB_POST = 256
TB_ROWS = 256
TM_EXP = 512
HALO = 8


def _rms(x, g):
    return x * lax.rsqrt(jnp.mean(x * x, axis=-1, keepdims=True) + EPS) * g


ROW_LINES = D_MODEL // 128


def _load_row_tiles(ref, rows, lead=()):
    return jnp.concatenate(
        [ref[lead + (pl.ds(c, rows, stride=ROW_LINES), slice(None))] for c in range(ROW_LINES)],
        axis=1)


def _store_row_tiles(ref, val):
    rows = val.shape[0]
    for c in range(ROW_LINES):
        ref[pl.ds(c, rows, stride=ROW_LINES), :] = val[:, 128 * c:128 * (c + 1)]


def _pre_kernel(x_ref, nm_ref, win_ref, convw_ref, qn_ref, wuq_ref, kvn_ref, wukv_ref,
                gqn_ref, gqr_ref, gkn_ref, gkr_ref, tbl_ref,
                conv_ref, q_ref, k_ref, vt_ref, ext_ref):
    s = pl.program_id(1)
    ts = x_ref.shape[1]
    sub = TQ
    starts = tuple(range(0, ts, sub))

    @pl.when(s == 0)
    def _():
        ext_ref[0:HALO, :] = jnp.zeros((HALO, CONV_WIDTH), F32)

    @pl.when(s != 0)
    def _():
        ext_ref[0:HALO, :] = ext_ref[ts:ts + HALO, :]

    projs = [jnp.dot(_rms(x_ref[0, r0:r0 + sub, :], nm_ref[...]).astype(BF16), win_ref[...],
                     preferred_element_type=F32) for r0 in starts]

    lane = lax.broadcasted_iota(jnp.int32, (sub, 128), 1)
    first_half = lane < QK_ROPE
    scale = math.log2(math.e) / math.sqrt(QK_HEAD)

    for r0, proj in zip(starts, projs):
        rows = slice(r0, r0 + sub)
        u = proj[:, OFF_C:OFF_C + CONV_WIDTH] * proj[:, OFF_X:OFF_X + CONV_WIDTH]
        ext_ref[HALO + r0:HALO + r0 + sub, :] = u
        y = (convw_ref[2:3, :] * u
             + convw_ref[1:2, :] * ext_ref[HALO + r0 - 1:HALO + r0 - 1 + sub, :]
             + convw_ref[0:1, :] * ext_ref[HALO + r0 - 2:HALO + r0 - 2 + sub, :])
        conv_ref[0, rows, :] = (proj[:, OFF_B:OFF_B + CONV_WIDTH] * y).astype(BF16)

        hq = _rms(proj[:, OFF_CQ:OFF_CQ + Q_LORA], qn_ref[...]).astype(BF16)
        qp = jnp.dot(hq, wuq_ref[...], preferred_element_type=F32)
        hkv = _rms(proj[:, OFF_CKV:OFF_CKV + KV_LORA], kvn_ref[...]).astype(BF16)
        kvp = jnp.dot(hkv, wukv_ref[...], preferred_element_type=F32)

        tbl = tbl_ref[rows, :]

        def rope_block(grp, gain):
            r = grp * tbl * gain
            r = r + pltpu.roll(r, QK_ROPE, axis=1)
            return jnp.where(first_half, r, 0.0)

        kpe = proj[:, OFF_KPE:OFF_KPE + 128]
        ss_kpe = jnp.sum(jnp.where(first_half, kpe * kpe, 0.0), axis=-1, keepdims=True)
        k_rope = rope_block(kpe, gkr_ref[...])
        for h in range(MLA_HEADS):
            qn = qp[:, 128 * h:128 * (h + 1)]
            qg = qp[:, 512 + 128 * h:512 + 128 * (h + 1)]
            ss = (jnp.sum(qn * qn, axis=-1, keepdims=True)
                  + jnp.sum(jnp.where(first_half, qg * qg, 0.0), axis=-1, keepdims=True))
            rs = lax.rsqrt(ss * (1.0 / QK_HEAD) + EPS) * scale
            q_ref[0, h, rows, 0:128] = (qn * rs * gqn_ref[...]).astype(BF16)
            q_ref[0, h, rows, 128:256] = (rope_block(qg, gqr_ref[...]) * rs).astype(BF16)

            kn = kvp[:, 256 * h:256 * h + 128]
            ssk = jnp.sum(kn * kn, axis=-1, keepdims=True) + ss_kpe
            rsk = lax.rsqrt(ssk * (1.0 / QK_HEAD) + EPS)
            k_ref[0, h, rows, 0:128] = (kn * rsk * gkn_ref[...]).astype(BF16)
            k_ref[0, h, rows, 128:256] = (k_rope * rsk).astype(BF16)
            v = kvp[:, 256 * h + 128:256 * (h + 1)]
            vt_ref[0, h, r0 // TQ] = v.T.astype(BF16)


def _pre_call(x, nm, win, convw, qn, wuq, kvn, wukv, gqn, gqr, gkn, gkr, tbl):
    b, s, d = x.shape
    ts = TS_PRE
    const = lambda shape: pl.BlockSpec(shape, lambda bi, si: (0,) * len(shape))
    return pl.pallas_call(
        _pre_kernel,
        grid=(b, s // ts),
        in_specs=[
            pl.BlockSpec((1, ts, d), lambda bi, si: (bi, si, 0)),
            const((1, d)), const((d, IN_PROJ_EXT)), const((3, CONV_WIDTH)),
            const((1, Q_LORA)), const((Q_LORA, 1024)),
            const((1, KV_LORA)), const((KV_LORA, 1024)),
            const((1, 128)), const((1, 128)), const((1, 128)), const((1, 128)),
            pl.BlockSpec((ts, 128), lambda bi, si: (si, 0)),
        ],
        out_specs=[
            pl.BlockSpec((1, ts, CONV_WIDTH), lambda bi, si: (bi, si, 0)),
            pl.BlockSpec((1, MLA_HEADS, ts, QK_PAD), lambda bi, si: (bi, 0, si, 0)),
            pl.BlockSpec((1, MLA_HEADS, ts, QK_PAD), lambda bi, si: (bi, 0, si, 0)),
            pl.BlockSpec((1, MLA_HEADS, ts // TQ, V_HEAD, TQ), lambda bi, si: (bi, 0, si, 0, 0)),
        ],
        out_shape=[
            jax.ShapeDtypeStruct((b, s, CONV_WIDTH), BF16),
            jax.ShapeDtypeStruct((b, MLA_HEADS, s, QK_PAD), BF16),
            jax.ShapeDtypeStruct((b, MLA_HEADS, s, QK_PAD), BF16),
            jax.ShapeDtypeStruct((b, MLA_HEADS, s // TQ, V_HEAD, TQ), BF16),
        ],
        scratch_shapes=[pltpu.VMEM((ts + HALO, CONV_WIDTH), F32)],
        compiler_params=pltpu.CompilerParams(
            dimension_semantics=("arbitrary", "arbitrary"), vmem_limit_bytes=VMEM_LIMIT),
        name="pre_mixer",
    )(x, nm, win, convw, qn, wuq, kvn, wukv, gqn, gqr, gkn, gkr, tbl)


def _attn_kernel(q_ref, k_ref, vt_ref, o_ref, m_ref, l_ref, acc_ref, s_ref):
    i = pl.program_id(1)
    tq = q_ref.shape[2]
    key_c = lax.broadcasted_iota(jnp.int32, (tq, tq), 0) // CHUNK
    qry_c = lax.broadcasted_iota(jnp.int32, (tq, tq), 1) // CHUNK
    allowed = key_c <= qry_c
    m_ref[...] = jnp.full(m_ref.shape, -1e30, F32)
    l_ref[...] = jnp.zeros(l_ref.shape, F32)
    acc_ref[...] = jnp.zeros(acc_ref.shape, F32)

    def scores(j, keep):
        start = pl.multiple_of(j * tq, tq)
        out = []
        for h in range(MLA_HEADS):
            kb = k_ref[0, h, pl.ds(start, tq), :]
            st = lax.dot_general(kb, q_ref[0, h], (((1,), (1,)), ((), ())),
                                 preferred_element_type=F32)
            out.append(st if keep is None else jnp.where(keep, st, -1e30))
        return out

    def update(j):
        probs, alphas = [], []
        for h in range(MLA_HEADS):
            st = s_ref[h]
            m = m_ref[h]
            m_new = jnp.maximum(m, jnp.max(st, axis=0, keepdims=True))
            alpha = jnp.exp2(m - m_new)
            p = jnp.exp2(st - m_new)
            l_ref[h] = alpha * l_ref[h] + jnp.sum(p, axis=0, keepdims=True)
            m_ref[h] = m_new
            probs.append(p.astype(BF16))
            alphas.append(alpha)
        for h in range(MLA_HEADS):
            acc_ref[h] = alphas[h] * acc_ref[h] + jnp.dot(vt_ref[0, h, j], probs[h],
                                                           preferred_element_type=F32)

    def stage(sts):
        for h in range(MLA_HEADS):
            s_ref[h] = sts[h]

    stage(scores(0, allowed | (i > 0)))

    def body(j, c):
        nxt = scores(j + 1, None)
        update(j)
        stage(nxt)
        return c

    lax.fori_loop(0, i - 1, body, 0)

    @pl.when(i > 0)
    def _():
        nxt = scores(i, allowed)
        update(i - 1)
        stage(nxt)

    update(i)
    for h in range(MLA_HEADS):
        o = acc_ref[h] / l_ref[h]
        o_ref[0, :, V_HEAD * h:V_HEAD * (h + 1)] = o.T.astype(BF16)


def _attn_call(q, k, vt):
    b, nh, s, _ = q.shape
    return pl.pallas_call(
        _attn_kernel,
        grid=(b, s // TQ),
        in_specs=[
            pl.BlockSpec((1, nh, TQ, QK_PAD), lambda bi, qi: (bi, 0, qi, 0)),
            pl.BlockSpec((1, nh, s, QK_PAD), lambda bi, qi: (bi, 0, 0, 0)),
            pl.BlockSpec((1, nh, s // TQ, V_HEAD, TQ), lambda bi, qi: (bi, 0, 0, 0, 0)),
        ],
        out_specs=pl.BlockSpec((1, TQ, nh * V_HEAD), lambda bi, qi: (bi, qi, 0)),
        out_shape=jax.ShapeDtypeStruct((b, s, nh * V_HEAD), BF16),
        scratch_shapes=[pltpu.VMEM((nh, 1, TQ), F32), pltpu.VMEM((nh, 1, TQ), F32),
                        pltpu.VMEM((nh, V_HEAD, TQ), F32), pltpu.VMEM((nh, TQ, TQ), F32)],
        compiler_params=pltpu.CompilerParams(
            dimension_semantics=("arbitrary", "arbitrary"), vmem_limit_bytes=VMEM_LIMIT),
        name="mla_attention",
    )(q, k, vt)


def _post_kernel(x_ref, conv_ref, attn_ref, wo_ref, g_ref, wr_ref, br_ref, tri_ref,
                 h1_ref, hb_ref, lpos_ref, gate_ref, tile_n_ref, tile_carry_ref, cnt_ref,
                 carry_ref):
    step = pl.program_id(0)
    tm = x_ref.shape[0]

    @pl.when(step == 0)
    def _():
        carry_ref[...] = jnp.zeros_like(carry_ref)

    sub = tri_ref.shape[0]
    subs = [slice(r0, r0 + sub) for r0 in range(0, tm, sub)]
    h1s = [x_ref[rows, :]
           + jnp.dot(conv_ref[rows, :], wo_ref[0:CONV_WIDTH, :], preferred_element_type=F32)
           + jnp.dot(attn_ref[rows, :], wo_ref[CONV_WIDTH:, :], preferred_element_type=F32)
           for rows in subs]
    lane = lax.broadcasted_iota(jnp.int32, (sub, N_EXPERTS), 1).astype(F32)
    lane4 = lax.broadcasted_iota(jnp.int32, (sub, TOP_K), 1)

    def cols(c):
        return jnp.where(lane4 == 0, c[0], jnp.where(lane4 == 1, c[1],
                         jnp.where(lane4 == 2, c[2], c[3])))

    picks, members = [], []
    for rows, h1 in zip(subs, h1s):
        h1_ref[rows, :] = h1
        hb = _rms(h1, g_ref[...]).astype(BF16)
        hb_ref[rows, :] = hb
        lg = jnp.dot(hb, wr_ref[...], preferred_element_type=F32) + br_ref[...]
        member = jnp.zeros((sub, N_EXPERTS), F32)
        vals, idxs, hots = [], [], []
        for _ in range(TOP_K):
            m = jnp.max(lg, axis=-1, keepdims=True)
            ix = jnp.min(jnp.where(lg == m, lane, float(N_EXPERTS)), axis=-1, keepdims=True)
            hot = lane == ix
            lg = jnp.where(hot, -jnp.inf, lg)
            member = member + hot.astype(F32)
            vals.append(m)
            idxs.append(ix)
            hots.append(hot)
        exps = [jnp.exp(v - vals[0]) for v in vals]
        denom = exps[0] + exps[1] + exps[2] + exps[3]
        gate_ref[rows, :] = cols([e / denom for e in exps])
        picks.append((idxs, hots))
        members.append(member)

    sub_n = [jnp.sum(member, axis=0, keepdims=True) for member in members]
    tile_n = sub_n[0]
    for n_s in sub_n[1:]:
        tile_n = tile_n + n_s
    earlier = jnp.zeros((1, N_EXPERTS), F32)
    for rows, member, n_s, (idxs, hots) in zip(subs, members, sub_n, picks):
        before = jnp.dot(tri_ref[...], member.astype(BF16), preferred_element_type=F32) + earlier
        lpos = [jnp.sum(jnp.where(hot, before, 0.0) + jnp.where(lane < ix, tile_n, 0.0),
                        axis=-1, keepdims=True) for hot, ix in zip(hots, idxs)]
        lpos_ref[rows, :] = cols(lpos).astype(jnp.int32)
        earlier = earlier + n_s
    tile_n_ref[0] = tile_n.astype(jnp.int32)
    tile_carry_ref[0] = carry_ref[...].astype(jnp.int32)
    carry_ref[...] = carry_ref[...] + tile_n
    cnt_ref[...] = carry_ref[...].astype(jnp.int32)


def _post_call(xf, conv, attn, wo, g, wr, br, tri):
    n, d = xf.shape
    tm = TM_POST
    const = lambda shape: pl.BlockSpec(shape, lambda i: (0,) * len(shape))
    row = lambda w: pl.BlockSpec((tm, w), lambda i: (i, 0))
    return pl.pallas_call(
        _post_kernel,
        grid=(n // tm,),
        in_specs=[row(d), row(CONV_WIDTH), row(CONV_WIDTH), const((d, d)), const((1, d)),
                  const((d, N_EXPERTS)), const((1, N_EXPERTS)), const(tri.shape)],
        out_specs=[row(d), row(d), row(TOP_K), row(TOP_K),
                   pl.BlockSpec((1, 1, N_EXPERTS), lambda i: (i, 0, 0)),
                   pl.BlockSpec((1, 1, N_EXPERTS), lambda i: (i, 0, 0)),
                   const((1, N_EXPERTS))],
        out_shape=[
            jax.ShapeDtypeStruct((n, d), F32),
            jax.ShapeDtypeStruct((n, d), BF16),
            jax.ShapeDtypeStruct((n, TOP_K), jnp.int32),
            jax.ShapeDtypeStruct((n, TOP_K), F32),
            jax.ShapeDtypeStruct((n // tm, 1, N_EXPERTS), jnp.int32),
            jax.ShapeDtypeStruct((n // tm, 1, N_EXPERTS), jnp.int32),
            jax.ShapeDtypeStruct((1, N_EXPERTS), jnp.int32),
        ],
        scratch_shapes=[pltpu.VMEM((1, N_EXPERTS), F32)],
        compiler_params=pltpu.CompilerParams(
            dimension_semantics=("arbitrary",), vmem_limit_bytes=VMEM_LIMIT),
        name="post_mixer_router",
    )(xf, conv, attn, wo, g, wr, br, tri)


RUN_BITS = 10


def _rows(ref, first_row, n_rows):
    return ref.at[pl.ds(pl.multiple_of(first_row * ROW_LINES, ROW_LINES), n_rows * ROW_LINES), :]


def _for_each_run_piece(n_ref, far_ref, fn, bits=RUN_BITS):
    def per_expert(e, local0):
        n_e = n_ref[0, 0, e]
        far0 = far_ref[0, 0, e]

        for b in reversed(range(bits)):
            done = (n_e >> (b + 1)) << (b + 1)

            @pl.when(((n_e >> b) & 1) == 1)
            def _():
                fn(local0 + done, far0 + done, 1 << b)
        return local0 + n_e

    lax.fori_loop(0, N_EXPERTS, per_expert, 0)


PAD_BITS = TM_EXP.bit_length() - 1
assert TM_EXP == 1 << PAD_BITS


def _scatter_kernel(n_ref, far_ref, pad_n_ref, pad_far_ref, used_ref, hb_ref, lpos_ref, xs_ref,
                    xl, zeros, sem, zsem):
    i = pl.program_id(0)
    steps = pl.num_programs(0)
    tm = hb_ref.shape[0]
    pairs = tm * TOP_K
    slot = i % 2

    def wait_slot(s):
        pltpu.make_async_copy(xl.at[s], _rows(xs_ref, 0, pairs), sem.at[s]).wait()

    @pl.when(i == 0)
    def _():
        zeros[...] = jnp.zeros(zeros.shape, F32)

        def fill(_, far_row, size):
            return pltpu.make_async_copy(_rows(zeros, 0, size), _rows(xs_ref, far_row, size), zsem)

        _for_each_run_piece(pad_n_ref, pad_far_ref, lambda *a: fill(*a).start(), PAD_BITS)
        _for_each_run_piece(pad_n_ref, pad_far_ref, lambda *a: fill(*a).wait(), PAD_BITS)

        tile_rows = zeros.shape[0] // ROW_LINES
        n_tiles = xs_ref.shape[0] // zeros.shape[0]

        def tail(op):
            def body(t, c):
                op(fill(None, t * tile_rows, tile_rows))
                return c
            lax.fori_loop(used_ref[0], n_tiles, body, 0)

        tail(lambda cp: cp.start())
        tail(lambda cp: cp.wait())

    @pl.when(i >= 2)
    def _():
        wait_slot(slot)

    lane = lax.broadcasted_iota(jnp.int32, (tm, 128), 1)
    lp = lpos_ref[...].astype(F32)
    cols = jnp.zeros((tm, 128), F32)
    for k in range(TOP_K):
        cols = jnp.where(lane == k, lp[:, k:k + 1], cols)
    rows_t = cols.T.astype(jnp.int16)
    j = lax.broadcasted_iota(jnp.int32, (pairs, tm), 0).astype(jnp.int16)
    onehot = jnp.zeros((pairs, tm), BF16)
    for k in range(TOP_K):
        onehot = jnp.where(j == rows_t[k:k + 1, :], jnp.ones((), BF16), onehot)
    _store_row_tiles(xl.at[slot], jnp.dot(onehot, hb_ref[...], preferred_element_type=F32))

    def send(local_row, far_row, size):
        pltpu.make_async_copy(_rows(xl.at[slot], local_row, size), _rows(xs_ref, far_row, size),
                              sem.at[slot]).start()

    _for_each_run_piece(n_ref, far_ref, send)

    @pl.when(i == steps - 1)
    def _():
        @pl.when(steps >= 2)
        def _():
            wait_slot(1 - slot)
        wait_slot(slot)


def _scatter_call(tile_n, tile_far, pad_n, pad_far, used_tiles, hb, lpos, sorted_rows):
    n, d = hb.shape
    tm = TM_POST
    table = pl.BlockSpec((1, 1, N_EXPERTS), lambda i: (i, 0, 0), memory_space=pltpu.SMEM)
    whole = pl.BlockSpec((1, 1, N_EXPERTS), lambda i: (0, 0, 0), memory_space=pltpu.SMEM)
    return pl.pallas_call(
        _scatter_kernel,
        grid=(n // tm,),
        in_specs=[table, table, whole, whole, pl.BlockSpec(memory_space=pltpu.SMEM),
                  pl.BlockSpec((tm, d), lambda i: (i, 0)),
                  pl.BlockSpec((tm, TOP_K), lambda i: (i, 0))],
        out_specs=pl.BlockSpec(memory_space=pl.ANY),
        out_shape=jax.ShapeDtypeStruct((sorted_rows * ROW_LINES, 128), F32),
        scratch_shapes=[pltpu.VMEM((2, tm * TOP_K * ROW_LINES, 128), F32),
                        pltpu.VMEM((TM_EXP * ROW_LINES, 128), F32),
                        pltpu.SemaphoreType.DMA((2,)), pltpu.SemaphoreType.DMA],
        compiler_params=pltpu.CompilerParams(
            dimension_semantics=("arbitrary",), vmem_limit_bytes=VMEM_LIMIT),
        name="dispatch_scatter",
    )(tile_n, tile_far, pad_n, pad_far, used_tiles, hb, lpos)


def _expert_kernel(tile_ref, exp_ref, flag_ref, next_ref,
                   xs_ref, wgu_hbm, bgu_ref, wd_hbm, bd_ref, o_ref,
                   wgu_f, wd_f, wgu_bf, wd_bf, wsem):
    i = pl.program_id(0)
    tm = xs_ref.shape[0] // ROW_LINES
    flag = flag_ref[i]
    cast_rows = 128

    def weight_copies(e, s):
        return (pltpu.make_async_copy(wgu_hbm.at[e], wgu_f.at[s], wsem.at[s, 0]),
                pltpu.make_async_copy(wd_hbm.at[e], wd_f.at[s], wsem.at[s, 1]))

    @pl.when(i == 0)
    def _():
        for cp in weight_copies(exp_ref[0], 0):
            cp.start()

    @pl.when((flag & 2) != 0)
    def _():
        s = (flag >> 3) & 1
        for cp in weight_copies(exp_ref[i], s):
            cp.wait()
        nxt = next_ref[i]

        @pl.when(nxt >= 0)
        def _():
            for cp in weight_copies(nxt, 1 - s):
                cp.start()

        def cast(r, c):
            rows = pl.ds(pl.multiple_of(r * cast_rows, cast_rows), cast_rows)
            wgu_bf[rows, :] = wgu_f[s, rows, :].astype(BF16)
            wd_bf[rows, :] = wd_f[s, rows, :].astype(BF16)
            return c
        lax.fori_loop(0, D_MODEL // cast_rows, cast, 0)

    def run(rows):
        x = _load_row_tiles(xs_ref, rows).astype(BF16)
        gu = jnp.dot(x, wgu_bf[...], preferred_element_type=F32) + bgu_ref[0]
        gate = jnp.minimum(gu[:, :D_FF], SWIGLU_LIMIT)
        up = jnp.clip(gu[:, D_FF:], -SWIGLU_LIMIT, SWIGLU_LIMIT)
        act = (up + 1.0) * (gate * (1.0 / (1.0 + jnp.exp(-SWIGLU_ALPHA * gate))))
        out = jnp.dot(act.astype(BF16), wd_bf[...], preferred_element_type=F32) + bd_ref[0]
        _store_row_tiles(o_ref, out)

    @pl.when((flag & 5) == 1)
    def _():
        run(tm)

    @pl.when((flag & 5) == 5)
    def _():
        run(tm // 2)
        o_ref[tm // 2 * ROW_LINES:, :] = jnp.zeros((tm // 2 * ROW_LINES, 128), F32)


def _expert_call(meta, xs, wgu, bgu, wd, bd):
    lines, w = xs.shape
    tm = TM_EXP
    n_items = meta[0].shape[0]
    return pl.pallas_call(
        _expert_kernel,
        grid_spec=pltpu.PrefetchScalarGridSpec(
            num_scalar_prefetch=4,
            grid=(n_items,),
            in_specs=[
                pl.BlockSpec((tm * ROW_LINES, w), lambda i, t, e, f, nx: (t[i], 0)),
                pl.BlockSpec(memory_space=pl.ANY),
                pl.BlockSpec((1, 1, 2 * D_FF), lambda i, t, e, f, nx: (e[i], 0, 0)),
                pl.BlockSpec(memory_space=pl.ANY),
                pl.BlockSpec((1, 1, D_MODEL), lambda i, t, e, f, nx: (e[i], 0, 0)),
            ],
            out_specs=pl.BlockSpec((tm * ROW_LINES, w), lambda i, t, e, f, nx: (t[i], 0)),
            scratch_shapes=[pltpu.VMEM((2, D_MODEL, 2 * D_FF), F32),
                            pltpu.VMEM((2, D_FF, D_MODEL), F32),
                            pltpu.VMEM((D_MODEL, 2 * D_FF), BF16),
                            pltpu.VMEM((D_FF, D_MODEL), BF16),
                            pltpu.SemaphoreType.DMA((2, 2))],
        ),
        out_shape=jax.ShapeDtypeStruct((lines, w), F32),
        input_output_aliases={len(meta): 0},
        compiler_params=pltpu.CompilerParams(
            dimension_semantics=("arbitrary",), vmem_limit_bytes=VMEM_LIMIT),
        name="moe_experts",
    )(*meta, xs, wgu, bgu, wd, bd)


def _expert_layout(counts, n_pairs, tm):
    n_tiles = n_pairs // tm + N_EXPERTS
    per_exp = (counts + tm - 1) // tm
    tile_end = jnp.cumsum(per_exp)
    tile0 = tile_end - per_exp
    used = tile_end[-1]
    step = jnp.arange(n_tiles, dtype=jnp.int32)
    tile = jnp.minimum(step, used - 1)
    e = jnp.sum((tile_end[None, :] <= tile[:, None]).astype(jnp.int32), axis=1)
    hot = e[:, None] == jnp.arange(N_EXPERTS, dtype=jnp.int32)[None, :]
    pick = lambda tab: jnp.sum(jnp.where(hot, tab[None, :], 0), axis=1)
    e_tile0 = pick(tile0)
    rows_used = pick(counts) - (step - e_tile0) * tm
    experts = jnp.arange(N_EXPERTS, dtype=jnp.int32)
    has_tiles = per_exp > 0
    slot_of_expert = (jnp.cumsum(has_tiles.astype(jnp.int32)) - 1) & 1
    later = (experts[None, :] > experts[:, None]) & has_tiles[None, :]
    next_of_expert = jnp.min(jnp.where(later, experts[None, :], N_EXPERTS), axis=1)
    next_of_expert = jnp.where(next_of_expert < N_EXPERTS, next_of_expert, -1)
    flag = jnp.where(step < used, 1 + 2 * (step == e_tile0) + 4 * (rows_used <= tm // 2)
                     + 8 * pick(slot_of_expert), 0)
    cast = lambda a: a.astype(jnp.int32)
    group_start = tile0 * tm
    pad_n = per_exp * tm - counts
    pad_start = group_start + counts
    return ((tile, cast(e), cast(flag), cast(pick(next_of_expert))), cast(used).reshape(1),
            cast(group_start), cast(pad_n), cast(pad_start))


def _combine_kernel(n_ref, far_ref, n_next_ref, far_next_ref, h1_ref, lpos_ref, gate_ref, ys_ref,
                    o_ref, yl, sem):
    i = pl.program_id(0)
    steps = pl.num_programs(0)
    tm = h1_ref.shape[0]
    pairs = tm * TOP_K
    slot = i % 2

    def fetch(nr, fr, s):
        def recv(local_row, far_row, size):
            pltpu.make_async_copy(_rows(ys_ref, far_row, size), _rows(yl.at[s], local_row, size),
                                  sem.at[s]).start()
        _for_each_run_piece(nr, fr, recv)

    @pl.when(i == 0)
    def _():
        fetch(n_ref, far_ref, 0)

    @pl.when(i + 1 < steps)
    def _():
        fetch(n_next_ref, far_next_ref, 1 - slot)

    pltpu.make_async_copy(_rows(ys_ref, 0, pairs), yl.at[slot], sem.at[slot]).wait()

    j = lax.broadcasted_iota(jnp.int32, (tm, pairs), 1).astype(jnp.int16)
    lp = lpos_ref[...].astype(jnp.int16)
    g = gate_ref[...].astype(BF16)
    w = jnp.zeros((tm, pairs), BF16)
    for k in range(TOP_K):
        w = jnp.where(j == lp[:, k:k + 1], g[:, k:k + 1], w)
    y = _load_row_tiles(yl, pairs, lead=(slot,)).astype(BF16)
    o_ref[...] = h1_ref[...] + jnp.dot(w, y, preferred_element_type=F32)


def _combine_call(tile_n, tile_far, h1, lpos, gates, ys):
    n, d = h1.shape
    tm = TM_POST
    steps = n // tm
    table = lambda fn: pl.BlockSpec((1, 1, N_EXPERTS), fn, memory_space=pltpu.SMEM)
    cur = lambda i: (i, 0, 0)
    nxt = lambda i: (jnp.minimum(i + 1, steps - 1), 0, 0)
    return pl.pallas_call(
        _combine_kernel,
        grid=(steps,),
        in_specs=[table(cur), table(cur), table(nxt), table(nxt),
                  pl.BlockSpec((tm, d), lambda i: (i, 0)),
                  pl.BlockSpec((tm, TOP_K), lambda i: (i, 0)),
                  pl.BlockSpec((tm, TOP_K), lambda i: (i, 0)),
                  pl.BlockSpec(memory_space=pl.ANY)],
        out_specs=pl.BlockSpec((tm, d), lambda i: (i, 0)),
        out_shape=jax.ShapeDtypeStruct((n, d), F32),
        scratch_shapes=[pltpu.VMEM((2, tm * TOP_K * ROW_LINES, 128), F32),
                        pltpu.SemaphoreType.DMA((2,))],
        compiler_params=pltpu.CompilerParams(
            dimension_semantics=("arbitrary",), vmem_limit_bytes=VMEM_LIMIT),
        name="moe_combine",
    )(tile_n, tile_far, tile_n, tile_far, h1, lpos, gates, ys)


def _rotate_half_perm():
    half = QK_ROPE // 2
    perm = np.concatenate([np.arange(half, QK_ROPE), np.arange(0, half)])
    sign = np.concatenate([-np.ones(half), np.ones(half)]).astype(np.float32)
    return perm, sign


def _layer(h, l, norm_mix, w_in, conv_w, q_norm, w_uq, kv_norm, w_ukv, q_head_norm,
           k_head_norm, w_o, norm_ffn, w_router, b_router, w_gate_up, b_gate_up,
           w_down, b_down):
    b, s, d = h.shape
    n = b * s
    perm, sign = _rotate_half_perm()

    kpe_cols = w_in[l][:, OFF_KPE:OFF_KPE + QK_ROPE]
    win = jnp.concatenate([w_in[l], kpe_cols[:, perm] * sign], axis=1).astype(BF16)
    wq = w_uq[l].reshape(Q_LORA, MLA_HEADS, QK_HEAD)
    wq_rope = wq[:, :, QK_NOPE:]
    wuq = jnp.concatenate(
        [wq[:, :, :QK_NOPE].reshape(Q_LORA, MLA_HEADS * QK_NOPE),
         jnp.concatenate([wq_rope, wq_rope[:, :, perm] * sign], axis=2)
         .reshape(Q_LORA, MLA_HEADS * 2 * QK_ROPE)], axis=1).astype(BF16)
    gq, gk = q_head_norm[l], k_head_norm[l]
    gqn, gkn = gq[None, :QK_NOPE], gk[None, :QK_NOPE]
    gqr = jnp.concatenate([gq[QK_NOPE:], gq[QK_NOPE:][perm]])[None, :]
    gkr = jnp.concatenate([gk[QK_NOPE:], gk[QK_NOPE:][perm]])[None, :]

    half = QK_ROPE // 2
    inv_freq = ROPE_THETA ** (-jnp.arange(half, dtype=F32) / half)
    ang = jnp.arange(s).astype(F32)[:, None] * inv_freq[None, :]
    cos, sin = jnp.cos(ang), jnp.sin(ang)
    tbl = jnp.concatenate([cos, cos, sin, sin], axis=1)

    conv, q, k, vt = _pre_call(
        h, norm_mix[l][None, :], win, conv_w[l], q_norm[l][None, :], wuq,
        kv_norm[l][None, :], w_ukv[l].astype(BF16), gqn, gqr, gkn, gkr, tbl)
    attn = _attn_call(q, k, vt)

    tri = jnp.tril(jnp.ones((SUB_POST, SUB_POST), BF16), -1)
    h1, hb, lpos, gates, tile_n, tile_carry, counts = _post_call(
        h.reshape(n, d), conv.reshape(n, CONV_WIDTH), attn.reshape(n, CONV_WIDTH),
        w_o[l].astype(BF16), norm_ffn[l][None, :], w_router[l].astype(BF16),
        b_router[l][None, :], tri)

    meta, used_tiles, group_start, pad_n, pad_start = _expert_layout(counts[0], n * TOP_K, TM_EXP)
    sorted_rows = n * TOP_K + N_EXPERTS * TM_EXP
    tile_far = (tile_carry + group_start[None, None, :]).astype(jnp.int32)
    xs = _scatter_call(tile_n, tile_far, pad_n[None, None, :], pad_start[None, None, :],
                       used_tiles, hb, lpos, sorted_rows)
    ys = _expert_call(meta, xs, w_gate_up[l], b_gate_up[l][:, None, :],
                      w_down[l], b_down[l][:, None, :])
    out = _combine_call(tile_n, tile_far, h1, lpos, gates, ys)
    return out.reshape(b, s, d)


def kernel(x, norm_mix, w_in, conv_w, q_norm, w_uq, kv_norm, w_ukv, q_head_norm, k_head_norm,
           w_o, norm_ffn, w_router, b_router, w_gate_up, b_gate_up, w_down, b_down):
    h = x
    for l in range(norm_mix.shape[0]):
        h = _layer(h, l, norm_mix, w_in, conv_w, q_norm, w_uq, kv_norm, w_ukv, q_head_norm,
                   k_head_norm, w_o, norm_ffn, w_router, b_router, w_gate_up, b_gate_up,
                   w_down, b_down)
    return h
```

```python
import math

import jax
import jax.numpy as jnp
import numpy as np
from jax import lax
from jax.experimental import pallas as pl
from jax.experimental.pallas import tpu as pltpu

F32 = jnp.float32
BF16 = jnp.bfloat16

D_MODEL = 1024
CONV_WIDTH = 512
MLA_HEADS = 4
QK_NOPE = 128
QK_ROPE = 64
QK_HEAD = QK_NOPE + QK_ROPE
QK_PAD = 256
V_HEAD = 128
Q_LORA = 384
KV_LORA = 256
ROPE_THETA = 10000.0
CHUNK = 64
N_EXPERTS = 32
TOP_K = 4
D_FF = 1024
SWIGLU_LIMIT = 7.0
SWIGLU_ALPHA = 1.702
EPS = 1e-6

OFF_B, OFF_C, OFF_X, OFF_CQ, OFF_CKV, OFF_KPE = 0, 512, 1024, 1536, 1920, 2176
IN_PROJ_EXT = 2304

VMEM_LIMIT = 56 * 1024 * 1024

TS_PRE = 1024
TQ = 256
Q_TILES_PER_STEP = 2
TM_POST = 512
SUB_POST = 256
TB_ROWS = 256
TM_EXP = 512
HALO = 8


def _rms(x, g):
    return x * lax.rsqrt(jnp.mean(x * x, axis=-1, keepdims=True) + EPS) * g


ROW_LINES = D_MODEL // 128


def _load_row_tiles(ref, rows, lead=()):
    return jnp.concatenate(
        [ref[lead + (pl.ds(c, rows, stride=ROW_LINES), slice(None))] for c in range(ROW_LINES)],
        axis=1)


def _store_row_tiles(ref, val):
    rows = val.shape[0]
    for c in range(ROW_LINES):
        ref[pl.ds(c, rows, stride=ROW_LINES), :] = val[:, 128 * c:128 * (c + 1)]


def _pre_kernel(x_ref, nm_ref, win_ref, convw_ref, qn_ref, wuq_ref, kvn_ref, wukv_ref,
                gqn_ref, gqr_ref, gkn_ref, gkr_ref, tbl_ref,
                conv_ref, q_ref, k_ref, vt_ref, ext_ref):
    s = pl.program_id(1)
    ts = x_ref.shape[1]
    sub = TQ
    starts = tuple(range(0, ts, sub))

    @pl.when(s == 0)
    def _():
        ext_ref[0:HALO, :] = jnp.zeros((HALO, CONV_WIDTH), F32)

    @pl.when(s != 0)
    def _():
        ext_ref[0:HALO, :] = ext_ref[ts:ts + HALO, :]

    projs = [jnp.dot(_rms(x_ref[0, r0:r0 + sub, :], nm_ref[...]).astype(BF16), win_ref[...],
                     preferred_element_type=F32) for r0 in starts]

    lane = lax.broadcasted_iota(jnp.int32, (sub, 128), 1)
    first_half = lane < QK_ROPE
    scale = math.log2(math.e) / math.sqrt(QK_HEAD)

    for r0, proj in zip(starts, projs):
        rows = slice(r0, r0 + sub)
        u = proj[:, OFF_C:OFF_C + CONV_WIDTH] * proj[:, OFF_X:OFF_X + CONV_WIDTH]
        ext_ref[HALO + r0:HALO + r0 + sub, :] = u
        y = (convw_ref[2:3, :] * u
             + convw_ref[1:2, :] * ext_ref[HALO + r0 - 1:HALO + r0 - 1 + sub, :]
             + convw_ref[0:1, :] * ext_ref[HALO + r0 - 2:HALO + r0 - 2 + sub, :])
        conv_ref[0, rows, :] = (proj[:, OFF_B:OFF_B + CONV_WIDTH] * y).astype(BF16)

        hq = _rms(proj[:, OFF_CQ:OFF_CQ + Q_LORA], qn_ref[...]).astype(BF16)
        qp = jnp.dot(hq, wuq_ref[...], preferred_element_type=F32)
        hkv = _rms(proj[:, OFF_CKV:OFF_CKV + KV_LORA], kvn_ref[...]).astype(BF16)
        kvp = jnp.dot(hkv, wukv_ref[...], preferred_element_type=F32)

        tbl = tbl_ref[rows, :]

        def rope_block(grp, gain):
            r = grp * tbl * gain
            r = r + pltpu.roll(r, QK_ROPE, axis=1)
            return jnp.where(first_half, r, 0.0)

        kpe = proj[:, OFF_KPE:OFF_KPE + 128]
        ss_kpe = jnp.sum(jnp.where(first_half, kpe * kpe, 0.0), axis=-1, keepdims=True)
        k_rope = rope_block(kpe, gkr_ref[...])
        for h in range(MLA_HEADS):
            qn = qp[:, 128 * h:128 * (h + 1)]
            qg = qp[:, 512 + 128 * h:512 + 128 * (h + 1)]
            ss = (jnp.sum(qn * qn, axis=-1, keepdims=True)
                  + jnp.sum(jnp.where(first_half, qg * qg, 0.0), axis=-1, keepdims=True))
            rs = lax.rsqrt(ss * (1.0 / QK_HEAD) + EPS) * scale
            q_ref[0, h, rows, 0:128] = (qn * rs * gqn_ref[...]).astype(BF16)
            q_ref[0, h, rows, 128:256] = (rope_block(qg, gqr_ref[...]) * rs).astype(BF16)

            kn = kvp[:, 256 * h:256 * h + 128]
            ssk = jnp.sum(kn * kn, axis=-1, keepdims=True) + ss_kpe
            rsk = lax.rsqrt(ssk * (1.0 / QK_HEAD) + EPS)
            k_ref[0, h, rows, 0:128] = (kn * rsk * gkn_ref[...]).astype(BF16)
            k_ref[0, h, rows, 128:256] = (k_rope * rsk).astype(BF16)
            v = kvp[:, 256 * h + 128:256 * (h + 1)]
            vt_ref[0, h, r0 // TQ] = v.T.astype(BF16)


def _pre_call(x, nm, win, convw, qn, wuq, kvn, wukv, gqn, gqr, gkn, gkr, tbl):
    b, s, d = x.shape
    ts = TS_PRE
    const = lambda shape: pl.BlockSpec(shape, lambda bi, si: (0,) * len(shape))
    return pl.pallas_call(
        _pre_kernel,
        grid=(b, s // ts),
        in_specs=[
            pl.BlockSpec((1, ts, d), lambda bi, si: (bi, si, 0)),
            const((1, d)), const((d, IN_PROJ_EXT)), const((3, CONV_WIDTH)),
            const((1, Q_LORA)), const((Q_LORA, 1024)),
            const((1, KV_LORA)), const((KV_LORA, 1024)),
            const((1, 128)), const((1, 128)), const((1, 128)), const((1, 128)),
            pl.BlockSpec((ts, 128), lambda bi, si: (si, 0)),
        ],
        out_specs=[
            pl.BlockSpec((1, ts, CONV_WIDTH), lambda bi, si: (bi, si, 0)),
            pl.BlockSpec((1, MLA_HEADS, ts, QK_PAD), lambda bi, si: (bi, 0, si, 0)),
            pl.BlockSpec((1, MLA_HEADS, ts, QK_PAD), lambda bi, si: (bi, 0, si, 0)),
            pl.BlockSpec((1, MLA_HEADS, ts // TQ, V_HEAD, TQ), lambda bi, si: (bi, 0, si, 0, 0)),
        ],
        out_shape=[
            jax.ShapeDtypeStruct((b, s, CONV_WIDTH), BF16),
            jax.ShapeDtypeStruct((b, MLA_HEADS, s, QK_PAD), BF16),
            jax.ShapeDtypeStruct((b, MLA_HEADS, s, QK_PAD), BF16),
            jax.ShapeDtypeStruct((b, MLA_HEADS, s // TQ, V_HEAD, TQ), BF16),
        ],
        scratch_shapes=[pltpu.VMEM((ts + HALO, CONV_WIDTH), F32)],
        compiler_params=pltpu.CompilerParams(
            dimension_semantics=("arbitrary", "arbitrary"), vmem_limit_bytes=VMEM_LIMIT),
        name="pre_mixer",
    )(x, nm, win, convw, qn, wuq, kvn, wukv, gqn, gqr, gkn, gkr, tbl)


def _attn_kernel(q_ref, k_ref, vt_ref, o_ref, m_ref, l_ref, acc_ref, s_ref):
    tq = TQ
    key_c = lax.broadcasted_iota(jnp.int32, (tq, tq), 0) // CHUNK
    qry_c = lax.broadcasted_iota(jnp.int32, (tq, tq), 1) // CHUNK
    allowed = key_c <= qry_c
    for t in range(q_ref.shape[2] // tq):
        _attn_tile(pl.program_id(1) * (q_ref.shape[2] // tq) + t, slice(t * tq, (t + 1) * tq),
                   allowed, q_ref, k_ref, vt_ref, o_ref, m_ref, l_ref, acc_ref, s_ref)


def _attn_tile(i, q_rows, allowed, q_ref, k_ref, vt_ref, o_ref, m_ref, l_ref, acc_ref, s_ref):
    tq = TQ
    m_ref[...] = jnp.full(m_ref.shape, -1e30, F32)
    l_ref[...] = jnp.zeros(l_ref.shape, F32)
    acc_ref[...] = jnp.zeros(acc_ref.shape, F32)

    def scores(j, keep):
        start = pl.multiple_of(j * tq, tq)
        out = []
        for h in range(MLA_HEADS):
            kb = k_ref[0, h, pl.ds(start, tq), :]
            st = lax.dot_general(kb, q_ref[0, h, q_rows, :], (((1,), (1,)), ((), ())),
                                 preferred_element_type=F32)
            out.append(st if keep is None else jnp.where(keep, st, -1e30))
        return out

    def update(j):
        probs, alphas = [], []
        for h in range(MLA_HEADS):
            st = s_ref[h]
            m = m_ref[h]
            m_new = jnp.maximum(m, jnp.max(st, axis=0, keepdims=True))
            alpha = jnp.exp2(m - m_new)
            p = jnp.exp2(st - m_new)
            l_ref[h] = alpha * l_ref[h] + jnp.sum(p, axis=0, keepdims=True)
            m_ref[h] = m_new
            probs.append(p.astype(BF16))
            alphas.append(alpha)
        for h in range(MLA_HEADS):
            acc_ref[h] = alphas[h] * acc_ref[h] + jnp.dot(vt_ref[0, h, j], probs[h],
                                                           preferred_element_type=F32)

    def stage(sts):
        for h in range(MLA_HEADS):
            s_ref[h] = sts[h]

    stage(scores(0, allowed | (i > 0)))

    def body(j, c):
        nxt = scores(j + 1, None)
        update(j)
        stage(nxt)
        return c

    lax.fori_loop(0, i - 1, body, 0)

    @pl.when(i > 0)
    def _():
        nxt = scores(i, allowed)
        update(i - 1)
        stage(nxt)

    update(i)
    for h in range(MLA_HEADS):
        o = acc_ref[h] / l_ref[h]
        o_ref[0, q_rows, V_HEAD * h:V_HEAD * (h + 1)] = o.T.astype(BF16)


def _attn_call(q, k, vt):
    b, nh, s, _ = q.shape
    rows = TQ * Q_TILES_PER_STEP
    return pl.pallas_call(
        _attn_kernel,
        grid=(b, s // rows),
        in_specs=[
            pl.BlockSpec((1, nh, rows, QK_PAD), lambda bi, qi: (bi, 0, qi, 0)),
            pl.BlockSpec((1, nh, s, QK_PAD), lambda bi, qi: (bi, 0, 0, 0)),
            pl.BlockSpec((1, nh, s // TQ, V_HEAD, TQ), lambda bi, qi: (bi, 0, 0, 0, 0)),
        ],
        out_specs=pl.BlockSpec((1, rows, nh * V_HEAD), lambda bi, qi: (bi, qi, 0)),
        out_shape=jax.ShapeDtypeStruct((b, s, nh * V_HEAD), BF16),
        scratch_shapes=[pltpu.VMEM((nh, 1, TQ), F32), pltpu.VMEM((nh, 1, TQ), F32),
                        pltpu.VMEM((nh, V_HEAD, TQ), F32), pltpu.VMEM((nh, TQ, TQ), F32)],
        compiler_params=pltpu.CompilerParams(
            dimension_semantics=("arbitrary", "arbitrary"), vmem_limit_bytes=VMEM_LIMIT),
        name="mla_attention",
    )(q, k, vt)


def _post_kernel(x_ref, conv_ref, attn_ref, wo_ref, g_ref, wr_ref, br_ref, tri_ref,
                 h1_ref, hb_ref, lpos_ref, gate_ref, tile_n_ref, tile_carry_ref, cnt_ref,
                 carry_ref):
    step = pl.program_id(0)
    tm = x_ref.shape[0]

    @pl.when(step == 0)
    def _():
        carry_ref[...] = jnp.zeros_like(carry_ref)

    sub = tri_ref.shape[0]
    subs = [slice(r0, r0 + sub) for r0 in range(0, tm, sub)]
    h1s = [x_ref[rows, :]
           + jnp.dot(conv_ref[rows, :], wo_ref[0:CONV_WIDTH, :], preferred_element_type=F32)
           + jnp.dot(attn_ref[rows, :], wo_ref[CONV_WIDTH:, :], preferred_element_type=F32)
           for rows in subs]
    lane = lax.broadcasted_iota(jnp.int32, (sub, N_EXPERTS), 1).astype(F32)
    lane4 = lax.broadcasted_iota(jnp.int32, (sub, TOP_K), 1)

    def cols(c):
        return jnp.where(lane4 == 0, c[0], jnp.where(lane4 == 1, c[1],
                         jnp.where(lane4 == 2, c[2], c[3])))

    picks, members = [], []
    for rows, h1 in zip(subs, h1s):
        h1_ref[rows, :] = h1
        hb = _rms(h1, g_ref[...]).astype(BF16)
        hb_ref[rows, :] = hb
        lg = jnp.dot(hb, wr_ref[...], preferred_element_type=F32) + br_ref[...]
        member = jnp.zeros((sub, N_EXPERTS), F32)
        vals, idxs, hots = [], [], []
        for _ in range(TOP_K):
            m = jnp.max(lg, axis=-1, keepdims=True)
            ix = jnp.min(jnp.where(lg == m, lane, float(N_EXPERTS)), axis=-1, keepdims=True)
            hot = lane == ix
            lg = jnp.where(hot, -jnp.inf, lg)
            member = member + hot.astype(F32)
            vals.append(m)
            idxs.append(ix)
            hots.append(hot)
        exps = [jnp.exp(v - vals[0]) for v in vals]
        denom = exps[0] + exps[1] + exps[2] + exps[3]
        gate_ref[rows, :] = cols([e / denom for e in exps])
        picks.append((idxs, hots))
        members.append(member)

    sub_n = [jnp.sum(member, axis=0, keepdims=True) for member in members]
    tile_n = sub_n[0]
    for n_s in sub_n[1:]:
        tile_n = tile_n + n_s
    earlier = jnp.zeros((1, N_EXPERTS), F32)
    for rows, member, n_s, (idxs, hots) in zip(subs, members, sub_n, picks):
        before = jnp.dot(tri_ref[...], member.astype(BF16), preferred_element_type=F32) + earlier
        lpos = [jnp.sum(jnp.where(hot, before, 0.0) + jnp.where(lane < ix, tile_n, 0.0),
                        axis=-1, keepdims=True) for hot, ix in zip(hots, idxs)]
        lpos_ref[rows, :] = cols(lpos).astype(jnp.int32)
        earlier = earlier + n_s
    tile_n_ref[0] = tile_n.astype(jnp.int32)
    tile_carry_ref[0] = carry_ref[...].astype(jnp.int32)
    carry_ref[...] = carry_ref[...] + tile_n
    cnt_ref[...] = carry_ref[...].astype(jnp.int32)


def _post_call(xf, conv, attn, wo, g, wr, br, tri):
    n, d = xf.shape
    tm = TM_POST
    const = lambda shape: pl.BlockSpec(shape, lambda i: (0,) * len(shape))
    row = lambda w: pl.BlockSpec((tm, w), lambda i: (i, 0))
    return pl.pallas_call(
        _post_kernel,
        grid=(n // tm,),
        in_specs=[row(d), row(CONV_WIDTH), row(CONV_WIDTH), const((d, d)), const((1, d)),
                  const((d, N_EXPERTS)), const((1, N_EXPERTS)), const(tri.shape)],
        out_specs=[row(d), row(d), row(TOP_K), row(TOP_K),
                   pl.BlockSpec((1, 1, N_EXPERTS), lambda i: (i, 0, 0)),
                   pl.BlockSpec((1, 1, N_EXPERTS), lambda i: (i, 0, 0)),
                   const((1, N_EXPERTS))],
        out_shape=[
            jax.ShapeDtypeStruct((n, d), F32),
            jax.ShapeDtypeStruct((n, d), BF16),
            jax.ShapeDtypeStruct((n, TOP_K), jnp.int32),
            jax.ShapeDtypeStruct((n, TOP_K), F32),
            jax.ShapeDtypeStruct((n // tm, 1, N_EXPERTS), jnp.int32),
            jax.ShapeDtypeStruct((n // tm, 1, N_EXPERTS), jnp.int32),
            jax.ShapeDtypeStruct((1, N_EXPERTS), jnp.int32),
        ],
        scratch_shapes=[pltpu.VMEM((1, N_EXPERTS), F32)],
        compiler_params=pltpu.CompilerParams(
            dimension_semantics=("arbitrary",), vmem_limit_bytes=VMEM_LIMIT),
        name="post_mixer_router",
    )(xf, conv, attn, wo, g, wr, br, tri)


RUN_BITS = 10


def _rows(ref, first_row, n_rows):
    return ref.at[pl.ds(pl.multiple_of(first_row * ROW_LINES, ROW_LINES), n_rows * ROW_LINES), :]


def _for_each_run_piece(n_ref, far_ref, fn, bits=RUN_BITS):
    def per_expert(e, local0):
        n_e = n_ref[0, 0, e]
        far0 = far_ref[0, 0, e]

        for b in reversed(range(bits)):
            done = (n_e >> (b + 1)) << (b + 1)

            @pl.when(((n_e >> b) & 1) == 1)
            def _():
                fn(local0 + done, far0 + done, 1 << b)
        return local0 + n_e

    lax.fori_loop(0, N_EXPERTS, per_expert, 0)


PAD_BITS = TM_EXP.bit_length() - 1
assert TM_EXP == 1 << PAD_BITS


def _scatter_kernel(n_ref, far_ref, pad_n_ref, pad_far_ref, used_ref, hb_ref, lpos_ref, xs_ref,
                    xl, zeros, sem, zsem):
    i = pl.program_id(0)
    steps = pl.num_programs(0)
    tm = hb_ref.shape[0]
    pairs = tm * TOP_K
    slot = i % 2

    def wait_slot(s):
        pltpu.make_async_copy(xl.at[s], _rows(xs_ref, 0, pairs), sem.at[s]).wait()

    @pl.when(i == 0)
    def _():
        zeros[...] = jnp.zeros(zeros.shape, F32)

        def fill(_, far_row, size):
            return pltpu.make_async_copy(_rows(zeros, 0, size), _rows(xs_ref, far_row, size), zsem)

        _for_each_run_piece(pad_n_ref, pad_far_ref, lambda *a: fill(*a).start(), PAD_BITS)
        _for_each_run_piece(pad_n_ref, pad_far_ref, lambda *a: fill(*a).wait(), PAD_BITS)

        tile_rows = zeros.shape[0] // ROW_LINES
        n_tiles = xs_ref.shape[0] // zeros.shape[0]

        def tail(op):
            def body(t, c):
                op(fill(None, t * tile_rows, tile_rows))
                return c
            lax.fori_loop(used_ref[0], n_tiles, body, 0)

        tail(lambda cp: cp.start())
        tail(lambda cp: cp.wait())

    @pl.when(i >= 2)
    def _():
        wait_slot(slot)

    lane = lax.broadcasted_iota(jnp.int32, (tm, 128), 1)
    lp = lpos_ref[...].astype(F32)
    cols = jnp.zeros((tm, 128), F32)
    for k in range(TOP_K):
        cols = jnp.where(lane == k, lp[:, k:k + 1], cols)
    rows_t = cols.T.astype(jnp.int16)
    j = lax.broadcasted_iota(jnp.int32, (pairs, tm), 0).astype(jnp.int16)
    onehot = jnp.zeros((pairs, tm), BF16)
    for k in range(TOP_K):
        onehot = jnp.where(j == rows_t[k:k + 1, :], jnp.ones((), BF16), onehot)
    _store_row_tiles(xl.at[slot], jnp.dot(onehot, hb_ref[...], preferred_element_type=F32))

    def send(local_row, far_row, size):
        pltpu.make_async_copy(_rows(xl.at[slot], local_row, size), _rows(xs_ref, far_row, size),
                              sem.at[slot]).start()

    _for_each_run_piece(n_ref, far_ref, send)

    @pl.when(i == steps - 1)
    def _():
        @pl.when(steps >= 2)
        def _():
            wait_slot(1 - slot)
        wait_slot(slot)


def _scatter_call(tile_n, tile_far, pad_n, pad_far, used_tiles, hb, lpos, sorted_rows):
    n, d = hb.shape
    tm = TM_POST
    table = pl.BlockSpec((1, 1, N_EXPERTS), lambda i: (i, 0, 0), memory_space=pltpu.SMEM)
    whole = pl.BlockSpec((1, 1, N_EXPERTS), lambda i: (0, 0, 0), memory_space=pltpu.SMEM)
    return pl.pallas_call(
        _scatter_kernel,
        grid=(n // tm,),
        in_specs=[table, table, whole, whole, pl.BlockSpec(memory_space=pltpu.SMEM),
                  pl.BlockSpec((tm, d), lambda i: (i, 0)),
                  pl.BlockSpec((tm, TOP_K), lambda i: (i, 0))],
        out_specs=pl.BlockSpec(memory_space=pl.ANY),
        out_shape=jax.ShapeDtypeStruct((sorted_rows * ROW_LINES, 128), F32),
        scratch_shapes=[pltpu.VMEM((2, tm * TOP_K * ROW_LINES, 128), F32),
                        pltpu.VMEM((TM_EXP * ROW_LINES, 128), F32),
                        pltpu.SemaphoreType.DMA((2,)), pltpu.SemaphoreType.DMA],
        compiler_params=pltpu.CompilerParams(
            dimension_semantics=("arbitrary",), vmem_limit_bytes=VMEM_LIMIT),
        name="dispatch_scatter",
    )(tile_n, tile_far, pad_n, pad_far, used_tiles, hb, lpos)


def _expert_kernel(tile_ref, exp_ref, flag_ref, next_ref,
                   xs_ref, wgu_hbm, bgu_ref, wd_hbm, bd_ref, o_ref,
                   wgu_f, wd_f, wgu_bf, wd_bf, wsem):
    i = pl.program_id(0)
    tm = xs_ref.shape[0] // ROW_LINES
    flag = flag_ref[i]
    cast_rows = 128

    def weight_copies(e, s):
        return (pltpu.make_async_copy(wgu_hbm.at[e], wgu_f.at[s], wsem.at[s, 0]),
                pltpu.make_async_copy(wd_hbm.at[e], wd_f.at[s], wsem.at[s, 1]))

    @pl.when(i == 0)
    def _():
        for cp in weight_copies(exp_ref[0], 0):
            cp.start()

    @pl.when((flag & 2) != 0)
    def _():
        s = (flag >> 3) & 1
        for cp in weight_copies(exp_ref[i], s):
            cp.wait()
        nxt = next_ref[i]

        @pl.when(nxt >= 0)
        def _():
            for cp in weight_copies(nxt, 1 - s):
                cp.start()

        def cast(r, c):
            rows = pl.ds(pl.multiple_of(r * cast_rows, cast_rows), cast_rows)
            wgu_bf[rows, :] = wgu_f[s, rows, :].astype(BF16)
            wd_bf[rows, :] = wd_f[s, rows, :].astype(BF16)
            return c
        lax.fori_loop(0, D_MODEL // cast_rows, cast, 0)

    def run(rows):
        x = _load_row_tiles(xs_ref, rows).astype(BF16)
        gu = jnp.dot(x, wgu_bf[...], preferred_element_type=F32) + bgu_ref[0]
        gate = jnp.minimum(gu[:, :D_FF], SWIGLU_LIMIT)
        up = jnp.clip(gu[:, D_FF:], -SWIGLU_LIMIT, SWIGLU_LIMIT)
        act = (up + 1.0) * (gate * (1.0 / (1.0 + jnp.exp(-SWIGLU_ALPHA * gate))))
        out = jnp.dot(act.astype(BF16), wd_bf[...], preferred_element_type=F32) + bd_ref[0]
        _store_row_tiles(o_ref, out)

    @pl.when((flag & 5) == 1)
    def _():
        run(tm)

    @pl.when((flag & 5) == 5)
    def _():
        run(tm // 2)
        o_ref[tm // 2 * ROW_LINES:, :] = jnp.zeros((tm // 2 * ROW_LINES, 128), F32)


def _expert_call(meta, xs, wgu, bgu, wd, bd):
    lines, w = xs.shape
    tm = TM_EXP
    n_items = meta[0].shape[0]
    return pl.pallas_call(
        _expert_kernel,
        grid_spec=pltpu.PrefetchScalarGridSpec(
            num_scalar_prefetch=4,
            grid=(n_items,),
            in_specs=[
                pl.BlockSpec((tm * ROW_LINES, w), lambda i, t, e, f, nx: (t[i], 0)),
                pl.BlockSpec(memory_space=pl.ANY),
                pl.BlockSpec((1, 1, 2 * D_FF), lambda i, t, e, f, nx: (e[i], 0, 0)),
                pl.BlockSpec(memory_space=pl.ANY),
                pl.BlockSpec((1, 1, D_MODEL), lambda i, t, e, f, nx: (e[i], 0, 0)),
            ],
            out_specs=pl.BlockSpec((tm * ROW_LINES, w), lambda i, t, e, f, nx: (t[i], 0)),
            scratch_shapes=[pltpu.VMEM((2, D_MODEL, 2 * D_FF), F32),
                            pltpu.VMEM((2, D_FF, D_MODEL), F32),
                            pltpu.VMEM((D_MODEL, 2 * D_FF), BF16),
                            pltpu.VMEM((D_FF, D_MODEL), BF16),
                            pltpu.SemaphoreType.DMA((2, 2))],
        ),
        out_shape=jax.ShapeDtypeStruct((lines, w), F32),
        input_output_aliases={len(meta): 0},
        compiler_params=pltpu.CompilerParams(
            dimension_semantics=("arbitrary",), vmem_limit_bytes=VMEM_LIMIT),
        name="moe_experts",
    )(*meta, xs, wgu, bgu, wd, bd)


def _expert_layout(counts, n_pairs, tm):
    n_tiles = n_pairs // tm + N_EXPERTS
    per_exp = (counts + tm - 1) // tm
    tile_end = jnp.cumsum(per_exp)
    tile0 = tile_end - per_exp
    used = tile_end[-1]
    step = jnp.arange(n_tiles, dtype=jnp.int32)
    tile = jnp.minimum(step, used - 1)
    e = jnp.sum((tile_end[None, :] <= tile[:, None]).astype(jnp.int32), axis=1)
    hot = e[:, None] == jnp.arange(N_EXPERTS, dtype=jnp.int32)[None, :]
    pick = lambda tab: jnp.sum(jnp.where(hot, tab[None, :], 0), axis=1)
    e_tile0 = pick(tile0)
    rows_used = pick(counts) - (step - e_tile0) * tm
    experts = jnp.arange(N_EXPERTS, dtype=jnp.int32)
    has_tiles = per_exp > 0
    slot_of_expert = (jnp.cumsum(has_tiles.astype(jnp.int32)) - 1) & 1
    later = (experts[None, :] > experts[:, None]) & has_tiles[None, :]
    next_of_expert = jnp.min(jnp.where(later, experts[None, :], N_EXPERTS), axis=1)
    next_of_expert = jnp.where(next_of_expert < N_EXPERTS, next_of_expert, -1)
    flag = jnp.where(step < used, 1 + 2 * (step == e_tile0) + 4 * (rows_used <= tm // 2)
                     + 8 * pick(slot_of_expert), 0)
    cast = lambda a: a.astype(jnp.int32)
    group_start = tile0 * tm
    pad_n = per_exp * tm - counts
    pad_start = group_start + counts
    return ((tile, cast(e), cast(flag), cast(pick(next_of_expert))), cast(used).reshape(1),
            cast(group_start), cast(pad_n), cast(pad_start))


def _combine_kernel(n_ref, far_ref, n_next_ref, far_next_ref, h1_ref, lpos_ref, gate_ref, ys_ref,
                    o_ref, yl, sem):
    i = pl.program_id(0)
    steps = pl.num_programs(0)
    tm = h1_ref.shape[0]
    pairs = tm * TOP_K
    slot = i % 2

    def fetch(nr, fr, s):
        def recv(local_row, far_row, size):
            pltpu.make_async_copy(_rows(ys_ref, far_row, size), _rows(yl.at[s], local_row, size),
                                  sem.at[s]).start()
        _for_each_run_piece(nr, fr, recv)

    @pl.when(i == 0)
    def _():
        fetch(n_ref, far_ref, 0)

    @pl.when(i + 1 < steps)
    def _():
        fetch(n_next_ref, far_next_ref, 1 - slot)

    pltpu.make_async_copy(_rows(ys_ref, 0, pairs), yl.at[slot], sem.at[slot]).wait()

    j = lax.broadcasted_iota(jnp.int32, (tm, pairs), 1).astype(jnp.int16)
    lp = lpos_ref[...].astype(jnp.int16)
    g = gate_ref[...].astype(BF16)
    w = jnp.zeros((tm, pairs), BF16)
    for k in range(TOP_K):
        w = jnp.where(j == lp[:, k:k + 1], g[:, k:k + 1], w)
    y = _load_row_tiles(yl, pairs, lead=(slot,)).astype(BF16)
    o_ref[...] = h1_ref[...] + jnp.dot(w, y, preferred_element_type=F32)


def _combine_call(tile_n, tile_far, h1, lpos, gates, ys):
    n, d = h1.shape
    tm = TM_POST
    steps = n // tm
    table = lambda fn: pl.BlockSpec((1, 1, N_EXPERTS), fn, memory_space=pltpu.SMEM)
    cur = lambda i: (i, 0, 0)
    nxt = lambda i: (jnp.minimum(i + 1, steps - 1), 0, 0)
    return pl.pallas_call(
        _combine_kernel,
        grid=(steps,),
        in_specs=[table(cur), table(cur), table(nxt), table(nxt),
                  pl.BlockSpec((tm, d), lambda i: (i, 0)),
                  pl.BlockSpec((tm, TOP_K), lambda i: (i, 0)),
                  pl.BlockSpec((tm, TOP_K), lambda i: (i, 0)),
                  pl.BlockSpec(memory_space=pl.ANY)],
        out_specs=pl.BlockSpec((tm, d), lambda i: (i, 0)),
        out_shape=jax.ShapeDtypeStruct((n, d), F32),
        scratch_shapes=[pltpu.VMEM((2, tm * TOP_K * ROW_LINES, 128), F32),
                        pltpu.SemaphoreType.DMA((2,))],
        compiler_params=pltpu.CompilerParams(
            dimension_semantics=("arbitrary",), vmem_limit_bytes=VMEM_LIMIT),
        name="moe_combine",
    )(tile_n, tile_far, tile_n, tile_far, h1, lpos, gates, ys)


def _rotate_half_perm():
    half = QK_ROPE // 2
    perm = np.concatenate([np.arange(half, QK_ROPE), np.arange(0, half)])
    sign = np.concatenate([-np.ones(half), np.ones(half)]).astype(np.float32)
    return perm, sign


def _layer(h, l, norm_mix, w_in, conv_w, q_norm, w_uq, kv_norm, w_ukv, q_head_norm,
           k_head_norm, w_o, norm_ffn, w_router, b_router, w_gate_up, b_gate_up,
           w_down, b_down):
    b, s, d = h.shape
    n = b * s
    perm, sign = _rotate_half_perm()

    kpe_cols = w_in[l][:, OFF_KPE:OFF_KPE + QK_ROPE]
    win = jnp.concatenate([w_in[l], kpe_cols[:, perm] * sign], axis=1).astype(BF16)
    wq = w_uq[l].reshape(Q_LORA, MLA_HEADS, QK_HEAD)
    wq_rope = wq[:, :, QK_NOPE:]
    wuq = jnp.concatenate(
        [wq[:, :, :QK_NOPE].reshape(Q_LORA, MLA_HEADS * QK_NOPE),
         jnp.concatenate([wq_rope, wq_rope[:, :, perm] * sign], axis=2)
         .reshape(Q_LORA, MLA_HEADS * 2 * QK_ROPE)], axis=1).astype(BF16)
    gq, gk = q_head_norm[l], k_head_norm[l]
    gqn, gkn = gq[None, :QK_NOPE], gk[None, :QK_NOPE]
    gqr = jnp.concatenate([gq[QK_NOPE:], gq[QK_NOPE:][perm]])[None, :]
    gkr = jnp.concatenate([gk[QK_NOPE:], gk[QK_NOPE:][perm]])[None, :]

    half = QK_ROPE // 2
    inv_freq = ROPE_THETA ** (-jnp.arange(half, dtype=F32) / half)
    ang = jnp.arange(s).astype(F32)[:, None] * inv_freq[None, :]
    cos, sin = jnp.cos(ang), jnp.sin(ang)
    tbl = jnp.concatenate([cos, cos, sin, sin], axis=1)

    conv, q, k, vt = _pre_call(
        h, norm_mix[l][None, :], win, conv_w[l], q_norm[l][None, :], wuq,
        kv_norm[l][None, :], w_ukv[l].astype(BF16), gqn, gqr, gkn, gkr, tbl)
    attn = _attn_call(q, k, vt)

    tri = jnp.tril(jnp.ones((SUB_POST, SUB_POST), BF16), -1)
    h1, hb, lpos, gates, tile_n, tile_carry, counts = _post_call(
        h.reshape(n, d), conv.reshape(n, CONV_WIDTH), attn.reshape(n, CONV_WIDTH),
        w_o[l].astype(BF16), norm_ffn[l][None, :], w_router[l].astype(BF16),
        b_router[l][None, :], tri)

    meta, used_tiles, group_start, pad_n, pad_start = _expert_layout(counts[0], n * TOP_K, TM_EXP)
    sorted_rows = n * TOP_K + N_EXPERTS * TM_EXP
    tile_far = (tile_carry + group_start[None, None, :]).astype(jnp.int32)
    xs = _scatter_call(tile_n, tile_far, pad_n[None, None, :], pad_start[None, None, :],
                       used_tiles, hb, lpos, sorted_rows)
    ys = _expert_call(meta, xs, w_gate_up[l], b_gate_up[l][:, None, :],
                      w_down[l], b_down[l][:, None, :])
    out = _combine_call(tile_n, tile_far, h1, lpos, gates, ys)
    return out.reshape(b, s, d)


def kernel(x, norm_mix, w_in, conv_w, q_norm, w_uq, kv_norm, w_ukv, q_head_norm, k_head_norm,
           w_o, norm_ffn, w_router, b_router, w_gate_up, b_gate_up, w_down, b_down):
    h = x
    for l in range(norm_mix.shape[0]):
        h = _layer(h, l, norm_mix, w_in, conv_w, q_norm, w_uq, kv_norm, w_ukv, q_head_norm,
                   k_head_norm, w_o, norm_ffn, w_router, b_router, w_gate_up, b_gate_up,
                   w_down, b_down)
    return h
```

```python
import math

import jax
import jax.numpy as jnp
import numpy as np
from jax import lax
from jax.experimental import pallas as pl
from jax.experimental.pallas import tpu as pltpu

F32 = jnp.float32
BF16 = jnp.bfloat16

D_MODEL = 1024
CONV_WIDTH = 512
MLA_HEADS = 4
QK_NOPE = 128
QK_ROPE = 64
QK_HEAD = QK_NOPE + QK_ROPE
QK_PAD = 256
V_HEAD = 128
Q_LORA = 384
KV_LORA = 256
ROPE_THETA = 10000.0
CHUNK = 64
N_EXPERTS = 32
TOP_K = 4
D_FF = 1024
SWIGLU_LIMIT = 7.0
SWIGLU_ALPHA = 1.702
EPS = 1e-6

OFF_B, OFF_C, OFF_X, OFF_CQ, OFF_CKV, OFF_KPE = 0, 512, 1024, 1536, 1920, 2176
IN_PROJ_EXT = 2304

VMEM_LIMIT = 56 * 1024 * 1024

TS_PRE = 1024
TQ = 256
Q_TILES_PER_STEP = 8
TM_POST = 512
SUB_POST = 256
TB_ROWS = 256
TM_EXP = 512
HALO = 8


def _rms(x, g):
    return x * lax.rsqrt(jnp.mean(x * x, axis=-1, keepdims=True) + EPS) * g


ROW_LINES = D_MODEL // 128


def _load_row_tiles(ref, rows, lead=()):
    return jnp.concatenate(
        [ref[lead + (pl.ds(c, rows, stride=ROW_LINES), slice(None))] for c in range(ROW_LINES)],
        axis=1)


def _store_row_tiles(ref, val):
    rows = val.shape[0]
    for c in range(ROW_LINES):
        ref[pl.ds(c, rows, stride=ROW_LINES), :] = val[:, 128 * c:128 * (c + 1)]


def _pre_kernel(x_ref, nm_ref, win_ref, convw_ref, qn_ref, wuq_ref, kvn_ref, wukv_ref,
                gqn_ref, gqr_ref, gkn_ref, gkr_ref, tbl_ref,
                conv_ref, q_ref, k_ref, vt_ref, ext_ref):
    s = pl.program_id(1)
    ts = x_ref.shape[1]
    sub = TQ
    starts = tuple(range(0, ts, sub))

    @pl.when(s == 0)
    def _():
        ext_ref[0:HALO, :] = jnp.zeros((HALO, CONV_WIDTH), F32)

    @pl.when(s != 0)
    def _():
        ext_ref[0:HALO, :] = ext_ref[ts:ts + HALO, :]

    projs = [jnp.dot(_rms(x_ref[0, r0:r0 + sub, :], nm_ref[...]).astype(BF16), win_ref[...],
                     preferred_element_type=F32) for r0 in starts]

    lane = lax.broadcasted_iota(jnp.int32, (sub, 128), 1)
    first_half = lane < QK_ROPE
    scale = math.log2(math.e) / math.sqrt(QK_HEAD)

    for r0, proj in zip(starts, projs):
        rows = slice(r0, r0 + sub)
        u = proj[:, OFF_C:OFF_C + CONV_WIDTH] * proj[:, OFF_X:OFF_X + CONV_WIDTH]
        ext_ref[HALO + r0:HALO + r0 + sub, :] = u
        y = (convw_ref[2:3, :] * u
             + convw_ref[1:2, :] * ext_ref[HALO + r0 - 1:HALO + r0 - 1 + sub, :]
             + convw_ref[0:1, :] * ext_ref[HALO + r0 - 2:HALO + r0 - 2 + sub, :])
        conv_ref[0, rows, :] = (proj[:, OFF_B:OFF_B + CONV_WIDTH] * y).astype(BF16)

        hq = _rms(proj[:, OFF_CQ:OFF_CQ + Q_LORA], qn_ref[...]).astype(BF16)
        qp = jnp.dot(hq, wuq_ref[...], preferred_element_type=F32)
        hkv = _rms(proj[:, OFF_CKV:OFF_CKV + KV_LORA], kvn_ref[...]).astype(BF16)
        kvp = jnp.dot(hkv, wukv_ref[...], preferred_element_type=F32)

        tbl = tbl_ref[rows, :]

        def rope_block(grp, gain):
            r = grp * tbl * gain
            r = r + pltpu.roll(r, QK_ROPE, axis=1)
            return jnp.where(first_half, r, 0.0)

        kpe = proj[:, OFF_KPE:OFF_KPE + 128]
        ss_kpe = jnp.sum(jnp.where(first_half, kpe * kpe, 0.0), axis=-1, keepdims=True)
        k_rope = rope_block(kpe, gkr_ref[...])
        for h in range(MLA_HEADS):
            qn = qp[:, 128 * h:128 * (h + 1)]
            qg = qp[:, 512 + 128 * h:512 + 128 * (h + 1)]
            ss = (jnp.sum(qn * qn, axis=-1, keepdims=True)
                  + jnp.sum(jnp.where(first_half, qg * qg, 0.0), axis=-1, keepdims=True))
            rs = lax.rsqrt(ss * (1.0 / QK_HEAD) + EPS) * scale
            q_ref[0, h, rows, 0:128] = (qn * rs * gqn_ref[...]).astype(BF16)
            q_ref[0, h, rows, 128:256] = (rope_block(qg, gqr_ref[...]) * rs).astype(BF16)

            kn = kvp[:, 256 * h:256 * h + 128]
            ssk = jnp.sum(kn * kn, axis=-1, keepdims=True) + ss_kpe
            rsk = lax.rsqrt(ssk * (1.0 / QK_HEAD) + EPS)
            k_ref[0, h, rows, 0:128] = (kn * rsk * gkn_ref[...]).astype(BF16)
            k_ref[0, h, rows, 128:256] = (k_rope * rsk).astype(BF16)
            v = kvp[:, 256 * h + 128:256 * (h + 1)]
            vt_ref[0, h, r0 // TQ] = v.T.astype(BF16)


def _pre_call(x, nm, win, convw, qn, wuq, kvn, wukv, gqn, gqr, gkn, gkr, tbl):
    b, s, d = x.shape
    ts = TS_PRE
    const = lambda shape: pl.BlockSpec(shape, lambda bi, si: (0,) * len(shape))
    return pl.pallas_call(
        _pre_kernel,
        grid=(b, s // ts),
        in_specs=[
            pl.BlockSpec((1, ts, d), lambda bi, si: (bi, si, 0)),
            const((1, d)), const((d, IN_PROJ_EXT)), const((3, CONV_WIDTH)),
            const((1, Q_LORA)), const((Q_LORA, 1024)),
            const((1, KV_LORA)), const((KV_LORA, 1024)),
            const((1, 128)), const((1, 128)), const((1, 128)), const((1, 128)),
            pl.BlockSpec((ts, 128), lambda bi, si: (si, 0)),
        ],
        out_specs=[
            pl.BlockSpec((1, ts, CONV_WIDTH), lambda bi, si: (bi, si, 0)),
            pl.BlockSpec((1, MLA_HEADS, ts, QK_PAD), lambda bi, si: (bi, 0, si, 0)),
            pl.BlockSpec((1, MLA_HEADS, ts, QK_PAD), lambda bi, si: (bi, 0, si, 0)),
            pl.BlockSpec((1, MLA_HEADS, ts // TQ, V_HEAD, TQ), lambda bi, si: (bi, 0, si, 0, 0)),
        ],
        out_shape=[
            jax.ShapeDtypeStruct((b, s, CONV_WIDTH), BF16),
            jax.ShapeDtypeStruct((b, MLA_HEADS, s, QK_PAD), BF16),
            jax.ShapeDtypeStruct((b, MLA_HEADS, s, QK_PAD), BF16),
            jax.ShapeDtypeStruct((b, MLA_HEADS, s // TQ, V_HEAD, TQ), BF16),
        ],
        scratch_shapes=[pltpu.VMEM((ts + HALO, CONV_WIDTH), F32)],
        compiler_params=pltpu.CompilerParams(
            dimension_semantics=("arbitrary", "arbitrary"), vmem_limit_bytes=VMEM_LIMIT),
        name="pre_mixer",
    )(x, nm, win, convw, qn, wuq, kvn, wukv, gqn, gqr, gkn, gkr, tbl)


def _attn_kernel(q_ref, k_ref, vt_ref, o_ref, m_ref, l_ref, acc_ref, s_ref):
    tq = TQ
    key_c = lax.broadcasted_iota(jnp.int32, (tq, tq), 0) // CHUNK
    qry_c = lax.broadcasted_iota(jnp.int32, (tq, tq), 1) // CHUNK
    allowed = key_c <= qry_c
    for t in range(q_ref.shape[2] // tq):
        _attn_tile(pl.program_id(1) * (q_ref.shape[2] // tq) + t, slice(t * tq, (t + 1) * tq),
                   allowed, q_ref, k_ref, vt_ref, o_ref, m_ref, l_ref, acc_ref, s_ref)


def _attn_tile(i, q_rows, allowed, q_ref, k_ref, vt_ref, o_ref, m_ref, l_ref, acc_ref, s_ref):
    tq = TQ
    m_ref[...] = jnp.full(m_ref.shape, -1e30, F32)
    l_ref[...] = jnp.zeros(l_ref.shape, F32)
    acc_ref[...] = jnp.zeros(acc_ref.shape, F32)

    def scores(j, keep):
        start = pl.multiple_of(j * tq, tq)
        out = []
        for h in range(MLA_HEADS):
            kb = k_ref[0, h, pl.ds(start, tq), :]
            st = lax.dot_general(kb, q_ref[0, h, q_rows, :], (((1,), (1,)), ((), ())),
                                 preferred_element_type=F32)
            out.append(st if keep is None else jnp.where(keep, st, -1e30))
        return out

    def update(j):
        probs, alphas = [], []
        for h in range(MLA_HEADS):
            st = s_ref[h]
            m = m_ref[h]
            m_new = jnp.maximum(m, jnp.max(st, axis=0, keepdims=True))
            alpha = jnp.exp2(m - m_new)
            p = jnp.exp2(st - m_new)
            l_ref[h] = alpha * l_ref[h] + jnp.sum(p, axis=0, keepdims=True)
            m_ref[h] = m_new
            probs.append(p.astype(BF16))
            alphas.append(alpha)
        for h in range(MLA_HEADS):
            acc_ref[h] = alphas[h] * acc_ref[h] + jnp.dot(vt_ref[0, h, j], probs[h],
                                                           preferred_element_type=F32)

    def stage(sts):
        for h in range(MLA_HEADS):
            s_ref[h] = sts[h]

    stage(scores(0, allowed | (i > 0)))

    def body(j, c):
        nxt = scores(j + 1, None)
        update(j)
        stage(nxt)
        return c

    lax.fori_loop(0, i - 1, body, 0)

    @pl.when(i > 0)
    def _():
        nxt = scores(i, allowed)
        update(i - 1)
        stage(nxt)

    update(i)
    for h in range(MLA_HEADS):
        o = acc_ref[h] / l_ref[h]
        o_ref[0, q_rows, V_HEAD * h:V_HEAD * (h + 1)] = o.T.astype(BF16)


def _attn_call(q, k, vt):
    b, nh, s, _ = q.shape
    rows = TQ * Q_TILES_PER_STEP
    return pl.pallas_call(
        _attn_kernel,
        grid=(b, s // rows),
        in_specs=[
            pl.BlockSpec((1, nh, rows, QK_PAD), lambda bi, qi: (bi, 0, qi, 0)),
            pl.BlockSpec((1, nh, s, QK_PAD), lambda bi, qi: (bi, 0, 0, 0)),
            pl.BlockSpec((1, nh, s // TQ, V_HEAD, TQ), lambda bi, qi: (bi, 0, 0, 0, 0)),
        ],
        out_specs=pl.BlockSpec((1, rows, nh * V_HEAD), lambda bi, qi: (bi, qi, 0)),
        out_shape=jax.ShapeDtypeStruct((b, s, nh * V_HEAD), BF16),
        scratch_shapes=[pltpu.VMEM((nh, 1, TQ), F32), pltpu.VMEM((nh, 1, TQ), F32),
                        pltpu.VMEM((nh, V_HEAD, TQ), F32), pltpu.VMEM((nh, TQ, TQ), F32)],
        compiler_params=pltpu.CompilerParams(
            dimension_semantics=("arbitrary", "arbitrary"), vmem_limit_bytes=VMEM_LIMIT),
        name="mla_attention",
    )(q, k, vt)


def _post_kernel(x_ref, conv_ref, attn_ref, wo_ref, g_ref, wr_ref, br_ref, tri_ref,
                 h1_ref, hb_ref, lpos_ref, gate_ref, tile_n_ref, tile_carry_ref, cnt_ref,
                 carry_ref):
    step = pl.program_id(0)
    tm = x_ref.shape[0]

    @pl.when(step == 0)
    def _():
        carry_ref[...] = jnp.zeros_like(carry_ref)

    sub = tri_ref.shape[0]
    subs = [slice(r0, r0 + sub) for r0 in range(0, tm, sub)]
    h1s = [x_ref[rows, :]
           + jnp.dot(conv_ref[rows, :], wo_ref[0:CONV_WIDTH, :], preferred_element_type=F32)
           + jnp.dot(attn_ref[rows, :], wo_ref[CONV_WIDTH:, :], preferred_element_type=F32)
           for rows in subs]
    lane = lax.broadcasted_iota(jnp.int32, (sub, N_EXPERTS), 1).astype(F32)
    lane4 = lax.broadcasted_iota(jnp.int32, (sub, TOP_K), 1)

    def cols(c):
        return jnp.where(lane4 == 0, c[0], jnp.where(lane4 == 1, c[1],
                         jnp.where(lane4 == 2, c[2], c[3])))

    picks, members = [], []
    for rows, h1 in zip(subs, h1s):
        h1_ref[rows, :] = h1
        hb = _rms(h1, g_ref[...]).astype(BF16)
        hb_ref[rows, :] = hb
        lg = jnp.dot(hb, wr_ref[...], preferred_element_type=F32) + br_ref[...]
        member = jnp.zeros((sub, N_EXPERTS), F32)
        vals, idxs, hots = [], [], []
        for _ in range(TOP_K):
            m = jnp.max(lg, axis=-1, keepdims=True)
            ix = jnp.min(jnp.where(lg == m, lane, float(N_EXPERTS)), axis=-1, keepdims=True)
            hot = lane == ix
            lg = jnp.where(hot, -jnp.inf, lg)
            member = member + hot.astype(F32)
            vals.append(m)
            idxs.append(ix)
            hots.append(hot)
        exps = [jnp.exp(v - vals[0]) for v in vals]
        denom = exps[0] + exps[1] + exps[2] + exps[3]
        gate_ref[rows, :] = cols([e / denom for e in exps])
        picks.append((idxs, hots))
        members.append(member)

    sub_n = [jnp.sum(member, axis=0, keepdims=True) for member in members]
    tile_n = sub_n[0]
    for n_s in sub_n[1:]:
        tile_n = tile_n + n_s
    earlier = jnp.zeros((1, N_EXPERTS), F32)
    for rows, member, n_s, (idxs, hots) in zip(subs, members, sub_n, picks):
        before = jnp.dot(tri_ref[...], member.astype(BF16), preferred_element_type=F32) + earlier
        lpos = [jnp.sum(jnp.where(hot, before, 0.0) + jnp.where(lane < ix, tile_n, 0.0),
                        axis=-1, keepdims=True) for hot, ix in zip(hots, idxs)]
        lpos_ref[rows, :] = cols(lpos).astype(jnp.int32)
        earlier = earlier + n_s
    tile_n_ref[0] = tile_n.astype(jnp.int32)
    tile_carry_ref[0] = carry_ref[...].astype(jnp.int32)
    carry_ref[...] = carry_ref[...] + tile_n
    cnt_ref[...] = carry_ref[...].astype(jnp.int32)


def _post_call(xf, conv, attn, wo, g, wr, br, tri):
    n, d = xf.shape
    tm = TM_POST
    const = lambda shape: pl.BlockSpec(shape, lambda i: (0,) * len(shape))
    row = lambda w: pl.BlockSpec((tm, w), lambda i: (i, 0))
    return pl.pallas_call(
        _post_kernel,
        grid=(n // tm,),
        in_specs=[row(d), row(CONV_WIDTH), row(CONV_WIDTH), const((d, d)), const((1, d)),
                  const((d, N_EXPERTS)), const((1, N_EXPERTS)), const(tri.shape)],
        out_specs=[row(d), row(d), row(TOP_K), row(TOP_K),
                   pl.BlockSpec((1, 1, N_EXPERTS), lambda i: (i, 0, 0)),
                   pl.BlockSpec((1, 1, N_EXPERTS), lambda i: (i, 0, 0)),
                   const((1, N_EXPERTS))],
        out_shape=[
            jax.ShapeDtypeStruct((n, d), F32),
            jax.ShapeDtypeStruct((n, d), BF16),
            jax.ShapeDtypeStruct((n, TOP_K), jnp.int32),
            jax.ShapeDtypeStruct((n, TOP_K), F32),
            jax.ShapeDtypeStruct((n // tm, 1, N_EXPERTS), jnp.int32),
            jax.ShapeDtypeStruct((n // tm, 1, N_EXPERTS), jnp.int32),
            jax.ShapeDtypeStruct((1, N_EXPERTS), jnp.int32),
        ],
        scratch_shapes=[pltpu.VMEM((1, N_EXPERTS), F32)],
        compiler_params=pltpu.CompilerParams(
            dimension_semantics=("arbitrary",), vmem_limit_bytes=VMEM_LIMIT),
        name="post_mixer_router",
    )(xf, conv, attn, wo, g, wr, br, tri)


RUN_BITS = 10


def _rows(ref, first_row, n_rows):
    return ref.at[pl.ds(pl.multiple_of(first_row * ROW_LINES, ROW_LINES), n_rows * ROW_LINES), :]


def _for_each_run_piece(n_ref, far_ref, fn, bits=RUN_BITS):
    def per_expert(e, local0):
        n_e = n_ref[0, 0, e]
        far0 = far_ref[0, 0, e]

        for b in reversed(range(bits)):
            done = (n_e >> (b + 1)) << (b + 1)

            @pl.when(((n_e >> b) & 1) == 1)
            def _():
                fn(local0 + done, far0 + done, 1 << b)
        return local0 + n_e

    lax.fori_loop(0, N_EXPERTS, per_expert, 0)


PAD_BITS = TM_EXP.bit_length() - 1
assert TM_EXP == 1 << PAD_BITS


def _scatter_kernel(n_ref, far_ref, pad_n_ref, pad_far_ref, used_ref, hb_ref, lpos_ref, xs_ref,
                    xl, zeros, sem, zsem):
    i = pl.program_id(0)
    steps = pl.num_programs(0)
    tm = hb_ref.shape[0]
    pairs = tm * TOP_K
    slot = i % 2

    def wait_slot(s):
        pltpu.make_async_copy(xl.at[s], _rows(xs_ref, 0, pairs), sem.at[s]).wait()

    @pl.when(i == 0)
    def _():
        zeros[...] = jnp.zeros(zeros.shape, F32)

        def fill(_, far_row, size):
            return pltpu.make_async_copy(_rows(zeros, 0, size), _rows(xs_ref, far_row, size), zsem)

        _for_each_run_piece(pad_n_ref, pad_far_ref, lambda *a: fill(*a).start(), PAD_BITS)
        _for_each_run_piece(pad_n_ref, pad_far_ref, lambda *a: fill(*a).wait(), PAD_BITS)

        tile_rows = zeros.shape[0] // ROW_LINES
        n_tiles = xs_ref.shape[0] // zeros.shape[0]

        def tail(op):
            def body(t, c):
                op(fill(None, t * tile_rows, tile_rows))
                return c
            lax.fori_loop(used_ref[0], n_tiles, body, 0)

        tail(lambda cp: cp.start())
        tail(lambda cp: cp.wait())

    @pl.when(i >= 2)
    def _():
        wait_slot(slot)

    lane = lax.broadcasted_iota(jnp.int32, (tm, 128), 1)
    lp = lpos_ref[...].astype(F32)
    cols = jnp.zeros((tm, 128), F32)
    for k in range(TOP_K):
        cols = jnp.where(lane == k, lp[:, k:k + 1], cols)
    rows_t = cols.T.astype(jnp.int16)
    j = lax.broadcasted_iota(jnp.int32, (pairs, tm), 0).astype(jnp.int16)
    onehot = jnp.zeros((pairs, tm), BF16)
    for k in range(TOP_K):
        onehot = jnp.where(j == rows_t[k:k + 1, :], jnp.ones((), BF16), onehot)
    _store_row_tiles(xl.at[slot], jnp.dot(onehot, hb_ref[...], preferred_element_type=F32))

    def send(local_row, far_row, size):
        pltpu.make_async_copy(_rows(xl.at[slot], local_row, size), _rows(xs_ref, far_row, size),
                              sem.at[slot]).start()

    _for_each_run_piece(n_ref, far_ref, send)

    @pl.when(i == steps - 1)
    def _():
        @pl.when(steps >= 2)
        def _():
            wait_slot(1 - slot)
        wait_slot(slot)


def _scatter_call(tile_n, tile_far, pad_n, pad_far, used_tiles, hb, lpos, sorted_rows):
    n, d = hb.shape
    tm = TM_POST
    table = pl.BlockSpec((1, 1, N_EXPERTS), lambda i: (i, 0, 0), memory_space=pltpu.SMEM)
    whole = pl.BlockSpec((1, 1, N_EXPERTS), lambda i: (0, 0, 0), memory_space=pltpu.SMEM)
    return pl.pallas_call(
        _scatter_kernel,
        grid=(n // tm,),
        in_specs=[table, table, whole, whole, pl.BlockSpec(memory_space=pltpu.SMEM),
                  pl.BlockSpec((tm, d), lambda i: (i, 0)),
                  pl.BlockSpec((tm, TOP_K), lambda i: (i, 0))],
        out_specs=pl.BlockSpec(memory_space=pl.ANY),
        out_shape=jax.ShapeDtypeStruct((sorted_rows * ROW_LINES, 128), F32),
        scratch_shapes=[pltpu.VMEM((2, tm * TOP_K * ROW_LINES, 128), F32),
                        pltpu.VMEM((TM_EXP * ROW_LINES, 128), F32),
                        pltpu.SemaphoreType.DMA((2,)), pltpu.SemaphoreType.DMA],
        compiler_params=pltpu.CompilerParams(
            dimension_semantics=("arbitrary",), vmem_limit_bytes=VMEM_LIMIT),
        name="dispatch_scatter",
    )(tile_n, tile_far, pad_n, pad_far, used_tiles, hb, lpos)


def _expert_kernel(tile_ref, exp_ref, flag_ref, next_ref,
                   xs_ref, wgu_hbm, bgu_ref, wd_hbm, bd_ref, o_ref,
                   wgu_f, wd_f, wgu_bf, wd_bf, wsem):
    i = pl.program_id(0)
    tm = xs_ref.shape[0] // ROW_LINES
    flag = flag_ref[i]
    cast_rows = 128

    def weight_copies(e, s):
        return (pltpu.make_async_copy(wgu_hbm.at[e], wgu_f.at[s], wsem.at[s, 0]),
                pltpu.make_async_copy(wd_hbm.at[e], wd_f.at[s], wsem.at[s, 1]))

    @pl.when(i == 0)
    def _():
        for cp in weight_copies(exp_ref[0], 0):
            cp.start()

    @pl.when((flag & 2) != 0)
    def _():
        s = (flag >> 3) & 1
        for cp in weight_copies(exp_ref[i], s):
            cp.wait()
        nxt = next_ref[i]

        @pl.when(nxt >= 0)
        def _():
            for cp in weight_copies(nxt, 1 - s):
                cp.start()

        def cast(r, c):
            rows = pl.ds(pl.multiple_of(r * cast_rows, cast_rows), cast_rows)
            wgu_bf[rows, :] = wgu_f[s, rows, :].astype(BF16)
            wd_bf[rows, :] = wd_f[s, rows, :].astype(BF16)
            return c
        lax.fori_loop(0, D_MODEL // cast_rows, cast, 0)

    def run(rows):
        x = _load_row_tiles(xs_ref, rows).astype(BF16)
        gu = jnp.dot(x, wgu_bf[...], preferred_element_type=F32) + bgu_ref[0]
        gate = jnp.minimum(gu[:, :D_FF], SWIGLU_LIMIT)
        up = jnp.clip(gu[:, D_FF:], -SWIGLU_LIMIT, SWIGLU_LIMIT)
        act = (up + 1.0) * (gate * (1.0 / (1.0 + jnp.exp(-SWIGLU_ALPHA * gate))))
        out = jnp.dot(act.astype(BF16), wd_bf[...], preferred_element_type=F32) + bd_ref[0]
        _store_row_tiles(o_ref, out)

    @pl.when((flag & 5) == 1)
    def _():
        run(tm)

    @pl.when((flag & 5) == 5)
    def _():
        run(tm // 2)
        o_ref[tm // 2 * ROW_LINES:, :] = jnp.zeros((tm // 2 * ROW_LINES, 128), F32)


def _expert_call(meta, xs, wgu, bgu, wd, bd):
    lines, w = xs.shape
    tm = TM_EXP
    n_items = meta[0].shape[0]
    return pl.pallas_call(
        _expert_kernel,
        grid_spec=pltpu.PrefetchScalarGridSpec(
            num_scalar_prefetch=4,
            grid=(n_items,),
            in_specs=[
                pl.BlockSpec((tm * ROW_LINES, w), lambda i, t, e, f, nx: (t[i], 0)),
                pl.BlockSpec(memory_space=pl.ANY),
                pl.BlockSpec((1, 1, 2 * D_FF), lambda i, t, e, f, nx: (e[i], 0, 0)),
                pl.BlockSpec(memory_space=pl.ANY),
                pl.BlockSpec((1, 1, D_MODEL), lambda i, t, e, f, nx: (e[i], 0, 0)),
            ],
            out_specs=pl.BlockSpec((tm * ROW_LINES, w), lambda i, t, e, f, nx: (t[i], 0)),
            scratch_shapes=[pltpu.VMEM((2, D_MODEL, 2 * D_FF), F32),
                            pltpu.VMEM((2, D_FF, D_MODEL), F32),
                            pltpu.VMEM((D_MODEL, 2 * D_FF), BF16),
                            pltpu.VMEM((D_FF, D_MODEL), BF16),
                            pltpu.SemaphoreType.DMA((2, 2))],
        ),
        out_shape=jax.ShapeDtypeStruct((lines, w), F32),
        input_output_aliases={len(meta): 0},
        compiler_params=pltpu.CompilerParams(
            dimension_semantics=("arbitrary",), vmem_limit_bytes=VMEM_LIMIT),
        name="moe_experts",
    )(*meta, xs, wgu, bgu, wd, bd)


def _expert_layout(counts, n_pairs, tm):
    n_tiles = n_pairs // tm + N_EXPERTS
    per_exp = (counts + tm - 1) // tm
    tile_end = jnp.cumsum(per_exp)
    tile0 = tile_end - per_exp
    used = tile_end[-1]
    step = jnp.arange(n_tiles, dtype=jnp.int32)
    tile = jnp.minimum(step, used - 1)
    e = jnp.sum((tile_end[None, :] <= tile[:, None]).astype(jnp.int32), axis=1)
    hot = e[:, None] == jnp.arange(N_EXPERTS, dtype=jnp.int32)[None, :]
    pick = lambda tab: jnp.sum(jnp.where(hot, tab[None, :], 0), axis=1)
    e_tile0 = pick(tile0)
    rows_used = pick(counts) - (step - e_tile0) * tm
    experts = jnp.arange(N_EXPERTS, dtype=jnp.int32)
    has_tiles = per_exp > 0
    slot_of_expert = (jnp.cumsum(has_tiles.astype(jnp.int32)) - 1) & 1
    later = (experts[None, :] > experts[:, None]) & has_tiles[None, :]
    next_of_expert = jnp.min(jnp.where(later, experts[None, :], N_EXPERTS), axis=1)
    next_of_expert = jnp.where(next_of_expert < N_EXPERTS, next_of_expert, -1)
    flag = jnp.where(step < used, 1 + 2 * (step == e_tile0) + 4 * (rows_used <= tm // 2)
                     + 8 * pick(slot_of_expert), 0)
    cast = lambda a: a.astype(jnp.int32)
    group_start = tile0 * tm
    pad_n = per_exp * tm - counts
    pad_start = group_start + counts
    return ((tile, cast(e), cast(flag), cast(pick(next_of_expert))), cast(used).reshape(1),
            cast(group_start), cast(pad_n), cast(pad_start))


def _combine_kernel(n_ref, far_ref, n_next_ref, far_next_ref, h1_ref, lpos_ref, gate_ref, ys_ref,
                    o_ref, yl, sem):
    i = pl.program_id(0)
    steps = pl.num_programs(0)
    tm = h1_ref.shape[0]
    pairs = tm * TOP_K
    slot = i % 2

    def fetch(nr, fr, s):
        def recv(local_row, far_row, size):
            pltpu.make_async_copy(_rows(ys_ref, far_row, size), _rows(yl.at[s], local_row, size),
                                  sem.at[s]).start()
        _for_each_run_piece(nr, fr, recv)

    @pl.when(i == 0)
    def _():
        fetch(n_ref, far_ref, 0)

    @pl.when(i + 1 < steps)
    def _():
        fetch(n_next_ref, far_next_ref, 1 - slot)

    pltpu.make_async_copy(_rows(ys_ref, 0, pairs), yl.at[slot], sem.at[slot]).wait()

    j = lax.broadcasted_iota(jnp.int32, (tm, pairs), 1).astype(jnp.int16)
    lp = lpos_ref[...].astype(jnp.int16)
    g = gate_ref[...].astype(BF16)
    w = jnp.zeros((tm, pairs), BF16)
    for k in range(TOP_K):
        w = jnp.where(j == lp[:, k:k + 1], g[:, k:k + 1], w)
    y = _load_row_tiles(yl, pairs, lead=(slot,)).astype(BF16)
    o_ref[...] = h1_ref[...] + jnp.dot(w, y, preferred_element_type=F32)


def _combine_call(tile_n, tile_far, h1, lpos, gates, ys):
    n, d = h1.shape
    tm = TM_POST
    steps = n // tm
    table = lambda fn: pl.BlockSpec((1, 1, N_EXPERTS), fn, memory_space=pltpu.SMEM)
    cur = lambda i: (i, 0, 0)
    nxt = lambda i: (jnp.minimum(i + 1, steps - 1), 0, 0)
    return pl.pallas_call(
        _combine_kernel,
        grid=(steps,),
        in_specs=[table(cur), table(cur), table(nxt), table(nxt),
                  pl.BlockSpec((tm, d), lambda i: (i, 0)),
                  pl.BlockSpec((tm, TOP_K), lambda i: (i, 0)),
                  pl.BlockSpec((tm, TOP_K), lambda i: (i, 0)),
                  pl.BlockSpec(memory_space=pl.ANY)],
        out_specs=pl.BlockSpec((tm, d), lambda i: (i, 0)),
        out_shape=jax.ShapeDtypeStruct((n, d), F32),
        scratch_shapes=[pltpu.VMEM((2, tm * TOP_K * ROW_LINES, 128), F32),
                        pltpu.SemaphoreType.DMA((2,))],
        compiler_params=pltpu.CompilerParams(
            dimension_semantics=("arbitrary",), vmem_limit_bytes=VMEM_LIMIT),
        name="moe_combine",
    )(tile_n, tile_far, tile_n, tile_far, h1, lpos, gates, ys)


def _rotate_half_perm():
    half = QK_ROPE // 2
    perm = np.concatenate([np.arange(half, QK_ROPE), np.arange(0, half)])
    sign = np.concatenate([-np.ones(half), np.ones(half)]).astype(np.float32)
    return perm, sign


def _layer(h, l, norm_mix, w_in, conv_w, q_norm, w_uq, kv_norm, w_ukv, q_head_norm,
           k_head_norm, w_o, norm_ffn, w_router, b_router, w_gate_up, b_gate_up,
           w_down, b_down):
    b, s, d = h.shape
    n = b * s
    perm, sign = _rotate_half_perm()

    kpe_cols = w_in[l][:, OFF_KPE:OFF_KPE + QK_ROPE]
    win = jnp.concatenate([w_in[l], kpe_cols[:, perm] * sign], axis=1).astype(BF16)
    wq = w_uq[l].reshape(Q_LORA, MLA_HEADS, QK_HEAD)
    wq_rope = wq[:, :, QK_NOPE:]
    wuq = jnp.concatenate(
        [wq[:, :, :QK_NOPE].reshape(Q_LORA, MLA_HEADS * QK_NOPE),
         jnp.concatenate([wq_rope, wq_rope[:, :, perm] * sign], axis=2)
         .reshape(Q_LORA, MLA_HEADS * 2 * QK_ROPE)], axis=1).astype(BF16)
    gq, gk = q_head_norm[l], k_head_norm[l]
    gqn, gkn = gq[None, :QK_NOPE], gk[None, :QK_NOPE]
    gqr = jnp.concatenate([gq[QK_NOPE:], gq[QK_NOPE:][perm]])[None, :]
    gkr = jnp.concatenate([gk[QK_NOPE:], gk[QK_NOPE:][perm]])[None, :]

    half = QK_ROPE // 2
    inv_freq = ROPE_THETA ** (-jnp.arange(half, dtype=F32) / half)
    ang = jnp.arange(s).astype(F32)[:, None] * inv_freq[None, :]
    cos, sin = jnp.cos(ang), jnp.sin(ang)
    tbl = jnp.concatenate([cos, cos, sin, sin], axis=1)

    conv, q, k, vt = _pre_call(
        h, norm_mix[l][None, :], win, conv_w[l], q_norm[l][None, :], wuq,
        kv_norm[l][None, :], w_ukv[l].astype(BF16), gqn, gqr, gkn, gkr, tbl)
    attn = _attn_call(q, k, vt)

    tri = jnp.tril(jnp.ones((SUB_POST, SUB_POST), BF16), -1)
    h1, hb, lpos, gates, tile_n, tile_carry, counts = _post_call(
        h.reshape(n, d), conv.reshape(n, CONV_WIDTH), attn.reshape(n, CONV_WIDTH),
        w_o[l].astype(BF16), norm_ffn[l][None, :], w_router[l].astype(BF16),
        b_router[l][None, :], tri)

    meta, used_tiles, group_start, pad_n, pad_start = _expert_layout(counts[0], n * TOP_K, TM_EXP)
    sorted_rows = n * TOP_K + N_EXPERTS * TM_EXP
    tile_far = (tile_carry + group_start[None, None, :]).astype(jnp.int32)
    xs = _scatter_call(tile_n, tile_far, pad_n[None, None, :], pad_start[None, None, :],
                       used_tiles, hb, lpos, sorted_rows)
    ys = _expert_call(meta, xs, w_gate_up[l], b_gate_up[l][:, None, :],
                      w_down[l], b_down[l][:, None, :])
    out = _combine_call(tile_n, tile_far, h1, lpos, gates, ys)
    return out.reshape(b, s, d)


def kernel(x, norm_mix, w_in, conv_w, q_norm, w_uq, kv_norm, w_ukv, q_head_norm, k_head_norm,
           w_o, norm_ffn, w_router, b_router, w_gate_up, b_gate_up, w_down, b_down):
    h = x
    for l in range(norm_mix.shape[0]):
        h = _layer(h, l, norm_mix, w_in, conv_w, q_norm, w_uq, kv_norm, w_ukv, q_head_norm,
                   k_head_norm, w_o, norm_ffn, w_router, b_router, w_gate_up, b_gate_up,
                   w_down, b_down)
    return h
```

```python
import math

import jax
import jax.numpy as jnp
import numpy as np
from jax import lax
from jax.experimental import pallas as pl
from jax.experimental.pallas import tpu as pltpu

F32 = jnp.float32
BF16 = jnp.bfloat16

D_MODEL = 1024
CONV_WIDTH = 512
MLA_HEADS = 4
QK_NOPE = 128
QK_ROPE = 64
QK_HEAD = QK_NOPE + QK_ROPE
QK_PAD = 256
V_HEAD = 128
Q_LORA = 384
KV_LORA = 256
ROPE_THETA = 10000.0
CHUNK = 64
N_EXPERTS = 32
TOP_K = 4
D_FF = 1024
SWIGLU_LIMIT = 7.0
SWIGLU_ALPHA = 1.702
EPS = 1e-6

OFF_B, OFF_C, OFF_X, OFF_CQ, OFF_CKV, OFF_KPE = 0, 512, 1024, 1536, 1920, 2176
IN_PROJ_EXT = 2304

VMEM_LIMIT = 56 * 1024 * 1024

TS_PRE = 1024
TQ = 256
Q_TILES_PER_STEP = 8
TM_POST = 512
SUB_POST = 256
POST_TILES_PER_STEP = 2
TB_ROWS = 256
TM_EXP = 512
HALO = 8


def _rms(x, g):
    return x * lax.rsqrt(jnp.mean(x * x, axis=-1, keepdims=True) + EPS) * g


ROW_LINES = D_MODEL // 128


def _load_row_tiles(ref, rows, lead=()):
    return jnp.concatenate(
        [ref[lead + (pl.ds(c, rows, stride=ROW_LINES), slice(None))] for c in range(ROW_LINES)],
        axis=1)


def _store_row_tiles(ref, val):
    rows = val.shape[0]
    for c in range(ROW_LINES):
        ref[pl.ds(c, rows, stride=ROW_LINES), :] = val[:, 128 * c:128 * (c + 1)]


def _pre_kernel(x_ref, nm_ref, win_ref, convw_ref, qn_ref, wuq_ref, kvn_ref, wukv_ref,
                gqn_ref, gqr_ref, gkn_ref, gkr_ref, tbl_ref,
                conv_ref, q_ref, k_ref, vt_ref, ext_ref):
    s = pl.program_id(1)
    ts = x_ref.shape[1]
    sub = TQ
    starts = tuple(range(0, ts, sub))

    @pl.when(s == 0)
    def _():
        ext_ref[0:HALO, :] = jnp.zeros((HALO, CONV_WIDTH), F32)

    @pl.when(s != 0)
    def _():
        ext_ref[0:HALO, :] = ext_ref[ts:ts + HALO, :]

    projs = [jnp.dot(_rms(x_ref[0, r0:r0 + sub, :], nm_ref[...]).astype(BF16), win_ref[...],
                     preferred_element_type=F32) for r0 in starts]

    lane = lax.broadcasted_iota(jnp.int32, (sub, 128), 1)
    first_half = lane < QK_ROPE
    scale = math.log2(math.e) / math.sqrt(QK_HEAD)

    for r0, proj in zip(starts, projs):
        rows = slice(r0, r0 + sub)
        u = proj[:, OFF_C:OFF_C + CONV_WIDTH] * proj[:, OFF_X:OFF_X + CONV_WIDTH]
        ext_ref[HALO + r0:HALO + r0 + sub, :] = u
        y = (convw_ref[2:3, :] * u
             + convw_ref[1:2, :] * ext_ref[HALO + r0 - 1:HALO + r0 - 1 + sub, :]
             + convw_ref[0:1, :] * ext_ref[HALO + r0 - 2:HALO + r0 - 2 + sub, :])
        conv_ref[0, rows, :] = (proj[:, OFF_B:OFF_B + CONV_WIDTH] * y).astype(BF16)

        hq = _rms(proj[:, OFF_CQ:OFF_CQ + Q_LORA], qn_ref[...]).astype(BF16)
        qp = jnp.dot(hq, wuq_ref[...], preferred_element_type=F32)
        hkv = _rms(proj[:, OFF_CKV:OFF_CKV + KV_LORA], kvn_ref[...]).astype(BF16)
        kvp = jnp.dot(hkv, wukv_ref[...], preferred_element_type=F32)

        tbl = tbl_ref[rows, :]

        def rope_block(grp, gain):
            r = grp * tbl * gain
            r = r + pltpu.roll(r, QK_ROPE, axis=1)
            return jnp.where(first_half, r, 0.0)

        kpe = proj[:, OFF_KPE:OFF_KPE + 128]
        ss_kpe = jnp.sum(jnp.where(first_half, kpe * kpe, 0.0), axis=-1, keepdims=True)
        k_rope = rope_block(kpe, gkr_ref[...])
        for h in range(MLA_HEADS):
            qn = qp[:, 128 * h:128 * (h + 1)]
            qg = qp[:, 512 + 128 * h:512 + 128 * (h + 1)]
            ss = (jnp.sum(qn * qn, axis=-1, keepdims=True)
                  + jnp.sum(jnp.where(first_half, qg * qg, 0.0), axis=-1, keepdims=True))
            rs = lax.rsqrt(ss * (1.0 / QK_HEAD) + EPS) * scale
            q_ref[0, h, rows, 0:128] = (qn * rs * gqn_ref[...]).astype(BF16)
            q_ref[0, h, rows, 128:256] = (rope_block(qg, gqr_ref[...]) * rs).astype(BF16)

            kn = kvp[:, 256 * h:256 * h + 128]
            ssk = jnp.sum(kn * kn, axis=-1, keepdims=True) + ss_kpe
            rsk = lax.rsqrt(ssk * (1.0 / QK_HEAD) + EPS)
            k_ref[0, h, rows, 0:128] = (kn * rsk * gkn_ref[...]).astype(BF16)
            k_ref[0, h, rows, 128:256] = (k_rope * rsk).astype(BF16)
            v = kvp[:, 256 * h + 128:256 * (h + 1)]
            vt_ref[0, h, r0 // TQ] = v.T.astype(BF16)


def _pre_call(x, nm, win, convw, qn, wuq, kvn, wukv, gqn, gqr, gkn, gkr, tbl):
    b, s, d = x.shape
    ts = TS_PRE
    const = lambda shape: pl.BlockSpec(shape, lambda bi, si: (0,) * len(shape))
    return pl.pallas_call(
        _pre_kernel,
        grid=(b, s // ts),
        in_specs=[
            pl.BlockSpec((1, ts, d), lambda bi, si: (bi, si, 0)),
            const((1, d)), const((d, IN_PROJ_EXT)), const((3, CONV_WIDTH)),
            const((1, Q_LORA)), const((Q_LORA, 1024)),
            const((1, KV_LORA)), const((KV_LORA, 1024)),
            const((1, 128)), const((1, 128)), const((1, 128)), const((1, 128)),
            pl.BlockSpec((ts, 128), lambda bi, si: (si, 0)),
        ],
        out_specs=[
            pl.BlockSpec((1, ts, CONV_WIDTH), lambda bi, si: (bi, si, 0)),
            pl.BlockSpec((1, MLA_HEADS, ts, QK_PAD), lambda bi, si: (bi, 0, si, 0)),
            pl.BlockSpec((1, MLA_HEADS, ts, QK_PAD), lambda bi, si: (bi, 0, si, 0)),
            pl.BlockSpec((1, MLA_HEADS, ts // TQ, V_HEAD, TQ), lambda bi, si: (bi, 0, si, 0, 0)),
        ],
        out_shape=[
            jax.ShapeDtypeStruct((b, s, CONV_WIDTH), BF16),
            jax.ShapeDtypeStruct((b, MLA_HEADS, s, QK_PAD), BF16),
            jax.ShapeDtypeStruct((b, MLA_HEADS, s, QK_PAD), BF16),
            jax.ShapeDtypeStruct((b, MLA_HEADS, s // TQ, V_HEAD, TQ), BF16),
        ],
        scratch_shapes=[pltpu.VMEM((ts + HALO, CONV_WIDTH), F32)],
        compiler_params=pltpu.CompilerParams(
            dimension_semantics=("arbitrary", "arbitrary"), vmem_limit_bytes=VMEM_LIMIT),
        name="pre_mixer",
    )(x, nm, win, convw, qn, wuq, kvn, wukv, gqn, gqr, gkn, gkr, tbl)


def _attn_kernel(q_ref, k_ref, vt_ref, o_ref, m_ref, l_ref, acc_ref, s_ref):
    tq = TQ
    key_c = lax.broadcasted_iota(jnp.int32, (tq, tq), 0) // CHUNK
    qry_c = lax.broadcasted_iota(jnp.int32, (tq, tq), 1) // CHUNK
    allowed = key_c <= qry_c
    for t in range(q_ref.shape[2] // tq):
        _attn_tile(pl.program_id(1) * (q_ref.shape[2] // tq) + t, slice(t * tq, (t + 1) * tq),
                   allowed, q_ref, k_ref, vt_ref, o_ref, m_ref, l_ref, acc_ref, s_ref)


def _attn_tile(i, q_rows, allowed, q_ref, k_ref, vt_ref, o_ref, m_ref, l_ref, acc_ref, s_ref):
    tq = TQ
    m_ref[...] = jnp.full(m_ref.shape, -1e30, F32)
    l_ref[...] = jnp.zeros(l_ref.shape, F32)
    acc_ref[...] = jnp.zeros(acc_ref.shape, F32)

    def scores(j, keep):
        start = pl.multiple_of(j * tq, tq)
        out = []
        for h in range(MLA_HEADS):
            kb = k_ref[0, h, pl.ds(start, tq), :]
            st = lax.dot_general(kb, q_ref[0, h, q_rows, :], (((1,), (1,)), ((), ())),
                                 preferred_element_type=F32)
            out.append(st if keep is None else jnp.where(keep, st, -1e30))
        return out

    def update(j):
        probs, alphas = [], []
        for h in range(MLA_HEADS):
            st = s_ref[h]
            m = m_ref[h]
            m_new = jnp.maximum(m, jnp.max(st, axis=0, keepdims=True))
            alpha = jnp.exp2(m - m_new)
            p = jnp.exp2(st - m_new)
            l_ref[h] = alpha * l_ref[h] + jnp.sum(p, axis=0, keepdims=True)
            m_ref[h] = m_new
            probs.append(p.astype(BF16))
            alphas.append(alpha)
        for h in range(MLA_HEADS):
            acc_ref[h] = alphas[h] * acc_ref[h] + jnp.dot(vt_ref[0, h, j], probs[h],
                                                           preferred_element_type=F32)

    def stage(sts):
        for h in range(MLA_HEADS):
            s_ref[h] = sts[h]

    stage(scores(0, allowed | (i > 0)))

    def body(j, c):
        nxt = scores(j + 1, None)
        update(j)
        stage(nxt)
        return c

    lax.fori_loop(0, i - 1, body, 0)

    @pl.when(i > 0)
    def _():
        nxt = scores(i, allowed)
        update(i - 1)
        stage(nxt)

    update(i)
    for h in range(MLA_HEADS):
        o = acc_ref[h] / l_ref[h]
        o_ref[0, q_rows, V_HEAD * h:V_HEAD * (h + 1)] = o.T.astype(BF16)


def _attn_call(q, k, vt):
    b, nh, s, _ = q.shape
    rows = TQ * Q_TILES_PER_STEP
    return pl.pallas_call(
        _attn_kernel,
        grid=(b, s // rows),
        in_specs=[
            pl.BlockSpec((1, nh, rows, QK_PAD), lambda bi, qi: (bi, 0, qi, 0)),
            pl.BlockSpec((1, nh, s, QK_PAD), lambda bi, qi: (bi, 0, 0, 0)),
            pl.BlockSpec((1, nh, s // TQ, V_HEAD, TQ), lambda bi, qi: (bi, 0, 0, 0, 0)),
        ],
        out_specs=pl.BlockSpec((1, rows, nh * V_HEAD), lambda bi, qi: (bi, qi, 0)),
        out_shape=jax.ShapeDtypeStruct((b, s, nh * V_HEAD), BF16),
        scratch_shapes=[pltpu.VMEM((nh, 1, TQ), F32), pltpu.VMEM((nh, 1, TQ), F32),
                        pltpu.VMEM((nh, V_HEAD, TQ), F32), pltpu.VMEM((nh, TQ, TQ), F32)],
        compiler_params=pltpu.CompilerParams(
            dimension_semantics=("arbitrary", "arbitrary"), vmem_limit_bytes=VMEM_LIMIT),
        name="mla_attention",
    )(q, k, vt)


def _post_kernel(x_ref, conv_ref, attn_ref, wo_ref, g_ref, wr_ref, br_ref, tri_ref,
                 h1_ref, hb_ref, lpos_ref, gate_ref, tile_n_ref, tile_carry_ref, cnt_ref,
                 carry_ref):
    step = pl.program_id(0)

    @pl.when(step == 0)
    def _():
        carry_ref[...] = jnp.zeros_like(carry_ref)

    for t in range(x_ref.shape[0] // TM_POST):
        _post_tile(t, x_ref, conv_ref, attn_ref, wo_ref, g_ref, wr_ref, br_ref, tri_ref,
                   h1_ref, hb_ref, lpos_ref, gate_ref, tile_n_ref, tile_carry_ref, carry_ref)
    cnt_ref[...] = carry_ref[...].astype(jnp.int32)


def _post_tile(t, x_ref, conv_ref, attn_ref, wo_ref, g_ref, wr_ref, br_ref, tri_ref,
               h1_ref, hb_ref, lpos_ref, gate_ref, tile_n_ref, tile_carry_ref, carry_ref):
    tm = TM_POST
    sub = tri_ref.shape[0]
    subs = [slice(r0, r0 + sub) for r0 in range(t * tm, (t + 1) * tm, sub)]
    h1s = [x_ref[rows, :]
           + jnp.dot(conv_ref[rows, :], wo_ref[0:CONV_WIDTH, :], preferred_element_type=F32)
           + jnp.dot(attn_ref[rows, :], wo_ref[CONV_WIDTH:, :], preferred_element_type=F32)
           for rows in subs]
    lane = lax.broadcasted_iota(jnp.int32, (sub, N_EXPERTS), 1).astype(F32)
    lane4 = lax.broadcasted_iota(jnp.int32, (sub, TOP_K), 1)

    def cols(c):
        return jnp.where(lane4 == 0, c[0], jnp.where(lane4 == 1, c[1],
                         jnp.where(lane4 == 2, c[2], c[3])))

    picks, members = [], []
    for rows, h1 in zip(subs, h1s):
        h1_ref[rows, :] = h1
        hb = _rms(h1, g_ref[...]).astype(BF16)
        hb_ref[rows, :] = hb
        lg = jnp.dot(hb, wr_ref[...], preferred_element_type=F32) + br_ref[...]
        member = jnp.zeros((sub, N_EXPERTS), F32)
        vals, idxs, hots = [], [], []
        for _ in range(TOP_K):
            m = jnp.max(lg, axis=-1, keepdims=True)
            ix = jnp.min(jnp.where(lg == m, lane, float(N_EXPERTS)), axis=-1, keepdims=True)
            hot = lane == ix
            lg = jnp.where(hot, -jnp.inf, lg)
            member = member + hot.astype(F32)
            vals.append(m)
            idxs.append(ix)
            hots.append(hot)
        exps = [jnp.exp(v - vals[0]) for v in vals]
        denom = exps[0] + exps[1] + exps[2] + exps[3]
        gate_ref[rows, :] = cols([e / denom for e in exps])
        picks.append((idxs, hots))
        members.append(member)

    sub_n = [jnp.sum(member, axis=0, keepdims=True) for member in members]
    tile_n = sub_n[0]
    for n_s in sub_n[1:]:
        tile_n = tile_n + n_s
    earlier = jnp.zeros((1, N_EXPERTS), F32)
    for rows, member, n_s, (idxs, hots) in zip(subs, members, sub_n, picks):
        before = jnp.dot(tri_ref[...], member.astype(BF16), preferred_element_type=F32) + earlier
        lpos = [jnp.sum(jnp.where(hot, before, 0.0) + jnp.where(lane < ix, tile_n, 0.0),
                        axis=-1, keepdims=True) for hot, ix in zip(hots, idxs)]
        lpos_ref[rows, :] = cols(lpos).astype(jnp.int32)
        earlier = earlier + n_s
    tile_n_ref[t] = tile_n.astype(jnp.int32)
    tile_carry_ref[t] = carry_ref[...].astype(jnp.int32)
    carry_ref[...] = carry_ref[...] + tile_n


def _post_call(xf, conv, attn, wo, g, wr, br, tri):
    n, d = xf.shape
    tm = TM_POST * POST_TILES_PER_STEP
    const = lambda shape: pl.BlockSpec(shape, lambda i: (0,) * len(shape))
    row = lambda w: pl.BlockSpec((tm, w), lambda i: (i, 0))
    table = pl.BlockSpec((POST_TILES_PER_STEP, 1, N_EXPERTS), lambda i: (i, 0, 0))
    return pl.pallas_call(
        _post_kernel,
        grid=(n // tm,),
        in_specs=[row(d), row(CONV_WIDTH), row(CONV_WIDTH), const((d, d)), const((1, d)),
                  const((d, N_EXPERTS)), const((1, N_EXPERTS)), const(tri.shape)],
        out_specs=[row(d), row(d), row(TOP_K), row(TOP_K), table, table,
                   const((1, N_EXPERTS))],
        out_shape=[
            jax.ShapeDtypeStruct((n, d), F32),
            jax.ShapeDtypeStruct((n, d), BF16),
            jax.ShapeDtypeStruct((n, TOP_K), jnp.int32),
            jax.ShapeDtypeStruct((n, TOP_K), F32),
            jax.ShapeDtypeStruct((n // TM_POST, 1, N_EXPERTS), jnp.int32),
            jax.ShapeDtypeStruct((n // TM_POST, 1, N_EXPERTS), jnp.int32),
            jax.ShapeDtypeStruct((1, N_EXPERTS), jnp.int32),
        ],
        scratch_shapes=[pltpu.VMEM((1, N_EXPERTS), F32)],
        compiler_params=pltpu.CompilerParams(
            dimension_semantics=("arbitrary",), vmem_limit_bytes=VMEM_LIMIT),
        name="post_mixer_router",
    )(xf, conv, attn, wo, g, wr, br, tri)


RUN_BITS = 10


def _rows(ref, first_row, n_rows):
    return ref.at[pl.ds(pl.multiple_of(first_row * ROW_LINES, ROW_LINES), n_rows * ROW_LINES), :]


def _for_each_run_piece(n_ref, far_ref, fn, bits=RUN_BITS):
    def per_expert(e, local0):
        n_e = n_ref[0, 0, e]
        far0 = far_ref[0, 0, e]

        for b in reversed(range(bits)):
            done = (n_e >> (b + 1)) << (b + 1)

            @pl.when(((n_e >> b) & 1) == 1)
            def _():
                fn(local0 + done, far0 + done, 1 << b)
        return local0 + n_e

    lax.fori_loop(0, N_EXPERTS, per_expert, 0)


PAD_BITS = TM_EXP.bit_length() - 1
assert TM_EXP == 1 << PAD_BITS


def _scatter_kernel(n_ref, far_ref, pad_n_ref, pad_far_ref, used_ref, hb_ref, lpos_ref, xs_ref,
                    xl, zeros, sem, zsem):
    i = pl.program_id(0)
    steps = pl.num_programs(0)
    tm = TM_POST
    pairs = tm * TOP_K

    def wait_slot(s):
        pltpu.make_async_copy(xl.at[s], _rows(xs_ref, 0, pairs), sem.at[s]).wait()

    @pl.when(i == 0)
    def _():
        zeros[...] = jnp.zeros(zeros.shape, F32)

        def fill(_, far_row, size):
            return pltpu.make_async_copy(_rows(zeros, 0, size), _rows(xs_ref, far_row, size), zsem)

        _for_each_run_piece(pad_n_ref, pad_far_ref, lambda *a: fill(*a).start(), PAD_BITS)
        _for_each_run_piece(pad_n_ref, pad_far_ref, lambda *a: fill(*a).wait(), PAD_BITS)

        tile_rows = zeros.shape[0] // ROW_LINES
        n_tiles = xs_ref.shape[0] // zeros.shape[0]

        def tail(op):
            def body(t, c):
                op(fill(None, t * tile_rows, tile_rows))
                return c
            lax.fori_loop(used_ref[0], n_tiles, body, 0)

        tail(lambda cp: cp.start())
        tail(lambda cp: cp.wait())

    lane = lax.broadcasted_iota(jnp.int32, (tm, 128), 1)
    j = lax.broadcasted_iota(jnp.int32, (pairs, tm), 0).astype(jnp.int16)
    for slot in range(hb_ref.shape[0] // tm):
        rows = slice(slot * tm, (slot + 1) * tm)

        @pl.when(i >= 1)
        def _():
            wait_slot(slot)

        lp = lpos_ref[rows, :].astype(F32)
        cols = jnp.zeros((tm, 128), F32)
        for k in range(TOP_K):
            cols = jnp.where(lane == k, lp[:, k:k + 1], cols)
        rows_t = cols.T.astype(jnp.int16)
        onehot = jnp.zeros((pairs, tm), BF16)
        for k in range(TOP_K):
            onehot = jnp.where(j == rows_t[k:k + 1, :], jnp.ones((), BF16), onehot)
        _store_row_tiles(xl.at[slot],
                         jnp.dot(onehot, hb_ref[rows, :], preferred_element_type=F32))

        def send(local_row, far_row, size, slot=slot):
            pltpu.make_async_copy(_rows(xl.at[slot], local_row, size),
                                  _rows(xs_ref, far_row, size), sem.at[slot]).start()

        _for_each_run_piece(n_ref.at[pl.ds(slot, 1)], far_ref.at[pl.ds(slot, 1)], send)

    @pl.when(i == steps - 1)
    def _():
        for slot in range(hb_ref.shape[0] // tm):
            wait_slot(slot)


def _scatter_call(tile_n, tile_far, pad_n, pad_far, used_tiles, hb, lpos, sorted_rows):
    n, d = hb.shape
    tiles = 2
    tm = TM_POST * tiles
    table = pl.BlockSpec((tiles, 1, N_EXPERTS), lambda i: (i, 0, 0), memory_space=pltpu.SMEM)
    whole = pl.BlockSpec((1, 1, N_EXPERTS), lambda i: (0, 0, 0), memory_space=pltpu.SMEM)
    return pl.pallas_call(
        _scatter_kernel,
        grid=(n // tm,),
        in_specs=[table, table, whole, whole, pl.BlockSpec(memory_space=pltpu.SMEM),
                  pl.BlockSpec((tm, d), lambda i: (i, 0)),
                  pl.BlockSpec((tm, TOP_K), lambda i: (i, 0))],
        out_specs=pl.BlockSpec(memory_space=pl.ANY),
        out_shape=jax.ShapeDtypeStruct((sorted_rows * ROW_LINES, 128), F32),
        scratch_shapes=[pltpu.VMEM((tiles, TM_POST * TOP_K * ROW_LINES, 128), F32),
                        pltpu.VMEM((TM_EXP * ROW_LINES, 128), F32),
                        pltpu.SemaphoreType.DMA((2,)), pltpu.SemaphoreType.DMA],
        compiler_params=pltpu.CompilerParams(
            dimension_semantics=("arbitrary",), vmem_limit_bytes=VMEM_LIMIT),
        name="dispatch_scatter",
    )(tile_n, tile_far, pad_n, pad_far, used_tiles, hb, lpos)


def _expert_kernel(tile_ref, exp_ref, flag_ref, next_ref,
                   xs_ref, wgu_hbm, bgu_ref, wd_hbm, bd_ref, o_ref,
                   wgu_f, wd_f, wgu_bf, wd_bf, wsem):
    i = pl.program_id(0)
    tm = xs_ref.shape[0] // ROW_LINES
    flag = flag_ref[i]
    cast_rows = 128

    def weight_copies(e, s):
        return (pltpu.make_async_copy(wgu_hbm.at[e], wgu_f.at[s], wsem.at[s, 0]),
                pltpu.make_async_copy(wd_hbm.at[e], wd_f.at[s], wsem.at[s, 1]))

    @pl.when(i == 0)
    def _():
        for cp in weight_copies(exp_ref[0], 0):
            cp.start()

    @pl.when((flag & 2) != 0)
    def _():
        s = (flag >> 3) & 1
        for cp in weight_copies(exp_ref[i], s):
            cp.wait()
        nxt = next_ref[i]

        @pl.when(nxt >= 0)
        def _():
            for cp in weight_copies(nxt, 1 - s):
                cp.start()

        def cast(r, c):
            rows = pl.ds(pl.multiple_of(r * cast_rows, cast_rows), cast_rows)
            wgu_bf[rows, :] = wgu_f[s, rows, :].astype(BF16)
            wd_bf[rows, :] = wd_f[s, rows, :].astype(BF16)
            return c
        lax.fori_loop(0, D_MODEL // cast_rows, cast, 0)

    def run(rows):
        x = _load_row_tiles(xs_ref, rows).astype(BF16)
        gu = jnp.dot(x, wgu_bf[...], preferred_element_type=F32) + bgu_ref[0]
        gate = jnp.minimum(gu[:, :D_FF], SWIGLU_LIMIT)
        up = jnp.clip(gu[:, D_FF:], -SWIGLU_LIMIT, SWIGLU_LIMIT)
        act = (up + 1.0) * (gate * (1.0 / (1.0 + jnp.exp(-SWIGLU_ALPHA * gate))))
        out = jnp.dot(act.astype(BF16), wd_bf[...], preferred_element_type=F32) + bd_ref[0]
        _store_row_tiles(o_ref, out)

    @pl.when((flag & 5) == 1)
    def _():
        run(tm)

    @pl.when((flag & 5) == 5)
    def _():
        run(tm // 2)
        o_ref[tm // 2 * ROW_LINES:, :] = jnp.zeros((tm // 2 * ROW_LINES, 128), F32)


def _expert_call(meta, xs, wgu, bgu, wd, bd):
    lines, w = xs.shape
    tm = TM_EXP
    n_items = meta[0].shape[0]
    return pl.pallas_call(
        _expert_kernel,
        grid_spec=pltpu.PrefetchScalarGridSpec(
            num_scalar_prefetch=4,
            grid=(n_items,),
            in_specs=[
                pl.BlockSpec((tm * ROW_LINES, w), lambda i, t, e, f, nx: (t[i], 0)),
                pl.BlockSpec(memory_space=pl.ANY),
                pl.BlockSpec((1, 1, 2 * D_FF), lambda i, t, e, f, nx: (e[i], 0, 0)),
                pl.BlockSpec(memory_space=pl.ANY),
                pl.BlockSpec((1, 1, D_MODEL), lambda i, t, e, f, nx: (e[i], 0, 0)),
            ],
            out_specs=pl.BlockSpec((tm * ROW_LINES, w), lambda i, t, e, f, nx: (t[i], 0)),
            scratch_shapes=[pltpu.VMEM((2, D_MODEL, 2 * D_FF), F32),
                            pltpu.VMEM((2, D_FF, D_MODEL), F32),
                            pltpu.VMEM((D_MODEL, 2 * D_FF), BF16),
                            pltpu.VMEM((D_FF, D_MODEL), BF16),
                            pltpu.SemaphoreType.DMA((2, 2))],
        ),
        out_shape=jax.ShapeDtypeStruct((lines, w), F32),
        input_output_aliases={len(meta): 0},
        compiler_params=pltpu.CompilerParams(
            dimension_semantics=("arbitrary",), vmem_limit_bytes=VMEM_LIMIT),
        name="moe_experts",
    )(*meta, xs, wgu, bgu, wd, bd)


def _expert_layout(counts, n_pairs, tm):
    n_tiles = n_pairs // tm + N_EXPERTS
    per_exp = (counts + tm - 1) // tm
    tile_end = jnp.cumsum(per_exp)
    tile0 = tile_end - per_exp
    used = tile_end[-1]
    step = jnp.arange(n_tiles, dtype=jnp.int32)
    tile = jnp.minimum(step, used - 1)
    e = jnp.sum((tile_end[None, :] <= tile[:, None]).astype(jnp.int32), axis=1)
    hot = e[:, None] == jnp.arange(N_EXPERTS, dtype=jnp.int32)[None, :]
    pick = lambda tab: jnp.sum(jnp.where(hot, tab[None, :], 0), axis=1)
    e_tile0 = pick(tile0)
    rows_used = pick(counts) - (step - e_tile0) * tm
    experts = jnp.arange(N_EXPERTS, dtype=jnp.int32)
    has_tiles = per_exp > 0
    slot_of_expert = (jnp.cumsum(has_tiles.astype(jnp.int32)) - 1) & 1
    later = (experts[None, :] > experts[:, None]) & has_tiles[None, :]
    next_of_expert = jnp.min(jnp.where(later, experts[None, :], N_EXPERTS), axis=1)
    next_of_expert = jnp.where(next_of_expert < N_EXPERTS, next_of_expert, -1)
    flag = jnp.where(step < used, 1 + 2 * (step == e_tile0) + 4 * (rows_used <= tm // 2)
                     + 8 * pick(slot_of_expert), 0)
    cast = lambda a: a.astype(jnp.int32)
    group_start = tile0 * tm
    pad_n = per_exp * tm - counts
    pad_start = group_start + counts
    return ((tile, cast(e), cast(flag), cast(pick(next_of_expert))), cast(used).reshape(1),
            cast(group_start), cast(pad_n), cast(pad_start))


def _combine_kernel(n_ref, far_ref, n_next_ref, far_next_ref, h1_ref, lpos_ref, gate_ref, ys_ref,
                    o_ref, yl, sem):
    i = pl.program_id(0)
    steps = pl.num_programs(0)
    tm = TM_POST
    pairs = tm * TOP_K

    def fetch(nr, fr, s):
        def recv(local_row, far_row, size):
            pltpu.make_async_copy(_rows(ys_ref, far_row, size), _rows(yl.at[s], local_row, size),
                                  sem.at[s]).start()
        _for_each_run_piece(nr, fr, recv)

    first = lambda ref: ref.at[pl.ds(0, 1)]
    second = lambda ref: ref.at[pl.ds(1, 1)]

    @pl.when(i == 0)
    def _():
        fetch(first(n_ref), first(far_ref), 0)

    j = lax.broadcasted_iota(jnp.int32, (tm, pairs), 1).astype(jnp.int16)
    for slot in range(2):
        rows = slice(slot * tm, (slot + 1) * tm)
        if slot == 0:
            fetch(second(n_ref), second(far_ref), 1)
        else:
            @pl.when(i + 1 < steps)
            def _():
                fetch(n_next_ref, far_next_ref, 0)

        pltpu.make_async_copy(_rows(ys_ref, 0, pairs), yl.at[slot], sem.at[slot]).wait()
        lp = lpos_ref[rows, :].astype(jnp.int16)
        g = gate_ref[rows, :].astype(BF16)
        w = jnp.zeros((tm, pairs), BF16)
        for k in range(TOP_K):
            w = jnp.where(j == lp[:, k:k + 1], g[:, k:k + 1], w)
        y = _load_row_tiles(yl, pairs, lead=(slot,)).astype(BF16)
        o_ref[rows, :] = h1_ref[rows, :] + jnp.dot(w, y, preferred_element_type=F32)


def _combine_call(tile_n, tile_far, h1, lpos, gates, ys):
    n, d = h1.shape
    tm = 2 * TM_POST
    steps = n // tm
    table = lambda shape, fn: pl.BlockSpec(shape, fn, memory_space=pltpu.SMEM)
    cur = lambda i: (i, 0, 0)
    nxt = lambda i: (jnp.minimum(2 * i + 2, 2 * steps - 1), 0, 0)
    return pl.pallas_call(
        _combine_kernel,
        grid=(steps,),
        in_specs=[table((2, 1, N_EXPERTS), cur), table((2, 1, N_EXPERTS), cur),
                  table((1, 1, N_EXPERTS), nxt), table((1, 1, N_EXPERTS), nxt),
                  pl.BlockSpec((tm, d), lambda i: (i, 0)),
                  pl.BlockSpec((tm, TOP_K), lambda i: (i, 0)),
                  pl.BlockSpec((tm, TOP_K), lambda i: (i, 0)),
                  pl.BlockSpec(memory_space=pl.ANY)],
        out_specs=pl.BlockSpec((tm, d), lambda i: (i, 0)),
        out_shape=jax.ShapeDtypeStruct((n, d), F32),
        scratch_shapes=[pltpu.VMEM((2, TM_POST * TOP_K * ROW_LINES, 128), F32),
                        pltpu.SemaphoreType.DMA((2,))],
        compiler_params=pltpu.CompilerParams(
            dimension_semantics=("arbitrary",), vmem_limit_bytes=VMEM_LIMIT),
        name="moe_combine",
    )(tile_n, tile_far, tile_n, tile_far, h1, lpos, gates, ys)


def _rotate_half_perm():
    half = QK_ROPE // 2
    perm = np.concatenate([np.arange(half, QK_ROPE), np.arange(0, half)])
    sign = np.concatenate([-np.ones(half), np.ones(half)]).astype(np.float32)
    return perm, sign


def _layer(h, l, norm_mix, w_in, conv_w, q_norm, w_uq, kv_norm, w_ukv, q_head_norm,
           k_head_norm, w_o, norm_ffn, w_router, b_router, w_gate_up, b_gate_up,
           w_down, b_down):
    b, s, d = h.shape
    n = b * s
    perm, sign = _rotate_half_perm()

    kpe_cols = w_in[l][:, OFF_KPE:OFF_KPE + QK_ROPE]
    win = jnp.concatenate([w_in[l], kpe_cols[:, perm] * sign], axis=1).astype(BF16)
    wq = w_uq[l].reshape(Q_LORA, MLA_HEADS, QK_HEAD)
    wq_rope = wq[:, :, QK_NOPE:]
    wuq = jnp.concatenate(
        [wq[:, :, :QK_NOPE].reshape(Q_LORA, MLA_HEADS * QK_NOPE),
         jnp.concatenate([wq_rope, wq_rope[:, :, perm] * sign], axis=2)
         .reshape(Q_LORA, MLA_HEADS * 2 * QK_ROPE)], axis=1).astype(BF16)
    gq, gk = q_head_norm[l], k_head_norm[l]
    gqn, gkn = gq[None, :QK_NOPE], gk[None, :QK_NOPE]
    gqr = jnp.concatenate([gq[QK_NOPE:], gq[QK_NOPE:][perm]])[None, :]
    gkr = jnp.concatenate([gk[QK_NOPE:], gk[QK_NOPE:][perm]])[None, :]

    half = QK_ROPE // 2
    inv_freq = ROPE_THETA ** (-jnp.arange(half, dtype=F32) / half)
    ang = jnp.arange(s).astype(F32)[:, None] * inv_freq[None, :]
    cos, sin = jnp.cos(ang), jnp.sin(ang)
    tbl = jnp.concatenate([cos, cos, sin, sin], axis=1)

    conv, q, k, vt = _pre_call(
        h, norm_mix[l][None, :], win, conv_w[l], q_norm[l][None, :], wuq,
        kv_norm[l][None, :], w_ukv[l].astype(BF16), gqn, gqr, gkn, gkr, tbl)
    attn = _attn_call(q, k, vt)

    tri = jnp.tril(jnp.ones((SUB_POST, SUB_POST), BF16), -1)
    h1, hb, lpos, gates, tile_n, tile_carry, counts = _post_call(
        h.reshape(n, d), conv.reshape(n, CONV_WIDTH), attn.reshape(n, CONV_WIDTH),
        w_o[l].astype(BF16), norm_ffn[l][None, :], w_router[l].astype(BF16),
        b_router[l][None, :], tri)

    meta, used_tiles, group_start, pad_n, pad_start = _expert_layout(counts[0], n * TOP_K, TM_EXP)
    sorted_rows = n * TOP_K + N_EXPERTS * TM_EXP
    tile_far = (tile_carry + group_start[None, None, :]).astype(jnp.int32)
    xs = _scatter_call(tile_n, tile_far, pad_n[None, None, :], pad_start[None, None, :],
                       used_tiles, hb, lpos, sorted_rows)
    ys = _expert_call(meta, xs, w_gate_up[l], b_gate_up[l][:, None, :],
                      w_down[l], b_down[l][:, None, :])
    out = _combine_call(tile_n, tile_far, h1, lpos, gates, ys)
    return out.reshape(b, s, d)


def kernel(x, norm_mix, w_in, conv_w, q_norm, w_uq, kv_norm, w_ukv, q_head_norm, k_head_norm,
           w_o, norm_ffn, w_router, b_router, w_gate_up, b_gate_up, w_down, b_down):
    h = x
    for l in range(norm_mix.shape[0]):
        h = _layer(h, l, norm_mix, w_in, conv_w, q_norm, w_uq, kv_norm, w_ukv, q_head_norm,
                   k_head_norm, w_o, norm_ffn, w_router, b_router, w_gate_up, b_gate_up,
                   w_down, b_down)
    return h
```

```python
import math

import jax
import jax.numpy as jnp
import numpy as np
from jax import lax
from jax.experimental import pallas as pl
from jax.experimental.pallas import tpu as pltpu

F32 = jnp.float32
BF16 = jnp.bfloat16

D_MODEL = 1024
CONV_WIDTH = 512
MLA_HEADS = 4
QK_NOPE = 128
QK_ROPE = 64
QK_HEAD = QK_NOPE + QK_ROPE
QK_PAD = 256
V_HEAD = 128
Q_LORA = 384
KV_LORA = 256
ROPE_THETA = 10000.0
CHUNK = 64
N_EXPERTS = 32
TOP_K = 4
D_FF = 1024
SWIGLU_LIMIT = 7.0
SWIGLU_ALPHA = 1.702
EPS = 1e-6

OFF_B, OFF_C, OFF_X, OFF_CQ, OFF_CKV, OFF_KPE = 0, 512, 1024, 1536, 1920, 2176
IN_PROJ_EXT = 2304

VMEM_LIMIT = 56 * 1024 * 1024

TS_PRE = 1024
TQ = 256
Q_TILES_PER_STEP = 8
TM_POST = 512
SUB_POST = 256
POST_TILES_PER_STEP = 2
TB_ROWS = 256
TM_EXP = 512
HALO = 8


def _rms(x, g):
    return x * lax.rsqrt(jnp.mean(x * x, axis=-1, keepdims=True) + EPS) * g


ROW_LINES = D_MODEL // 128


def _load_row_tiles(ref, rows, lead=()):
    return jnp.concatenate(
        [ref[lead + (pl.ds(c, rows, stride=ROW_LINES), slice(None))] for c in range(ROW_LINES)],
        axis=1)


def _store_row_tiles(ref, val):
    rows = val.shape[0]
    for c in range(ROW_LINES):
        ref[pl.ds(c, rows, stride=ROW_LINES), :] = val[:, 128 * c:128 * (c + 1)]


def _pre_kernel(x_ref, nm_ref, win_ref, convw_ref, qn_ref, wuq_ref, kvn_ref, wukv_ref,
                gqn_ref, gqr_ref, gkn_ref, gkr_ref, tbl_ref,
                conv_ref, q_ref, k_ref, vt_ref, ext_ref):
    s = pl.program_id(1)
    ts = x_ref.shape[1]
    sub = TQ
    starts = tuple(range(0, ts, sub))

    @pl.when(s == 0)
    def _():
        ext_ref[0:HALO, :] = jnp.zeros((HALO, CONV_WIDTH), F32)

    @pl.when(s != 0)
    def _():
        ext_ref[0:HALO, :] = ext_ref[ts:ts + HALO, :]

    projs = [jnp.dot(_rms(x_ref[0, r0:r0 + sub, :], nm_ref[...]).astype(BF16), win_ref[...],
                     preferred_element_type=F32) for r0 in starts]

    lane = lax.broadcasted_iota(jnp.int32, (sub, 128), 1)
    first_half = lane < QK_ROPE
    scale = math.log2(math.e) / math.sqrt(QK_HEAD)

    for r0, proj in zip(starts, projs):
        rows = slice(r0, r0 + sub)
        u = proj[:, OFF_C:OFF_C + CONV_WIDTH] * proj[:, OFF_X:OFF_X + CONV_WIDTH]
        ext_ref[HALO + r0:HALO + r0 + sub, :] = u
        y = (convw_ref[2:3, :] * u
             + convw_ref[1:2, :] * ext_ref[HALO + r0 - 1:HALO + r0 - 1 + sub, :]
             + convw_ref[0:1, :] * ext_ref[HALO + r0 - 2:HALO + r0 - 2 + sub, :])
        conv_ref[0, rows, :] = (proj[:, OFF_B:OFF_B + CONV_WIDTH] * y).astype(BF16)

        hq = _rms(proj[:, OFF_CQ:OFF_CQ + Q_LORA], qn_ref[...]).astype(BF16)
        qp = jnp.dot(hq, wuq_ref[...], preferred_element_type=F32)
        hkv = _rms(proj[:, OFF_CKV:OFF_CKV + KV_LORA], kvn_ref[...]).astype(BF16)
        kvp = jnp.dot(hkv, wukv_ref[...], preferred_element_type=F32)

        tbl = tbl_ref[rows, :]

        def rope_block(grp, gain):
            r = grp * tbl * gain
            r = r + pltpu.roll(r, QK_ROPE, axis=1)
            return jnp.where(first_half, r, 0.0)

        kpe = proj[:, OFF_KPE:OFF_KPE + 128]
        ss_kpe = jnp.sum(jnp.where(first_half, kpe * kpe, 0.0), axis=-1, keepdims=True)
        k_rope = rope_block(kpe, gkr_ref[...])
        for h in range(MLA_HEADS):
            qn = qp[:, 128 * h:128 * (h + 1)]
            qg = qp[:, 512 + 128 * h:512 + 128 * (h + 1)]
            ss = (jnp.sum(qn * qn, axis=-1, keepdims=True)
                  + jnp.sum(jnp.where(first_half, qg * qg, 0.0), axis=-1, keepdims=True))
            rs = lax.rsqrt(ss * (1.0 / QK_HEAD) + EPS) * scale
            q_ref[0, h, rows, 0:128] = (qn * rs * gqn_ref[...]).astype(BF16)
            q_ref[0, h, rows, 128:256] = (rope_block(qg, gqr_ref[...]) * rs).astype(BF16)

            kn = kvp[:, 256 * h:256 * h + 128]
            ssk = jnp.sum(kn * kn, axis=-1, keepdims=True) + ss_kpe
            rsk = lax.rsqrt(ssk * (1.0 / QK_HEAD) + EPS)
            k_ref[0, h, rows, 0:128] = (kn * rsk * gkn_ref[...]).astype(BF16)
            k_ref[0, h, rows, 128:256] = (k_rope * rsk).astype(BF16)
            v = kvp[:, 256 * h + 128:256 * (h + 1)]
            vt_ref[0, h, r0 // TQ] = v.T.astype(BF16)


def _pre_call(x, nm, win, convw, qn, wuq, kvn, wukv, gqn, gqr, gkn, gkr, tbl):
    b, s, d = x.shape
    ts = TS_PRE
    const = lambda shape: pl.BlockSpec(shape, lambda bi, si: (0,) * len(shape))
    return pl.pallas_call(
        _pre_kernel,
        grid=(b, s // ts),
        in_specs=[
            pl.BlockSpec((1, ts, d), lambda bi, si: (bi, si, 0)),
            const((1, d)), const((d, IN_PROJ_EXT)), const((3, CONV_WIDTH)),
            const((1, Q_LORA)), const((Q_LORA, 1024)),
            const((1, KV_LORA)), const((KV_LORA, 1024)),
            const((1, 128)), const((1, 128)), const((1, 128)), const((1, 128)),
            pl.BlockSpec((ts, 128), lambda bi, si: (si, 0)),
        ],
        out_specs=[
            pl.BlockSpec((1, ts, CONV_WIDTH), lambda bi, si: (bi, si, 0)),
            pl.BlockSpec((1, MLA_HEADS, ts, QK_PAD), lambda bi, si: (bi, 0, si, 0)),
            pl.BlockSpec((1, MLA_HEADS, ts, QK_PAD), lambda bi, si: (bi, 0, si, 0)),
            pl.BlockSpec((1, MLA_HEADS, ts // TQ, V_HEAD, TQ), lambda bi, si: (bi, 0, si, 0, 0)),
        ],
        out_shape=[
            jax.ShapeDtypeStruct((b, s, CONV_WIDTH), BF16),
            jax.ShapeDtypeStruct((b, MLA_HEADS, s, QK_PAD), BF16),
            jax.ShapeDtypeStruct((b, MLA_HEADS, s, QK_PAD), BF16),
            jax.ShapeDtypeStruct((b, MLA_HEADS, s // TQ, V_HEAD, TQ), BF16),
        ],
        scratch_shapes=[pltpu.VMEM((ts + HALO, CONV_WIDTH), F32)],
        compiler_params=pltpu.CompilerParams(
            dimension_semantics=("arbitrary", "arbitrary"), vmem_limit_bytes=VMEM_LIMIT),
        name="pre_mixer",
    )(x, nm, win, convw, qn, wuq, kvn, wukv, gqn, gqr, gkn, gkr, tbl)


def _attn_kernel(q_ref, k_ref, vt_ref, o_ref, m_ref, l_ref, acc_ref, s_ref):
    tq = TQ
    key_c = lax.broadcasted_iota(jnp.int32, (tq, tq), 0) // CHUNK
    qry_c = lax.broadcasted_iota(jnp.int32, (tq, tq), 1) // CHUNK
    allowed = key_c <= qry_c
    for t in range(q_ref.shape[2] // tq):
        _attn_tile(pl.program_id(1) * (q_ref.shape[2] // tq) + t, slice(t * tq, (t + 1) * tq),
                   allowed, q_ref, k_ref, vt_ref, o_ref, m_ref, l_ref, acc_ref, s_ref)


def _attn_tile(i, q_rows, allowed, q_ref, k_ref, vt_ref, o_ref, m_ref, l_ref, acc_ref, s_ref):
    tq = TQ
    m_ref[...] = jnp.full(m_ref.shape, -1e30, F32)
    l_ref[...] = jnp.zeros(l_ref.shape, F32)
    acc_ref[...] = jnp.zeros(acc_ref.shape, F32)

    def scores(j, keep):
        start = pl.multiple_of(j * tq, tq)
        out = []
        for h in range(MLA_HEADS):
            kb = k_ref[0, h, pl.ds(start, tq), :]
            st = lax.dot_general(kb, q_ref[0, h, q_rows, :], (((1,), (1,)), ((), ())),
                                 preferred_element_type=F32)
            out.append(st if keep is None else jnp.where(keep, st, -1e30))
        return out

    def update(j):
        probs, alphas = [], []
        for h in range(MLA_HEADS):
            st = s_ref[h]
            m = m_ref[h]
            m_new = jnp.maximum(m, jnp.max(st, axis=0, keepdims=True))
            alpha = jnp.exp2(m - m_new)
            p = jnp.exp2(st - m_new)
            l_ref[h] = alpha * l_ref[h] + jnp.sum(p, axis=0, keepdims=True)
            m_ref[h] = m_new
            probs.append(p.astype(BF16))
            alphas.append(alpha)
        for h in range(MLA_HEADS):
            acc_ref[h] = alphas[h] * acc_ref[h] + jnp.dot(vt_ref[0, h, j], probs[h],
                                                           preferred_element_type=F32)

    def stage(sts):
        for h in range(MLA_HEADS):
            s_ref[h] = sts[h]

    stage(scores(0, allowed | (i > 0)))

    def body(j, c):
        nxt = scores(j + 1, None)
        update(j)
        stage(nxt)
        return c

    lax.fori_loop(0, i - 1, body, 0)

    @pl.when(i > 0)
    def _():
        nxt = scores(i, allowed)
        update(i - 1)
        stage(nxt)

    update(i)
    for h in range(MLA_HEADS):
        o = acc_ref[h] / l_ref[h]
        o_ref[0, q_rows, V_HEAD * h:V_HEAD * (h + 1)] = o.T.astype(BF16)


def _attn_call(q, k, vt):
    b, nh, s, _ = q.shape
    rows = TQ * Q_TILES_PER_STEP
    return pl.pallas_call(
        _attn_kernel,
        grid=(b, s // rows),
        in_specs=[
            pl.BlockSpec((1, nh, rows, QK_PAD), lambda bi, qi: (bi, 0, qi, 0)),
            pl.BlockSpec((1, nh, s, QK_PAD), lambda bi, qi: (bi, 0, 0, 0)),
            pl.BlockSpec((1, nh, s // TQ, V_HEAD, TQ), lambda bi, qi: (bi, 0, 0, 0, 0)),
        ],
        out_specs=pl.BlockSpec((1, rows, nh * V_HEAD), lambda bi, qi: (bi, qi, 0)),
        out_shape=jax.ShapeDtypeStruct((b, s, nh * V_HEAD), BF16),
        scratch_shapes=[pltpu.VMEM((nh, 1, TQ), F32), pltpu.VMEM((nh, 1, TQ), F32),
                        pltpu.VMEM((nh, V_HEAD, TQ), F32), pltpu.VMEM((nh, TQ, TQ), F32)],
        compiler_params=pltpu.CompilerParams(
            dimension_semantics=("arbitrary", "arbitrary"), vmem_limit_bytes=VMEM_LIMIT),
        name="mla_attention",
    )(q, k, vt)


def _post_kernel(x_ref, conv_ref, attn_ref, wo_ref, g_ref, wr_ref, br_ref, tri_ref,
                 h1_ref, hb_ref, lpos_ref, gate_ref, tile_n_ref, tile_carry_ref, cnt_ref,
                 carry_ref):
    step = pl.program_id(0)

    @pl.when(step == 0)
    def _():
        carry_ref[...] = jnp.zeros_like(carry_ref)

    for t in range(x_ref.shape[0] // TM_POST):
        _post_tile(t, x_ref, conv_ref, attn_ref, wo_ref, g_ref, wr_ref, br_ref, tri_ref,
                   h1_ref, hb_ref, lpos_ref, gate_ref, tile_n_ref, tile_carry_ref, carry_ref)
    cnt_ref[...] = carry_ref[...].astype(jnp.int32)


def _post_tile(t, x_ref, conv_ref, attn_ref, wo_ref, g_ref, wr_ref, br_ref, tri_ref,
               h1_ref, hb_ref, lpos_ref, gate_ref, tile_n_ref, tile_carry_ref, carry_ref):
    tm = TM_POST
    sub = tri_ref.shape[0]
    subs = [slice(r0, r0 + sub) for r0 in range(t * tm, (t + 1) * tm, sub)]
    h1s = [x_ref[rows, :]
           + jnp.dot(conv_ref[rows, :], wo_ref[0:CONV_WIDTH, :], preferred_element_type=F32)
           + jnp.dot(attn_ref[rows, :], wo_ref[CONV_WIDTH:, :], preferred_element_type=F32)
           for rows in subs]
    lane = lax.broadcasted_iota(jnp.int32, (sub, N_EXPERTS), 1).astype(F32)
    lane4 = lax.broadcasted_iota(jnp.int32, (sub, TOP_K), 1)

    def cols(c):
        return jnp.where(lane4 == 0, c[0], jnp.where(lane4 == 1, c[1],
                         jnp.where(lane4 == 2, c[2], c[3])))

    picks, members = [], []
    for rows, h1 in zip(subs, h1s):
        h1_ref[rows, :] = h1
        hb = _rms(h1, g_ref[...]).astype(BF16)
        hb_ref[rows, :] = hb
        lg = jnp.dot(hb, wr_ref[...], preferred_element_type=F32) + br_ref[...]
        member = jnp.zeros((sub, N_EXPERTS), F32)
        vals, idxs, hots = [], [], []
        for _ in range(TOP_K):
            m = jnp.max(lg, axis=-1, keepdims=True)
            ix = jnp.min(jnp.where(lg == m, lane, float(N_EXPERTS)), axis=-1, keepdims=True)
            hot = lane == ix
            lg = jnp.where(hot, -jnp.inf, lg)
            member = member + hot.astype(F32)
            vals.append(m)
            idxs.append(ix)
            hots.append(hot)
        exps = [jnp.exp(v - vals[0]) for v in vals]
        denom = exps[0] + exps[1] + exps[2] + exps[3]
        gate_ref[rows, :] = cols([e / denom for e in exps])
        picks.append((idxs, hots))
        members.append(member)

    sub_n = [jnp.sum(member, axis=0, keepdims=True) for member in members]
    tile_n = sub_n[0]
    for n_s in sub_n[1:]:
        tile_n = tile_n + n_s
    earlier = jnp.zeros((1, N_EXPERTS), F32)
    for rows, member, n_s, (idxs, hots) in zip(subs, members, sub_n, picks):
        before = jnp.dot(tri_ref[...], member.astype(BF16), preferred_element_type=F32) + earlier
        lpos = [jnp.sum(jnp.where(hot, before, 0.0) + jnp.where(lane < ix, tile_n, 0.0),
                        axis=-1, keepdims=True) for hot, ix in zip(hots, idxs)]
        lpos_ref[rows, :] = cols(lpos).astype(jnp.int32)
        earlier = earlier + n_s
    tile_n_ref[t] = tile_n.astype(jnp.int32)
    tile_carry_ref[t] = carry_ref[...].astype(jnp.int32)
    carry_ref[...] = carry_ref[...] + tile_n


def _post_call(xf, conv, attn, wo, g, wr, br, tri):
    n, d = xf.shape
    tm = TM_POST * POST_TILES_PER_STEP
    const = lambda shape: pl.BlockSpec(shape, lambda i: (0,) * len(shape))
    row = lambda w: pl.BlockSpec((tm, w), lambda i: (i, 0))
    table = pl.BlockSpec((POST_TILES_PER_STEP, 1, N_EXPERTS), lambda i: (i, 0, 0))
    return pl.pallas_call(
        _post_kernel,
        grid=(n // tm,),
        in_specs=[row(d), row(CONV_WIDTH), row(CONV_WIDTH), const((d, d)), const((1, d)),
                  const((d, N_EXPERTS)), const((1, N_EXPERTS)), const(tri.shape)],
        out_specs=[row(d), row(d), row(TOP_K), row(TOP_K), table, table,
                   const((1, N_EXPERTS))],
        out_shape=[
            jax.ShapeDtypeStruct((n, d), F32),
            jax.ShapeDtypeStruct((n, d), BF16),
            jax.ShapeDtypeStruct((n, TOP_K), jnp.int32),
            jax.ShapeDtypeStruct((n, TOP_K), F32),
            jax.ShapeDtypeStruct((n // TM_POST, 1, N_EXPERTS), jnp.int32),
            jax.ShapeDtypeStruct((n // TM_POST, 1, N_EXPERTS), jnp.int32),
            jax.ShapeDtypeStruct((1, N_EXPERTS), jnp.int32),
        ],
        scratch_shapes=[pltpu.VMEM((1, N_EXPERTS), F32)],
        compiler_params=pltpu.CompilerParams(
            dimension_semantics=("arbitrary",), vmem_limit_bytes=VMEM_LIMIT),
        name="post_mixer_router",
    )(xf, conv, attn, wo, g, wr, br, tri)


RUN_BITS = 10


def _rows(ref, first_row, n_rows):
    return ref.at[pl.ds(pl.multiple_of(first_row * ROW_LINES, ROW_LINES), n_rows * ROW_LINES), :]


def _for_each_run_piece(n_ref, far_ref, fn, bits=RUN_BITS):
    def per_expert(e, local0):
        n_e = n_ref[0, 0, e]
        far0 = far_ref[0, 0, e]

        for b in reversed(range(bits)):
            done = (n_e >> (b + 1)) << (b + 1)

            @pl.when(((n_e >> b) & 1) == 1)
            def _():
                fn(local0 + done, far0 + done, 1 << b)
        return local0 + n_e

    lax.fori_loop(0, N_EXPERTS, per_expert, 0)


PAD_BITS = TM_EXP.bit_length() - 1
assert TM_EXP == 1 << PAD_BITS


def _scatter_kernel(n_ref, far_ref, pad_n_ref, pad_far_ref, used_ref, hb_ref, lpos_ref, xs_ref,
                    xl, zeros, sem, zsem):
    i = pl.program_id(0)
    steps = pl.num_programs(0)
    tm = hb_ref.shape[0]
    pairs = tm * TOP_K
    slot = i % 2

    def wait_slot(s):
        pltpu.make_async_copy(xl.at[s], _rows(xs_ref, 0, pairs), sem.at[s]).wait()

    @pl.when(i == 0)
    def _():
        zeros[...] = jnp.zeros(zeros.shape, F32)

        def fill(_, far_row, size):
            return pltpu.make_async_copy(_rows(zeros, 0, size), _rows(xs_ref, far_row, size), zsem)

        _for_each_run_piece(pad_n_ref, pad_far_ref, lambda *a: fill(*a).start(), PAD_BITS)
        _for_each_run_piece(pad_n_ref, pad_far_ref, lambda *a: fill(*a).wait(), PAD_BITS)

        tile_rows = zeros.shape[0] // ROW_LINES
        n_tiles = xs_ref.shape[0] // zeros.shape[0]

        def tail(op):
            def body(t, c):
                op(fill(None, t * tile_rows, tile_rows))
                return c
            lax.fori_loop(used_ref[0], n_tiles, body, 0)

        tail(lambda cp: cp.start())
        tail(lambda cp: cp.wait())

    @pl.when(i >= 2)
    def _():
        wait_slot(slot)

    lane = lax.broadcasted_iota(jnp.int32, (tm, 128), 1)
    lp = lpos_ref[...].astype(F32)
    cols = jnp.zeros((tm, 128), F32)
    for k in range(TOP_K):
        cols = jnp.where(lane == k, lp[:, k:k + 1], cols)
    rows_t = cols.T.astype(jnp.int16)
    j = lax.broadcasted_iota(jnp.int32, (pairs, tm), 0).astype(jnp.int16)
    onehot = jnp.zeros((pairs, tm), BF16)
    for k in range(TOP_K):
        onehot = jnp.where(j == rows_t[k:k + 1, :], jnp.ones((), BF16), onehot)
    _store_row_tiles(xl.at[slot], jnp.dot(onehot, hb_ref[...], preferred_element_type=F32))

    def send(local_row, far_row, size):
        pltpu.make_async_copy(_rows(xl.at[slot], local_row, size), _rows(xs_ref, far_row, size),
                              sem.at[slot]).start()

    _for_each_run_piece(n_ref, far_ref, send)

    @pl.when(i == steps - 1)
    def _():
        @pl.when(steps >= 2)
        def _():
            wait_slot(1 - slot)
        wait_slot(slot)


def _scatter_call(tile_n, tile_far, pad_n, pad_far, used_tiles, hb, lpos, sorted_rows):
    n, d = hb.shape
    tm = TM_POST
    table = pl.BlockSpec((1, 1, N_EXPERTS), lambda i: (i, 0, 0), memory_space=pltpu.SMEM)
    whole = pl.BlockSpec((1, 1, N_EXPERTS), lambda i: (0, 0, 0), memory_space=pltpu.SMEM)
    return pl.pallas_call(
        _scatter_kernel,
        grid=(n // tm,),
        in_specs=[table, table, whole, whole, pl.BlockSpec(memory_space=pltpu.SMEM),
                  pl.BlockSpec((tm, d), lambda i: (i, 0)),
                  pl.BlockSpec((tm, TOP_K), lambda i: (i, 0))],
        out_specs=pl.BlockSpec(memory_space=pl.ANY),
        out_shape=jax.ShapeDtypeStruct((sorted_rows * ROW_LINES, 128), F32),
        scratch_shapes=[pltpu.VMEM((2, tm * TOP_K * ROW_LINES, 128), F32),
                        pltpu.VMEM((TM_EXP * ROW_LINES, 128), F32),
                        pltpu.SemaphoreType.DMA((2,)), pltpu.SemaphoreType.DMA],
        compiler_params=pltpu.CompilerParams(
            dimension_semantics=("arbitrary",), vmem_limit_bytes=VMEM_LIMIT),
        name="dispatch_scatter",
    )(tile_n, tile_far, pad_n, pad_far, used_tiles, hb, lpos)


def _expert_kernel(tile_ref, exp_ref, flag_ref, next_ref,
                   xs_ref, wgu_hbm, bgu_ref, wd_hbm, bd_ref, o_ref,
                   wgu_f, wd_f, wgu_bf, wd_bf, wsem):
    i = pl.program_id(0)
    tm = xs_ref.shape[0] // ROW_LINES
    flag = flag_ref[i]
    cast_rows = 128

    def weight_copies(e, s):
        return (pltpu.make_async_copy(wgu_hbm.at[e], wgu_f.at[s], wsem.at[s, 0]),
                pltpu.make_async_copy(wd_hbm.at[e], wd_f.at[s], wsem.at[s, 1]))

    @pl.when(i == 0)
    def _():
        for cp in weight_copies(exp_ref[0], 0):
            cp.start()

    @pl.when((flag & 2) != 0)
    def _():
        s = (flag >> 3) & 1
        for cp in weight_copies(exp_ref[i], s):
            cp.wait()
        nxt = next_ref[i]

        @pl.when(nxt >= 0)
        def _():
            for cp in weight_copies(nxt, 1 - s):
                cp.start()

        def cast(r, c):
            rows = pl.ds(pl.multiple_of(r * cast_rows, cast_rows), cast_rows)
            wgu_bf[rows, :] = wgu_f[s, rows, :].astype(BF16)
            wd_bf[rows, :] = wd_f[s, rows, :].astype(BF16)
            return c
        lax.fori_loop(0, D_MODEL // cast_rows, cast, 0)

    def run(rows):
        x = _load_row_tiles(xs_ref, rows).astype(BF16)
        gu = jnp.dot(x, wgu_bf[...], preferred_element_type=F32) + bgu_ref[0]
        gate = jnp.minimum(gu[:, :D_FF], SWIGLU_LIMIT)
        up = jnp.clip(gu[:, D_FF:], -SWIGLU_LIMIT, SWIGLU_LIMIT)
        act = (up + 1.0) * (gate * (1.0 / (1.0 + jnp.exp(-SWIGLU_ALPHA * gate))))
        out = jnp.dot(act.astype(BF16), wd_bf[...], preferred_element_type=F32) + bd_ref[0]
        _store_row_tiles(o_ref, out)

    @pl.when((flag & 5) == 1)
    def _():
        run(tm)

    @pl.when((flag & 5) == 5)
    def _():
        run(tm // 2)
        o_ref[tm // 2 * ROW_LINES:, :] = jnp.zeros((tm // 2 * ROW_LINES, 128), F32)


def _expert_call(meta, xs, wgu, bgu, wd, bd):
    lines, w = xs.shape
    tm = TM_EXP
    n_items = meta[0].shape[0]
    return pl.pallas_call(
        _expert_kernel,
        grid_spec=pltpu.PrefetchScalarGridSpec(
            num_scalar_prefetch=4,
            grid=(n_items,),
            in_specs=[
                pl.BlockSpec((tm * ROW_LINES, w), lambda i, t, e, f, nx: (t[i], 0)),
                pl.BlockSpec(memory_space=pl.ANY),
                pl.BlockSpec((1, 1, 2 * D_FF), lambda i, t, e, f, nx: (e[i], 0, 0)),
                pl.BlockSpec(memory_space=pl.ANY),
                pl.BlockSpec((1, 1, D_MODEL), lambda i, t, e, f, nx: (e[i], 0, 0)),
            ],
            out_specs=pl.BlockSpec((tm * ROW_LINES, w), lambda i, t, e, f, nx: (t[i], 0)),
            scratch_shapes=[pltpu.VMEM((2, D_MODEL, 2 * D_FF), F32),
                            pltpu.VMEM((2, D_FF, D_MODEL), F32),
                            pltpu.VMEM((D_MODEL, 2 * D_FF), BF16),
                            pltpu.VMEM((D_FF, D_MODEL), BF16),
                            pltpu.SemaphoreType.DMA((2, 2))],
        ),
        out_shape=jax.ShapeDtypeStruct((lines, w), F32),
        input_output_aliases={len(meta): 0},
        compiler_params=pltpu.CompilerParams(
            dimension_semantics=("arbitrary",), vmem_limit_bytes=VMEM_LIMIT),
        name="moe_experts",
    )(*meta, xs, wgu, bgu, wd, bd)


def _expert_layout(counts, n_pairs, tm):
    n_tiles = n_pairs // tm + N_EXPERTS
    per_exp = (counts + tm - 1) // tm
    tile_end = jnp.cumsum(per_exp)
    tile0 = tile_end - per_exp
    used = tile_end[-1]
    step = jnp.arange(n_tiles, dtype=jnp.int32)
    tile = jnp.minimum(step, used - 1)
    e = jnp.sum((tile_end[None, :] <= tile[:, None]).astype(jnp.int32), axis=1)
    hot = e[:, None] == jnp.arange(N_EXPERTS, dtype=jnp.int32)[None, :]
    pick = lambda tab: jnp.sum(jnp.where(hot, tab[None, :], 0), axis=1)
    e_tile0 = pick(tile0)
    rows_used = pick(counts) - (step - e_tile0) * tm
    experts = jnp.arange(N_EXPERTS, dtype=jnp.int32)
    has_tiles = per_exp > 0
    slot_of_expert = (jnp.cumsum(has_tiles.astype(jnp.int32)) - 1) & 1
    later = (experts[None, :] > experts[:, None]) & has_tiles[None, :]
    next_of_expert = jnp.min(jnp.where(later, experts[None, :], N_EXPERTS), axis=1)
    next_of_expert = jnp.where(next_of_expert < N_EXPERTS, next_of_expert, -1)
    flag = jnp.where(step < used, 1 + 2 * (step == e_tile0) + 4 * (rows_used <= tm // 2)
                     + 8 * pick(slot_of_expert), 0)
    cast = lambda a: a.astype(jnp.int32)
    group_start = tile0 * tm
    pad_n = per_exp * tm - counts
    pad_start = group_start + counts
    return ((tile, cast(e), cast(flag), cast(pick(next_of_expert))), cast(used).reshape(1),
            cast(group_start), cast(pad_n), cast(pad_start))


def _combine_kernel(n_ref, far_ref, n_next_ref, far_next_ref, h1_ref, lpos_ref, gate_ref, ys_ref,
                    o_ref, yl, sem):
    i = pl.program_id(0)
    steps = pl.num_programs(0)
    tm = h1_ref.shape[0]
    pairs = tm * TOP_K
    slot = i % 2

    def fetch(nr, fr, s):
        def recv(local_row, far_row, size):
            pltpu.make_async_copy(_rows(ys_ref, far_row, size), _rows(yl.at[s], local_row, size),
                                  sem.at[s]).start()
        _for_each_run_piece(nr, fr, recv)

    @pl.when(i == 0)
    def _():
        fetch(n_ref, far_ref, 0)

    @pl.when(i + 1 < steps)
    def _():
        fetch(n_next_ref, far_next_ref, 1 - slot)

    pltpu.make_async_copy(_rows(ys_ref, 0, pairs), yl.at[slot], sem.at[slot]).wait()

    j = lax.broadcasted_iota(jnp.int32, (tm, pairs), 1).astype(jnp.int16)
    lp = lpos_ref[...].astype(jnp.int16)
    g = gate_ref[...].astype(BF16)
    w = jnp.zeros((tm, pairs), BF16)
    for k in range(TOP_K):
        w = jnp.where(j == lp[:, k:k + 1], g[:, k:k + 1], w)
    y = _load_row_tiles(yl, pairs, lead=(slot,)).astype(BF16)
    o_ref[...] = h1_ref[...] + jnp.dot(w, y, preferred_element_type=F32)


def _combine_call(tile_n, tile_far, h1, lpos, gates, ys):
    n, d = h1.shape
    tm = TM_POST
    steps = n // tm
    table = lambda fn: pl.BlockSpec((1, 1, N_EXPERTS), fn, memory_space=pltpu.SMEM)
    cur = lambda i: (i, 0, 0)
    nxt = lambda i: (jnp.minimum(i + 1, steps - 1), 0, 0)
    return pl.pallas_call(
        _combine_kernel,
        grid=(steps,),
        in_specs=[table(cur), table(cur), table(nxt), table(nxt),
                  pl.BlockSpec((tm, d), lambda i: (i, 0)),
                  pl.BlockSpec((tm, TOP_K), lambda i: (i, 0)),
                  pl.BlockSpec((tm, TOP_K), lambda i: (i, 0)),
                  pl.BlockSpec(memory_space=pl.ANY)],
        out_specs=pl.BlockSpec((tm, d), lambda i: (i, 0)),
        out_shape=jax.ShapeDtypeStruct((n, d), F32),
        scratch_shapes=[pltpu.VMEM((2, tm * TOP_K * ROW_LINES, 128), F32),
                        pltpu.SemaphoreType.DMA((2,))],
        compiler_params=pltpu.CompilerParams(
            dimension_semantics=("arbitrary",), vmem_limit_bytes=VMEM_LIMIT),
        name="moe_combine",
    )(tile_n, tile_far, tile_n, tile_far, h1, lpos, gates, ys)


def _rotate_half_perm():
    half = QK_ROPE // 2
    perm = np.concatenate([np.arange(half, QK_ROPE), np.arange(0, half)])
    sign = np.concatenate([-np.ones(half), np.ones(half)]).astype(np.float32)
    return perm, sign


def _layer(h, l, norm_mix, w_in, conv_w, q_norm, w_uq, kv_norm, w_ukv, q_head_norm,
           k_head_norm, w_o, norm_ffn, w_router, b_router, w_gate_up, b_gate_up,
           w_down, b_down):
    b, s, d = h.shape
    n = b * s
    perm, sign = _rotate_half_perm()

    kpe_cols = w_in[l][:, OFF_KPE:OFF_KPE + QK_ROPE]
    win = jnp.concatenate([w_in[l], kpe_cols[:, perm] * sign], axis=1).astype(BF16)
    wq = w_uq[l].reshape(Q_LORA, MLA_HEADS, QK_HEAD)
    wq_rope = wq[:, :, QK_NOPE:]
    wuq = jnp.concatenate(
        [wq[:, :, :QK_NOPE].reshape(Q_LORA, MLA_HEADS * QK_NOPE),
         jnp.concatenate([wq_rope, wq_rope[:, :, perm] * sign], axis=2)
         .reshape(Q_LORA, MLA_HEADS * 2 * QK_ROPE)], axis=1).astype(BF16)
    gq, gk = q_head_norm[l], k_head_norm[l]
    gqn, gkn = gq[None, :QK_NOPE], gk[None, :QK_NOPE]
    gqr = jnp.concatenate([gq[QK_NOPE:], gq[QK_NOPE:][perm]])[None, :]
    gkr = jnp.concatenate([gk[QK_NOPE:], gk[QK_NOPE:][perm]])[None, :]

    half = QK_ROPE // 2
    inv_freq = ROPE_THETA ** (-jnp.arange(half, dtype=F32) / half)
    ang = jnp.arange(s).astype(F32)[:, None] * inv_freq[None, :]
    cos, sin = jnp.cos(ang), jnp.sin(ang)
    tbl = jnp.concatenate([cos, cos, sin, sin], axis=1)

    conv, q, k, vt = _pre_call(
        h, norm_mix[l][None, :], win, conv_w[l], q_norm[l][None, :], wuq,
        kv_norm[l][None, :], w_ukv[l].astype(BF16), gqn, gqr, gkn, gkr, tbl)
    attn = _attn_call(q, k, vt)

    tri = jnp.tril(jnp.ones((SUB_POST, SUB_POST), BF16), -1)
    h1, hb, lpos, gates, tile_n, tile_carry, counts = _post_call(
        h.reshape(n, d), conv.reshape(n, CONV_WIDTH), attn.reshape(n, CONV_WIDTH),
        w_o[l].astype(BF16), norm_ffn[l][None, :], w_router[l].astype(BF16),
        b_router[l][None, :], tri)

    meta, used_tiles, group_start, pad_n, pad_start = _expert_layout(counts[0], n * TOP_K, TM_EXP)
    sorted_rows = n * TOP_K + N_EXPERTS * TM_EXP
    tile_far = (tile_carry + group_start[None, None, :]).astype(jnp.int32)
    xs = _scatter_call(tile_n, tile_far, pad_n[None, None, :], pad_start[None, None, :],
                       used_tiles, hb, lpos, sorted_rows)
    ys = _expert_call(meta, xs, w_gate_up[l], b_gate_up[l][:, None, :],
                      w_down[l], b_down[l][:, None, :])
    out = _combine_call(tile_n, tile_far, h1, lpos, gates, ys)
    return out.reshape(b, s, d)


def kernel(x, norm_mix, w_in, conv_w, q_norm, w_uq, kv_norm, w_ukv, q_head_norm, k_head_norm,
           w_o, norm_ffn, w_router, b_router, w_gate_up, b_gate_up, w_down, b_down):
    h = x
    for l in range(norm_mix.shape[0]):
        h = _layer(h, l, norm_mix, w_in, conv_w, q_norm, w_uq, kv_norm, w_ukv, q_head_norm,
                   k_head_norm, w_o, norm_ffn, w_router, b_router, w_gate_up, b_gate_up,
                   w_down, b_down)
    return h
```

```python
import math

import jax
import jax.numpy as jnp
import numpy as np
from jax import lax
from jax.experimental import pallas as pl
from jax.experimental.pallas import tpu as pltpu

F32 = jnp.float32
BF16 = jnp.bfloat16

D_MODEL = 1024
CONV_WIDTH = 512
MLA_HEADS = 4
QK_NOPE = 128
QK_ROPE = 64
QK_HEAD = QK_NOPE + QK_ROPE
QK_PAD = 256
V_HEAD = 128
Q_LORA = 384
KV_LORA = 256
ROPE_THETA = 10000.0
CHUNK = 64
N_EXPERTS = 32
TOP_K = 4
D_FF = 1024
SWIGLU_LIMIT = 7.0
SWIGLU_ALPHA = 1.702
EPS = 1e-6

OFF_B, OFF_C, OFF_X, OFF_CQ, OFF_CKV, OFF_KPE = 0, 512, 1024, 1536, 1920, 2176
IN_PROJ_EXT = 2304

VMEM_LIMIT = 56 * 1024 * 1024

TS_PRE = 1024
TQ = 256
Q_TILES_PER_STEP = 8
TM_POST = 512
SUB_POST = 256
POST_TILES_PER_STEP = 2
TB_ROWS = 256
TM_EXP = 512
EXP_TILES_PER_STEP = 2
HALO = 8


def _rms(x, g):
    return x * lax.rsqrt(jnp.mean(x * x, axis=-1, keepdims=True) + EPS) * g


ROW_LINES = D_MODEL // 128


def _load_row_tiles(ref, rows, lead=()):
    return jnp.concatenate(
        [ref[lead + (pl.ds(c, rows, stride=ROW_LINES), slice(None))] for c in range(ROW_LINES)],
        axis=1)


def _store_row_tiles(ref, val):
    rows = val.shape[0]
    for c in range(ROW_LINES):
        ref[pl.ds(c, rows, stride=ROW_LINES), :] = val[:, 128 * c:128 * (c + 1)]


def _pre_kernel(x_ref, nm_ref, win_ref, convw_ref, qn_ref, wuq_ref, kvn_ref, wukv_ref,
                gqn_ref, gqr_ref, gkn_ref, gkr_ref, tbl_ref,
                conv_ref, q_ref, k_ref, vt_ref, ext_ref):
    s = pl.program_id(1)
    ts = x_ref.shape[1]
    sub = TQ
    starts = tuple(range(0, ts, sub))

    @pl.when(s == 0)
    def _():
        ext_ref[0:HALO, :] = jnp.zeros((HALO, CONV_WIDTH), F32)

    @pl.when(s != 0)
    def _():
        ext_ref[0:HALO, :] = ext_ref[ts:ts + HALO, :]

    projs = [jnp.dot(_rms(x_ref[0, r0:r0 + sub, :], nm_ref[...]).astype(BF16), win_ref[...],
                     preferred_element_type=F32) for r0 in starts]

    lane = lax.broadcasted_iota(jnp.int32, (sub, 128), 1)
    first_half = lane < QK_ROPE
    scale = math.log2(math.e) / math.sqrt(QK_HEAD)

    for r0, proj in zip(starts, projs):
        rows = slice(r0, r0 + sub)
        u = proj[:, OFF_C:OFF_C + CONV_WIDTH] * proj[:, OFF_X:OFF_X + CONV_WIDTH]
        ext_ref[HALO + r0:HALO + r0 + sub, :] = u
        y = (convw_ref[2:3, :] * u
             + convw_ref[1:2, :] * ext_ref[HALO + r0 - 1:HALO + r0 - 1 + sub, :]
             + convw_ref[0:1, :] * ext_ref[HALO + r0 - 2:HALO + r0 - 2 + sub, :])
        conv_ref[0, rows, :] = (proj[:, OFF_B:OFF_B + CONV_WIDTH] * y).astype(BF16)

        hq = _rms(proj[:, OFF_CQ:OFF_CQ + Q_LORA], qn_ref[...]).astype(BF16)
        qp = jnp.dot(hq, wuq_ref[...], preferred_element_type=F32)
        hkv = _rms(proj[:, OFF_CKV:OFF_CKV + KV_LORA], kvn_ref[...]).astype(BF16)
        kvp = jnp.dot(hkv, wukv_ref[...], preferred_element_type=F32)

        tbl = tbl_ref[rows, :]

        def rope_block(grp, gain):
            r = grp * tbl * gain
            r = r + pltpu.roll(r, QK_ROPE, axis=1)
            return jnp.where(first_half, r, 0.0)

        kpe = proj[:, OFF_KPE:OFF_KPE + 128]
        ss_kpe = jnp.sum(jnp.where(first_half, kpe * kpe, 0.0), axis=-1, keepdims=True)
        k_rope = rope_block(kpe, gkr_ref[...])
        for h in range(MLA_HEADS):
            qn = qp[:, 128 * h:128 * (h + 1)]
            qg = qp[:, 512 + 128 * h:512 + 128 * (h + 1)]
            ss = (jnp.sum(qn * qn, axis=-1, keepdims=True)
                  + jnp.sum(jnp.where(first_half, qg * qg, 0.0), axis=-1, keepdims=True))
            rs = lax.rsqrt(ss * (1.0 / QK_HEAD) + EPS) * scale
            q_ref[0, h, rows, 0:128] = (qn * rs * gqn_ref[...]).astype(BF16)
            q_ref[0, h, rows, 128:256] = (rope_block(qg, gqr_ref[...]) * rs).astype(BF16)

            kn = kvp[:, 256 * h:256 * h + 128]
            ssk = jnp.sum(kn * kn, axis=-1, keepdims=True) + ss_kpe
            rsk = lax.rsqrt(ssk * (1.0 / QK_HEAD) + EPS)
            k_ref[0, h, rows, 0:128] = (kn * rsk * gkn_ref[...]).astype(BF16)
            k_ref[0, h, rows, 128:256] = (k_rope * rsk).astype(BF16)
            v = kvp[:, 256 * h + 128:256 * (h + 1)]
            vt_ref[0, h, r0 // TQ] = v.T.astype(BF16)


def _pre_call(x, nm, win, convw, qn, wuq, kvn, wukv, gqn, gqr, gkn, gkr, tbl):
    b, s, d = x.shape
    ts = TS_PRE
    const = lambda shape: pl.BlockSpec(shape, lambda bi, si: (0,) * len(shape))
    return pl.pallas_call(
        _pre_kernel,
        grid=(b, s // ts),
        in_specs=[
            pl.BlockSpec((1, ts, d), lambda bi, si: (bi, si, 0)),
            const((1, d)), const((d, IN_PROJ_EXT)), const((3, CONV_WIDTH)),
            const((1, Q_LORA)), const((Q_LORA, 1024)),
            const((1, KV_LORA)), const((KV_LORA, 1024)),
            const((1, 128)), const((1, 128)), const((1, 128)), const((1, 128)),
            pl.BlockSpec((ts, 128), lambda bi, si: (si, 0)),
        ],
        out_specs=[
            pl.BlockSpec((1, ts, CONV_WIDTH), lambda bi, si: (bi, si, 0)),
            pl.BlockSpec((1, MLA_HEADS, ts, QK_PAD), lambda bi, si: (bi, 0, si, 0)),
            pl.BlockSpec((1, MLA_HEADS, ts, QK_PAD), lambda bi, si: (bi, 0, si, 0)),
            pl.BlockSpec((1, MLA_HEADS, ts // TQ, V_HEAD, TQ), lambda bi, si: (bi, 0, si, 0, 0)),
        ],
        out_shape=[
            jax.ShapeDtypeStruct((b, s, CONV_WIDTH), BF16),
            jax.ShapeDtypeStruct((b, MLA_HEADS, s, QK_PAD), BF16),
            jax.ShapeDtypeStruct((b, MLA_HEADS, s, QK_PAD), BF16),
            jax.ShapeDtypeStruct((b, MLA_HEADS, s // TQ, V_HEAD, TQ), BF16),
        ],
        scratch_shapes=[pltpu.VMEM((ts + HALO, CONV_WIDTH), F32)],
        compiler_params=pltpu.CompilerParams(
            dimension_semantics=("arbitrary", "arbitrary"), vmem_limit_bytes=VMEM_LIMIT),
        name="pre_mixer",
    )(x, nm, win, convw, qn, wuq, kvn, wukv, gqn, gqr, gkn, gkr, tbl)


def _attn_kernel(q_ref, k_ref, vt_ref, o_ref, m_ref, l_ref, acc_ref, s_ref):
    tq = TQ
    key_c = lax.broadcasted_iota(jnp.int32, (tq, tq), 0) // CHUNK
    qry_c = lax.broadcasted_iota(jnp.int32, (tq, tq), 1) // CHUNK
    allowed = key_c <= qry_c
    for t in range(q_ref.shape[2] // tq):
        _attn_tile(pl.program_id(1) * (q_ref.shape[2] // tq) + t, slice(t * tq, (t + 1) * tq),
                   allowed, q_ref, k_ref, vt_ref, o_ref, m_ref, l_ref, acc_ref, s_ref)


def _attn_tile(i, q_rows, allowed, q_ref, k_ref, vt_ref, o_ref, m_ref, l_ref, acc_ref, s_ref):
    tq = TQ
    m_ref[...] = jnp.full(m_ref.shape, -1e30, F32)
    l_ref[...] = jnp.zeros(l_ref.shape, F32)
    acc_ref[...] = jnp.zeros(acc_ref.shape, F32)

    def scores(j, keep):
        start = pl.multiple_of(j * tq, tq)
        out = []
        for h in range(MLA_HEADS):
            kb = k_ref[0, h, pl.ds(start, tq), :]
            st = lax.dot_general(kb, q_ref[0, h, q_rows, :], (((1,), (1,)), ((), ())),
                                 preferred_element_type=F32)
            out.append(st if keep is None else jnp.where(keep, st, -1e30))
        return out

    def update(j):
        probs, alphas = [], []
        for h in range(MLA_HEADS):
            st = s_ref[h]
            m = m_ref[h]
            m_new = jnp.maximum(m, jnp.max(st, axis=0, keepdims=True))
            alpha = jnp.exp2(m - m_new)
            p = jnp.exp2(st - m_new)
            l_ref[h] = alpha * l_ref[h] + jnp.sum(p, axis=0, keepdims=True)
            m_ref[h] = m_new
            probs.append(p.astype(BF16))
            alphas.append(alpha)
        for h in range(MLA_HEADS):
            acc_ref[h] = alphas[h] * acc_ref[h] + jnp.dot(vt_ref[0, h, j], probs[h],
                                                           preferred_element_type=F32)

    def stage(sts):
        for h in range(MLA_HEADS):
            s_ref[h] = sts[h]

    stage(scores(0, allowed | (i > 0)))

    def body(j, c):
        nxt = scores(j + 1, None)
        update(j)
        stage(nxt)
        return c

    lax.fori_loop(0, i - 1, body, 0)

    @pl.when(i > 0)
    def _():
        nxt = scores(i, allowed)
        update(i - 1)
        stage(nxt)

    update(i)
    for h in range(MLA_HEADS):
        o = acc_ref[h] / l_ref[h]
        o_ref[0, q_rows, V_HEAD * h:V_HEAD * (h + 1)] = o.T.astype(BF16)


def _attn_call(q, k, vt):
    b, nh, s, _ = q.shape
    rows = TQ * Q_TILES_PER_STEP
    return pl.pallas_call(
        _attn_kernel,
        grid=(b, s // rows),
        in_specs=[
            pl.BlockSpec((1, nh, rows, QK_PAD), lambda bi, qi: (bi, 0, qi, 0)),
            pl.BlockSpec((1, nh, s, QK_PAD), lambda bi, qi: (bi, 0, 0, 0)),
            pl.BlockSpec((1, nh, s // TQ, V_HEAD, TQ), lambda bi, qi: (bi, 0, 0, 0, 0)),
        ],
        out_specs=pl.BlockSpec((1, rows, nh * V_HEAD), lambda bi, qi: (bi, qi, 0)),
        out_shape=jax.ShapeDtypeStruct((b, s, nh * V_HEAD), BF16),
        scratch_shapes=[pltpu.VMEM((nh, 1, TQ), F32), pltpu.VMEM((nh, 1, TQ), F32),
                        pltpu.VMEM((nh, V_HEAD, TQ), F32), pltpu.VMEM((nh, TQ, TQ), F32)],
        compiler_params=pltpu.CompilerParams(
            dimension_semantics=("arbitrary", "arbitrary"), vmem_limit_bytes=VMEM_LIMIT),
        name="mla_attention",
    )(q, k, vt)


def _post_kernel(x_ref, conv_ref, attn_ref, wo_ref, g_ref, wr_ref, br_ref, tri_ref,
                 h1_ref, hb_ref, lpos_ref, gate_ref, tile_n_ref, tile_carry_ref, cnt_ref,
                 carry_ref):
    step = pl.program_id(0)

    @pl.when(step == 0)
    def _():
        carry_ref[...] = jnp.zeros_like(carry_ref)

    for t in range(x_ref.shape[0] // TM_POST):
        _post_tile(t, x_ref, conv_ref, attn_ref, wo_ref, g_ref, wr_ref, br_ref, tri_ref,
                   h1_ref, hb_ref, lpos_ref, gate_ref, tile_n_ref, tile_carry_ref, carry_ref)
    cnt_ref[...] = carry_ref[...].astype(jnp.int32)


def _post_tile(t, x_ref, conv_ref, attn_ref, wo_ref, g_ref, wr_ref, br_ref, tri_ref,
               h1_ref, hb_ref, lpos_ref, gate_ref, tile_n_ref, tile_carry_ref, carry_ref):
    tm = TM_POST
    sub = tri_ref.shape[0]
    subs = [slice(r0, r0 + sub) for r0 in range(t * tm, (t + 1) * tm, sub)]
    h1s = [x_ref[rows, :]
           + jnp.dot(conv_ref[rows, :], wo_ref[0:CONV_WIDTH, :], preferred_element_type=F32)
           + jnp.dot(attn_ref[rows, :], wo_ref[CONV_WIDTH:, :], preferred_element_type=F32)
           for rows in subs]
    lane = lax.broadcasted_iota(jnp.int32, (sub, N_EXPERTS), 1).astype(F32)
    lane4 = lax.broadcasted_iota(jnp.int32, (sub, TOP_K), 1)

    def cols(c):
        return jnp.where(lane4 == 0, c[0], jnp.where(lane4 == 1, c[1],
                         jnp.where(lane4 == 2, c[2], c[3])))

    picks, members = [], []
    for rows, h1 in zip(subs, h1s):
        h1_ref[rows, :] = h1
        hb = _rms(h1, g_ref[...]).astype(BF16)
        hb_ref[rows, :] = hb
        lg = jnp.dot(hb, wr_ref[...], preferred_element_type=F32) + br_ref[...]
        member = jnp.zeros((sub, N_EXPERTS), F32)
        vals, idxs, hots = [], [], []
        for _ in range(TOP_K):
            m = jnp.max(lg, axis=-1, keepdims=True)
            ix = jnp.min(jnp.where(lg == m, lane, float(N_EXPERTS)), axis=-1, keepdims=True)
            hot = lane == ix
            lg = jnp.where(hot, -jnp.inf, lg)
            member = member + hot.astype(F32)
            vals.append(m)
            idxs.append(ix)
            hots.append(hot)
        exps = [jnp.exp(v - vals[0]) for v in vals]
        denom = exps[0] + exps[1] + exps[2] + exps[3]
        gate_ref[rows, :] = cols([e / denom for e in exps])
        picks.append((idxs, hots))
        members.append(member)

    sub_n = [jnp.sum(member, axis=0, keepdims=True) for member in members]
    tile_n = sub_n[0]
    for n_s in sub_n[1:]:
        tile_n = tile_n + n_s
    earlier = jnp.zeros((1, N_EXPERTS), F32)
    for rows, member, n_s, (idxs, hots) in zip(subs, members, sub_n, picks):
        before = jnp.dot(tri_ref[...], member.astype(BF16), preferred_element_type=F32) + earlier
        lpos = [jnp.sum(jnp.where(hot, before, 0.0) + jnp.where(lane < ix, tile_n, 0.0),
                        axis=-1, keepdims=True) for hot, ix in zip(hots, idxs)]
        lpos_ref[rows, :] = cols(lpos).astype(jnp.int32)
        earlier = earlier + n_s
    tile_n_ref[t] = tile_n.astype(jnp.int32)
    tile_carry_ref[t] = carry_ref[...].astype(jnp.int32)
    carry_ref[...] = carry_ref[...] + tile_n


def _post_call(xf, conv, attn, wo, g, wr, br, tri):
    n, d = xf.shape
    tm = TM_POST * POST_TILES_PER_STEP
    const = lambda shape: pl.BlockSpec(shape, lambda i: (0,) * len(shape))
    row = lambda w: pl.BlockSpec((tm, w), lambda i: (i, 0))
    table = pl.BlockSpec((POST_TILES_PER_STEP, 1, N_EXPERTS), lambda i: (i, 0, 0))
    return pl.pallas_call(
        _post_kernel,
        grid=(n // tm,),
        in_specs=[row(d), row(CONV_WIDTH), row(CONV_WIDTH), const((d, d)), const((1, d)),
                  const((d, N_EXPERTS)), const((1, N_EXPERTS)), const(tri.shape)],
        out_specs=[row(d), row(d), row(TOP_K), row(TOP_K), table, table,
                   const((1, N_EXPERTS))],
        out_shape=[
            jax.ShapeDtypeStruct((n, d), F32),
            jax.ShapeDtypeStruct((n, d), BF16),
            jax.ShapeDtypeStruct((n, TOP_K), jnp.int32),
            jax.ShapeDtypeStruct((n, TOP_K), F32),
            jax.ShapeDtypeStruct((n // TM_POST, 1, N_EXPERTS), jnp.int32),
            jax.ShapeDtypeStruct((n // TM_POST, 1, N_EXPERTS), jnp.int32),
            jax.ShapeDtypeStruct((1, N_EXPERTS), jnp.int32),
        ],
        scratch_shapes=[pltpu.VMEM((1, N_EXPERTS), F32)],
        compiler_params=pltpu.CompilerParams(
            dimension_semantics=("arbitrary",), vmem_limit_bytes=VMEM_LIMIT),
        name="post_mixer_router",
    )(xf, conv, attn, wo, g, wr, br, tri)


RUN_BITS = 10


def _rows(ref, first_row, n_rows):
    return ref.at[pl.ds(pl.multiple_of(first_row * ROW_LINES, ROW_LINES), n_rows * ROW_LINES), :]


def _for_each_run_piece(n_ref, far_ref, fn, bits=RUN_BITS):
    def per_expert(e, local0):
        n_e = n_ref[0, 0, e]
        far0 = far_ref[0, 0, e]

        for b in reversed(range(bits)):
            done = (n_e >> (b + 1)) << (b + 1)

            @pl.when(((n_e >> b) & 1) == 1)
            def _():
                fn(local0 + done, far0 + done, 1 << b)
        return local0 + n_e

    lax.fori_loop(0, N_EXPERTS, per_expert, 0)


PAD_BITS = TM_EXP.bit_length() - 1
assert TM_EXP == 1 << PAD_BITS


def _scatter_kernel(n_ref, far_ref, pad_n_ref, pad_far_ref, used_ref, hb_ref, lpos_ref, xs_ref,
                    xl, zeros, sem, zsem):
    i = pl.program_id(0)
    steps = pl.num_programs(0)
    tm = hb_ref.shape[0]
    pairs = tm * TOP_K
    slot = i % 2

    def wait_slot(s):
        pltpu.make_async_copy(xl.at[s], _rows(xs_ref, 0, pairs), sem.at[s]).wait()

    @pl.when(i == 0)
    def _():
        zeros[...] = jnp.zeros(zeros.shape, F32)

        def fill(_, far_row, size):
            return pltpu.make_async_copy(_rows(zeros, 0, size), _rows(xs_ref, far_row, size), zsem)

        _for_each_run_piece(pad_n_ref, pad_far_ref, lambda *a: fill(*a).start(), PAD_BITS)
        _for_each_run_piece(pad_n_ref, pad_far_ref, lambda *a: fill(*a).wait(), PAD_BITS)

        tile_rows = zeros.shape[0] // ROW_LINES
        n_tiles = xs_ref.shape[0] // zeros.shape[0]

        def tail(op):
            def body(t, c):
                op(fill(None, t * tile_rows, tile_rows))
                return c
            lax.fori_loop(used_ref[0], n_tiles, body, 0)

        tail(lambda cp: cp.start())
        tail(lambda cp: cp.wait())

    @pl.when(i >= 2)
    def _():
        wait_slot(slot)

    lane = lax.broadcasted_iota(jnp.int32, (tm, 128), 1)
    lp = lpos_ref[...].astype(F32)
    cols = jnp.zeros((tm, 128), F32)
    for k in range(TOP_K):
        cols = jnp.where(lane == k, lp[:, k:k + 1], cols)
    rows_t = cols.T.astype(jnp.int16)
    j = lax.broadcasted_iota(jnp.int32, (pairs, tm), 0).astype(jnp.int16)
    onehot = jnp.zeros((pairs, tm), BF16)
    for k in range(TOP_K):
        onehot = jnp.where(j == rows_t[k:k + 1, :], jnp.ones((), BF16), onehot)
    _store_row_tiles(xl.at[slot], jnp.dot(onehot, hb_ref[...], preferred_element_type=F32))

    def send(local_row, far_row, size):
        pltpu.make_async_copy(_rows(xl.at[slot], local_row, size), _rows(xs_ref, far_row, size),
                              sem.at[slot]).start()

    _for_each_run_piece(n_ref, far_ref, send)

    @pl.when(i == steps - 1)
    def _():
        @pl.when(steps >= 2)
        def _():
            wait_slot(1 - slot)
        wait_slot(slot)


def _scatter_call(tile_n, tile_far, pad_n, pad_far, used_tiles, hb, lpos, sorted_rows):
    n, d = hb.shape
    tm = TM_POST
    table = pl.BlockSpec((1, 1, N_EXPERTS), lambda i: (i, 0, 0), memory_space=pltpu.SMEM)
    whole = pl.BlockSpec((1, 1, N_EXPERTS), lambda i: (0, 0, 0), memory_space=pltpu.SMEM)
    return pl.pallas_call(
        _scatter_kernel,
        grid=(n // tm,),
        in_specs=[table, table, whole, whole, pl.BlockSpec(memory_space=pltpu.SMEM),
                  pl.BlockSpec((tm, d), lambda i: (i, 0)),
                  pl.BlockSpec((tm, TOP_K), lambda i: (i, 0))],
        out_specs=pl.BlockSpec(memory_space=pl.ANY),
        out_shape=jax.ShapeDtypeStruct((sorted_rows * ROW_LINES, 128), F32),
        scratch_shapes=[pltpu.VMEM((2, tm * TOP_K * ROW_LINES, 128), F32),
                        pltpu.VMEM((TM_EXP * ROW_LINES, 128), F32),
                        pltpu.SemaphoreType.DMA((2,)), pltpu.SemaphoreType.DMA],
        compiler_params=pltpu.CompilerParams(
            dimension_semantics=("arbitrary",), vmem_limit_bytes=VMEM_LIMIT),
        name="dispatch_scatter",
    )(tile_n, tile_far, pad_n, pad_far, used_tiles, hb, lpos)


def _expert_kernel(tile_ref, exp_ref, flag_ref, next_ref,
                   xs_ref, wgu_hbm, bgu_ref, wd_hbm, bd_ref, o_ref,
                   wgu_f, wd_f, wgu_bf, wd_bf, wsem):
    i = pl.program_id(0)
    tm = TM_EXP
    tile_lines = tm * ROW_LINES
    cast_rows = 128

    def weight_copies(e, s):
        return (pltpu.make_async_copy(wgu_hbm.at[e], wgu_f.at[s], wsem.at[s, 0]),
                pltpu.make_async_copy(wd_hbm.at[e], wd_f.at[s], wsem.at[s, 1]))

    @pl.when(i == 0)
    def _():
        for cp in weight_copies(exp_ref[0], 0):
            cp.start()

    for t in range(EXP_TILES_PER_STEP):
        g = i * EXP_TILES_PER_STEP + t
        flag = flag_ref[g]
        e = exp_ref[g]
        x_t = xs_ref.at[pl.ds(t * tile_lines, tile_lines), :]
        o_t = o_ref.at[pl.ds(t * tile_lines, tile_lines), :]

        @pl.when((flag & 2) != 0)
        def _():
            s = (flag >> 3) & 1
            for cp in weight_copies(e, s):
                cp.wait()
            nxt = next_ref[g]

            @pl.when(nxt >= 0)
            def _():
                for cp in weight_copies(nxt, 1 - s):
                    cp.start()

            def cast(r, c):
                rows = pl.ds(pl.multiple_of(r * cast_rows, cast_rows), cast_rows)
                wgu_bf[rows, :] = wgu_f[s, rows, :].astype(BF16)
                wd_bf[rows, :] = wd_f[s, rows, :].astype(BF16)
                return c
            lax.fori_loop(0, D_MODEL // cast_rows, cast, 0)

        def run(rows):
            x = _load_row_tiles(x_t, rows).astype(BF16)
            gu = jnp.dot(x, wgu_bf[...], preferred_element_type=F32) + bgu_ref[e]
            gate = jnp.minimum(gu[:, :D_FF], SWIGLU_LIMIT)
            up = jnp.clip(gu[:, D_FF:], -SWIGLU_LIMIT, SWIGLU_LIMIT)
            act = (up + 1.0) * (gate * (1.0 / (1.0 + jnp.exp(-SWIGLU_ALPHA * gate))))
            out = jnp.dot(act.astype(BF16), wd_bf[...], preferred_element_type=F32) + bd_ref[e]
            _store_row_tiles(o_t, out)

        @pl.when((flag & 5) == 1)
        def _():
            run(tm)

        @pl.when((flag & 5) == 5)
        def _():
            run(tm // 2)
            o_t[tm // 2 * ROW_LINES:, :] = jnp.zeros((tm // 2 * ROW_LINES, 128), F32)

        @pl.when((flag & 16) != 0)
        def _():
            o_t[...] = jnp.zeros((tile_lines, 128), F32)


def _expert_call(meta, xs, wgu, bgu, wd, bd):
    lines, w = xs.shape
    tm = TM_EXP * EXP_TILES_PER_STEP
    n_items = meta[0].shape[0]
    whole = lambda a: pl.BlockSpec(a.shape, lambda i, *_: (0,) * a.ndim)
    return pl.pallas_call(
        _expert_kernel,
        grid_spec=pltpu.PrefetchScalarGridSpec(
            num_scalar_prefetch=4,
            grid=(n_items,),
            in_specs=[
                pl.BlockSpec((tm * ROW_LINES, w), lambda i, t, e, f, nx: (t[i], 0)),
                pl.BlockSpec(memory_space=pl.ANY), whole(bgu),
                pl.BlockSpec(memory_space=pl.ANY), whole(bd),
            ],
            out_specs=pl.BlockSpec((tm * ROW_LINES, w), lambda i, t, e, f, nx: (t[i], 0)),
            scratch_shapes=[pltpu.VMEM((2, D_MODEL, 2 * D_FF), F32),
                            pltpu.VMEM((2, D_FF, D_MODEL), F32),
                            pltpu.VMEM((D_MODEL, 2 * D_FF), BF16),
                            pltpu.VMEM((D_FF, D_MODEL), BF16),
                            pltpu.SemaphoreType.DMA((2, 2))],
        ),
        out_shape=jax.ShapeDtypeStruct((lines, w), F32),
        input_output_aliases={len(meta): 0},
        compiler_params=pltpu.CompilerParams(
            dimension_semantics=("arbitrary",), vmem_limit_bytes=VMEM_LIMIT),
        name="moe_experts",
    )(*meta, xs, wgu, bgu, wd, bd)


def _expert_layout(counts, n_pairs, tm):
    n_tiles = n_pairs // tm + N_EXPERTS
    assert n_tiles % EXP_TILES_PER_STEP == 0
    per_exp = (counts + tm - 1) // tm
    tile_end = jnp.cumsum(per_exp)
    tile0 = tile_end - per_exp
    used = tile_end[-1]
    step = jnp.arange(n_tiles, dtype=jnp.int32)
    tile = jnp.minimum(step, used - 1)
    e = jnp.sum((tile_end[None, :] <= tile[:, None]).astype(jnp.int32), axis=1)
    hot = e[:, None] == jnp.arange(N_EXPERTS, dtype=jnp.int32)[None, :]
    pick = lambda tab: jnp.sum(jnp.where(hot, tab[None, :], 0), axis=1)
    e_tile0 = pick(tile0)
    rows_used = pick(counts) - (step - e_tile0) * tm
    experts = jnp.arange(N_EXPERTS, dtype=jnp.int32)
    has_tiles = per_exp > 0
    slot_of_expert = (jnp.cumsum(has_tiles.astype(jnp.int32)) - 1) & 1
    later = (experts[None, :] > experts[:, None]) & has_tiles[None, :]
    next_of_expert = jnp.min(jnp.where(later, experts[None, :], N_EXPERTS), axis=1)
    next_of_expert = jnp.where(next_of_expert < N_EXPERTS, next_of_expert, -1)
    last_block = (used - 1) // EXP_TILES_PER_STEP
    flag = jnp.where(step < used, 1 + 2 * (step == e_tile0) + 4 * (rows_used <= tm // 2)
                     + 8 * pick(slot_of_expert),
                     jnp.where(step // EXP_TILES_PER_STEP == last_block, 16, 0))
    cast = lambda a: a.astype(jnp.int32)
    block = jnp.minimum(jnp.arange(n_tiles // EXP_TILES_PER_STEP, dtype=jnp.int32), last_block)
    group_start = tile0 * tm
    pad_n = per_exp * tm - counts
    pad_start = group_start + counts
    return ((cast(block), cast(e), cast(flag), cast(pick(next_of_expert))), cast(used).reshape(1),
            cast(group_start), cast(pad_n), cast(pad_start))


def _combine_kernel(n_ref, far_ref, n_next_ref, far_next_ref, h1_ref, lpos_ref, gate_ref, ys_ref,
                    o_ref, yl, sem):
    i = pl.program_id(0)
    steps = pl.num_programs(0)
    tm = h1_ref.shape[0]
    pairs = tm * TOP_K
    slot = i % 2

    def fetch(nr, fr, s):
        def recv(local_row, far_row, size):
            pltpu.make_async_copy(_rows(ys_ref, far_row, size), _rows(yl.at[s], local_row, size),
                                  sem.at[s]).start()
        _for_each_run_piece(nr, fr, recv)

    @pl.when(i == 0)
    def _():
        fetch(n_ref, far_ref, 0)

    @pl.when(i + 1 < steps)
    def _():
        fetch(n_next_ref, far_next_ref, 1 - slot)

    pltpu.make_async_copy(_rows(ys_ref, 0, pairs), yl.at[slot], sem.at[slot]).wait()

    j = lax.broadcasted_iota(jnp.int32, (tm, pairs), 1).astype(jnp.int16)
    lp = lpos_ref[...].astype(jnp.int16)
    g = gate_ref[...].astype(BF16)
    w = jnp.zeros((tm, pairs), BF16)
    for k in range(TOP_K):
        w = jnp.where(j == lp[:, k:k + 1], g[:, k:k + 1], w)
    y = _load_row_tiles(yl, pairs, lead=(slot,)).astype(BF16)
    o_ref[...] = h1_ref[...] + jnp.dot(w, y, preferred_element_type=F32)


def _combine_call(tile_n, tile_far, h1, lpos, gates, ys):
    n, d = h1.shape
    tm = TM_POST
    steps = n // tm
    table = lambda fn: pl.BlockSpec((1, 1, N_EXPERTS), fn, memory_space=pltpu.SMEM)
    cur = lambda i: (i, 0, 0)
    nxt = lambda i: (jnp.minimum(i + 1, steps - 1), 0, 0)
    return pl.pallas_call(
        _combine_kernel,
        grid=(steps,),
        in_specs=[table(cur), table(cur), table(nxt), table(nxt),
                  pl.BlockSpec((tm, d), lambda i: (i, 0)),
                  pl.BlockSpec((tm, TOP_K), lambda i: (i, 0)),
                  pl.BlockSpec((tm, TOP_K), lambda i: (i, 0)),
                  pl.BlockSpec(memory_space=pl.ANY)],
        out_specs=pl.BlockSpec((tm, d), lambda i: (i, 0)),
        out_shape=jax.ShapeDtypeStruct((n, d), F32),
        scratch_shapes=[pltpu.VMEM((2, tm * TOP_K * ROW_LINES, 128), F32),
                        pltpu.SemaphoreType.DMA((2,))],
        compiler_params=pltpu.CompilerParams(
            dimension_semantics=("arbitrary",), vmem_limit_bytes=VMEM_LIMIT),
        name="moe_combine",
    )(tile_n, tile_far, tile_n, tile_far, h1, lpos, gates, ys)


def _rotate_half_perm():
    half = QK_ROPE // 2
    perm = np.concatenate([np.arange(half, QK_ROPE), np.arange(0, half)])
    sign = np.concatenate([-np.ones(half), np.ones(half)]).astype(np.float32)
    return perm, sign


def _layer(h, l, norm_mix, w_in, conv_w, q_norm, w_uq, kv_norm, w_ukv, q_head_norm,
           k_head_norm, w_o, norm_ffn, w_router, b_router, w_gate_up, b_gate_up,
           w_down, b_down):
    b, s, d = h.shape
    n = b * s
    perm, sign = _rotate_half_perm()

    kpe_cols = w_in[l][:, OFF_KPE:OFF_KPE + QK_ROPE]
    win = jnp.concatenate([w_in[l], kpe_cols[:, perm] * sign], axis=1).astype(BF16)
    wq = w_uq[l].reshape(Q_LORA, MLA_HEADS, QK_HEAD)
    wq_rope = wq[:, :, QK_NOPE:]
    wuq = jnp.concatenate(
        [wq[:, :, :QK_NOPE].reshape(Q_LORA, MLA_HEADS * QK_NOPE),
         jnp.concatenate([wq_rope, wq_rope[:, :, perm] * sign], axis=2)
         .reshape(Q_LORA, MLA_HEADS * 2 * QK_ROPE)], axis=1).astype(BF16)
    gq, gk = q_head_norm[l], k_head_norm[l]
    gqn, gkn = gq[None, :QK_NOPE], gk[None, :QK_NOPE]
    gqr = jnp.concatenate([gq[QK_NOPE:], gq[QK_NOPE:][perm]])[None, :]
    gkr = jnp.concatenate([gk[QK_NOPE:], gk[QK_NOPE:][perm]])[None, :]

    half = QK_ROPE // 2
    inv_freq = ROPE_THETA ** (-jnp.arange(half, dtype=F32) / half)
    ang = jnp.arange(s).astype(F32)[:, None] * inv_freq[None, :]
    cos, sin = jnp.cos(ang), jnp.sin(ang)
    tbl = jnp.concatenate([cos, cos, sin, sin], axis=1)

    conv, q, k, vt = _pre_call(
        h, norm_mix[l][None, :], win, conv_w[l], q_norm[l][None, :], wuq,
        kv_norm[l][None, :], w_ukv[l].astype(BF16), gqn, gqr, gkn, gkr, tbl)
    attn = _attn_call(q, k, vt)

    tri = jnp.tril(jnp.ones((SUB_POST, SUB_POST), BF16), -1)
    h1, hb, lpos, gates, tile_n, tile_carry, counts = _post_call(
        h.reshape(n, d), conv.reshape(n, CONV_WIDTH), attn.reshape(n, CONV_WIDTH),
        w_o[l].astype(BF16), norm_ffn[l][None, :], w_router[l].astype(BF16),
        b_router[l][None, :], tri)

    meta, used_tiles, group_start, pad_n, pad_start = _expert_layout(counts[0], n * TOP_K, TM_EXP)
    sorted_rows = n * TOP_K + N_EXPERTS * TM_EXP
    tile_far = (tile_carry + group_start[None, None, :]).astype(jnp.int32)
    xs = _scatter_call(tile_n, tile_far, pad_n[None, None, :], pad_start[None, None, :],
                       used_tiles, hb, lpos, sorted_rows)
    ys = _expert_call(meta, xs, w_gate_up[l], b_gate_up[l][:, None, :],
                      w_down[l], b_down[l][:, None, :])
    out = _combine_call(tile_n, tile_far, h1, lpos, gates, ys)
    return out.reshape(b, s, d)


def kernel(x, norm_mix, w_in, conv_w, q_norm, w_uq, kv_norm, w_ukv, q_head_norm, k_head_norm,
           w_o, norm_ffn, w_router, b_router, w_gate_up, b_gate_up, w_down, b_down):
    h = x
    for l in range(norm_mix.shape[0]):
        h = _layer(h, l, norm_mix, w_in, conv_w, q_norm, w_uq, kv_norm, w_ukv, q_head_norm,
                   k_head_norm, w_o, norm_ffn, w_router, b_router, w_gate_up, b_gate_up,
                   w_down, b_down)
    return h
```

```python
import math

import jax
import jax.numpy as jnp
import numpy as np
from jax import lax
from jax.experimental import pallas as pl
from jax.experimental.pallas import tpu as pltpu

F32 = jnp.float32
BF16 = jnp.bfloat16

D_MODEL = 1024
CONV_WIDTH = 512
MLA_HEADS = 4
QK_NOPE = 128
QK_ROPE = 64
QK_HEAD = QK_NOPE + QK_ROPE
QK_PAD = 256
V_HEAD = 128
Q_LORA = 384
KV_LORA = 256
ROPE_THETA = 10000.0
CHUNK = 64
N_EXPERTS = 32
TOP_K = 4
D_FF = 1024
SWIGLU_LIMIT = 7.0
SWIGLU_ALPHA = 1.702
EPS = 1e-6

OFF_B, OFF_C, OFF_X, OFF_CQ, OFF_CKV, OFF_KPE = 0, 512, 1024, 1536, 1920, 2176
IN_PROJ_EXT = 2304

VMEM_LIMIT = 56 * 1024 * 1024

TS_PRE = 1024
TQ = 256
Q_TILES_PER_STEP = 8
TM_POST = 512
SUB_POST = 256
POST_TILES_PER_STEP = 2
TB_ROWS = 256
TM_EXP = 512
EXP_TILES_PER_STEP = 2
HALO = 8


def _rms(x, g):
    return x * lax.rsqrt(jnp.mean(x * x, axis=-1, keepdims=True) + EPS) * g


ROW_LINES = D_MODEL // 128


def _load_row_tiles(ref, rows, lead=()):
    return jnp.concatenate(
        [ref[lead + (pl.ds(c, rows, stride=ROW_LINES), slice(None))] for c in range(ROW_LINES)],
        axis=1)


def _store_row_tiles(ref, val):
    rows = val.shape[0]
    for c in range(ROW_LINES):
        ref[pl.ds(c, rows, stride=ROW_LINES), :] = val[:, 128 * c:128 * (c + 1)]


def _pre_kernel(x_ref, nm_ref, win_ref, convw_ref, qn_ref, wuq_ref, kvn_ref, wukv_ref,
                gqn_ref, gqr_ref, gkn_ref, gkr_ref, tbl_ref,
                conv_ref, q_ref, k_ref, vt_ref, ext_ref):
    s = pl.program_id(1)
    ts = x_ref.shape[1]
    sub = TQ
    starts = tuple(range(0, ts, sub))

    @pl.when(s == 0)
    def _():
        ext_ref[0:HALO, :] = jnp.zeros((HALO, CONV_WIDTH), F32)

    @pl.when(s != 0)
    def _():
        ext_ref[0:HALO, :] = ext_ref[ts:ts + HALO, :]

    projs = [jnp.dot(_rms(x_ref[0, r0:r0 + sub, :], nm_ref[...]).astype(BF16), win_ref[...],
                     preferred_element_type=F32) for r0 in starts]

    lane = lax.broadcasted_iota(jnp.int32, (sub, 128), 1)
    first_half = lane < QK_ROPE
    scale = math.log2(math.e) / math.sqrt(QK_HEAD)

    for r0, proj in zip(starts, projs):
        rows = slice(r0, r0 + sub)
        u = proj[:, OFF_C:OFF_C + CONV_WIDTH] * proj[:, OFF_X:OFF_X + CONV_WIDTH]
        ext_ref[HALO + r0:HALO + r0 + sub, :] = u
        y = (convw_ref[2:3, :] * u
             + convw_ref[1:2, :] * ext_ref[HALO + r0 - 1:HALO + r0 - 1 + sub, :]
             + convw_ref[0:1, :] * ext_ref[HALO + r0 - 2:HALO + r0 - 2 + sub, :])
        conv_ref[0, rows, :] = (proj[:, OFF_B:OFF_B + CONV_WIDTH] * y).astype(BF16)

        hq = _rms(proj[:, OFF_CQ:OFF_CQ + Q_LORA], qn_ref[...]).astype(BF16)
        qp = jnp.dot(hq, wuq_ref[...], preferred_element_type=F32)
        hkv = _rms(proj[:, OFF_CKV:OFF_CKV + KV_LORA], kvn_ref[...]).astype(BF16)
        kvp = jnp.dot(hkv, wukv_ref[...], preferred_element_type=F32)

        tbl = tbl_ref[rows, :]

        def rope_block(grp, gain):
            r = grp * tbl * gain
            r = r + pltpu.roll(r, QK_ROPE, axis=1)
            return jnp.where(first_half, r, 0.0)

        kpe = proj[:, OFF_KPE:OFF_KPE + 128]
        ss_kpe = jnp.sum(jnp.where(first_half, kpe * kpe, 0.0), axis=-1, keepdims=True)
        k_rope = rope_block(kpe, gkr_ref[...])
        for h in range(MLA_HEADS):
            qn = qp[:, 128 * h:128 * (h + 1)]
            qg = qp[:, 512 + 128 * h:512 + 128 * (h + 1)]
            ss = (jnp.sum(qn * qn, axis=-1, keepdims=True)
                  + jnp.sum(jnp.where(first_half, qg * qg, 0.0), axis=-1, keepdims=True))
            rs = lax.rsqrt(ss * (1.0 / QK_HEAD) + EPS) * scale
            q_ref[0, h, rows, 0:128] = (qn * rs * gqn_ref[...]).astype(BF16)
            q_ref[0, h, rows, 128:256] = (rope_block(qg, gqr_ref[...]) * rs).astype(BF16)

            kn = kvp[:, 256 * h:256 * h + 128]
            ssk = jnp.sum(kn * kn, axis=-1, keepdims=True) + ss_kpe
            rsk = lax.rsqrt(ssk * (1.0 / QK_HEAD) + EPS)
            k_ref[0, h, rows, 0:128] = (kn * rsk * gkn_ref[...]).astype(BF16)
            k_ref[0, h, rows, 128:256] = (k_rope * rsk).astype(BF16)
            v = kvp[:, 256 * h + 128:256 * (h + 1)]
            vt_ref[0, h, r0 // TQ] = v.T.astype(BF16)


def _pre_call(x, nm, win, convw, qn, wuq, kvn, wukv, gqn, gqr, gkn, gkr, tbl):
    b, s, d = x.shape
    ts = TS_PRE
    const = lambda shape: pl.BlockSpec(shape, lambda bi, si: (0,) * len(shape))
    return pl.pallas_call(
        _pre_kernel,
        grid=(b, s // ts),
        in_specs=[
            pl.BlockSpec((1, ts, d), lambda bi, si: (bi, si, 0)),
            const((1, d)), const((d, IN_PROJ_EXT)), const((3, CONV_WIDTH)),
            const((1, Q_LORA)), const((Q_LORA, 1024)),
            const((1, KV_LORA)), const((KV_LORA, 1024)),
            const((1, 128)), const((1, 128)), const((1, 128)), const((1, 128)),
            pl.BlockSpec((ts, 128), lambda bi, si: (si, 0)),
        ],
        out_specs=[
            pl.BlockSpec((1, ts, CONV_WIDTH), lambda bi, si: (bi, si, 0)),
            pl.BlockSpec((1, MLA_HEADS, ts, QK_PAD), lambda bi, si: (bi, 0, si, 0)),
            pl.BlockSpec((1, MLA_HEADS, ts, QK_PAD), lambda bi, si: (bi, 0, si, 0)),
            pl.BlockSpec((1, MLA_HEADS, ts // TQ, V_HEAD, TQ), lambda bi, si: (bi, 0, si, 0, 0)),
        ],
        out_shape=[
            jax.ShapeDtypeStruct((b, s, CONV_WIDTH), BF16),
            jax.ShapeDtypeStruct((b, MLA_HEADS, s, QK_PAD), BF16),
            jax.ShapeDtypeStruct((b, MLA_HEADS, s, QK_PAD), BF16),
            jax.ShapeDtypeStruct((b, MLA_HEADS, s // TQ, V_HEAD, TQ), BF16),
        ],
        scratch_shapes=[pltpu.VMEM((ts + HALO, CONV_WIDTH), F32)],
        compiler_params=pltpu.CompilerParams(
            dimension_semantics=("arbitrary", "arbitrary"), vmem_limit_bytes=VMEM_LIMIT),
        name="pre_mixer",
    )(x, nm, win, convw, qn, wuq, kvn, wukv, gqn, gqr, gkn, gkr, tbl)


def _attn_kernel(q_ref, k_ref, vt_ref, o_ref, m_ref, l_ref, acc_ref, s_ref):
    tq = TQ
    key_c = lax.broadcasted_iota(jnp.int32, (tq, tq), 0) // CHUNK
    qry_c = lax.broadcasted_iota(jnp.int32, (tq, tq), 1) // CHUNK
    allowed = key_c <= qry_c
    for t in range(q_ref.shape[2] // tq):
        _attn_tile(pl.program_id(1) * (q_ref.shape[2] // tq) + t, slice(t * tq, (t + 1) * tq),
                   allowed, q_ref, k_ref, vt_ref, o_ref, m_ref, l_ref, acc_ref, s_ref)


def _attn_tile(i, q_rows, allowed, q_ref, k_ref, vt_ref, o_ref, m_ref, l_ref, acc_ref, s_ref):
    tq = TQ
    m_ref[...] = jnp.full(m_ref.shape, -1e30, F32)
    l_ref[...] = jnp.zeros(l_ref.shape, F32)
    acc_ref[...] = jnp.zeros(acc_ref.shape, F32)

    def scores(j, keep):
        start = pl.multiple_of(j * tq, tq)
        out = []
        for h in range(MLA_HEADS):
            kb = k_ref[0, h, pl.ds(start, tq), :]
            st = lax.dot_general(kb, q_ref[0, h, q_rows, :], (((1,), (1,)), ((), ())),
                                 preferred_element_type=F32)
            out.append(st if keep is None else jnp.where(keep, st, -1e30))
        return out

    def update(j):
        probs, alphas = [], []
        for h in range(MLA_HEADS):
            st = s_ref[h]
            m = m_ref[h]
            m_new = jnp.maximum(m, jnp.max(st, axis=0, keepdims=True))
            alpha = jnp.exp2(m - m_new)
            p = jnp.exp2(st - m_new)
            l_ref[h] = alpha * l_ref[h] + jnp.sum(p, axis=0, keepdims=True)
            m_ref[h] = m_new
            probs.append(p.astype(BF16))
            alphas.append(alpha)
        for h in range(MLA_HEADS):
            acc_ref[h] = alphas[h] * acc_ref[h] + jnp.dot(vt_ref[0, h, j], probs[h],
                                                           preferred_element_type=F32)

    def stage(sts):
        for h in range(MLA_HEADS):
            s_ref[h] = sts[h]

    stage(scores(0, allowed | (i > 0)))

    def body(j, c):
        nxt = scores(j + 1, None)
        update(j)
        stage(nxt)
        return c

    lax.fori_loop(0, i - 1, body, 0)

    @pl.when(i > 0)
    def _():
        nxt = scores(i, allowed)
        update(i - 1)
        stage(nxt)

    update(i)
    for h in range(MLA_HEADS):
        o = acc_ref[h] / l_ref[h]
        o_ref[0, q_rows, V_HEAD * h:V_HEAD * (h + 1)] = o.T.astype(BF16)


def _attn_call(q, k, vt):
    b, nh, s, _ = q.shape
    rows = TQ * Q_TILES_PER_STEP
    return pl.pallas_call(
        _attn_kernel,
        grid=(b, s // rows),
        in_specs=[
            pl.BlockSpec((1, nh, rows, QK_PAD), lambda bi, qi: (bi, 0, qi, 0)),
            pl.BlockSpec((1, nh, s, QK_PAD), lambda bi, qi: (bi, 0, 0, 0)),
            pl.BlockSpec((1, nh, s // TQ, V_HEAD, TQ), lambda bi, qi: (bi, 0, 0, 0, 0)),
        ],
        out_specs=pl.BlockSpec((1, rows, nh * V_HEAD), lambda bi, qi: (bi, qi, 0)),
        out_shape=jax.ShapeDtypeStruct((b, s, nh * V_HEAD), BF16),
        scratch_shapes=[pltpu.VMEM((nh, 1, TQ), F32), pltpu.VMEM((nh, 1, TQ), F32),
                        pltpu.VMEM((nh, V_HEAD, TQ), F32), pltpu.VMEM((nh, TQ, TQ), F32)],
        compiler_params=pltpu.CompilerParams(
            dimension_semantics=("arbitrary", "arbitrary"), vmem_limit_bytes=VMEM_LIMIT),
        name="mla_attention",
    )(q, k, vt)


def _post_kernel(x_ref, conv_ref, attn_ref, wo_ref, g_ref, wr_ref, br_ref, tri_ref,
                 h1_ref, hb_ref, lpos_ref, gate_ref, tile_n_ref, tile_carry_ref, cnt_ref,
                 carry_ref):
    step = pl.program_id(0)

    @pl.when(step == 0)
    def _():
        carry_ref[...] = jnp.zeros_like(carry_ref)

    for t in range(x_ref.shape[0] // TM_POST):
        _post_tile(t, x_ref, conv_ref, attn_ref, wo_ref, g_ref, wr_ref, br_ref, tri_ref,
                   h1_ref, hb_ref, lpos_ref, gate_ref, tile_n_ref, tile_carry_ref, carry_ref)
    cnt_ref[...] = carry_ref[...].astype(jnp.int32)


def _post_tile(t, x_ref, conv_ref, attn_ref, wo_ref, g_ref, wr_ref, br_ref, tri_ref,
               h1_ref, hb_ref, lpos_ref, gate_ref, tile_n_ref, tile_carry_ref, carry_ref):
    tm = TM_POST
    sub = tri_ref.shape[0]
    subs = [slice(r0, r0 + sub) for r0 in range(t * tm, (t + 1) * tm, sub)]
    h1s = [x_ref[rows, :]
           + jnp.dot(conv_ref[rows, :], wo_ref[0:CONV_WIDTH, :], preferred_element_type=F32)
           + jnp.dot(attn_ref[rows, :], wo_ref[CONV_WIDTH:, :], preferred_element_type=F32)
           for rows in subs]
    lane = lax.broadcasted_iota(jnp.int32, (sub, N_EXPERTS), 1).astype(F32)
    lane4 = lax.broadcasted_iota(jnp.int32, (sub, TOP_K), 1)

    def cols(c):
        return jnp.where(lane4 == 0, c[0], jnp.where(lane4 == 1, c[1],
                         jnp.where(lane4 == 2, c[2], c[3])))

    picks, members = [], []
    for rows, h1 in zip(subs, h1s):
        h1_ref[rows, :] = h1
        hb = _rms(h1, g_ref[...]).astype(BF16)
        hb_ref[rows, :] = hb
        lg = jnp.dot(hb, wr_ref[...], preferred_element_type=F32) + br_ref[...]
        member = jnp.zeros((sub, N_EXPERTS), F32)
        vals, idxs, hots = [], [], []
        for _ in range(TOP_K):
            m = jnp.max(lg, axis=-1, keepdims=True)
            ix = jnp.min(jnp.where(lg == m, lane, float(N_EXPERTS)), axis=-1, keepdims=True)
            hot = lane == ix
            lg = jnp.where(hot, -jnp.inf, lg)
            member = member + hot.astype(F32)
            vals.append(m)
            idxs.append(ix)
            hots.append(hot)
        exps = [jnp.exp(v - vals[0]) for v in vals]
        denom = exps[0] + exps[1] + exps[2] + exps[3]
        gate_ref[rows, :] = cols([e / denom for e in exps])
        picks.append((idxs, hots))
        members.append(member)

    sub_n = [jnp.sum(member, axis=0, keepdims=True) for member in members]
    tile_n = sub_n[0]
    for n_s in sub_n[1:]:
        tile_n = tile_n + n_s
    earlier = jnp.zeros((1, N_EXPERTS), F32)
    for rows, member, n_s, (idxs, hots) in zip(subs, members, sub_n, picks):
        before = jnp.dot(tri_ref[...], member.astype(BF16), preferred_element_type=F32) + earlier
        lpos = [jnp.sum(jnp.where(hot, before, 0.0) + jnp.where(lane < ix, tile_n, 0.0),
                        axis=-1, keepdims=True) for hot, ix in zip(hots, idxs)]
        lpos_ref[rows, :] = cols(lpos).astype(jnp.int32)
        earlier = earlier + n_s
    tile_n_ref[t] = tile_n.astype(jnp.int32)
    tile_carry_ref[t] = carry_ref[...].astype(jnp.int32)
    carry_ref[...] = carry_ref[...] + tile_n


def _post_call(xf, conv, attn, wo, g, wr, br, tri):
    n, d = xf.shape
    tm = TM_POST * POST_TILES_PER_STEP
    const = lambda shape: pl.BlockSpec(shape, lambda i: (0,) * len(shape))
    row = lambda w: pl.BlockSpec((tm, w), lambda i: (i, 0))
    table = pl.BlockSpec((POST_TILES_PER_STEP, 1, N_EXPERTS), lambda i: (i, 0, 0))
    return pl.pallas_call(
        _post_kernel,
        grid=(n // tm,),
        in_specs=[row(d), row(CONV_WIDTH), row(CONV_WIDTH), const((d, d)), const((1, d)),
                  const((d, N_EXPERTS)), const((1, N_EXPERTS)), const(tri.shape)],
        out_specs=[row(d), row(d), row(TOP_K), row(TOP_K), table, table,
                   const((1, N_EXPERTS))],
        out_shape=[
            jax.ShapeDtypeStruct((n, d), F32),
            jax.ShapeDtypeStruct((n, d), BF16),
            jax.ShapeDtypeStruct((n, TOP_K), jnp.int32),
            jax.ShapeDtypeStruct((n, TOP_K), F32),
            jax.ShapeDtypeStruct((n // TM_POST, 1, N_EXPERTS), jnp.int32),
            jax.ShapeDtypeStruct((n // TM_POST, 1, N_EXPERTS), jnp.int32),
            jax.ShapeDtypeStruct((1, N_EXPERTS), jnp.int32),
        ],
        scratch_shapes=[pltpu.VMEM((1, N_EXPERTS), F32)],
        compiler_params=pltpu.CompilerParams(
            dimension_semantics=("arbitrary",), vmem_limit_bytes=VMEM_LIMIT),
        name="post_mixer_router",
    )(xf, conv, attn, wo, g, wr, br, tri)


RUN_BITS = 10


def _rows(ref, first_row, n_rows):
    return ref.at[pl.ds(pl.multiple_of(first_row * ROW_LINES, ROW_LINES), n_rows * ROW_LINES), :]


def _for_each_run_piece(n_ref, far_ref, fn, bits=RUN_BITS):
    def per_expert(e, local0):
        n_e = n_ref[0, 0, e]
        far0 = far_ref[0, 0, e]

        for b in reversed(range(bits)):
            done = (n_e >> (b + 1)) << (b + 1)

            @pl.when(((n_e >> b) & 1) == 1)
            def _():
                fn(local0 + done, far0 + done, 1 << b)
        return local0 + n_e

    lax.fori_loop(0, N_EXPERTS, per_expert, 0)


PAD_BITS = TM_EXP.bit_length() - 1
assert TM_EXP == 1 << PAD_BITS


def _scatter_kernel(n_ref, far_ref, pad_n_ref, pad_far_ref, used_ref, hb_ref, lpos_ref, xs_ref,
                    xl, zeros, sem, zsem):
    i = pl.program_id(0)
    steps = pl.num_programs(0)
    tm = hb_ref.shape[0]
    pairs = tm * TOP_K
    slot = i % 2

    def wait_slot(s):
        pltpu.make_async_copy(xl.at[s], _rows(xs_ref, 0, pairs), sem.at[s]).wait()

    def fill(_, far_row, size):
        return pltpu.make_async_copy(_rows(zeros, 0, size), _rows(xs_ref, far_row, size), zsem)

    def zero_fills(op):
        _for_each_run_piece(pad_n_ref, pad_far_ref, lambda *a: op(fill(*a)), PAD_BITS)
        tile_rows = zeros.shape[0] // ROW_LINES
        n_tiles = xs_ref.shape[0] // zeros.shape[0]

        def body(t, c):
            op(fill(None, t * tile_rows, tile_rows))
            return c
        lax.fori_loop(used_ref[0], n_tiles, body, 0)

    @pl.when(i == 0)
    def _():
        zeros[...] = jnp.zeros(zeros.shape, F32)
        zero_fills(lambda cp: cp.start())

    @pl.when(i >= 2)
    def _():
        wait_slot(slot)

    lane = lax.broadcasted_iota(jnp.int32, (tm, 128), 1)
    lp = lpos_ref[...].astype(F32)
    cols = jnp.zeros((tm, 128), F32)
    for k in range(TOP_K):
        cols = jnp.where(lane == k, lp[:, k:k + 1], cols)
    rows_t = cols.T.astype(jnp.int16)
    j = lax.broadcasted_iota(jnp.int32, (pairs, tm), 0).astype(jnp.int16)
    onehot = jnp.zeros((pairs, tm), BF16)
    for k in range(TOP_K):
        onehot = jnp.where(j == rows_t[k:k + 1, :], jnp.ones((), BF16), onehot)
    _store_row_tiles(xl.at[slot], jnp.dot(onehot, hb_ref[...], preferred_element_type=F32))

    def send(local_row, far_row, size):
        pltpu.make_async_copy(_rows(xl.at[slot], local_row, size), _rows(xs_ref, far_row, size),
                              sem.at[slot]).start()

    _for_each_run_piece(n_ref, far_ref, send)

    @pl.when(i == steps - 1)
    def _():
        @pl.when(steps >= 2)
        def _():
            wait_slot(1 - slot)
        wait_slot(slot)
        zero_fills(lambda cp: cp.wait())


def _scatter_call(tile_n, tile_far, pad_n, pad_far, used_tiles, hb, lpos, sorted_rows):
    n, d = hb.shape
    tm = TM_POST
    table = pl.BlockSpec((1, 1, N_EXPERTS), lambda i: (i, 0, 0), memory_space=pltpu.SMEM)
    whole = pl.BlockSpec((1, 1, N_EXPERTS), lambda i: (0, 0, 0), memory_space=pltpu.SMEM)
    return pl.pallas_call(
        _scatter_kernel,
        grid=(n // tm,),
        in_specs=[table, table, whole, whole, pl.BlockSpec(memory_space=pltpu.SMEM),
                  pl.BlockSpec((tm, d), lambda i: (i, 0)),
                  pl.BlockSpec((tm, TOP_K), lambda i: (i, 0))],
        out_specs=pl.BlockSpec(memory_space=pl.ANY),
        out_shape=jax.ShapeDtypeStruct((sorted_rows * ROW_LINES, 128), F32),
        scratch_shapes=[pltpu.VMEM((2, tm * TOP_K * ROW_LINES, 128), F32),
                        pltpu.VMEM((TM_EXP * ROW_LINES, 128), F32),
                        pltpu.SemaphoreType.DMA((2,)), pltpu.SemaphoreType.DMA],
        compiler_params=pltpu.CompilerParams(
            dimension_semantics=("arbitrary",), vmem_limit_bytes=VMEM_LIMIT),
        name="dispatch_scatter",
    )(tile_n, tile_far, pad_n, pad_far, used_tiles, hb, lpos)


def _expert_kernel(tile_ref, exp_ref, flag_ref, next_ref,
                   xs_ref, wgu_hbm, bgu_ref, wd_hbm, bd_ref, o_ref,
                   wgu_f, wd_f, wgu_bf, wd_bf, wsem):
    i = pl.program_id(0)
    tm = TM_EXP
    tile_lines = tm * ROW_LINES
    cast_rows = 128

    def weight_copies(e, s):
        return (pltpu.make_async_copy(wgu_hbm.at[e], wgu_f.at[s], wsem.at[s, 0]),
                pltpu.make_async_copy(wd_hbm.at[e], wd_f.at[s], wsem.at[s, 1]))

    @pl.when(i == 0)
    def _():
        for cp in weight_copies(exp_ref[0], 0):
            cp.start()

    for t in range(EXP_TILES_PER_STEP):
        g = i * EXP_TILES_PER_STEP + t
        flag = flag_ref[g]
        e = exp_ref[g]
        x_t = xs_ref.at[pl.ds(t * tile_lines, tile_lines), :]
        o_t = o_ref.at[pl.ds(t * tile_lines, tile_lines), :]

        @pl.when((flag & 2) != 0)
        def _():
            s = (flag >> 3) & 1
            for cp in weight_copies(e, s):
                cp.wait()
            nxt = next_ref[g]

            @pl.when(nxt >= 0)
            def _():
                for cp in weight_copies(nxt, 1 - s):
                    cp.start()

            def cast(r, c):
                rows = pl.ds(pl.multiple_of(r * cast_rows, cast_rows), cast_rows)
                wgu_bf[rows, :] = wgu_f[s, rows, :].astype(BF16)
                wd_bf[rows, :] = wd_f[s, rows, :].astype(BF16)
                return c
            lax.fori_loop(0, D_MODEL // cast_rows, cast, 0)

        def run(rows):
            x = _load_row_tiles(x_t, rows).astype(BF16)
            gu = jnp.dot(x, wgu_bf[...], preferred_element_type=F32) + bgu_ref[e]
            gate = jnp.minimum(gu[:, :D_FF], SWIGLU_LIMIT)
            up = jnp.clip(gu[:, D_FF:], -SWIGLU_LIMIT, SWIGLU_LIMIT)
            act = (up + 1.0) * (gate * (1.0 / (1.0 + jnp.exp(-SWIGLU_ALPHA * gate))))
            out = jnp.dot(act.astype(BF16), wd_bf[...], preferred_element_type=F32) + bd_ref[e]
            _store_row_tiles(o_t, out)

        @pl.when((flag & 5) == 1)
        def _():
            run(tm)

        @pl.when((flag & 5) == 5)
        def _():
            run(tm // 2)
            o_t[tm // 2 * ROW_LINES:, :] = jnp.zeros((tm // 2 * ROW_LINES, 128), F32)

        @pl.when((flag & 16) != 0)
        def _():
            o_t[...] = jnp.zeros((tile_lines, 128), F32)


def _expert_call(meta, xs, wgu, bgu, wd, bd):
    lines, w = xs.shape
    tm = TM_EXP * EXP_TILES_PER_STEP
    n_items = meta[0].shape[0]
    whole = lambda a: pl.BlockSpec(a.shape, lambda i, *_: (0,) * a.ndim)
    return pl.pallas_call(
        _expert_kernel,
        grid_spec=pltpu.PrefetchScalarGridSpec(
            num_scalar_prefetch=4,
            grid=(n_items,),
            in_specs=[
                pl.BlockSpec((tm * ROW_LINES, w), lambda i, t, e, f, nx: (t[i], 0)),
                pl.BlockSpec(memory_space=pl.ANY), whole(bgu),
                pl.BlockSpec(memory_space=pl.ANY), whole(bd),
            ],
            out_specs=pl.BlockSpec((tm * ROW_LINES, w), lambda i, t, e, f, nx: (t[i], 0)),
            scratch_shapes=[pltpu.VMEM((2, D_MODEL, 2 * D_FF), F32),
                            pltpu.VMEM((2, D_FF, D_MODEL), F32),
                            pltpu.VMEM((D_MODEL, 2 * D_FF), BF16),
                            pltpu.VMEM((D_FF, D_MODEL), BF16),
                            pltpu.SemaphoreType.DMA((2, 2))],
        ),
        out_shape=jax.ShapeDtypeStruct((lines, w), F32),
        input_output_aliases={len(meta): 0},
        compiler_params=pltpu.CompilerParams(
            dimension_semantics=("arbitrary",), vmem_limit_bytes=VMEM_LIMIT),
        name="moe_experts",
    )(*meta, xs, wgu, bgu, wd, bd)


def _expert_layout(counts, n_pairs, tm):
    n_tiles = n_pairs // tm + N_EXPERTS
    assert n_tiles % EXP_TILES_PER_STEP == 0
    per_exp = (counts + tm - 1) // tm
    tile_end = jnp.cumsum(per_exp)
    tile0 = tile_end - per_exp
    used = tile_end[-1]
    step = jnp.arange(n_tiles, dtype=jnp.int32)
    tile = jnp.minimum(step, used - 1)
    e = jnp.sum((tile_end[None, :] <= tile[:, None]).astype(jnp.int32), axis=1)
    hot = e[:, None] == jnp.arange(N_EXPERTS, dtype=jnp.int32)[None, :]
    pick = lambda tab: jnp.sum(jnp.where(hot, tab[None, :], 0), axis=1)
    e_tile0 = pick(tile0)
    rows_used = pick(counts) - (step - e_tile0) * tm
    experts = jnp.arange(N_EXPERTS, dtype=jnp.int32)
    has_tiles = per_exp > 0
    slot_of_expert = (jnp.cumsum(has_tiles.astype(jnp.int32)) - 1) & 1
    later = (experts[None, :] > experts[:, None]) & has_tiles[None, :]
    next_of_expert = jnp.min(jnp.where(later, experts[None, :], N_EXPERTS), axis=1)
    next_of_expert = jnp.where(next_of_expert < N_EXPERTS, next_of_expert, -1)
    last_block = (used - 1) // EXP_TILES_PER_STEP
    flag = jnp.where(step < used, 1 + 2 * (step == e_tile0) + 4 * (rows_used <= tm // 2)
                     + 8 * pick(slot_of_expert),
                     jnp.where(step // EXP_TILES_PER_STEP == last_block, 16, 0))
    cast = lambda a: a.astype(jnp.int32)
    block = jnp.minimum(jnp.arange(n_tiles // EXP_TILES_PER_STEP, dtype=jnp.int32), last_block)
    group_start = tile0 * tm
    pad_n = per_exp * tm - counts
    pad_start = group_start + counts
    return ((cast(block), cast(e), cast(flag), cast(pick(next_of_expert))), cast(used).reshape(1),
            cast(group_start), cast(pad_n), cast(pad_start))


def _combine_kernel(n_ref, far_ref, n_next_ref, far_next_ref, h1_ref, lpos_ref, gate_ref, ys_ref,
                    o_ref, yl, sem):
    i = pl.program_id(0)
    steps = pl.num_programs(0)
    tm = h1_ref.shape[0]
    pairs = tm * TOP_K
    slot = i % 2

    def fetch(nr, fr, s):
        def recv(local_row, far_row, size):
            pltpu.make_async_copy(_rows(ys_ref, far_row, size), _rows(yl.at[s], local_row, size),
                                  sem.at[s]).start()
        _for_each_run_piece(nr, fr, recv)

    @pl.when(i == 0)
    def _():
        fetch(n_ref, far_ref, 0)

    @pl.when(i + 1 < steps)
    def _():
        fetch(n_next_ref, far_next_ref, 1 - slot)

    pltpu.make_async_copy(_rows(ys_ref, 0, pairs), yl.at[slot], sem.at[slot]).wait()

    j = lax.broadcasted_iota(jnp.int32, (tm, pairs), 1).astype(jnp.int16)
    lp = lpos_ref[...].astype(jnp.int16)
    g = gate_ref[...].astype(BF16)
    w = jnp.zeros((tm, pairs), BF16)
    for k in range(TOP_K):
        w = jnp.where(j == lp[:, k:k + 1], g[:, k:k + 1], w)
    y = _load_row_tiles(yl, pairs, lead=(slot,)).astype(BF16)
    o_ref[...] = h1_ref[...] + jnp.dot(w, y, preferred_element_type=F32)


def _combine_call(tile_n, tile_far, h1, lpos, gates, ys):
    n, d = h1.shape
    tm = TM_POST
    steps = n // tm
    table = lambda fn: pl.BlockSpec((1, 1, N_EXPERTS), fn, memory_space=pltpu.SMEM)
    cur = lambda i: (i, 0, 0)
    nxt = lambda i: (jnp.minimum(i + 1, steps - 1), 0, 0)
    return pl.pallas_call(
        _combine_kernel,
        grid=(steps,),
        in_specs=[table(cur), table(cur), table(nxt), table(nxt),
                  pl.BlockSpec((tm, d), lambda i: (i, 0)),
                  pl.BlockSpec((tm, TOP_K), lambda i: (i, 0)),
                  pl.BlockSpec((tm, TOP_K), lambda i: (i, 0)),
                  pl.BlockSpec(memory_space=pl.ANY)],
        out_specs=pl.BlockSpec((tm, d), lambda i: (i, 0)),
        out_shape=jax.ShapeDtypeStruct((n, d), F32),
        scratch_shapes=[pltpu.VMEM((2, tm * TOP_K * ROW_LINES, 128), F32),
                        pltpu.SemaphoreType.DMA((2,))],
        compiler_params=pltpu.CompilerParams(
            dimension_semantics=("arbitrary",), vmem_limit_bytes=VMEM_LIMIT),
        name="moe_combine",
    )(tile_n, tile_far, tile_n, tile_far, h1, lpos, gates, ys)


def _rotate_half_perm():
    half = QK_ROPE // 2
    perm = np.concatenate([np.arange(half, QK_ROPE), np.arange(0, half)])
    sign = np.concatenate([-np.ones(half), np.ones(half)]).astype(np.float32)
    return perm, sign


def _layer(h, l, norm_mix, w_in, conv_w, q_norm, w_uq, kv_norm, w_ukv, q_head_norm,
           k_head_norm, w_o, norm_ffn, w_router, b_router, w_gate_up, b_gate_up,
           w_down, b_down):
    b, s, d = h.shape
    n = b * s
    perm, sign = _rotate_half_perm()

    kpe_cols = w_in[l][:, OFF_KPE:OFF_KPE + QK_ROPE]
    win = jnp.concatenate([w_in[l], kpe_cols[:, perm] * sign], axis=1).astype(BF16)
    wq = w_uq[l].reshape(Q_LORA, MLA_HEADS, QK_HEAD)
    wq_rope = wq[:, :, QK_NOPE:]
    wuq = jnp.concatenate(
        [wq[:, :, :QK_NOPE].reshape(Q_LORA, MLA_HEADS * QK_NOPE),
         jnp.concatenate([wq_rope, wq_rope[:, :, perm] * sign], axis=2)
         .reshape(Q_LORA, MLA_HEADS * 2 * QK_ROPE)], axis=1).astype(BF16)
    gq, gk = q_head_norm[l], k_head_norm[l]
    gqn, gkn = gq[None, :QK_NOPE], gk[None, :QK_NOPE]
    gqr = jnp.concatenate([gq[QK_NOPE:], gq[QK_NOPE:][perm]])[None, :]
    gkr = jnp.concatenate([gk[QK_NOPE:], gk[QK_NOPE:][perm]])[None, :]

    half = QK_ROPE // 2
    inv_freq = ROPE_THETA ** (-jnp.arange(half, dtype=F32) / half)
    ang = jnp.arange(s).astype(F32)[:, None] * inv_freq[None, :]
    cos, sin = jnp.cos(ang), jnp.sin(ang)
    tbl = jnp.concatenate([cos, cos, sin, sin], axis=1)

    conv, q, k, vt = _pre_call(
        h, norm_mix[l][None, :], win, conv_w[l], q_norm[l][None, :], wuq,
        kv_norm[l][None, :], w_ukv[l].astype(BF16), gqn, gqr, gkn, gkr, tbl)
    attn = _attn_call(q, k, vt)

    tri = jnp.tril(jnp.ones((SUB_POST, SUB_POST), BF16), -1)
    h1, hb, lpos, gates, tile_n, tile_carry, counts = _post_call(
        h.reshape(n, d), conv.reshape(n, CONV_WIDTH), attn.reshape(n, CONV_WIDTH),
        w_o[l].astype(BF16), norm_ffn[l][None, :], w_router[l].astype(BF16),
        b_router[l][None, :], tri)

    meta, used_tiles, group_start, pad_n, pad_start = _expert_layout(counts[0], n * TOP_K, TM_EXP)
    sorted_rows = n * TOP_K + N_EXPERTS * TM_EXP
    tile_far = (tile_carry + group_start[None, None, :]).astype(jnp.int32)
    xs = _scatter_call(tile_n, tile_far, pad_n[None, None, :], pad_start[None, None, :],
                       used_tiles, hb, lpos, sorted_rows)
    ys = _expert_call(meta, xs, w_gate_up[l], b_gate_up[l][:, None, :],
                      w_down[l], b_down[l][:, None, :])
    out = _combine_call(tile_n, tile_far, h1, lpos, gates, ys)
    return out.reshape(b, s, d)


def kernel(x, norm_mix, w_in, conv_w, q_norm, w_uq, kv_norm, w_ukv, q_head_norm, k_head_norm,
           w_o, norm_ffn, w_router, b_router, w_gate_up, b_gate_up, w_down, b_down):
    h = x
    for l in range(norm_mix.shape[0]):
        h = _layer(h, l, norm_mix, w_in, conv_w, q_norm, w_uq, kv_norm, w_ukv, q_head_norm,
                   k_head_norm, w_o, norm_ffn, w_router, b_router, w_gate_up, b_gate_up,
                   w_down, b_down)
    return h
```

```python
import math

import jax
import jax.numpy as jnp
import numpy as np
from jax import lax
from jax.experimental import pallas as pl
from jax.experimental.pallas import tpu as pltpu

F32 = jnp.float32
BF16 = jnp.bfloat16

D_MODEL = 1024
CONV_WIDTH = 512
MLA_HEADS = 4
QK_NOPE = 128
QK_ROPE = 64
QK_HEAD = QK_NOPE + QK_ROPE
QK_PAD = 256
V_HEAD = 128
Q_LORA = 384
KV_LORA = 256
ROPE_THETA = 10000.0
CHUNK = 64
N_EXPERTS = 32
TOP_K = 4
D_FF = 1024
SWIGLU_LIMIT = 7.0
SWIGLU_ALPHA = 1.702
EPS = 1e-6

OFF_B, OFF_C, OFF_X, OFF_CQ, OFF_CKV, OFF_KPE = 0, 512, 1024, 1536, 1920, 2176
IN_PROJ_EXT = 2304

VMEM_LIMIT = 60 * 1024 * 1024

TS_PRE = 1024
TQ = 256
Q_TILES_PER_STEP = 8
TM_POST = 512
SUB_POST = 256
POST_TILES_PER_STEP = 2
TM_EXP = 512
EXP_TILES_PER_STEP = 3
HALO = 8


def _rms(x, g):
    return x * lax.rsqrt(jnp.mean(x * x, axis=-1, keepdims=True) + EPS) * g


ROW_LINES = D_MODEL // 128


def _load_row_tiles(ref, rows, lead=()):
    return jnp.concatenate(
        [ref[lead + (pl.ds(c, rows, stride=ROW_LINES), slice(None))] for c in range(ROW_LINES)],
        axis=1)


def _store_row_tiles(ref, val):
    rows = val.shape[0]
    for c in range(ROW_LINES):
        ref[pl.ds(c, rows, stride=ROW_LINES), :] = val[:, 128 * c:128 * (c + 1)]


def _pre_kernel(x_ref, nm_ref, win_ref, convw_ref, qn_ref, wuq_ref, kvn_ref, wukv_ref,
                gqn_ref, gqr_ref, gkn_ref, gkr_ref, tbl_ref,
                conv_ref, q_ref, k_ref, vt_ref, ext_ref):
    s = pl.program_id(1)
    ts = x_ref.shape[1]
    sub = TQ
    starts = tuple(range(0, ts, sub))

    @pl.when(s == 0)
    def _():
        ext_ref[0:HALO, :] = jnp.zeros((HALO, CONV_WIDTH), F32)

    @pl.when(s != 0)
    def _():
        ext_ref[0:HALO, :] = ext_ref[ts:ts + HALO, :]

    projs = [jnp.dot(_rms(x_ref[0, r0:r0 + sub, :], nm_ref[...]).astype(BF16), win_ref[...],
                     preferred_element_type=F32) for r0 in starts]

    lane = lax.broadcasted_iota(jnp.int32, (sub, 128), 1)
    first_half = lane < QK_ROPE
    scale = math.log2(math.e) / math.sqrt(QK_HEAD)

    for r0, proj in zip(starts, projs):
        rows = slice(r0, r0 + sub)
        u = proj[:, OFF_C:OFF_C + CONV_WIDTH] * proj[:, OFF_X:OFF_X + CONV_WIDTH]
        ext_ref[HALO + r0:HALO + r0 + sub, :] = u
        y = (convw_ref[2:3, :] * u
             + convw_ref[1:2, :] * ext_ref[HALO + r0 - 1:HALO + r0 - 1 + sub, :]
             + convw_ref[0:1, :] * ext_ref[HALO + r0 - 2:HALO + r0 - 2 + sub, :])
        conv_ref[0, rows, :] = (proj[:, OFF_B:OFF_B + CONV_WIDTH] * y).astype(BF16)

        hq = _rms(proj[:, OFF_CQ:OFF_CQ + Q_LORA], qn_ref[...]).astype(BF16)
        qp = jnp.dot(hq, wuq_ref[...], preferred_element_type=F32)
        hkv = _rms(proj[:, OFF_CKV:OFF_CKV + KV_LORA], kvn_ref[...]).astype(BF16)
        kvp = jnp.dot(hkv, wukv_ref[...], preferred_element_type=F32)

        tbl = tbl_ref[rows, :]

        def rope_block(grp, gain):
            r = grp * tbl * gain
            r = r + pltpu.roll(r, QK_ROPE, axis=1)
            return jnp.where(first_half, r, 0.0)

        kpe = proj[:, OFF_KPE:OFF_KPE + 128]
        ss_kpe = jnp.sum(jnp.where(first_half, kpe * kpe, 0.0), axis=-1, keepdims=True)
        k_rope = rope_block(kpe, gkr_ref[...])
        for h in range(MLA_HEADS):
            qn = qp[:, 128 * h:128 * (h + 1)]
            qg = qp[:, 512 + 128 * h:512 + 128 * (h + 1)]
            ss = (jnp.sum(qn * qn, axis=-1, keepdims=True)
                  + jnp.sum(jnp.where(first_half, qg * qg, 0.0), axis=-1, keepdims=True))
            rs = lax.rsqrt(ss * (1.0 / QK_HEAD) + EPS) * scale
            q_ref[0, h, rows, 0:128] = (qn * rs * gqn_ref[...]).astype(BF16)
            q_ref[0, h, rows, 128:256] = (rope_block(qg, gqr_ref[...]) * rs).astype(BF16)

            kn = kvp[:, 256 * h:256 * h + 128]
            ssk = jnp.sum(kn * kn, axis=-1, keepdims=True) + ss_kpe
            rsk = lax.rsqrt(ssk * (1.0 / QK_HEAD) + EPS)
            k_ref[0, h, rows, 0:128] = (kn * rsk * gkn_ref[...]).astype(BF16)
            k_ref[0, h, rows, 128:256] = (k_rope * rsk).astype(BF16)
            v = kvp[:, 256 * h + 128:256 * (h + 1)]
            vt_ref[0, h, r0 // TQ] = v.T.astype(BF16)


def _pre_call(x, nm, win, convw, qn, wuq, kvn, wukv, gqn, gqr, gkn, gkr, tbl):
    b, s, d = x.shape
    ts = TS_PRE
    const = lambda shape: pl.BlockSpec(shape, lambda bi, si: (0,) * len(shape))
    return pl.pallas_call(
        _pre_kernel,
        grid=(b, s // ts),
        in_specs=[
            pl.BlockSpec((1, ts, d), lambda bi, si: (bi, si, 0)),
            const((1, d)), const((d, IN_PROJ_EXT)), const((3, CONV_WIDTH)),
            const((1, Q_LORA)), const((Q_LORA, 1024)),
            const((1, KV_LORA)), const((KV_LORA, 1024)),
            const((1, 128)), const((1, 128)), const((1, 128)), const((1, 128)),
            pl.BlockSpec((ts, 128), lambda bi, si: (si, 0)),
        ],
        out_specs=[
            pl.BlockSpec((1, ts, CONV_WIDTH), lambda bi, si: (bi, si, 0)),
            pl.BlockSpec((1, MLA_HEADS, ts, QK_PAD), lambda bi, si: (bi, 0, si, 0)),
            pl.BlockSpec((1, MLA_HEADS, ts, QK_PAD), lambda bi, si: (bi, 0, si, 0)),
            pl.BlockSpec((1, MLA_HEADS, ts // TQ, V_HEAD, TQ), lambda bi, si: (bi, 0, si, 0, 0)),
        ],
        out_shape=[
            jax.ShapeDtypeStruct((b, s, CONV_WIDTH), BF16),
            jax.ShapeDtypeStruct((b, MLA_HEADS, s, QK_PAD), BF16),
            jax.ShapeDtypeStruct((b, MLA_HEADS, s, QK_PAD), BF16),
            jax.ShapeDtypeStruct((b, MLA_HEADS, s // TQ, V_HEAD, TQ), BF16),
        ],
        scratch_shapes=[pltpu.VMEM((ts + HALO, CONV_WIDTH), F32)],
        compiler_params=pltpu.CompilerParams(
            dimension_semantics=("arbitrary", "arbitrary"), vmem_limit_bytes=VMEM_LIMIT),
        name="pre_mixer",
    )(x, nm, win, convw, qn, wuq, kvn, wukv, gqn, gqr, gkn, gkr, tbl)


def _attn_kernel(q_ref, k_ref, vt_ref, o_ref, m_ref, l_ref, acc_ref, s_ref):
    tq = TQ
    key_c = lax.broadcasted_iota(jnp.int32, (tq, tq), 0) // CHUNK
    qry_c = lax.broadcasted_iota(jnp.int32, (tq, tq), 1) // CHUNK
    allowed = key_c <= qry_c
    for t in range(q_ref.shape[2] // tq):
        _attn_tile(pl.program_id(1) * (q_ref.shape[2] // tq) + t, slice(t * tq, (t + 1) * tq),
                   allowed, q_ref, k_ref, vt_ref, o_ref, m_ref, l_ref, acc_ref, s_ref)


def _attn_tile(i, q_rows, allowed, q_ref, k_ref, vt_ref, o_ref, m_ref, l_ref, acc_ref, s_ref):
    tq = TQ
    m_ref[...] = jnp.full(m_ref.shape, -1e30, F32)
    l_ref[...] = jnp.zeros(l_ref.shape, F32)
    acc_ref[...] = jnp.zeros(acc_ref.shape, F32)

    def scores(j, keep):
        start = pl.multiple_of(j * tq, tq)
        out = []
        for h in range(MLA_HEADS):
            kb = k_ref[0, h, pl.ds(start, tq), :]
            st = lax.dot_general(kb, q_ref[0, h, q_rows, :], (((1,), (1,)), ((), ())),
                                 preferred_element_type=F32)
            out.append(st if keep is None else jnp.where(keep, st, -1e30))
        return out

    def update(j):
        probs, alphas = [], []
        for h in range(MLA_HEADS):
            st = s_ref[h]
            m = m_ref[h]
            m_new = jnp.maximum(m, jnp.max(st, axis=0, keepdims=True))
            alpha = jnp.exp2(m - m_new)
            p = jnp.exp2(st - m_new)
            l_ref[h] = alpha * l_ref[h] + jnp.sum(p, axis=0, keepdims=True)
            m_ref[h] = m_new
            probs.append(p.astype(BF16))
            alphas.append(alpha)
        for h in range(MLA_HEADS):
            acc_ref[h] = alphas[h] * acc_ref[h] + jnp.dot(vt_ref[0, h, j], probs[h],
                                                           preferred_element_type=F32)

    def stage(sts):
        for h in range(MLA_HEADS):
            s_ref[h] = sts[h]

    stage(scores(0, allowed | (i > 0)))

    def body(j, c):
        nxt = scores(j + 1, None)
        update(j)
        stage(nxt)
        return c

    lax.fori_loop(0, i - 1, body, 0)

    @pl.when(i > 0)
    def _():
        nxt = scores(i, allowed)
        update(i - 1)
        stage(nxt)

    update(i)
    for h in range(MLA_HEADS):
        o = acc_ref[h] / l_ref[h]
        o_ref[0, q_rows, V_HEAD * h:V_HEAD * (h + 1)] = o.T.astype(BF16)


def _attn_call(q, k, vt):
    b, nh, s, _ = q.shape
    rows = TQ * Q_TILES_PER_STEP
    return pl.pallas_call(
        _attn_kernel,
        grid=(b, s // rows),
        in_specs=[
            pl.BlockSpec((1, nh, rows, QK_PAD), lambda bi, qi: (bi, 0, qi, 0)),
            pl.BlockSpec((1, nh, s, QK_PAD), lambda bi, qi: (bi, 0, 0, 0)),
            pl.BlockSpec((1, nh, s // TQ, V_HEAD, TQ), lambda bi, qi: (bi, 0, 0, 0, 0)),
        ],
        out_specs=pl.BlockSpec((1, rows, nh * V_HEAD), lambda bi, qi: (bi, qi, 0)),
        out_shape=jax.ShapeDtypeStruct((b, s, nh * V_HEAD), BF16),
        scratch_shapes=[pltpu.VMEM((nh, 1, TQ), F32), pltpu.VMEM((nh, 1, TQ), F32),
                        pltpu.VMEM((nh, V_HEAD, TQ), F32), pltpu.VMEM((nh, TQ, TQ), F32)],
        compiler_params=pltpu.CompilerParams(
            dimension_semantics=("arbitrary", "arbitrary"), vmem_limit_bytes=VMEM_LIMIT),
        name="mla_attention",
    )(q, k, vt)


def _post_kernel(x_ref, conv_ref, attn_ref, wo_ref, g_ref, wr_ref, br_ref, tri_ref,
                 h1_ref, hb_ref, lpos_ref, gate_ref, tile_n_ref, tile_carry_ref, cnt_ref,
                 carry_ref):
    step = pl.program_id(0)

    @pl.when(step == 0)
    def _():
        carry_ref[...] = jnp.zeros_like(carry_ref)

    for t in range(x_ref.shape[0] // TM_POST):
        _post_tile(t, x_ref, conv_ref, attn_ref, wo_ref, g_ref, wr_ref, br_ref, tri_ref,
                   h1_ref, hb_ref, lpos_ref, gate_ref, tile_n_ref, tile_carry_ref, carry_ref)
    cnt_ref[...] = carry_ref[...].astype(jnp.int32)


def _post_tile(t, x_ref, conv_ref, attn_ref, wo_ref, g_ref, wr_ref, br_ref, tri_ref,
               h1_ref, hb_ref, lpos_ref, gate_ref, tile_n_ref, tile_carry_ref, carry_ref):
    tm = TM_POST
    sub = tri_ref.shape[0]
    subs = [slice(r0, r0 + sub) for r0 in range(t * tm, (t + 1) * tm, sub)]
    h1s = [x_ref[rows, :]
           + jnp.dot(conv_ref[rows, :], wo_ref[0:CONV_WIDTH, :], preferred_element_type=F32)
           + jnp.dot(attn_ref[rows, :], wo_ref[CONV_WIDTH:, :], preferred_element_type=F32)
           for rows in subs]
    lane = lax.broadcasted_iota(jnp.int32, (sub, N_EXPERTS), 1).astype(F32)
    lane4 = lax.broadcasted_iota(jnp.int32, (sub, TOP_K), 1)

    def cols(c):
        return jnp.where(lane4 == 0, c[0], jnp.where(lane4 == 1, c[1],
                         jnp.where(lane4 == 2, c[2], c[3])))

    picks, members = [], []
    for rows, h1 in zip(subs, h1s):
        h1_ref[rows, :] = h1
        hb = _rms(h1, g_ref[...]).astype(BF16)
        hb_ref[rows, :] = hb
        lg = jnp.dot(hb, wr_ref[...], preferred_element_type=F32) + br_ref[...]
        member = jnp.zeros((sub, N_EXPERTS), F32)
        vals, idxs, hots = [], [], []
        for _ in range(TOP_K):
            m = jnp.max(lg, axis=-1, keepdims=True)
            ix = jnp.min(jnp.where(lg == m, lane, float(N_EXPERTS)), axis=-1, keepdims=True)
            hot = lane == ix
            lg = jnp.where(hot, -jnp.inf, lg)
            member = member + hot.astype(F32)
            vals.append(m)
            idxs.append(ix)
            hots.append(hot)
        exps = [jnp.exp(v - vals[0]) for v in vals]
        denom = exps[0] + exps[1] + exps[2] + exps[3]
        gate_ref[rows, :] = cols([e / denom for e in exps])
        picks.append((idxs, hots))
        members.append(member)

    sub_n = [jnp.sum(member, axis=0, keepdims=True) for member in members]
    tile_n = sub_n[0]
    for n_s in sub_n[1:]:
        tile_n = tile_n + n_s
    earlier = jnp.zeros((1, N_EXPERTS), F32)
    for rows, member, n_s, (idxs, hots) in zip(subs, members, sub_n, picks):
        before = jnp.dot(tri_ref[...], member.astype(BF16), preferred_element_type=F32) + earlier
        lpos = [jnp.sum(jnp.where(hot, before, 0.0) + jnp.where(lane < ix, tile_n, 0.0),
                        axis=-1, keepdims=True) for hot, ix in zip(hots, idxs)]
        lpos_ref[rows, :] = cols(lpos).astype(jnp.int32)
        earlier = earlier + n_s
    tile_n_ref[t] = tile_n.astype(jnp.int32)
    tile_carry_ref[t] = carry_ref[...].astype(jnp.int32)
    carry_ref[...] = carry_ref[...] + tile_n


def _post_call(xf, conv, attn, wo, g, wr, br, tri):
    n, d = xf.shape
    tm = TM_POST * POST_TILES_PER_STEP
    const = lambda shape: pl.BlockSpec(shape, lambda i: (0,) * len(shape))
    row = lambda w: pl.BlockSpec((tm, w), lambda i: (i, 0))
    table = pl.BlockSpec((POST_TILES_PER_STEP, 1, N_EXPERTS), lambda i: (i, 0, 0))
    return pl.pallas_call(
        _post_kernel,
        grid=(n // tm,),
        in_specs=[row(d), row(CONV_WIDTH), row(CONV_WIDTH), const((d, d)), const((1, d)),
                  const((d, N_EXPERTS)), const((1, N_EXPERTS)), const(tri.shape)],
        out_specs=[row(d), row(d), row(TOP_K), row(TOP_K), table, table,
                   const((1, N_EXPERTS))],
        out_shape=[
            jax.ShapeDtypeStruct((n, d), F32),
            jax.ShapeDtypeStruct((n, d), BF16),
            jax.ShapeDtypeStruct((n, TOP_K), jnp.int32),
            jax.ShapeDtypeStruct((n, TOP_K), F32),
            jax.ShapeDtypeStruct((n // TM_POST, 1, N_EXPERTS), jnp.int32),
            jax.ShapeDtypeStruct((n // TM_POST, 1, N_EXPERTS), jnp.int32),
            jax.ShapeDtypeStruct((1, N_EXPERTS), jnp.int32),
        ],
        scratch_shapes=[pltpu.VMEM((1, N_EXPERTS), F32)],
        compiler_params=pltpu.CompilerParams(
            dimension_semantics=("arbitrary",), vmem_limit_bytes=VMEM_LIMIT),
        name="post_mixer_router",
    )(xf, conv, attn, wo, g, wr, br, tri)


RUN_BITS = 10


def _rows(ref, first_row, n_rows):
    return ref.at[pl.ds(pl.multiple_of(first_row * ROW_LINES, ROW_LINES), n_rows * ROW_LINES), :]


def _for_each_run_piece(n_ref, far_ref, fn, bits=RUN_BITS):
    def per_expert(e, local0):
        n_e = n_ref[0, 0, e]
        far0 = far_ref[0, 0, e]

        for b in reversed(range(bits)):
            done = (n_e >> (b + 1)) << (b + 1)

            @pl.when(((n_e >> b) & 1) == 1)
            def _():
                fn(local0 + done, far0 + done, 1 << b)
        return local0 + n_e

    lax.fori_loop(0, N_EXPERTS, per_expert, 0)


PAD_BITS = TM_EXP.bit_length() - 1
assert TM_EXP == 1 << PAD_BITS


def _scatter_kernel(n_ref, far_ref, pad_n_ref, pad_far_ref, used_ref, hb_ref, lpos_ref, xs_ref,
                    xl, zeros, sem, zsem):
    i = pl.program_id(0)
    steps = pl.num_programs(0)
    tm = hb_ref.shape[0]
    pairs = tm * TOP_K
    slot = i % 2

    def wait_slot(s):
        pltpu.make_async_copy(xl.at[s], _rows(xs_ref, 0, pairs), sem.at[s]).wait()

    @pl.when(i == 0)
    def _():
        zeros[...] = jnp.zeros(zeros.shape, F32)

        def fill(_, far_row, size):
            return pltpu.make_async_copy(_rows(zeros, 0, size), _rows(xs_ref, far_row, size), zsem)

        _for_each_run_piece(pad_n_ref, pad_far_ref, lambda *a: fill(*a).start(), PAD_BITS)
        _for_each_run_piece(pad_n_ref, pad_far_ref, lambda *a: fill(*a).wait(), PAD_BITS)

        tile_rows = zeros.shape[0] // ROW_LINES
        n_tiles = xs_ref.shape[0] // zeros.shape[0]

        def tail(op):
            def body(t, c):
                op(fill(None, t * tile_rows, tile_rows))
                return c
            lax.fori_loop(used_ref[0], n_tiles, body, 0)

        tail(lambda cp: cp.start())
        tail(lambda cp: cp.wait())

    @pl.when(i >= 2)
    def _():
        wait_slot(slot)

    lane = lax.broadcasted_iota(jnp.int32, (tm, 128), 1)
    lp = lpos_ref[...].astype(F32)
    cols = jnp.zeros((tm, 128), F32)
    for k in range(TOP_K):
        cols = jnp.where(lane == k, lp[:, k:k + 1], cols)
    rows_t = cols.T.astype(jnp.int16)
    j = lax.broadcasted_iota(jnp.int32, (pairs, tm), 0).astype(jnp.int16)
    onehot = jnp.zeros((pairs, tm), BF16)
    for k in range(TOP_K):
        onehot = jnp.where(j == rows_t[k:k + 1, :], jnp.ones((), BF16), onehot)
    _store_row_tiles(xl.at[slot], jnp.dot(onehot, hb_ref[...], preferred_element_type=F32))

    def send(local_row, far_row, size):
        pltpu.make_async_copy(_rows(xl.at[slot], local_row, size), _rows(xs_ref, far_row, size),
                              sem.at[slot]).start()

    _for_each_run_piece(n_ref, far_ref, send)

    @pl.when(i == steps - 1)
    def _():
        @pl.when(steps >= 2)
        def _():
            wait_slot(1 - slot)
        wait_slot(slot)


def _scatter_call(tile_n, tile_far, pad_n, pad_far, used_tiles, hb, lpos, sorted_rows):
    n, d = hb.shape
    tm = TM_POST
    table = pl.BlockSpec((1, 1, N_EXPERTS), lambda i: (i, 0, 0), memory_space=pltpu.SMEM)
    whole = pl.BlockSpec((1, 1, N_EXPERTS), lambda i: (0, 0, 0), memory_space=pltpu.SMEM)
    return pl.pallas_call(
        _scatter_kernel,
        grid=(n // tm,),
        in_specs=[table, table, whole, whole, pl.BlockSpec(memory_space=pltpu.SMEM),
                  pl.BlockSpec((tm, d), lambda i: (i, 0)),
                  pl.BlockSpec((tm, TOP_K), lambda i: (i, 0))],
        out_specs=pl.BlockSpec(memory_space=pl.ANY),
        out_shape=jax.ShapeDtypeStruct((sorted_rows * ROW_LINES, 128), F32),
        scratch_shapes=[pltpu.VMEM((2, tm * TOP_K * ROW_LINES, 128), F32),
                        pltpu.VMEM((TM_EXP * ROW_LINES, 128), F32),
                        pltpu.SemaphoreType.DMA((2,)), pltpu.SemaphoreType.DMA],
        compiler_params=pltpu.CompilerParams(
            dimension_semantics=("arbitrary",), vmem_limit_bytes=VMEM_LIMIT),
        name="dispatch_scatter",
    )(tile_n, tile_far, pad_n, pad_far, used_tiles, hb, lpos)


def _expert_kernel(tile_ref, exp_ref, flag_ref, next_ref,
                   xs_ref, wgu_hbm, bgu_ref, wd_hbm, bd_ref, o_ref,
                   wgu_f, wd_f, wgu_bf, wd_bf, wsem):
    i = pl.program_id(0)
    tm = TM_EXP
    tile_lines = tm * ROW_LINES
    cast_rows = 128

    def weight_copies(e, s):
        return (pltpu.make_async_copy(wgu_hbm.at[e], wgu_f.at[s], wsem.at[s, 0]),
                pltpu.make_async_copy(wd_hbm.at[e], wd_f.at[s], wsem.at[s, 1]))

    @pl.when(i == 0)
    def _():
        for cp in weight_copies(exp_ref[0], 0):
            cp.start()

    for t in range(EXP_TILES_PER_STEP):
        g = i * EXP_TILES_PER_STEP + t
        flag = flag_ref[g]
        e = exp_ref[g]
        x_t = xs_ref.at[pl.ds(t * tile_lines, tile_lines), :]
        o_t = o_ref.at[pl.ds(t * tile_lines, tile_lines), :]

        @pl.when((flag & 2) != 0)
        def _():
            s = (flag >> 3) & 1
            for cp in weight_copies(e, s):
                cp.wait()
            nxt = next_ref[g]

            @pl.when(nxt >= 0)
            def _():
                for cp in weight_copies(nxt, 1 - s):
                    cp.start()

            def cast(r, c):
                rows = pl.ds(pl.multiple_of(r * cast_rows, cast_rows), cast_rows)
                wgu_bf[rows, :] = wgu_f[s, rows, :].astype(BF16)
                wd_bf[rows, :] = wd_f[s, rows, :].astype(BF16)
                return c
            lax.fori_loop(0, D_MODEL // cast_rows, cast, 0)

        def run(rows):
            x = _load_row_tiles(x_t, rows).astype(BF16)
            gu = jnp.dot(x, wgu_bf[...], preferred_element_type=F32) + bgu_ref[e]
            gate = jnp.minimum(gu[:, :D_FF], SWIGLU_LIMIT)
            up = jnp.clip(gu[:, D_FF:], -SWIGLU_LIMIT, SWIGLU_LIMIT)
            act = (up + 1.0) * (gate * (1.0 / (1.0 + jnp.exp(-SWIGLU_ALPHA * gate))))
            out = jnp.dot(act.astype(BF16), wd_bf[...], preferred_element_type=F32) + bd_ref[e]
            _store_row_tiles(o_t, out)

        @pl.when((flag & 5) == 1)
        def _():
            run(tm)

        @pl.when((flag & 5) == 5)
        def _():
            run(tm // 2)
            o_t[tm // 2 * ROW_LINES:, :] = jnp.zeros((tm // 2 * ROW_LINES, 128), F32)

        @pl.when((flag & 16) != 0)
        def _():
            o_t[...] = jnp.zeros((tile_lines, 128), F32)


def _expert_call(meta, xs, wgu, bgu, wd, bd):
    lines, w = xs.shape
    tm = TM_EXP * EXP_TILES_PER_STEP
    n_items = meta[0].shape[0]
    whole = lambda a: pl.BlockSpec(a.shape, lambda i, *_: (0,) * a.ndim)
    return pl.pallas_call(
        _expert_kernel,
        grid_spec=pltpu.PrefetchScalarGridSpec(
            num_scalar_prefetch=4,
            grid=(n_items,),
            in_specs=[
                pl.BlockSpec((tm * ROW_LINES, w), lambda i, t, e, f, nx: (t[i], 0)),
                pl.BlockSpec(memory_space=pl.ANY), whole(bgu),
                pl.BlockSpec(memory_space=pl.ANY), whole(bd),
            ],
            out_specs=pl.BlockSpec((tm * ROW_LINES, w), lambda i, t, e, f, nx: (t[i], 0)),
            scratch_shapes=[pltpu.VMEM((2, D_MODEL, 2 * D_FF), F32),
                            pltpu.VMEM((2, D_FF, D_MODEL), F32),
                            pltpu.VMEM((D_MODEL, 2 * D_FF), BF16),
                            pltpu.VMEM((D_FF, D_MODEL), BF16),
                            pltpu.SemaphoreType.DMA((2, 2))],
        ),
        out_shape=jax.ShapeDtypeStruct((lines, w), F32),
        input_output_aliases={len(meta): 0},
        compiler_params=pltpu.CompilerParams(
            dimension_semantics=("arbitrary",), vmem_limit_bytes=VMEM_LIMIT),
        name="moe_experts",
    )(*meta, xs, wgu, bgu, wd, bd)


def _expert_layout(counts, n_pairs, tm):
    n_tiles = n_pairs // tm + N_EXPERTS
    assert n_tiles % EXP_TILES_PER_STEP == 0
    per_exp = (counts + tm - 1) // tm
    tile_end = jnp.cumsum(per_exp)
    tile0 = tile_end - per_exp
    used = tile_end[-1]
    step = jnp.arange(n_tiles, dtype=jnp.int32)
    tile = jnp.minimum(step, used - 1)
    e = jnp.sum((tile_end[None, :] <= tile[:, None]).astype(jnp.int32), axis=1)
    hot = e[:, None] == jnp.arange(N_EXPERTS, dtype=jnp.int32)[None, :]
    pick = lambda tab: jnp.sum(jnp.where(hot, tab[None, :], 0), axis=1)
    e_tile0 = pick(tile0)
    rows_used = pick(counts) - (step - e_tile0) * tm
    experts = jnp.arange(N_EXPERTS, dtype=jnp.int32)
    has_tiles = per_exp > 0
    slot_of_expert = (jnp.cumsum(has_tiles.astype(jnp.int32)) - 1) & 1
    later = (experts[None, :] > experts[:, None]) & has_tiles[None, :]
    next_of_expert = jnp.min(jnp.where(later, experts[None, :], N_EXPERTS), axis=1)
    next_of_expert = jnp.where(next_of_expert < N_EXPERTS, next_of_expert, -1)
    last_block = (used - 1) // EXP_TILES_PER_STEP
    flag = jnp.where(step < used, 1 + 2 * (step == e_tile0) + 4 * (rows_used <= tm // 2)
                     + 8 * pick(slot_of_expert),
                     jnp.where(step // EXP_TILES_PER_STEP == last_block, 16, 0))
    cast = lambda a: a.astype(jnp.int32)
    block = jnp.minimum(jnp.arange(n_tiles // EXP_TILES_PER_STEP, dtype=jnp.int32), last_block)
    group_start = tile0 * tm
    pad_n = per_exp * tm - counts
    pad_start = group_start + counts
    return ((cast(block), cast(e), cast(flag), cast(pick(next_of_expert))), cast(used).reshape(1),
            cast(group_start), cast(pad_n), cast(pad_start))


def _combine_kernel(n_ref, far_ref, n_next_ref, far_next_ref, h1_ref, lpos_ref, gate_ref, ys_ref,
                    o_ref, yl, sem):
    i = pl.program_id(0)
    steps = pl.num_programs(0)
    tm = h1_ref.shape[0]
    pairs = tm * TOP_K
    slot = i % 2

    def fetch(nr, fr, s):
        def recv(local_row, far_row, size):
            pltpu.make_async_copy(_rows(ys_ref, far_row, size), _rows(yl.at[s], local_row, size),
                                  sem.at[s]).start()
        _for_each_run_piece(nr, fr, recv)

    @pl.when(i == 0)
    def _():
        fetch(n_ref, far_ref, 0)

    @pl.when(i + 1 < steps)
    def _():
        fetch(n_next_ref, far_next_ref, 1 - slot)

    pltpu.make_async_copy(_rows(ys_ref, 0, pairs), yl.at[slot], sem.at[slot]).wait()

    j = lax.broadcasted_iota(jnp.int32, (tm, pairs), 1).astype(jnp.int16)
    lp = lpos_ref[...].astype(jnp.int16)
    g = gate_ref[...].astype(BF16)
    w = jnp.zeros((tm, pairs), BF16)
    for k in range(TOP_K):
        w = jnp.where(j == lp[:, k:k + 1], g[:, k:k + 1], w)
    y = _load_row_tiles(yl, pairs, lead=(slot,)).astype(BF16)
    o_ref[...] = h1_ref[...] + jnp.dot(w, y, preferred_element_type=F32)


def _combine_call(tile_n, tile_far, h1, lpos, gates, ys):
    n, d = h1.shape
    tm = TM_POST
    steps = n // tm
    table = lambda fn: pl.BlockSpec((1, 1, N_EXPERTS), fn, memory_space=pltpu.SMEM)
    cur = lambda i: (i, 0, 0)
    nxt = lambda i: (jnp.minimum(i + 1, steps - 1), 0, 0)
    return pl.pallas_call(
        _combine_kernel,
        grid=(steps,),
        in_specs=[table(cur), table(cur), table(nxt), table(nxt),
                  pl.BlockSpec((tm, d), lambda i: (i, 0)),
                  pl.BlockSpec((tm, TOP_K), lambda i: (i, 0)),
                  pl.BlockSpec((tm, TOP_K), lambda i: (i, 0)),
                  pl.BlockSpec(memory_space=pl.ANY)],
        out_specs=pl.BlockSpec((tm, d), lambda i: (i, 0)),
        out_shape=jax.ShapeDtypeStruct((n, d), F32),
        scratch_shapes=[pltpu.VMEM((2, tm * TOP_K * ROW_LINES, 128), F32),
                        pltpu.SemaphoreType.DMA((2,))],
        compiler_params=pltpu.CompilerParams(
            dimension_semantics=("arbitrary",), vmem_limit_bytes=VMEM_LIMIT),
        name="moe_combine",
    )(tile_n, tile_far, tile_n, tile_far, h1, lpos, gates, ys)


def _rotate_half_perm():
    half = QK_ROPE // 2
    perm = np.concatenate([np.arange(half, QK_ROPE), np.arange(0, half)])
    sign = np.concatenate([-np.ones(half), np.ones(half)]).astype(np.float32)
    return perm, sign


def _layer(h, l, norm_mix, w_in, conv_w, q_norm, w_uq, kv_norm, w_ukv, q_head_norm,
           k_head_norm, w_o, norm_ffn, w_router, b_router, w_gate_up, b_gate_up,
           w_down, b_down):
    b, s, d = h.shape
    n = b * s
    perm, sign = _rotate_half_perm()

    kpe_cols = w_in[l][:, OFF_KPE:OFF_KPE + QK_ROPE]
    win = jnp.concatenate([w_in[l], kpe_cols[:, perm] * sign], axis=1).astype(BF16)
    wq = w_uq[l].reshape(Q_LORA, MLA_HEADS, QK_HEAD)
    wq_rope = wq[:, :, QK_NOPE:]
    wuq = jnp.concatenate(
        [wq[:, :, :QK_NOPE].reshape(Q_LORA, MLA_HEADS * QK_NOPE),
         jnp.concatenate([wq_rope, wq_rope[:, :, perm] * sign], axis=2)
         .reshape(Q_LORA, MLA_HEADS * 2 * QK_ROPE)], axis=1).astype(BF16)
    gq, gk = q_head_norm[l], k_head_norm[l]
    gqn, gkn = gq[None, :QK_NOPE], gk[None, :QK_NOPE]
    gqr = jnp.concatenate([gq[QK_NOPE:], gq[QK_NOPE:][perm]])[None, :]
    gkr = jnp.concatenate([gk[QK_NOPE:], gk[QK_NOPE:][perm]])[None, :]

    half = QK_ROPE // 2
    inv_freq = ROPE_THETA ** (-jnp.arange(half, dtype=F32) / half)
    ang = jnp.arange(s).astype(F32)[:, None] * inv_freq[None, :]
    cos, sin = jnp.cos(ang), jnp.sin(ang)
    tbl = jnp.concatenate([cos, cos, sin, sin], axis=1)

    conv, q, k, vt = _pre_call(
        h, norm_mix[l][None, :], win, conv_w[l], q_norm[l][None, :], wuq,
        kv_norm[l][None, :], w_ukv[l].astype(BF16), gqn, gqr, gkn, gkr, tbl)
    attn = _attn_call(q, k, vt)

    tri = jnp.tril(jnp.ones((SUB_POST, SUB_POST), BF16), -1)
    h1, hb, lpos, gates, tile_n, tile_carry, counts = _post_call(
        h.reshape(n, d), conv.reshape(n, CONV_WIDTH), attn.reshape(n, CONV_WIDTH),
        w_o[l].astype(BF16), norm_ffn[l][None, :], w_router[l].astype(BF16),
        b_router[l][None, :], tri)

    meta, used_tiles, group_start, pad_n, pad_start = _expert_layout(counts[0], n * TOP_K, TM_EXP)
    sorted_rows = n * TOP_K + N_EXPERTS * TM_EXP
    tile_far = (tile_carry + group_start[None, None, :]).astype(jnp.int32)
    xs = _scatter_call(tile_n, tile_far, pad_n[None, None, :], pad_start[None, None, :],
                       used_tiles, hb, lpos, sorted_rows)
    ys = _expert_call(meta, xs, w_gate_up[l], b_gate_up[l][:, None, :],
                      w_down[l], b_down[l][:, None, :])
    out = _combine_call(tile_n, tile_far, h1, lpos, gates, ys)
    return out.reshape(b, s, d)


def kernel(x, norm_mix, w_in, conv_w, q_norm, w_uq, kv_norm, w_ukv, q_head_norm, k_head_norm,
           w_o, norm_ffn, w_router, b_router, w_gate_up, b_gate_up, w_down, b_down):
    h = x
    for l in range(norm_mix.shape[0]):
        h = _layer(h, l, norm_mix, w_in, conv_w, q_norm, w_uq, kv_norm, w_ukv, q_head_norm,
                   k_head_norm, w_o, norm_ffn, w_router, b_router, w_gate_up, b_gate_up,
                   w_down, b_down)
    return h
```

```python
import math

import jax
import jax.numpy as jnp
import numpy as np
from jax import lax
from jax.experimental import pallas as pl
from jax.experimental.pallas import tpu as pltpu

F32 = jnp.float32
BF16 = jnp.bfloat16

D_MODEL = 1024
CONV_WIDTH = 512
MLA_HEADS = 4
QK_NOPE = 128
QK_ROPE = 64
QK_HEAD = QK_NOPE + QK_ROPE
QK_PAD = 256
V_HEAD = 128
Q_LORA = 384
KV_LORA = 256
ROPE_THETA = 10000.0
CHUNK = 64
N_EXPERTS = 32
TOP_K = 4
D_FF = 1024
SWIGLU_LIMIT = 7.0
SWIGLU_ALPHA = 1.702
EPS = 1e-6

OFF_B, OFF_C, OFF_X, OFF_CQ, OFF_CKV, OFF_KPE = 0, 512, 1024, 1536, 1920, 2176
IN_PROJ_EXT = 2304

MIB = 1024 * 1024
VMEM_LIMIT = {"pre": 40 * MIB, "attn": 32 * MIB, "post": 36 * MIB, "scatter": 28 * MIB,
              "experts": 56 * MIB, "combine": 32 * MIB}

TS_PRE = 1024
TQ = 256
Q_TILES_PER_STEP = 8
TM_POST = 512
SUB_POST = 256
POST_TILES_PER_STEP = 2
TM_EXP = 512
EXP_TILES_PER_STEP = 2
HALO = 8


def _rms(x, g):
    return x * lax.rsqrt(jnp.mean(x * x, axis=-1, keepdims=True) + EPS) * g


ROW_LINES = D_MODEL // 128


def _load_row_tiles(ref, rows, lead=()):
    return jnp.concatenate(
        [ref[lead + (pl.ds(c, rows, stride=ROW_LINES), slice(None))] for c in range(ROW_LINES)],
        axis=1)


def _store_row_tiles(ref, val):
    rows = val.shape[0]
    for c in range(ROW_LINES):
        ref[pl.ds(c, rows, stride=ROW_LINES), :] = val[:, 128 * c:128 * (c + 1)]


def _pre_kernel(x_ref, nm_ref, win_ref, convw_ref, qn_ref, wuq_ref, kvn_ref, wukv_ref,
                gqn_ref, gqr_ref, gkn_ref, gkr_ref, tbl_ref,
                conv_ref, q_ref, k_ref, vt_ref, ext_ref):
    s = pl.program_id(1)
    ts = x_ref.shape[1]
    sub = TQ
    starts = tuple(range(0, ts, sub))

    @pl.when(s == 0)
    def _():
        ext_ref[0:HALO, :] = jnp.zeros((HALO, CONV_WIDTH), F32)

    @pl.when(s != 0)
    def _():
        ext_ref[0:HALO, :] = ext_ref[ts:ts + HALO, :]

    projs = [jnp.dot(_rms(x_ref[0, r0:r0 + sub, :], nm_ref[...]).astype(BF16), win_ref[...],
                     preferred_element_type=F32) for r0 in starts]

    lane = lax.broadcasted_iota(jnp.int32, (sub, 128), 1)
    first_half = lane < QK_ROPE
    scale = math.log2(math.e) / math.sqrt(QK_HEAD)

    for r0, proj in zip(starts, projs):
        rows = slice(r0, r0 + sub)
        u = proj[:, OFF_C:OFF_C + CONV_WIDTH] * proj[:, OFF_X:OFF_X + CONV_WIDTH]
        ext_ref[HALO + r0:HALO + r0 + sub, :] = u
        y = (convw_ref[2:3, :] * u
             + convw_ref[1:2, :] * ext_ref[HALO + r0 - 1:HALO + r0 - 1 + sub, :]
             + convw_ref[0:1, :] * ext_ref[HALO + r0 - 2:HALO + r0 - 2 + sub, :])
        conv_ref[0, rows, :] = (proj[:, OFF_B:OFF_B + CONV_WIDTH] * y).astype(BF16)

        hq = _rms(proj[:, OFF_CQ:OFF_CQ + Q_LORA], qn_ref[...]).astype(BF16)
        qp = jnp.dot(hq, wuq_ref[...], preferred_element_type=F32)
        hkv = _rms(proj[:, OFF_CKV:OFF_CKV + KV_LORA], kvn_ref[...]).astype(BF16)
        kvp = jnp.dot(hkv, wukv_ref[...], preferred_element_type=F32)

        tbl = tbl_ref[rows, :]

        def rope_block(grp, gain):
            r = grp * tbl * gain
            r = r + pltpu.roll(r, QK_ROPE, axis=1)
            return jnp.where(first_half, r, 0.0)

        kpe = proj[:, OFF_KPE:OFF_KPE + 128]
        ss_kpe = jnp.sum(jnp.where(first_half, kpe * kpe, 0.0), axis=-1, keepdims=True)
        k_rope = rope_block(kpe, gkr_ref[...])
        for h in range(MLA_HEADS):
            qn = qp[:, 128 * h:128 * (h + 1)]
            qg = qp[:, 512 + 128 * h:512 + 128 * (h + 1)]
            ss = (jnp.sum(qn * qn, axis=-1, keepdims=True)
                  + jnp.sum(jnp.where(first_half, qg * qg, 0.0), axis=-1, keepdims=True))
            rs = lax.rsqrt(ss * (1.0 / QK_HEAD) + EPS) * scale
            q_ref[0, h, rows, 0:128] = (qn * rs * gqn_ref[...]).astype(BF16)
            q_ref[0, h, rows, 128:256] = (rope_block(qg, gqr_ref[...]) * rs).astype(BF16)

            kn = kvp[:, 256 * h:256 * h + 128]
            ssk = jnp.sum(kn * kn, axis=-1, keepdims=True) + ss_kpe
            rsk = lax.rsqrt(ssk * (1.0 / QK_HEAD) + EPS)
            k_ref[0, h, rows, 0:128] = (kn * rsk * gkn_ref[...]).astype(BF16)
            k_ref[0, h, rows, 128:256] = (k_rope * rsk).astype(BF16)
            v = kvp[:, 256 * h + 128:256 * (h + 1)]
            vt_ref[0, h, r0 // TQ] = v.T.astype(BF16)


def _pre_call(x, nm, win, convw, qn, wuq, kvn, wukv, gqn, gqr, gkn, gkr, tbl):
    b, s, d = x.shape
    ts = TS_PRE
    const = lambda shape: pl.BlockSpec(shape, lambda bi, si: (0,) * len(shape))
    return pl.pallas_call(
        _pre_kernel,
        grid=(b, s // ts),
        in_specs=[
            pl.BlockSpec((1, ts, d), lambda bi, si: (bi, si, 0)),
            const((1, d)), const((d, IN_PROJ_EXT)), const((3, CONV_WIDTH)),
            const((1, Q_LORA)), const((Q_LORA, 1024)),
            const((1, KV_LORA)), const((KV_LORA, 1024)),
            const((1, 128)), const((1, 128)), const((1, 128)), const((1, 128)),
            pl.BlockSpec((ts, 128), lambda bi, si: (si, 0)),
        ],
        out_specs=[
            pl.BlockSpec((1, ts, CONV_WIDTH), lambda bi, si: (bi, si, 0)),
            pl.BlockSpec((1, MLA_HEADS, ts, QK_PAD), lambda bi, si: (bi, 0, si, 0)),
            pl.BlockSpec((1, MLA_HEADS, ts, QK_PAD), lambda bi, si: (bi, 0, si, 0)),
            pl.BlockSpec((1, MLA_HEADS, ts // TQ, V_HEAD, TQ), lambda bi, si: (bi, 0, si, 0, 0)),
        ],
        out_shape=[
            jax.ShapeDtypeStruct((b, s, CONV_WIDTH), BF16),
            jax.ShapeDtypeStruct((b, MLA_HEADS, s, QK_PAD), BF16),
            jax.ShapeDtypeStruct((b, MLA_HEADS, s, QK_PAD), BF16),
            jax.ShapeDtypeStruct((b, MLA_HEADS, s // TQ, V_HEAD, TQ), BF16),
        ],
        scratch_shapes=[pltpu.VMEM((ts + HALO, CONV_WIDTH), F32)],
        compiler_params=pltpu.CompilerParams(
            dimension_semantics=("arbitrary", "arbitrary"), vmem_limit_bytes=VMEM_LIMIT["pre"]),
        name="pre_mixer",
    )(x, nm, win, convw, qn, wuq, kvn, wukv, gqn, gqr, gkn, gkr, tbl)


def _attn_kernel(q_ref, k_ref, vt_ref, o_ref, m_ref, l_ref, acc_ref, s_ref):
    tq = TQ
    key_c = lax.broadcasted_iota(jnp.int32, (tq, tq), 0) // CHUNK
    qry_c = lax.broadcasted_iota(jnp.int32, (tq, tq), 1) // CHUNK
    allowed = key_c <= qry_c
    for t in range(q_ref.shape[2] // tq):
        _attn_tile(pl.program_id(1) * (q_ref.shape[2] // tq) + t, slice(t * tq, (t + 1) * tq),
                   allowed, q_ref, k_ref, vt_ref, o_ref, m_ref, l_ref, acc_ref, s_ref)


def _attn_tile(i, q_rows, allowed, q_ref, k_ref, vt_ref, o_ref, m_ref, l_ref, acc_ref, s_ref):
    tq = TQ
    m_ref[...] = jnp.full(m_ref.shape, -1e30, F32)
    l_ref[...] = jnp.zeros(l_ref.shape, F32)
    acc_ref[...] = jnp.zeros(acc_ref.shape, F32)

    def scores(j, keep):
        start = pl.multiple_of(j * tq, tq)
        out = []
        for h in range(MLA_HEADS):
            kb = k_ref[0, h, pl.ds(start, tq), :]
            st = lax.dot_general(kb, q_ref[0, h, q_rows, :], (((1,), (1,)), ((), ())),
                                 preferred_element_type=F32)
            out.append(st if keep is None else jnp.where(keep, st, -1e30))
        return out

    def update(j):
        probs, alphas = [], []
        for h in range(MLA_HEADS):
            st = s_ref[h]
            m = m_ref[h]
            m_new = jnp.maximum(m, jnp.max(st, axis=0, keepdims=True))
            alpha = jnp.exp2(m - m_new)
            p = jnp.exp2(st - m_new)
            l_ref[h] = alpha * l_ref[h] + jnp.sum(p, axis=0, keepdims=True)
            m_ref[h] = m_new
            probs.append(p.astype(BF16))
            alphas.append(alpha)
        for h in range(MLA_HEADS):
            acc_ref[h] = alphas[h] * acc_ref[h] + jnp.dot(vt_ref[0, h, j], probs[h],
                                                           preferred_element_type=F32)

    def stage(sts):
        for h in range(MLA_HEADS):
            s_ref[h] = sts[h]

    stage(scores(0, allowed | (i > 0)))

    def body(j, c):
        nxt = scores(j + 1, None)
        update(j)
        stage(nxt)
        return c

    lax.fori_loop(0, i - 1, body, 0)

    @pl.when(i > 0)
    def _():
        nxt = scores(i, allowed)
        update(i - 1)
        stage(nxt)

    update(i)
    for h in range(MLA_HEADS):
        o = acc_ref[h] / l_ref[h]
        o_ref[0, q_rows, V_HEAD * h:V_HEAD * (h + 1)] = o.T.astype(BF16)


def _attn_call(q, k, vt):
    b, nh, s, _ = q.shape
    rows = TQ * Q_TILES_PER_STEP
    return pl.pallas_call(
        _attn_kernel,
        grid=(b, s // rows),
        in_specs=[
            pl.BlockSpec((1, nh, rows, QK_PAD), lambda bi, qi: (bi, 0, qi, 0)),
            pl.BlockSpec((1, nh, s, QK_PAD), lambda bi, qi: (bi, 0, 0, 0)),
            pl.BlockSpec((1, nh, s // TQ, V_HEAD, TQ), lambda bi, qi: (bi, 0, 0, 0, 0)),
        ],
        out_specs=pl.BlockSpec((1, rows, nh * V_HEAD), lambda bi, qi: (bi, qi, 0)),
        out_shape=jax.ShapeDtypeStruct((b, s, nh * V_HEAD), BF16),
        scratch_shapes=[pltpu.VMEM((nh, 1, TQ), F32), pltpu.VMEM((nh, 1, TQ), F32),
                        pltpu.VMEM((nh, V_HEAD, TQ), F32), pltpu.VMEM((nh, TQ, TQ), F32)],
        compiler_params=pltpu.CompilerParams(
            dimension_semantics=("arbitrary", "arbitrary"), vmem_limit_bytes=VMEM_LIMIT["attn"]),
        name="mla_attention",
    )(q, k, vt)


def _post_kernel(x_ref, conv_ref, attn_ref, wo_ref, g_ref, wr_ref, br_ref, tri_ref,
                 h1_ref, hb_ref, lpos_ref, gate_ref, tile_n_ref, tile_carry_ref, cnt_ref,
                 carry_ref):
    step = pl.program_id(0)

    @pl.when(step == 0)
    def _():
        carry_ref[...] = jnp.zeros_like(carry_ref)

    for t in range(x_ref.shape[0] // TM_POST):
        _post_tile(t, x_ref, conv_ref, attn_ref, wo_ref, g_ref, wr_ref, br_ref, tri_ref,
                   h1_ref, hb_ref, lpos_ref, gate_ref, tile_n_ref, tile_carry_ref, carry_ref)
    cnt_ref[...] = carry_ref[...].astype(jnp.int32)


def _post_tile(t, x_ref, conv_ref, attn_ref, wo_ref, g_ref, wr_ref, br_ref, tri_ref,
               h1_ref, hb_ref, lpos_ref, gate_ref, tile_n_ref, tile_carry_ref, carry_ref):
    tm = TM_POST
    sub = tri_ref.shape[0]
    subs = [slice(r0, r0 + sub) for r0 in range(t * tm, (t + 1) * tm, sub)]
    h1s = [x_ref[rows, :]
           + jnp.dot(conv_ref[rows, :], wo_ref[0:CONV_WIDTH, :], preferred_element_type=F32)
           + jnp.dot(attn_ref[rows, :], wo_ref[CONV_WIDTH:, :], preferred_element_type=F32)
           for rows in subs]
    lane = lax.broadcasted_iota(jnp.int32, (sub, N_EXPERTS), 1).astype(F32)
    lane4 = lax.broadcasted_iota(jnp.int32, (sub, TOP_K), 1)

    def cols(c):
        return jnp.where(lane4 == 0, c[0], jnp.where(lane4 == 1, c[1],
                         jnp.where(lane4 == 2, c[2], c[3])))

    picks, members = [], []
    for rows, h1 in zip(subs, h1s):
        h1_ref[rows, :] = h1
        hb = _rms(h1, g_ref[...]).astype(BF16)
        hb_ref[rows, :] = hb
        lg = jnp.dot(hb, wr_ref[...], preferred_element_type=F32) + br_ref[...]
        member = jnp.zeros((sub, N_EXPERTS), F32)
        vals, idxs, hots = [], [], []
        for _ in range(TOP_K):
            m = jnp.max(lg, axis=-1, keepdims=True)
            ix = jnp.min(jnp.where(lg == m, lane, float(N_EXPERTS)), axis=-1, keepdims=True)
            hot = lane == ix
            lg = jnp.where(hot, -jnp.inf, lg)
            member = member + hot.astype(F32)
            vals.append(m)
            idxs.append(ix)
            hots.append(hot)
        exps = [jnp.exp(v - vals[0]) for v in vals]
        denom = exps[0] + exps[1] + exps[2] + exps[3]
        gate_ref[rows, :] = cols([e / denom for e in exps])
        picks.append((idxs, hots))
        members.append(member)

    sub_n = [jnp.sum(member, axis=0, keepdims=True) for member in members]
    tile_n = sub_n[0]
    for n_s in sub_n[1:]:
        tile_n = tile_n + n_s
    earlier = jnp.zeros((1, N_EXPERTS), F32)
    for rows, member, n_s, (idxs, hots) in zip(subs, members, sub_n, picks):
        before = jnp.dot(tri_ref[...], member.astype(BF16), preferred_element_type=F32) + earlier
        lpos = [jnp.sum(jnp.where(hot, before, 0.0) + jnp.where(lane < ix, tile_n, 0.0),
                        axis=-1, keepdims=True) for hot, ix in zip(hots, idxs)]
        lpos_ref[rows, :] = cols(lpos).astype(jnp.int32)
        earlier = earlier + n_s
    tile_n_ref[t] = tile_n.astype(jnp.int32)
    tile_carry_ref[t] = carry_ref[...].astype(jnp.int32)
    carry_ref[...] = carry_ref[...] + tile_n


def _post_call(xf, conv, attn, wo, g, wr, br, tri):
    n, d = xf.shape
    tm = TM_POST * POST_TILES_PER_STEP
    const = lambda shape: pl.BlockSpec(shape, lambda i: (0,) * len(shape))
    row = lambda w: pl.BlockSpec((tm, w), lambda i: (i, 0))
    table = pl.BlockSpec((POST_TILES_PER_STEP, 1, N_EXPERTS), lambda i: (i, 0, 0))
    return pl.pallas_call(
        _post_kernel,
        grid=(n // tm,),
        in_specs=[row(d), row(CONV_WIDTH), row(CONV_WIDTH), const((d, d)), const((1, d)),
                  const((d, N_EXPERTS)), const((1, N_EXPERTS)), const(tri.shape)],
        out_specs=[row(d), row(d), row(TOP_K), row(TOP_K), table, table,
                   const((1, N_EXPERTS))],
        out_shape=[
            jax.ShapeDtypeStruct((n, d), F32),
            jax.ShapeDtypeStruct((n, d), BF16),
            jax.ShapeDtypeStruct((n, TOP_K), jnp.int32),
            jax.ShapeDtypeStruct((n, TOP_K), F32),
            jax.ShapeDtypeStruct((n // TM_POST, 1, N_EXPERTS), jnp.int32),
            jax.ShapeDtypeStruct((n // TM_POST, 1, N_EXPERTS), jnp.int32),
            jax.ShapeDtypeStruct((1, N_EXPERTS), jnp.int32),
        ],
        scratch_shapes=[pltpu.VMEM((1, N_EXPERTS), F32)],
        compiler_params=pltpu.CompilerParams(
            dimension_semantics=("arbitrary",), vmem_limit_bytes=VMEM_LIMIT["post"]),
        name="post_mixer_router",
    )(xf, conv, attn, wo, g, wr, br, tri)


RUN_BITS = 10


def _rows(ref, first_row, n_rows):
    return ref.at[pl.ds(pl.multiple_of(first_row * ROW_LINES, ROW_LINES), n_rows * ROW_LINES), :]


def _for_each_run_piece(n_ref, far_ref, fn, bits=RUN_BITS):
    def per_expert(e, local0):
        n_e = n_ref[0, 0, e]
        far0 = far_ref[0, 0, e]

        for b in reversed(range(bits)):
            done = (n_e >> (b + 1)) << (b + 1)

            @pl.when(((n_e >> b) & 1) == 1)
            def _():
                fn(local0 + done, far0 + done, 1 << b)
        return local0 + n_e

    lax.fori_loop(0, N_EXPERTS, per_expert, 0)


PAD_BITS = TM_EXP.bit_length() - 1
assert TM_EXP == 1 << PAD_BITS


def _scatter_kernel(n_ref, far_ref, pad_n_ref, pad_far_ref, used_ref, hb_ref, lpos_ref, xs_ref,
                    xl, zeros, sem, zsem):
    i = pl.program_id(0)
    steps = pl.num_programs(0)
    tm = hb_ref.shape[0]
    pairs = tm * TOP_K
    slot = i % 2

    def wait_slot(s):
        pltpu.make_async_copy(xl.at[s], _rows(xs_ref, 0, pairs), sem.at[s]).wait()

    @pl.when(i == 0)
    def _():
        zeros[...] = jnp.zeros(zeros.shape, F32)

        def fill(_, far_row, size):
            return pltpu.make_async_copy(_rows(zeros, 0, size), _rows(xs_ref, far_row, size), zsem)

        _for_each_run_piece(pad_n_ref, pad_far_ref, lambda *a: fill(*a).start(), PAD_BITS)
        _for_each_run_piece(pad_n_ref, pad_far_ref, lambda *a: fill(*a).wait(), PAD_BITS)

        tile_rows = zeros.shape[0] // ROW_LINES
        n_tiles = xs_ref.shape[0] // zeros.shape[0]

        def tail(op):
            def body(t, c):
                op(fill(None, t * tile_rows, tile_rows))
                return c
            lax.fori_loop(used_ref[0], n_tiles, body, 0)

        tail(lambda cp: cp.start())
        tail(lambda cp: cp.wait())

    @pl.when(i >= 2)
    def _():
        wait_slot(slot)

    lane = lax.broadcasted_iota(jnp.int32, (tm, 128), 1)
    lp = lpos_ref[...].astype(F32)
    cols = jnp.zeros((tm, 128), F32)
    for k in range(TOP_K):
        cols = jnp.where(lane == k, lp[:, k:k + 1], cols)
    rows_t = cols.T.astype(jnp.int16)
    j = lax.broadcasted_iota(jnp.int32, (pairs, tm), 0).astype(jnp.int16)
    onehot = jnp.zeros((pairs, tm), BF16)
    for k in range(TOP_K):
        onehot = jnp.where(j == rows_t[k:k + 1, :], jnp.ones((), BF16), onehot)
    _store_row_tiles(xl.at[slot], jnp.dot(onehot, hb_ref[...], preferred_element_type=F32))

    def send(local_row, far_row, size):
        pltpu.make_async_copy(_rows(xl.at[slot], local_row, size), _rows(xs_ref, far_row, size),
                              sem.at[slot]).start()

    _for_each_run_piece(n_ref, far_ref, send)

    @pl.when(i == steps - 1)
    def _():
        @pl.when(steps >= 2)
        def _():
            wait_slot(1 - slot)
        wait_slot(slot)


def _scatter_call(tile_n, tile_far, pad_n, pad_far, used_tiles, hb, lpos, sorted_rows):
    n, d = hb.shape
    tm = TM_POST
    table = pl.BlockSpec((1, 1, N_EXPERTS), lambda i: (i, 0, 0), memory_space=pltpu.SMEM)
    whole = pl.BlockSpec((1, 1, N_EXPERTS), lambda i: (0, 0, 0), memory_space=pltpu.SMEM)
    return pl.pallas_call(
        _scatter_kernel,
        grid=(n // tm,),
        in_specs=[table, table, whole, whole, pl.BlockSpec(memory_space=pltpu.SMEM),
                  pl.BlockSpec((tm, d), lambda i: (i, 0)),
                  pl.BlockSpec((tm, TOP_K), lambda i: (i, 0))],
        out_specs=pl.BlockSpec(memory_space=pl.ANY),
        out_shape=jax.ShapeDtypeStruct((sorted_rows * ROW_LINES, 128), F32),
        scratch_shapes=[pltpu.VMEM((2, tm * TOP_K * ROW_LINES, 128), F32),
                        pltpu.VMEM((TM_EXP * ROW_LINES, 128), F32),
                        pltpu.SemaphoreType.DMA((2,)), pltpu.SemaphoreType.DMA],
        compiler_params=pltpu.CompilerParams(
            dimension_semantics=("arbitrary",), vmem_limit_bytes=VMEM_LIMIT["scatter"]),
        name="dispatch_scatter",
    )(tile_n, tile_far, pad_n, pad_far, used_tiles, hb, lpos)


def _expert_kernel(tile_ref, exp_ref, flag_ref, next_ref,
                   xs_ref, wgu_hbm, bgu_ref, wd_hbm, bd_ref, o_ref,
                   wgu_f, wd_f, wgu_bf, wd_bf, wsem):
    i = pl.program_id(0)
    tm = TM_EXP
    tile_lines = tm * ROW_LINES
    cast_rows = 128

    def weight_copies(e, s):
        return (pltpu.make_async_copy(wgu_hbm.at[e], wgu_f.at[s], wsem.at[s, 0]),
                pltpu.make_async_copy(wd_hbm.at[e], wd_f.at[s], wsem.at[s, 1]))

    @pl.when(i == 0)
    def _():
        for cp in weight_copies(exp_ref[0], 0):
            cp.start()

    for t in range(EXP_TILES_PER_STEP):
        g = i * EXP_TILES_PER_STEP + t
        flag = flag_ref[g]
        e = exp_ref[g]
        x_t = xs_ref.at[pl.ds(t * tile_lines, tile_lines), :]
        o_t = o_ref.at[pl.ds(t * tile_lines, tile_lines), :]

        @pl.when((flag & 2) != 0)
        def _():
            s = (flag >> 3) & 1
            for cp in weight_copies(e, s):
                cp.wait()
            nxt = next_ref[g]

            @pl.when(nxt >= 0)
            def _():
                for cp in weight_copies(nxt, 1 - s):
                    cp.start()

            def cast(r, c):
                rows = pl.ds(pl.multiple_of(r * cast_rows, cast_rows), cast_rows)
                wgu_bf[rows, :] = wgu_f[s, rows, :].astype(BF16)
                wd_bf[rows, :] = wd_f[s, rows, :].astype(BF16)
                return c
            lax.fori_loop(0, D_MODEL // cast_rows, cast, 0)

        def run(rows):
            x = _load_row_tiles(x_t, rows).astype(BF16)
            gu = jnp.dot(x, wgu_bf[...], preferred_element_type=F32) + bgu_ref[e]
            gate = jnp.minimum(gu[:, :D_FF], SWIGLU_LIMIT)
            up = jnp.clip(gu[:, D_FF:], -SWIGLU_LIMIT, SWIGLU_LIMIT)
            act = (up + 1.0) * (gate * (1.0 / (1.0 + jnp.exp(-SWIGLU_ALPHA * gate))))
            out = jnp.dot(act.astype(BF16), wd_bf[...], preferred_element_type=F32) + bd_ref[e]
            _store_row_tiles(o_t, out)

        @pl.when((flag & 5) == 1)
        def _():
            run(tm)

        @pl.when((flag & 5) == 5)
        def _():
            run(tm // 2)
            o_t[tm // 2 * ROW_LINES:, :] = jnp.zeros((tm // 2 * ROW_LINES, 128), F32)

        @pl.when((flag & 16) != 0)
        def _():
            o_t[...] = jnp.zeros((tile_lines, 128), F32)


def _expert_call(meta, xs, wgu, bgu, wd, bd):
    lines, w = xs.shape
    tm = TM_EXP * EXP_TILES_PER_STEP
    n_items = meta[0].shape[0]
    whole = lambda a: pl.BlockSpec(a.shape, lambda i, *_: (0,) * a.ndim)
    return pl.pallas_call(
        _expert_kernel,
        grid_spec=pltpu.PrefetchScalarGridSpec(
            num_scalar_prefetch=4,
            grid=(n_items,),
            in_specs=[
                pl.BlockSpec((tm * ROW_LINES, w), lambda i, t, e, f, nx: (t[i], 0)),
                pl.BlockSpec(memory_space=pl.ANY), whole(bgu),
                pl.BlockSpec(memory_space=pl.ANY), whole(bd),
            ],
            out_specs=pl.BlockSpec((tm * ROW_LINES, w), lambda i, t, e, f, nx: (t[i], 0)),
            scratch_shapes=[pltpu.VMEM((2, D_MODEL, 2 * D_FF), F32),
                            pltpu.VMEM((2, D_FF, D_MODEL), F32),
                            pltpu.VMEM((D_MODEL, 2 * D_FF), BF16),
                            pltpu.VMEM((D_FF, D_MODEL), BF16),
                            pltpu.SemaphoreType.DMA((2, 2))],
        ),
        out_shape=jax.ShapeDtypeStruct((lines, w), F32),
        input_output_aliases={len(meta): 0},
        compiler_params=pltpu.CompilerParams(
            dimension_semantics=("arbitrary",), vmem_limit_bytes=VMEM_LIMIT["experts"]),
        name="moe_experts",
    )(*meta, xs, wgu, bgu, wd, bd)


def _expert_layout(counts, n_pairs, tm):
    n_tiles = n_pairs // tm + N_EXPERTS
    assert n_tiles % EXP_TILES_PER_STEP == 0
    per_exp = (counts + tm - 1) // tm
    tile_end = jnp.cumsum(per_exp)
    tile0 = tile_end - per_exp
    used = tile_end[-1]
    step = jnp.arange(n_tiles, dtype=jnp.int32)
    tile = jnp.minimum(step, used - 1)
    e = jnp.sum((tile_end[None, :] <= tile[:, None]).astype(jnp.int32), axis=1)
    hot = e[:, None] == jnp.arange(N_EXPERTS, dtype=jnp.int32)[None, :]
    pick = lambda tab: jnp.sum(jnp.where(hot, tab[None, :], 0), axis=1)
    e_tile0 = pick(tile0)
    rows_used = pick(counts) - (step - e_tile0) * tm
    experts = jnp.arange(N_EXPERTS, dtype=jnp.int32)
    has_tiles = per_exp > 0
    slot_of_expert = (jnp.cumsum(has_tiles.astype(jnp.int32)) - 1) & 1
    later = (experts[None, :] > experts[:, None]) & has_tiles[None, :]
    next_of_expert = jnp.min(jnp.where(later, experts[None, :], N_EXPERTS), axis=1)
    next_of_expert = jnp.where(next_of_expert < N_EXPERTS, next_of_expert, -1)
    last_block = (used - 1) // EXP_TILES_PER_STEP
    flag = jnp.where(step < used, 1 + 2 * (step == e_tile0) + 4 * (rows_used <= tm // 2)
                     + 8 * pick(slot_of_expert),
                     jnp.where(step // EXP_TILES_PER_STEP == last_block, 16, 0))
    cast = lambda a: a.astype(jnp.int32)
    block = jnp.minimum(jnp.arange(n_tiles // EXP_TILES_PER_STEP, dtype=jnp.int32), last_block)
    group_start = tile0 * tm
    pad_n = per_exp * tm - counts
    pad_start = group_start + counts
    return ((cast(block), cast(e), cast(flag), cast(pick(next_of_expert))), cast(used).reshape(1),
            cast(group_start), cast(pad_n), cast(pad_start))


def _combine_kernel(n_ref, far_ref, n_next_ref, far_next_ref, h1_ref, lpos_ref, gate_ref, ys_ref,
                    o_ref, yl, sem):
    i = pl.program_id(0)
    steps = pl.num_programs(0)
    tm = h1_ref.shape[0]
    pairs = tm * TOP_K
    slot = i % 2

    def fetch(nr, fr, s):
        def recv(local_row, far_row, size):
            pltpu.make_async_copy(_rows(ys_ref, far_row, size), _rows(yl.at[s], local_row, size),
                                  sem.at[s]).start()
        _for_each_run_piece(nr, fr, recv)

    @pl.when(i == 0)
    def _():
        fetch(n_ref, far_ref, 0)

    @pl.when(i + 1 < steps)
    def _():
        fetch(n_next_ref, far_next_ref, 1 - slot)

    pltpu.make_async_copy(_rows(ys_ref, 0, pairs), yl.at[slot], sem.at[slot]).wait()

    j = lax.broadcasted_iota(jnp.int32, (tm, pairs), 1).astype(jnp.int16)
    lp = lpos_ref[...].astype(jnp.int16)
    g = gate_ref[...].astype(BF16)
    w = jnp.zeros((tm, pairs), BF16)
    for k in range(TOP_K):
        w = jnp.where(j == lp[:, k:k + 1], g[:, k:k + 1], w)
    y = _load_row_tiles(yl, pairs, lead=(slot,)).astype(BF16)
    o_ref[...] = h1_ref[...] + jnp.dot(w, y, preferred_element_type=F32)


def _combine_call(tile_n, tile_far, h1, lpos, gates, ys):
    n, d = h1.shape
    tm = TM_POST
    steps = n // tm
    table = lambda fn: pl.BlockSpec((1, 1, N_EXPERTS), fn, memory_space=pltpu.SMEM)
    cur = lambda i: (i, 0, 0)
    nxt = lambda i: (jnp.minimum(i + 1, steps - 1), 0, 0)
    return pl.pallas_call(
        _combine_kernel,
        grid=(steps,),
        in_specs=[table(cur), table(cur), table(nxt), table(nxt),
                  pl.BlockSpec((tm, d), lambda i: (i, 0)),
                  pl.BlockSpec((tm, TOP_K), lambda i: (i, 0)),
                  pl.BlockSpec((tm, TOP_K), lambda i: (i, 0)),
                  pl.BlockSpec(memory_space=pl.ANY)],
        out_specs=pl.BlockSpec((tm, d), lambda i: (i, 0)),
        out_shape=jax.ShapeDtypeStruct((n, d), F32),
        scratch_shapes=[pltpu.VMEM((2, tm * TOP_K * ROW_LINES, 128), F32),
                        pltpu.SemaphoreType.DMA((2,))],
        compiler_params=pltpu.CompilerParams(
            dimension_semantics=("arbitrary",), vmem_limit_bytes=VMEM_LIMIT["combine"]),
        name="moe_combine",
    )(tile_n, tile_far, tile_n, tile_far, h1, lpos, gates, ys)


def _rotate_half_perm():
    half = QK_ROPE // 2
    perm = np.concatenate([np.arange(half, QK_ROPE), np.arange(0, half)])
    sign = np.concatenate([-np.ones(half), np.ones(half)]).astype(np.float32)
    return perm, sign


def _layer(h, l, norm_mix, w_in, conv_w, q_norm, w_uq, kv_norm, w_ukv, q_head_norm,
           k_head_norm, w_o, norm_ffn, w_router, b_router, w_gate_up, b_gate_up,
           w_down, b_down):
    b, s, d = h.shape
    n = b * s
    perm, sign = _rotate_half_perm()

    kpe_cols = w_in[l][:, OFF_KPE:OFF_KPE + QK_ROPE]
    win = jnp.concatenate([w_in[l], kpe_cols[:, perm] * sign], axis=1).astype(BF16)
    wq = w_uq[l].reshape(Q_LORA, MLA_HEADS, QK_HEAD)
    wq_rope = wq[:, :, QK_NOPE:]
    wuq = jnp.concatenate(
        [wq[:, :, :QK_NOPE].reshape(Q_LORA, MLA_HEADS * QK_NOPE),
         jnp.concatenate([wq_rope, wq_rope[:, :, perm] * sign], axis=2)
         .reshape(Q_LORA, MLA_HEADS * 2 * QK_ROPE)], axis=1).astype(BF16)
    gq, gk = q_head_norm[l], k_head_norm[l]
    gqn, gkn = gq[None, :QK_NOPE], gk[None, :QK_NOPE]
    gqr = jnp.concatenate([gq[QK_NOPE:], gq[QK_NOPE:][perm]])[None, :]
    gkr = jnp.concatenate([gk[QK_NOPE:], gk[QK_NOPE:][perm]])[None, :]

    half = QK_ROPE // 2
    inv_freq = ROPE_THETA ** (-jnp.arange(half, dtype=F32) / half)
    ang = jnp.arange(s).astype(F32)[:, None] * inv_freq[None, :]
    cos, sin = jnp.cos(ang), jnp.sin(ang)
    tbl = jnp.concatenate([cos, cos, sin, sin], axis=1)

    conv, q, k, vt = _pre_call(
        h, norm_mix[l][None, :], win, conv_w[l], q_norm[l][None, :], wuq,
        kv_norm[l][None, :], w_ukv[l].astype(BF16), gqn, gqr, gkn, gkr, tbl)
    attn = _attn_call(q, k, vt)

    tri = jnp.tril(jnp.ones((SUB_POST, SUB_POST), BF16), -1)
    h1, hb, lpos, gates, tile_n, tile_carry, counts = _post_call(
        h.reshape(n, d), conv.reshape(n, CONV_WIDTH), attn.reshape(n, CONV_WIDTH),
        w_o[l].astype(BF16), norm_ffn[l][None, :], w_router[l].astype(BF16),
        b_router[l][None, :], tri)

    meta, used_tiles, group_start, pad_n, pad_start = _expert_layout(counts[0], n * TOP_K, TM_EXP)
    sorted_rows = n * TOP_K + N_EXPERTS * TM_EXP
    tile_far = (tile_carry + group_start[None, None, :]).astype(jnp.int32)
    xs = _scatter_call(tile_n, tile_far, pad_n[None, None, :], pad_start[None, None, :],
                       used_tiles, hb, lpos, sorted_rows)
    ys = _expert_call(meta, xs, w_gate_up[l], b_gate_up[l][:, None, :],
                      w_down[l], b_down[l][:, None, :])
    out = _combine_call(tile_n, tile_far, h1, lpos, gates, ys)
    return out.reshape(b, s, d)


def kernel(x, norm_mix, w_in, conv_w, q_norm, w_uq, kv_norm, w_ukv, q_head_norm, k_head_norm,
           w_o, norm_ffn, w_router, b_router, w_gate_up, b_gate_up, w_down, b_down):
    h = x
    for l in range(norm_mix.shape[0]):
        h = _layer(h, l, norm_mix, w_in, conv_w, q_norm, w_uq, kv_norm, w_ukv, q_head_norm,
                   k_head_norm, w_o, norm_ffn, w_router, b_router, w_gate_up, b_gate_up,
                   w_down, b_down)
    return h
```

```python
import math

import jax
import jax.numpy as jnp
import numpy as np
from jax import lax
from jax.experimental import pallas as pl
from jax.experimental.pallas import tpu as pltpu

F32 = jnp.float32
BF16 = jnp.bfloat16

D_MODEL = 1024
CONV_WIDTH = 512
MLA_HEADS = 4
QK_NOPE = 128
QK_ROPE = 64
QK_HEAD = QK_NOPE + QK_ROPE
QK_PAD = 256
V_HEAD = 128
Q_LORA = 384
KV_LORA = 256
ROPE_THETA = 10000.0
CHUNK = 64
N_EXPERTS = 32
TOP_K = 4
D_FF = 1024
SWIGLU_LIMIT = 7.0
SWIGLU_ALPHA = 1.702
EPS = 1e-6

OFF_B, OFF_C, OFF_X, OFF_CQ, OFF_CKV, OFF_KPE = 0, 512, 1024, 1536, 1920, 2176
IN_PROJ_EXT = 2304

MIB = 1024 * 1024
VMEM_LIMIT = {"pre": 40 * MIB, "attn": 32 * MIB, "post": 36 * MIB, "scatter": 28 * MIB,
              "experts": 56 * MIB, "combine": 32 * MIB}

TS_PRE = 1024
TQ = 256
Q_TILES_PER_STEP = 8
TM_POST = 512
SUB_POST = 256
POST_TILES_PER_STEP = 2
TM_EXP = 512
EXP_TILES_PER_STEP = 2
HALO = 8


def _rms(x, g):
    return x * lax.rsqrt(jnp.mean(x * x, axis=-1, keepdims=True) + EPS) * g


ROW_LINES = D_MODEL // 128


def _load_row_tiles(ref, rows, lead=()):
    return jnp.concatenate(
        [ref[lead + (pl.ds(c, rows, stride=ROW_LINES), slice(None))] for c in range(ROW_LINES)],
        axis=1)


def _store_row_tiles(ref, val):
    rows = val.shape[0]
    for c in range(ROW_LINES):
        ref[pl.ds(c, rows, stride=ROW_LINES), :] = val[:, 128 * c:128 * (c + 1)]


def _pre_kernel(x_ref, nm_ref, win_ref, convw_ref, qn_ref, wuq_ref, kvn_ref, wukv_ref,
                gqn_ref, gqr_ref, gkn_ref, gkr_ref, tbl_ref,
                conv_ref, q_ref, k_ref, vt_ref, ext_ref):
    s = pl.program_id(1)
    ts = x_ref.shape[1]
    sub = TQ
    starts = tuple(range(0, ts, sub))

    @pl.when(s == 0)
    def _():
        ext_ref[0:HALO, :] = jnp.zeros((HALO, CONV_WIDTH), F32)

    @pl.when(s != 0)
    def _():
        ext_ref[0:HALO, :] = ext_ref[ts:ts + HALO, :]

    projs = [jnp.dot(_rms(x_ref[0, r0:r0 + sub, :], nm_ref[...]).astype(BF16), win_ref[...],
                     preferred_element_type=F32) for r0 in starts]

    lane = lax.broadcasted_iota(jnp.int32, (sub, 128), 1)
    first_half = lane < QK_ROPE
    scale = math.log2(math.e) / math.sqrt(QK_HEAD)

    for r0, proj in zip(starts, projs):
        rows = slice(r0, r0 + sub)
        u = proj[:, OFF_C:OFF_C + CONV_WIDTH] * proj[:, OFF_X:OFF_X + CONV_WIDTH]
        ext_ref[HALO + r0:HALO + r0 + sub, :] = u
        y = (convw_ref[2:3, :] * u
             + convw_ref[1:2, :] * ext_ref[HALO + r0 - 1:HALO + r0 - 1 + sub, :]
             + convw_ref[0:1, :] * ext_ref[HALO + r0 - 2:HALO + r0 - 2 + sub, :])
        conv_ref[0, rows, :] = (proj[:, OFF_B:OFF_B + CONV_WIDTH] * y).astype(BF16)

        hq = _rms(proj[:, OFF_CQ:OFF_CQ + Q_LORA], qn_ref[...]).astype(BF16)
        qp = jnp.dot(hq, wuq_ref[...], preferred_element_type=F32)
        hkv = _rms(proj[:, OFF_CKV:OFF_CKV + KV_LORA], kvn_ref[...]).astype(BF16)
        kvp = jnp.dot(hkv, wukv_ref[...], preferred_element_type=F32)

        tbl = tbl_ref[rows, :]

        def rope_block(grp, gain):
            r = grp * tbl * gain
            r = r + pltpu.roll(r, QK_ROPE, axis=1)
            return jnp.where(first_half, r, 0.0)

        kpe = proj[:, OFF_KPE:OFF_KPE + 128]
        ss_kpe = jnp.sum(jnp.where(first_half, kpe * kpe, 0.0), axis=-1, keepdims=True)
        k_rope = rope_block(kpe, gkr_ref[...])
        for h in range(MLA_HEADS):
            qn = qp[:, 128 * h:128 * (h + 1)]
            qg = qp[:, 512 + 128 * h:512 + 128 * (h + 1)]
            ss = (jnp.sum(qn * qn, axis=-1, keepdims=True)
                  + jnp.sum(jnp.where(first_half, qg * qg, 0.0), axis=-1, keepdims=True))
            rs = lax.rsqrt(ss * (1.0 / QK_HEAD) + EPS) * scale
            q_ref[0, h, rows, 0:128] = (qn * rs * gqn_ref[...]).astype(BF16)
            q_ref[0, h, rows, 128:256] = (rope_block(qg, gqr_ref[...]) * rs).astype(BF16)

            kn = kvp[:, 256 * h:256 * h + 128]
            ssk = jnp.sum(kn * kn, axis=-1, keepdims=True) + ss_kpe
            rsk = lax.rsqrt(ssk * (1.0 / QK_HEAD) + EPS)
            k_ref[0, h, rows, 0:128] = (kn * rsk * gkn_ref[...]).astype(BF16)
            k_ref[0, h, rows, 128:256] = (k_rope * rsk).astype(BF16)
            v = kvp[:, 256 * h + 128:256 * (h + 1)]
            vt_ref[0, h, r0 // TQ] = v.T.astype(BF16)


def _pre_call(x, nm, win, convw, qn, wuq, kvn, wukv, gqn, gqr, gkn, gkr, tbl):
    b, s, d = x.shape
    ts = TS_PRE
    const = lambda shape: pl.BlockSpec(shape, lambda bi, si: (0,) * len(shape))
    return pl.pallas_call(
        _pre_kernel,
        grid=(b, s // ts),
        in_specs=[
            pl.BlockSpec((1, ts, d), lambda bi, si: (bi, si, 0)),
            const((1, d)), const((d, IN_PROJ_EXT)), const((3, CONV_WIDTH)),
            const((1, Q_LORA)), const((Q_LORA, 1024)),
            const((1, KV_LORA)), const((KV_LORA, 1024)),
            const((1, 128)), const((1, 128)), const((1, 128)), const((1, 128)),
            pl.BlockSpec((ts, 128), lambda bi, si: (si, 0)),
        ],
        out_specs=[
            pl.BlockSpec((1, ts, CONV_WIDTH), lambda bi, si: (bi, si, 0)),
            pl.BlockSpec((1, MLA_HEADS, ts, QK_PAD), lambda bi, si: (bi, 0, si, 0)),
            pl.BlockSpec((1, MLA_HEADS, ts, QK_PAD), lambda bi, si: (bi, 0, si, 0)),
            pl.BlockSpec((1, MLA_HEADS, ts // TQ, V_HEAD, TQ), lambda bi, si: (bi, 0, si, 0, 0)),
        ],
        out_shape=[
            jax.ShapeDtypeStruct((b, s, CONV_WIDTH), BF16),
            jax.ShapeDtypeStruct((b, MLA_HEADS, s, QK_PAD), BF16),
            jax.ShapeDtypeStruct((b, MLA_HEADS, s, QK_PAD), BF16),
            jax.ShapeDtypeStruct((b, MLA_HEADS, s // TQ, V_HEAD, TQ), BF16),
        ],
        scratch_shapes=[pltpu.VMEM((ts + HALO, CONV_WIDTH), F32)],
        compiler_params=pltpu.CompilerParams(
            dimension_semantics=("arbitrary", "arbitrary"), vmem_limit_bytes=VMEM_LIMIT["pre"],
            allow_input_fusion=[i in (2, 5, 7, 12) for i in range(13)]),
        name="pre_mixer",
    )(x, nm, win, convw, qn, wuq, kvn, wukv, gqn, gqr, gkn, gkr, tbl)


def _attn_kernel(q_ref, k_ref, vt_ref, o_ref, m_ref, l_ref, acc_ref, s_ref):
    tq = TQ
    key_c = lax.broadcasted_iota(jnp.int32, (tq, tq), 0) // CHUNK
    qry_c = lax.broadcasted_iota(jnp.int32, (tq, tq), 1) // CHUNK
    allowed = key_c <= qry_c
    for t in range(q_ref.shape[2] // tq):
        _attn_tile(pl.program_id(1) * (q_ref.shape[2] // tq) + t, slice(t * tq, (t + 1) * tq),
                   allowed, q_ref, k_ref, vt_ref, o_ref, m_ref, l_ref, acc_ref, s_ref)


def _attn_tile(i, q_rows, allowed, q_ref, k_ref, vt_ref, o_ref, m_ref, l_ref, acc_ref, s_ref):
    tq = TQ
    m_ref[...] = jnp.full(m_ref.shape, -1e30, F32)
    l_ref[...] = jnp.zeros(l_ref.shape, F32)
    acc_ref[...] = jnp.zeros(acc_ref.shape, F32)

    def scores(j, keep):
        start = pl.multiple_of(j * tq, tq)
        out = []
        for h in range(MLA_HEADS):
            kb = k_ref[0, h, pl.ds(start, tq), :]
            st = lax.dot_general(kb, q_ref[0, h, q_rows, :], (((1,), (1,)), ((), ())),
                                 preferred_element_type=F32)
            out.append(st if keep is None else jnp.where(keep, st, -1e30))
        return out

    def update(j):
        probs, alphas = [], []
        for h in range(MLA_HEADS):
            st = s_ref[h]
            m = m_ref[h]
            m_new = jnp.maximum(m, jnp.max(st, axis=0, keepdims=True))
            alpha = jnp.exp2(m - m_new)
            p = jnp.exp2(st - m_new)
            l_ref[h] = alpha * l_ref[h] + jnp.sum(p, axis=0, keepdims=True)
            m_ref[h] = m_new
            probs.append(p.astype(BF16))
            alphas.append(alpha)
        for h in range(MLA_HEADS):
            acc_ref[h] = alphas[h] * acc_ref[h] + jnp.dot(vt_ref[0, h, j], probs[h],
                                                           preferred_element_type=F32)

    def stage(sts):
        for h in range(MLA_HEADS):
            s_ref[h] = sts[h]

    stage(scores(0, allowed | (i > 0)))

    def body(j, c):
        nxt = scores(j + 1, None)
        update(j)
        stage(nxt)
        return c

    lax.fori_loop(0, i - 1, body, 0)

    @pl.when(i > 0)
    def _():
        nxt = scores(i, allowed)
        update(i - 1)
        stage(nxt)

    update(i)
    for h in range(MLA_HEADS):
        o = acc_ref[h] / l_ref[h]
        o_ref[0, q_rows, V_HEAD * h:V_HEAD * (h + 1)] = o.T.astype(BF16)


def _attn_call(q, k, vt):
    b, nh, s, _ = q.shape
    rows = TQ * Q_TILES_PER_STEP
    return pl.pallas_call(
        _attn_kernel,
        grid=(b, s // rows),
        in_specs=[
            pl.BlockSpec((1, nh, rows, QK_PAD), lambda bi, qi: (bi, 0, qi, 0)),
            pl.BlockSpec((1, nh, s, QK_PAD), lambda bi, qi: (bi, 0, 0, 0)),
            pl.BlockSpec((1, nh, s // TQ, V_HEAD, TQ), lambda bi, qi: (bi, 0, 0, 0, 0)),
        ],
        out_specs=pl.BlockSpec((1, rows, nh * V_HEAD), lambda bi, qi: (bi, qi, 0)),
        out_shape=jax.ShapeDtypeStruct((b, s, nh * V_HEAD), BF16),
        scratch_shapes=[pltpu.VMEM((nh, 1, TQ), F32), pltpu.VMEM((nh, 1, TQ), F32),
                        pltpu.VMEM((nh, V_HEAD, TQ), F32), pltpu.VMEM((nh, TQ, TQ), F32)],
        compiler_params=pltpu.CompilerParams(
            dimension_semantics=("arbitrary", "arbitrary"), vmem_limit_bytes=VMEM_LIMIT["attn"]),
        name="mla_attention",
    )(q, k, vt)


def _post_kernel(x_ref, conv_ref, attn_ref, wo_ref, g_ref, wr_ref, br_ref, tri_ref,
                 h1_ref, hb_ref, lpos_ref, gate_ref, tile_n_ref, tile_carry_ref, cnt_ref,
                 carry_ref):
    step = pl.program_id(0)

    @pl.when(step == 0)
    def _():
        carry_ref[...] = jnp.zeros_like(carry_ref)

    for t in range(x_ref.shape[0] // TM_POST):
        _post_tile(t, x_ref, conv_ref, attn_ref, wo_ref, g_ref, wr_ref, br_ref, tri_ref,
                   h1_ref, hb_ref, lpos_ref, gate_ref, tile_n_ref, tile_carry_ref, carry_ref)
    cnt_ref[...] = carry_ref[...].astype(jnp.int32)


def _post_tile(t, x_ref, conv_ref, attn_ref, wo_ref, g_ref, wr_ref, br_ref, tri_ref,
               h1_ref, hb_ref, lpos_ref, gate_ref, tile_n_ref, tile_carry_ref, carry_ref):
    tm = TM_POST
    sub = tri_ref.shape[0]
    subs = [slice(r0, r0 + sub) for r0 in range(t * tm, (t + 1) * tm, sub)]
    h1s = [x_ref[rows, :]
           + jnp.dot(conv_ref[rows, :], wo_ref[0:CONV_WIDTH, :], preferred_element_type=F32)
           + jnp.dot(attn_ref[rows, :], wo_ref[CONV_WIDTH:, :], preferred_element_type=F32)
           for rows in subs]
    lane = lax.broadcasted_iota(jnp.int32, (sub, N_EXPERTS), 1).astype(F32)
    lane4 = lax.broadcasted_iota(jnp.int32, (sub, TOP_K), 1)

    def cols(c):
        return jnp.where(lane4 == 0, c[0], jnp.where(lane4 == 1, c[1],
                         jnp.where(lane4 == 2, c[2], c[3])))

    picks, members = [], []
    for rows, h1 in zip(subs, h1s):
        h1_ref[rows, :] = h1
        hb = _rms(h1, g_ref[...]).astype(BF16)
        hb_ref[rows, :] = hb
        lg = jnp.dot(hb, wr_ref[...], preferred_element_type=F32) + br_ref[...]
        member = jnp.zeros((sub, N_EXPERTS), F32)
        vals, idxs, hots = [], [], []
        for _ in range(TOP_K):
            m = jnp.max(lg, axis=-1, keepdims=True)
            ix = jnp.min(jnp.where(lg == m, lane, float(N_EXPERTS)), axis=-1, keepdims=True)
            hot = lane == ix
            lg = jnp.where(hot, -jnp.inf, lg)
            member = member + hot.astype(F32)
            vals.append(m)
            idxs.append(ix)
            hots.append(hot)
        exps = [jnp.exp(v - vals[0]) for v in vals]
        denom = exps[0] + exps[1] + exps[2] + exps[3]
        gate_ref[rows, :] = cols([e / denom for e in exps])
        picks.append((idxs, hots))
        members.append(member)

    sub_n = [jnp.sum(member, axis=0, keepdims=True) for member in members]
    tile_n = sub_n[0]
    for n_s in sub_n[1:]:
        tile_n = tile_n + n_s
    earlier = jnp.zeros((1, N_EXPERTS), F32)
    for rows, member, n_s, (idxs, hots) in zip(subs, members, sub_n, picks):
        before = jnp.dot(tri_ref[...], member.astype(BF16), preferred_element_type=F32) + earlier
        lpos = [jnp.sum(jnp.where(hot, before, 0.0) + jnp.where(lane < ix, tile_n, 0.0),
                        axis=-1, keepdims=True) for hot, ix in zip(hots, idxs)]
        lpos_ref[rows, :] = cols(lpos).astype(jnp.int32)
        earlier = earlier + n_s
    tile_n_ref[t] = tile_n.astype(jnp.int32)
    tile_carry_ref[t] = carry_ref[...].astype(jnp.int32)
    carry_ref[...] = carry_ref[...] + tile_n


def _post_call(xf, conv, attn, wo, g, wr, br, tri):
    n, d = xf.shape
    tm = TM_POST * POST_TILES_PER_STEP
    const = lambda shape: pl.BlockSpec(shape, lambda i: (0,) * len(shape))
    row = lambda w: pl.BlockSpec((tm, w), lambda i: (i, 0))
    table = pl.BlockSpec((POST_TILES_PER_STEP, 1, N_EXPERTS), lambda i: (i, 0, 0))
    return pl.pallas_call(
        _post_kernel,
        grid=(n // tm,),
        in_specs=[row(d), row(CONV_WIDTH), row(CONV_WIDTH), const((d, d)), const((1, d)),
                  const((d, N_EXPERTS)), const((1, N_EXPERTS)), const(tri.shape)],
        out_specs=[row(d), row(d), row(TOP_K), row(TOP_K), table, table,
                   const((1, N_EXPERTS))],
        out_shape=[
            jax.ShapeDtypeStruct((n, d), F32),
            jax.ShapeDtypeStruct((n, d), BF16),
            jax.ShapeDtypeStruct((n, TOP_K), jnp.int32),
            jax.ShapeDtypeStruct((n, TOP_K), F32),
            jax.ShapeDtypeStruct((n // TM_POST, 1, N_EXPERTS), jnp.int32),
            jax.ShapeDtypeStruct((n // TM_POST, 1, N_EXPERTS), jnp.int32),
            jax.ShapeDtypeStruct((1, N_EXPERTS), jnp.int32),
        ],
        scratch_shapes=[pltpu.VMEM((1, N_EXPERTS), F32)],
        compiler_params=pltpu.CompilerParams(
            dimension_semantics=("arbitrary",), vmem_limit_bytes=VMEM_LIMIT["post"],
            allow_input_fusion=[i in (3, 5, 7) for i in range(8)]),
        name="post_mixer_router",
    )(xf, conv, attn, wo, g, wr, br, tri)


RUN_BITS = 10


def _rows(ref, first_row, n_rows):
    return ref.at[pl.ds(pl.multiple_of(first_row * ROW_LINES, ROW_LINES), n_rows * ROW_LINES), :]


def _for_each_run_piece(n_ref, far_ref, fn, bits=RUN_BITS):
    def per_expert(e, local0):
        n_e = n_ref[0, 0, e]
        far0 = far_ref[0, 0, e]

        for b in reversed(range(bits)):
            done = (n_e >> (b + 1)) << (b + 1)

            @pl.when(((n_e >> b) & 1) == 1)
            def _():
                fn(local0 + done, far0 + done, 1 << b)
        return local0 + n_e

    lax.fori_loop(0, N_EXPERTS, per_expert, 0)


PAD_BITS = TM_EXP.bit_length() - 1
assert TM_EXP == 1 << PAD_BITS


def _scatter_kernel(n_ref, far_ref, pad_n_ref, pad_far_ref, used_ref, hb_ref, lpos_ref, xs_ref,
                    xl, zeros, sem, zsem):
    i = pl.program_id(0)
    steps = pl.num_programs(0)
    tm = hb_ref.shape[0]
    pairs = tm * TOP_K
    slot = i % 2

    def wait_slot(s):
        pltpu.make_async_copy(xl.at[s], _rows(xs_ref, 0, pairs), sem.at[s]).wait()

    @pl.when(i == 0)
    def _():
        zeros[...] = jnp.zeros(zeros.shape, F32)

        def fill(_, far_row, size):
            return pltpu.make_async_copy(_rows(zeros, 0, size), _rows(xs_ref, far_row, size), zsem)

        _for_each_run_piece(pad_n_ref, pad_far_ref, lambda *a: fill(*a).start(), PAD_BITS)
        _for_each_run_piece(pad_n_ref, pad_far_ref, lambda *a: fill(*a).wait(), PAD_BITS)

        tile_rows = zeros.shape[0] // ROW_LINES
        n_tiles = xs_ref.shape[0] // zeros.shape[0]

        def tail(op):
            def body(t, c):
                op(fill(None, t * tile_rows, tile_rows))
                return c
            lax.fori_loop(used_ref[0], n_tiles, body, 0)

        tail(lambda cp: cp.start())
        tail(lambda cp: cp.wait())

    @pl.when(i >= 2)
    def _():
        wait_slot(slot)

    lane = lax.broadcasted_iota(jnp.int32, (tm, 128), 1)
    lp = lpos_ref[...].astype(F32)
    cols = jnp.zeros((tm, 128), F32)
    for k in range(TOP_K):
        cols = jnp.where(lane == k, lp[:, k:k + 1], cols)
    rows_t = cols.T.astype(jnp.int16)
    j = lax.broadcasted_iota(jnp.int32, (pairs, tm), 0).astype(jnp.int16)
    onehot = jnp.zeros((pairs, tm), BF16)
    for k in range(TOP_K):
        onehot = jnp.where(j == rows_t[k:k + 1, :], jnp.ones((), BF16), onehot)
    _store_row_tiles(xl.at[slot], jnp.dot(onehot, hb_ref[...], preferred_element_type=F32))

    def send(local_row, far_row, size):
        pltpu.make_async_copy(_rows(xl.at[slot], local_row, size), _rows(xs_ref, far_row, size),
                              sem.at[slot]).start()

    _for_each_run_piece(n_ref, far_ref, send)

    @pl.when(i == steps - 1)
    def _():
        @pl.when(steps >= 2)
        def _():
            wait_slot(1 - slot)
        wait_slot(slot)


def _scatter_call(tile_n, tile_far, pad_n, pad_far, used_tiles, hb, lpos, sorted_rows):
    n, d = hb.shape
    tm = TM_POST
    table = pl.BlockSpec((1, 1, N_EXPERTS), lambda i: (i, 0, 0), memory_space=pltpu.SMEM)
    whole = pl.BlockSpec((1, 1, N_EXPERTS), lambda i: (0, 0, 0), memory_space=pltpu.SMEM)
    return pl.pallas_call(
        _scatter_kernel,
        grid=(n // tm,),
        in_specs=[table, table, whole, whole, pl.BlockSpec(memory_space=pltpu.SMEM),
                  pl.BlockSpec((tm, d), lambda i: (i, 0)),
                  pl.BlockSpec((tm, TOP_K), lambda i: (i, 0))],
        out_specs=pl.BlockSpec(memory_space=pl.ANY),
        out_shape=jax.ShapeDtypeStruct((sorted_rows * ROW_LINES, 128), F32),
        scratch_shapes=[pltpu.VMEM((2, tm * TOP_K * ROW_LINES, 128), F32),
                        pltpu.VMEM((TM_EXP * ROW_LINES, 128), F32),
                        pltpu.SemaphoreType.DMA((2,)), pltpu.SemaphoreType.DMA],
        compiler_params=pltpu.CompilerParams(
            dimension_semantics=("arbitrary",), vmem_limit_bytes=VMEM_LIMIT["scatter"]),
        name="dispatch_scatter",
    )(tile_n, tile_far, pad_n, pad_far, used_tiles, hb, lpos)


def _expert_kernel(tile_ref, exp_ref, flag_ref, next_ref,
                   xs_ref, wgu_hbm, bgu_ref, wd_hbm, bd_ref, o_ref,
                   wgu_f, wd_f, wgu_bf, wd_bf, wsem):
    i = pl.program_id(0)
    tm = TM_EXP
    tile_lines = tm * ROW_LINES
    cast_rows = 128

    def weight_copies(e, s):
        return (pltpu.make_async_copy(wgu_hbm.at[e], wgu_f.at[s], wsem.at[s, 0]),
                pltpu.make_async_copy(wd_hbm.at[e], wd_f.at[s], wsem.at[s, 1]))

    @pl.when(i == 0)
    def _():
        for cp in weight_copies(exp_ref[0], 0):
            cp.start()

    for t in range(EXP_TILES_PER_STEP):
        g = i * EXP_TILES_PER_STEP + t
        flag = flag_ref[g]
        e = exp_ref[g]
        x_t = xs_ref.at[pl.ds(t * tile_lines, tile_lines), :]
        o_t = o_ref.at[pl.ds(t * tile_lines, tile_lines), :]

        @pl.when((flag & 2) != 0)
        def _():
            s = (flag >> 3) & 1
            for cp in weight_copies(e, s):
                cp.wait()
            nxt = next_ref[g]

            @pl.when(nxt >= 0)
            def _():
                for cp in weight_copies(nxt, 1 - s):
                    cp.start()

            def cast(r, c):
                rows = pl.ds(pl.multiple_of(r * cast_rows, cast_rows), cast_rows)
                wgu_bf[rows, :] = wgu_f[s, rows, :].astype(BF16)
                wd_bf[rows, :] = wd_f[s, rows, :].astype(BF16)
                return c
            lax.fori_loop(0, D_MODEL // cast_rows, cast, 0)

        def run(rows):
            x = _load_row_tiles(x_t, rows).astype(BF16)
            gu = jnp.dot(x, wgu_bf[...], preferred_element_type=F32) + bgu_ref[e]
            gate = jnp.minimum(gu[:, :D_FF], SWIGLU_LIMIT)
            up = jnp.clip(gu[:, D_FF:], -SWIGLU_LIMIT, SWIGLU_LIMIT)
            act = (up + 1.0) * (gate * (1.0 / (1.0 + jnp.exp(-SWIGLU_ALPHA * gate))))
            out = jnp.dot(act.astype(BF16), wd_bf[...], preferred_element_type=F32) + bd_ref[e]
            _store_row_tiles(o_t, out)

        @pl.when((flag & 5) == 1)
        def _():
            run(tm)

        @pl.when((flag & 5) == 5)
        def _():
            run(tm // 2)
            o_t[tm // 2 * ROW_LINES:, :] = jnp.zeros((tm // 2 * ROW_LINES, 128), F32)

        @pl.when((flag & 16) != 0)
        def _():
            o_t[...] = jnp.zeros((tile_lines, 128), F32)


def _expert_call(meta, xs, wgu, bgu, wd, bd):
    lines, w = xs.shape
    tm = TM_EXP * EXP_TILES_PER_STEP
    n_items = meta[0].shape[0]
    whole = lambda a: pl.BlockSpec(a.shape, lambda i, *_: (0,) * a.ndim)
    return pl.pallas_call(
        _expert_kernel,
        grid_spec=pltpu.PrefetchScalarGridSpec(
            num_scalar_prefetch=4,
            grid=(n_items,),
            in_specs=[
                pl.BlockSpec((tm * ROW_LINES, w), lambda i, t, e, f, nx: (t[i], 0)),
                pl.BlockSpec(memory_space=pl.ANY), whole(bgu),
                pl.BlockSpec(memory_space=pl.ANY), whole(bd),
            ],
            out_specs=pl.BlockSpec((tm * ROW_LINES, w), lambda i, t, e, f, nx: (t[i], 0)),
            scratch_shapes=[pltpu.VMEM((2, D_MODEL, 2 * D_FF), F32),
                            pltpu.VMEM((2, D_FF, D_MODEL), F32),
                            pltpu.VMEM((D_MODEL, 2 * D_FF), BF16),
                            pltpu.VMEM((D_FF, D_MODEL), BF16),
                            pltpu.SemaphoreType.DMA((2, 2))],
        ),
        out_shape=jax.ShapeDtypeStruct((lines, w), F32),
        input_output_aliases={len(meta): 0},
        compiler_params=pltpu.CompilerParams(
            dimension_semantics=("arbitrary",), vmem_limit_bytes=VMEM_LIMIT["experts"]),
        name="moe_experts",
    )(*meta, xs, wgu, bgu, wd, bd)


def _expert_layout(counts, n_pairs, tm):
    n_tiles = n_pairs // tm + N_EXPERTS
    assert n_tiles % EXP_TILES_PER_STEP == 0
    per_exp = (counts + tm - 1) // tm
    tile_end = jnp.cumsum(per_exp)
    tile0 = tile_end - per_exp
    used = tile_end[-1]
    step = jnp.arange(n_tiles, dtype=jnp.int32)
    tile = jnp.minimum(step, used - 1)
    e = jnp.sum((tile_end[None, :] <= tile[:, None]).astype(jnp.int32), axis=1)
    hot = e[:, None] == jnp.arange(N_EXPERTS, dtype=jnp.int32)[None, :]
    pick = lambda tab: jnp.sum(jnp.where(hot, tab[None, :], 0), axis=1)
    e_tile0 = pick(tile0)
    rows_used = pick(counts) - (step - e_tile0) * tm
    experts = jnp.arange(N_EXPERTS, dtype=jnp.int32)
    has_tiles = per_exp > 0
    slot_of_expert = (jnp.cumsum(has_tiles.astype(jnp.int32)) - 1) & 1
    later = (experts[None, :] > experts[:, None]) & has_tiles[None, :]
    next_of_expert = jnp.min(jnp.where(later, experts[None, :], N_EXPERTS), axis=1)
    next_of_expert = jnp.where(next_of_expert < N_EXPERTS, next_of_expert, -1)
    last_block = (used - 1) // EXP_TILES_PER_STEP
    flag = jnp.where(step < used, 1 + 2 * (step == e_tile0) + 4 * (rows_used <= tm // 2)
                     + 8 * pick(slot_of_expert),
                     jnp.where(step // EXP_TILES_PER_STEP == last_block, 16, 0))
    cast = lambda a: a.astype(jnp.int32)
    block = jnp.minimum(jnp.arange(n_tiles // EXP_TILES_PER_STEP, dtype=jnp.int32), last_block)
    group_start = tile0 * tm
    pad_n = per_exp * tm - counts
    pad_start = group_start + counts
    return ((cast(block), cast(e), cast(flag), cast(pick(next_of_expert))), cast(used).reshape(1),
            cast(group_start), cast(pad_n), cast(pad_start))


def _combine_kernel(n_ref, far_ref, n_next_ref, far_next_ref, h1_ref, lpos_ref, gate_ref, ys_ref,
                    o_ref, yl, sem):
    i = pl.program_id(0)
    steps = pl.num_programs(0)
    tm = h1_ref.shape[0]
    pairs = tm * TOP_K
    slot = i % 2

    def fetch(nr, fr, s):
        def recv(local_row, far_row, size):
            pltpu.make_async_copy(_rows(ys_ref, far_row, size), _rows(yl.at[s], local_row, size),
                                  sem.at[s]).start()
        _for_each_run_piece(nr, fr, recv)

    @pl.when(i == 0)
    def _():
        fetch(n_ref, far_ref, 0)

    @pl.when(i + 1 < steps)
    def _():
        fetch(n_next_ref, far_next_ref, 1 - slot)

    pltpu.make_async_copy(_rows(ys_ref, 0, pairs), yl.at[slot], sem.at[slot]).wait()

    j = lax.broadcasted_iota(jnp.int32, (tm, pairs), 1).astype(jnp.int16)
    lp = lpos_ref[...].astype(jnp.int16)
    g = gate_ref[...].astype(BF16)
    w = jnp.zeros((tm, pairs), BF16)
    for k in range(TOP_K):
        w = jnp.where(j == lp[:, k:k + 1], g[:, k:k + 1], w)
    y = _load_row_tiles(yl, pairs, lead=(slot,)).astype(BF16)
    o_ref[...] = h1_ref[...] + jnp.dot(w, y, preferred_element_type=F32)


def _combine_call(tile_n, tile_far, h1, lpos, gates, ys):
    n, d = h1.shape
    tm = TM_POST
    steps = n // tm
    table = lambda fn: pl.BlockSpec((1, 1, N_EXPERTS), fn, memory_space=pltpu.SMEM)
    cur = lambda i: (i, 0, 0)
    nxt = lambda i: (jnp.minimum(i + 1, steps - 1), 0, 0)
    return pl.pallas_call(
        _combine_kernel,
        grid=(steps,),
        in_specs=[table(cur), table(cur), table(nxt), table(nxt),
                  pl.BlockSpec((tm, d), lambda i: (i, 0)),
                  pl.BlockSpec((tm, TOP_K), lambda i: (i, 0)),
                  pl.BlockSpec((tm, TOP_K), lambda i: (i, 0)),
                  pl.BlockSpec(memory_space=pl.ANY)],
        out_specs=pl.BlockSpec((tm, d), lambda i: (i, 0)),
        out_shape=jax.ShapeDtypeStruct((n, d), F32),
        scratch_shapes=[pltpu.VMEM((2, tm * TOP_K * ROW_LINES, 128), F32),
                        pltpu.SemaphoreType.DMA((2,))],
        compiler_params=pltpu.CompilerParams(
            dimension_semantics=("arbitrary",), vmem_limit_bytes=VMEM_LIMIT["combine"]),
        name="moe_combine",
    )(tile_n, tile_far, tile_n, tile_far, h1, lpos, gates, ys)


def _rotate_half_perm():
    half = QK_ROPE // 2
    perm = np.concatenate([np.arange(half, QK_ROPE), np.arange(0, half)])
    sign = np.concatenate([-np.ones(half), np.ones(half)]).astype(np.float32)
    return perm, sign


def _layer(h, l, norm_mix, w_in, conv_w, q_norm, w_uq, kv_norm, w_ukv, q_head_norm,
           k_head_norm, w_o, norm_ffn, w_router, b_router, w_gate_up, b_gate_up,
           w_down, b_down):
    b, s, d = h.shape
    n = b * s
    perm, sign = _rotate_half_perm()

    kpe_cols = w_in[l][:, OFF_KPE:OFF_KPE + QK_ROPE]
    win = jnp.concatenate([w_in[l], kpe_cols[:, perm] * sign], axis=1).astype(BF16)
    wq = w_uq[l].reshape(Q_LORA, MLA_HEADS, QK_HEAD)
    wq_rope = wq[:, :, QK_NOPE:]
    wuq = jnp.concatenate(
        [wq[:, :, :QK_NOPE].reshape(Q_LORA, MLA_HEADS * QK_NOPE),
         jnp.concatenate([wq_rope, wq_rope[:, :, perm] * sign], axis=2)
         .reshape(Q_LORA, MLA_HEADS * 2 * QK_ROPE)], axis=1).astype(BF16)
    gq, gk = q_head_norm[l], k_head_norm[l]
    gqn, gkn = gq[None, :QK_NOPE], gk[None, :QK_NOPE]
    gqr = jnp.concatenate([gq[QK_NOPE:], gq[QK_NOPE:][perm]])[None, :]
    gkr = jnp.concatenate([gk[QK_NOPE:], gk[QK_NOPE:][perm]])[None, :]

    half = QK_ROPE // 2
    inv_freq = ROPE_THETA ** (-jnp.arange(half, dtype=F32) / half)
    ang = jnp.arange(s).astype(F32)[:, None] * inv_freq[None, :]
    cos, sin = jnp.cos(ang), jnp.sin(ang)
    tbl = jnp.concatenate([cos, cos, sin, sin], axis=1)

    conv, q, k, vt = _pre_call(
        h, norm_mix[l][None, :], win, conv_w[l], q_norm[l][None, :], wuq,
        kv_norm[l][None, :], w_ukv[l].astype(BF16), gqn, gqr, gkn, gkr, tbl)
    attn = _attn_call(q, k, vt)

    tri = jnp.tril(jnp.ones((SUB_POST, SUB_POST), BF16), -1)
    h1, hb, lpos, gates, tile_n, tile_carry, counts = _post_call(
        h.reshape(n, d), conv.reshape(n, CONV_WIDTH), attn.reshape(n, CONV_WIDTH),
        w_o[l].astype(BF16), norm_ffn[l][None, :], w_router[l].astype(BF16),
        b_router[l][None, :], tri)

    meta, used_tiles, group_start, pad_n, pad_start = _expert_layout(counts[0], n * TOP_K, TM_EXP)
    sorted_rows = n * TOP_K + N_EXPERTS * TM_EXP
    tile_far = (tile_carry + group_start[None, None, :]).astype(jnp.int32)
    xs = _scatter_call(tile_n, tile_far, pad_n[None, None, :], pad_start[None, None, :],
                       used_tiles, hb, lpos, sorted_rows)
    ys = _expert_call(meta, xs, w_gate_up[l], b_gate_up[l][:, None, :],
                      w_down[l], b_down[l][:, None, :])
    out = _combine_call(tile_n, tile_far, h1, lpos, gates, ys)
    return out.reshape(b, s, d)


def kernel(x, norm_mix, w_in, conv_w, q_norm, w_uq, kv_norm, w_ukv, q_head_norm, k_head_norm,
           w_o, norm_ffn, w_router, b_router, w_gate_up, b_gate_up, w_down, b_down):
    h = x
    for l in range(norm_mix.shape[0]):
        h = _layer(h, l, norm_mix, w_in, conv_w, q_norm, w_uq, kv_norm, w_ukv, q_head_norm,
                   k_head_norm, w_o, norm_ffn, w_router, b_router, w_gate_up, b_gate_up,
                   w_down, b_down)
    return h
```
